```python
import math
import jax, jax.numpy as jnp
from jax import lax
import numpy as np

D_MODEL = 1024
BATCH = 8
SEQ = 2048
DEPTH = 1
DEC_BATCH = 128
DEC_SEQ = 1
PAST_LEN = 16384
PAGE_SIZE = 128

HEAD_DIM = 64
ML_HEADS = D_MODEL // (2 * HEAD_DIM)
RW_HEADS = D_MODEL // (2 * HEAD_DIM)
ML_W = ML_HEADS * HEAD_DIM
RW_W = RW_HEADS * HEAD_DIM
MIX_W = ML_W + RW_W
CONV_W = 4
MLSTM_CHUNK = 64
D_DECAY_LORA = 64
D_AAA_LORA = 64
D_GATE_LORA = 128
D_FF = 4 * D_MODEL
EPS = 1e-6
GN_EPS = 64e-5
ML_IN_W = 4 * ML_W + 2 * ML_HEADS
RW_IN_W = 3 * RW_W + D_DECAY_LORA + D_AAA_LORA + D_GATE_LORA
IN_W = ML_IN_W + RW_IN_W

kernel_name = "hymba_mlstm_rwkv7_decode_step"


def rms_norm(x, w):
    xf = x.astype(jnp.float32)
    y = xf * lax.rsqrt(jnp.mean(xf * xf, axis=-1, keepdims=True) + EPS)
    return (y * w.astype(jnp.float32)).astype(x.dtype)


def head_rms_norm(h, w, n_heads):
    B, T, W = h.shape
    hf = h.astype(jnp.float32).reshape(B, T, n_heads, W // n_heads)
    hf = hf * lax.rsqrt(jnp.mean(hf * hf, axis=-1, keepdims=True) + EPS)
    return (hf.reshape(B, T, W) * w.astype(jnp.float32)).astype(h.dtype)


def head_group_norm(y, w, b, n_heads):
    B, T, W = y.shape
    yf = y.astype(jnp.float32).reshape(B, T, n_heads, W // n_heads)
    mu = jnp.mean(yf, axis=-1, keepdims=True)
    d = yf - mu
    yf = d * lax.rsqrt(jnp.mean(d * d, axis=-1, keepdims=True) + GN_EPS)
    return (yf.reshape(B, T, W) * w.astype(jnp.float32) + b.astype(jnp.float32)).astype(y.dtype)


def causal_conv(u, buf, w, b):
    T = u.shape[1]
    ext = jnp.concatenate([buf.astype(u.dtype), u], axis=1)
    out = b
    for j in range(CONV_W):
        out = out + ext[:, j:j + T] * w[j]
    return out, ext[:, -(CONV_W - 1):]


def token_shift(p, prev, mu):
    shifted = jnp.concatenate([prev.astype(p.dtype), p[:, :-1]], axis=1)
    return p + mu * (shifted - p), p[:, -1:]


def mlstm_chunkwise(q, k, v, i_pre, logf, C0, n0, m0):
    B, T, H, Dh = q.shape
    L = math.gcd(T, MLSTM_CHUNK)
    NC = T // L
    f32 = lambda a: a.astype(jnp.float32)

    def to_chunks(a):
        return jnp.moveaxis(f32(a).reshape((B, NC, L) + a.shape[2:]), 1, 0)

    xs = tuple(to_chunks(a) for a in (q, k, v, i_pre, logf))
    causal = jnp.tril(jnp.ones((L, L), dtype=bool))[None, :, :, None]

    def step(carry, blk):
        C, n, m = carry
        qb, kb, vb, ib, fb = blk
        bcum = jnp.cumsum(fb, axis=1)
        dlog = bcum[:, :, None, :] - bcum[:, None, :, :] + ib[:, None, :, :]
        dlog = jnp.where(causal, dlog, -jnp.inf)
        inter = bcum + m[:, None, :]
        m_t = jnp.maximum(inter, jnp.max(dlog, axis=2))
        wts = jnp.exp(dlog - m_t[:, :, None, :])
        sc_inter = jnp.exp(inter - m_t)
        qk = jnp.einsum('bthd,bshd->btsh', qb, kb) * wts
        num = jnp.einsum('btsh,bshd->bthd', qk, vb) + sc_inter[..., None] * jnp.einsum('bthk,bhkv->bthv', qb, C)
        den = jnp.sum(qk, axis=2) + sc_inter * jnp.einsum('bthk,bhk->bth', qb, n)
        h = num / jnp.maximum(jnp.abs(den), jnp.exp(-m_t))[..., None]
        b_end = bcum[:, -1]
        g = b_end[:, None, :] - bcum + ib
        m_new = jnp.maximum(b_end + m, jnp.max(g, axis=1))
        ws = jnp.exp(g - m_new[:, None, :])
        dec = jnp.exp(b_end + m - m_new)
        C_new = dec[..., None, None] * C + jnp.einsum('bsh,bshk,bshv->bhkv', ws, kb, vb)
        n_new = dec[..., None] * n + jnp.einsum('bsh,bshk->bhk', ws, kb)
        return (C_new, n_new, m_new), h

    (C, n, m), hc = lax.scan(step, (f32(C0), f32(n0), f32(m0)), xs)
    h = jnp.moveaxis(hc, 0, 1).reshape(B, T, H, Dh)
    return h, C, n, m


def rwkv7_scan(r, w_log, k, v, kk, a, S0):
    def step(S, xs):
        rt, wt, kt, vt, kkt, at = xs
        sa = jnp.einsum('bhvk,bhk->bhv', S, -kkt)
        S = (S * jnp.exp(wt)[:, :, None, :]
             + sa[..., None] * (kkt * at)[:, :, None, :]
             + vt[..., None] * kt[:, :, None, :])
        return S, jnp.einsum('bhvk,bhk->bhv', S, rt)

    xs = tuple(jnp.moveaxis(t, 1, 0) for t in (r, w_log, k, v, kk, a))
    S, ys = lax.scan(step, S0.astype(jnp.float32), xs)
    return jnp.moveaxis(ys, 0, 1), S


def layer_forward(x, C0, n0, m0, conv0, S0, shift0,
                  norm_mix_w, w_in, mlstm_conv_w, mlstm_conv_b, mlstm_i_b, mlstm_f_b, mlstm_norm_w,
                  rw_mu, rw_w0, rw_w_up, rw_a0, rw_a_up, rw_g_up, rw_k_k, rw_k_a, rw_r_k,
                  rw_ln_w, rw_ln_b, w_out, norm_mlp_w, mlp_up, mlp_down):
    B, T, _ = x.shape
    dt = x.dtype
    xn = rms_norm(x, norm_mix_w)
    proj = xn @ w_in
    ml = proj[..., :ML_IN_W]
    rw = proj[..., ML_IN_W:]

    qk_pre = ml[..., :2 * ML_W]
    v_ml = ml[..., 2 * ML_W:3 * ML_W].reshape(B, T, ML_HEADS, HEAD_DIM)
    o_pre = ml[..., 3 * ML_W:4 * ML_W]
    i_pre = ml[..., 4 * ML_W:4 * ML_W + ML_HEADS].astype(jnp.float32) + mlstm_i_b.astype(jnp.float32)
    f_pre = ml[..., 4 * ML_W + ML_HEADS:].astype(jnp.float32) + mlstm_f_b.astype(jnp.float32)
    qk_c, conv_new = causal_conv(qk_pre, conv0, mlstm_conv_w, mlstm_conv_b)
    qk_c = jax.nn.silu(qk_c)
    q_ml = qk_c[..., :ML_W].reshape(B, T, ML_HEADS, HEAD_DIM)
    k_ml = qk_c[..., ML_W:].reshape(B, T, ML_HEADS, HEAD_DIM) * (HEAD_DIM ** -0.5)
    h_til, C_new, n_new, m_new = mlstm_chunkwise(q_ml, k_ml, v_ml, i_pre, jax.nn.log_sigmoid(f_pre), C0, n0, m0)
    h_ml = jax.nn.sigmoid(o_pre) * h_til.reshape(B, T, ML_W).astype(dt)
    h_ml = head_rms_norm(h_ml, mlstm_norm_w, ML_HEADS)

    rs, shift_new = token_shift(rw, shift0, rw_mu)
    r = rs[..., :RW_W]
    kr = rs[..., RW_W:2 * RW_W]
    vr = rs[..., 2 * RW_W:3 * RW_W]
    o = 3 * RW_W
    xw = rs[..., o:o + D_DECAY_LORA]
    o += D_DECAY_LORA
    xa = rs[..., o:o + D_AAA_LORA]
    o += D_AAA_LORA
    xg = rs[..., o:o + D_GATE_LORA]
    w_raw = (rw_w0 + jnp.tanh(xw) @ rw_w_up).astype(jnp.float32)
    w_log = -jnp.exp(-jax.nn.softplus(-w_raw) - 0.5)
    a = jax.nn.sigmoid((rw_a0 + xa @ rw_a_up).astype(jnp.float32))
    g = jax.nn.sigmoid(xg) @ rw_g_up
    kf = kr.astype(jnp.float32)
    kk = (kf * rw_k_k.astype(jnp.float32)).reshape(B, T, RW_HEADS, HEAD_DIM)
    kk = kk / jnp.maximum(jnp.sqrt(jnp.sum(kk * kk, axis=-1, keepdims=True)), 1e-12)
    k_eff = kf * (1.0 + (a - 1.0) * rw_k_a.astype(jnp.float32))
    heads = lambda t: t.astype(jnp.float32).reshape(B, T, RW_HEADS, HEAD_DIM)
    rh, kh, vh, wh, ah = heads(r), heads(k_eff), heads(vr), heads(w_log), heads(a)
    y, S_new = rwkv7_scan(rh, wh, kh, vh, kk, ah, S0)
    bonus = jnp.sum(rh * kh * rw_r_k.astype(jnp.float32), axis=-1, keepdims=True) * vh
    y_rw = (y + bonus).reshape(B, T, RW_W).astype(dt)
    y_rw = head_group_norm(y_rw, rw_ln_w, rw_ln_b, RW_HEADS) * g

    x = x + jnp.concatenate([h_ml, y_rw], axis=-1) @ w_out
    hid = jax.nn.relu(rms_norm(x, norm_mlp_w) @ mlp_up)
    x = x + jnp.square(hid) @ mlp_down
    sd = C0.dtype
    return (x, C_new.astype(sd), n_new.astype(sd), m_new.astype(sd), conv_new.astype(sd),
            S_new.astype(sd), shift_new.astype(sd))


def setup_inputs(seed: int = 0) -> dict:
    key = jax.random.key(seed)
    ks = iter(jax.random.split(key, 48))
    f32 = jnp.float32

    def nrm(shape, s):
        return jax.random.normal(next(ks), shape, f32) * s

    def unif(shape, lo, hi):
        return jax.random.uniform(next(ks), shape, f32, lo, hi)

    Ld = DEPTH
    f_bias = jnp.broadcast_to(jnp.linspace(3.0, 6.0, ML_HEADS, dtype=f32), (Ld, ML_HEADS))
    return {
        "x_prompt": nrm((BATCH, SEQ, D_MODEL), 1.0),
        "x_sample": nrm((DEC_BATCH, DEC_SEQ, D_MODEL), 1.0),
        "state_mlstm_C": nrm((Ld, DEC_BATCH, ML_HEADS, HEAD_DIM, HEAD_DIM), 0.1),
        "state_mlstm_n": nrm((Ld, DEC_BATCH, ML_HEADS, HEAD_DIM), 0.5),
        "state_mlstm_m": nrm((Ld, DEC_BATCH, ML_HEADS), 1.0),
        "state_mlstm_conv": nrm((Ld, DEC_BATCH, CONV_W - 1, 2 * ML_W), 1.0),
        "state_rwkv_S": nrm((Ld, DEC_BATCH, RW_HEADS, HEAD_DIM, HEAD_DIM), 0.2),
        "state_rwkv_shift": nrm((Ld, DEC_BATCH, 1, RW_IN_W), 1.0),
        "norm_mix_w": 1.0 + nrm((Ld, D_MODEL), 0.01),
        "w_in": nrm((Ld, D_MODEL, IN_W), D_MODEL ** -0.5),
        "mlstm_conv_w": nrm((Ld, CONV_W, 2 * ML_W), CONV_W ** -0.5),
        "mlstm_conv_b": nrm((Ld, 2 * ML_W), 0.01),
        "mlstm_i_b": nrm((Ld, ML_HEADS), 0.1),
        "mlstm_f_b": f_bias + nrm((Ld, ML_HEADS), 0.1),
        "mlstm_norm_w": 1.0 + nrm((Ld, ML_W), 0.01),
        "rw_mu": unif((Ld, RW_IN_W), 0.0, 1.0),
        "rw_w0": unif((Ld, RW_W), -6.0, -1.0),
        "rw_w_up": nrm((Ld, D_DECAY_LORA, RW_W), 0.1),
        "rw_a0": nrm((Ld, RW_W), 0.1),
        "rw_a_up": nrm((Ld, D_AAA_LORA, RW_W), 0.1),
        "rw_g_up": nrm((Ld, D_GATE_LORA, RW_W), D_GATE_LORA ** -0.5),
        "rw_k_k": 0.85 + nrm((Ld, RW_W), 0.05),
        "rw_k_a": 1.0 + nrm((Ld, RW_W), 0.05),
        "rw_r_k": nrm((Ld, RW_HEADS, HEAD_DIM), 0.1),
        "rw_ln_w": 1.0 + nrm((Ld, RW_W), 0.01),
        "rw_ln_b": nrm((Ld, RW_W), 0.01),
        "w_out": nrm((Ld, MIX_W, D_MODEL), MIX_W ** -0.5),
        "norm_mlp_w": 1.0 + nrm((Ld, D_MODEL), 0.01),
        "mlp_up": nrm((Ld, D_MODEL, D_FF), D_MODEL ** -0.5),
        "mlp_down": nrm((Ld, D_FF, D_MODEL), D_FF ** -0.5),
        "norm_f_w": 1.0 + nrm((D_MODEL,), 0.01),
    }


def reference(x_prompt, x_sample, state_mlstm_C, state_mlstm_n, state_mlstm_m, state_mlstm_conv,
              state_rwkv_S, state_rwkv_shift, norm_mix_w, w_in, mlstm_conv_w, mlstm_conv_b, mlstm_i_b,
              mlstm_f_b, mlstm_norm_w, rw_mu, rw_w0, rw_w_up, rw_a0, rw_a_up, rw_g_up, rw_k_k, rw_k_a,
              rw_r_k, rw_ln_w, rw_ln_b, w_out, norm_mlp_w, mlp_up, mlp_down, norm_f_w):
    Bp = x_prompt.shape[0]
    sd = state_mlstm_C.dtype
    xp, xs = x_prompt, x_sample
    pC, pn, pm, pconv, pS, pshift = [], [], [], [], [], []
    sC, sn, sm, sconv, sS, sshift = [], [], [], [], [], []
    for l in range(DEPTH):
        wl = (norm_mix_w[l], w_in[l], mlstm_conv_w[l], mlstm_conv_b[l], mlstm_i_b[l], mlstm_f_b[l],
              mlstm_norm_w[l], rw_mu[l], rw_w0[l], rw_w_up[l], rw_a0[l], rw_a_up[l], rw_g_up[l],
              rw_k_k[l], rw_k_a[l], rw_r_k[l], rw_ln_w[l], rw_ln_b[l], w_out[l], norm_mlp_w[l],
              mlp_up[l], mlp_down[l])
        outp = layer_forward(
            xp,
            jnp.zeros((Bp, ML_HEADS, HEAD_DIM, HEAD_DIM), sd),
            jnp.zeros((Bp, ML_HEADS, HEAD_DIM), sd),
            jnp.zeros((Bp, ML_HEADS), sd),
            jnp.zeros((Bp, CONV_W - 1, 2 * ML_W), sd),
            jnp.zeros((Bp, RW_HEADS, HEAD_DIM, HEAD_DIM), sd),
            jnp.zeros((Bp, 1, RW_IN_W), sd),
            *wl)
        xp = outp[0]
        pC.append(outp[1]); pn.append(outp[2]); pm.append(outp[3])
        pconv.append(outp[4]); pS.append(outp[5]); pshift.append(outp[6])
        outs = layer_forward(xs, state_mlstm_C[l], state_mlstm_n[l], state_mlstm_m[l],
                             state_mlstm_conv[l], state_rwkv_S[l], state_rwkv_shift[l], *wl)
        xs = outs[0]
        sC.append(outs[1]); sn.append(outs[2]); sm.append(outs[3])
        sconv.append(outs[4]); sS.append(outs[5]); sshift.append(outs[6])
    y_prompt = rms_norm(xp, norm_f_w)
    y_sample = rms_norm(xs, norm_f_w)
    return (y_prompt, y_sample,
            jnp.stack(pC), jnp.stack(pn), jnp.stack(pm), jnp.stack(pconv), jnp.stack(pS), jnp.stack(pshift),
            jnp.stack(sC), jnp.stack(sn), jnp.stack(sm), jnp.stack(sconv), jnp.stack(sS), jnp.stack(sshift))
```

```python
import functools
import math

import jax
import jax.numpy as jnp
from jax import lax
from jax.experimental import pallas as pl
from jax.experimental.pallas import tpu as pltpu

F32 = jnp.float32
BF16 = jnp.bfloat16

D_MODEL = 1024
HEAD_DIM = 64
N_HEADS = 8
PAIRS = N_HEADS // 2
GROUP_W = N_HEADS * HEAD_DIM
CONV_W = 4
D_LORA = 64
D_GATE = 128
D_FF = 4 * D_MODEL
EPS = 1e-6
GN_EPS = 64e-5
LANES = 128
SUBLANES = 8

Q0, K0, V0, O0 = 0, GROUP_W, 2 * GROUP_W, 3 * GROUP_W
GI0 = 4 * GROUP_W
GF0 = GI0 + LANES
RW0 = GF0 + LANES
RW_IN_W = 3 * GROUP_W + 2 * D_LORA + D_GATE
IN_WP = RW0 + RW_IN_W
RS_R, RS_K, RS_V = 0, GROUP_W, 2 * GROUP_W
RS_LORA = 3 * GROUP_W
RS_G = RS_LORA + 2 * D_LORA

TB = 256
LM = 128
LR = 64
TM_IN = 256
TM_OUT = 256
FF_CHUNK = 1024
RB = 128
NEG = -1e30
W_SCALE = math.exp(-0.5)
VMEM_LIMIT = 56 * 1024 * 1024


def _dot(a, b):
    return jnp.dot(a.astype(BF16), b.astype(BF16), preferred_element_type=F32)


def _dot_nt(a, b):
    return lax.dot_general(a.astype(BF16), b.astype(BF16), (((1,), (1,)), ((), ())),
                           preferred_element_type=F32)


def _split3(x):
    h1 = x.astype(BF16)
    r1 = x - h1.astype(F32)
    h2 = r1.astype(BF16)
    h3 = (r1 - h2.astype(F32)).astype(BF16)
    return h1, h2, h3


def _dot01_l(m01, x):
    h1, h2, h3 = _split3(x)
    d = lambda h: jnp.dot(m01, h, preferred_element_type=F32)
    return d(h1) + d(h2) + d(h3)


def _dot01_r(x, m01):
    h1, h2, h3 = _split3(x)
    d = lambda h: jnp.dot(h, m01, preferred_element_type=F32)
    return d(h1) + d(h2) + d(h3)


def _log_sigmoid(x):
    return jnp.minimum(x, 0.0) - jnp.log(1.0 + jnp.exp(-jnp.abs(x)))


def _iota(shape, dim):
    return lax.broadcasted_iota(jnp.int32, shape, dim)


def _rowsum(x):
    return jnp.sum(x, axis=1, keepdims=True)


def _pair_rowsum(x, lo):
    s_lo = _rowsum(jnp.where(lo, x, 0.0))
    s_hi = _rowsum(jnp.where(lo, 0.0, x))
    return jnp.where(lo, s_lo, s_hi)


def _stack2(x, lo):
    return jnp.concatenate([jnp.where(lo, x, 0.0), jnp.where(lo, 0.0, x)], axis=0)


def _rms(x, w):
    ms = jnp.mean(x * x, axis=-1, keepdims=True)
    return x * lax.rsqrt(ms + EPS) * w


def _inproj_kernel(x_ref, nw_ref, w_ref, o_ref):
    xn = _rms(x_ref[...], nw_ref[...])
    o_ref[...] = jnp.dot(xn.astype(BF16), w_ref[...], preferred_element_type=F32)


def _inproj(x2d, norm_w, w_in_p):
    rows = x2d.shape[0]
    tm = min(TM_IN, rows)
    return pl.pallas_call(
        _inproj_kernel,
        out_shape=jax.ShapeDtypeStruct((rows, IN_WP), F32),
        grid=(rows // tm,),
        in_specs=[
            pl.BlockSpec((tm, D_MODEL), lambda i: (i, 0)),
            pl.BlockSpec((1, D_MODEL), lambda i: (0, 0)),
            pl.BlockSpec((D_MODEL, IN_WP), lambda i: (0, 0)),
        ],
        out_specs=pl.BlockSpec((tm, IN_WP), lambda i: (i, 0)),
        compiler_params=pltpu.CompilerParams(
            dimension_semantics=("arbitrary",), vmem_limit_bytes=VMEM_LIMIT),
        name="inproj",
    )(x2d, norm_w, w_in_p)


def _outmlp_kernel(x_ref, mix_ref, wout_ref, nmw_ref, up_ref, down_ref, nfw_ref, y_ref):
    x1 = x_ref[...] + jnp.dot(mix_ref[...].astype(BF16), wout_ref[...], preferred_element_type=F32)
    xn = _rms(x1, nmw_ref[...]).astype(BF16)
    acc = x1
    for c in range(D_FF // FF_CHUNK):
        sl = slice(c * FF_CHUNK, (c + 1) * FF_CHUNK)
        hid = jnp.maximum(jnp.dot(xn, up_ref[:, sl], preferred_element_type=F32), 0.0)
        acc = acc + jnp.dot((hid * hid).astype(BF16), down_ref[sl, :], preferred_element_type=F32)
    y_ref[...] = _rms(acc, nfw_ref[...])


def _outmlp(x2d, mix2d, w_out, norm_mlp_w, mlp_up, mlp_down, norm_f_w):
    rows = x2d.shape[0]
    tm = min(TM_OUT, rows)
    const = lambda shape: pl.BlockSpec(shape, lambda i: (0, 0), pipeline_mode=pl.Buffered(1))
    return pl.pallas_call(
        _outmlp_kernel,
        out_shape=jax.ShapeDtypeStruct((rows, D_MODEL), F32),
        grid=(rows // tm,),
        in_specs=[
            pl.BlockSpec((tm, D_MODEL), lambda i: (i, 0)),
            pl.BlockSpec((tm, D_MODEL), lambda i: (i, 0)),
            const((D_MODEL, D_MODEL)),
            const((1, D_MODEL)),
            const((D_MODEL, D_FF)),
            const((D_FF, D_MODEL)),
            const((1, D_MODEL)),
        ],
        out_specs=pl.BlockSpec((tm, D_MODEL), lambda i: (i, 0)),
        compiler_params=pltpu.CompilerParams(
            dimension_semantics=("arbitrary",), vmem_limit_bytes=VMEM_LIMIT),
        name="outmlp",
    )(x2d, mix2d, w_out, norm_mlp_w, mlp_up, mlp_down, norm_f_w)


def _rwkv_token_prep(lora_t, xg, w0, wup, a0, aup, gup):
    w_raw = w0 + jnp.dot(jnp.tanh(lora_t).astype(BF16), wup, preferred_element_type=F32)
    wlog = -W_SCALE * jax.nn.sigmoid(w_raw)
    a = jax.nn.sigmoid(a0 + jnp.dot(lora_t.astype(BF16), aup, preferred_element_type=F32))
    g = jnp.dot(jax.nn.sigmoid(xg).astype(BF16), gup, preferred_element_type=F32)
    return wlog, a, g


def _mlstm_chunk(c, proj_ref, q_ref, k_ref, ib_ref, fb_ref, mlnw_ref, mix_ref, C_ref, n_ref, m_ref):
    rows = pl.ds(pl.multiple_of(c * LM, LM), LM)
    row = _iota((LM, LM), 0)
    col = _iota((LM, LM), 1)
    causal = col <= row
    tril = jnp.where(causal, 1.0, 0.0).astype(BF16)
    blockdiag = (row < HEAD_DIM) == (col < HEAD_DIM)
    lane = _iota((1, LANES), 1)
    lo = lane < HEAD_DIM

    i_tile = proj_ref[0, rows, GI0:GI0 + LANES] + ib_ref[...]
    f_tile = _log_sigmoid(proj_ref[0, rows, GF0:GF0 + LANES] + fb_ref[...])
    bc = _dot01_l(tril, f_tile)
    u = i_tile - bc
    ut = u.T

    for j in range(PAIRS):
        sl = slice(j * LANES, (j + 1) * LANES)
        q_t = q_ref[rows, sl]
        k_t = k_ref[rows, sl]
        v_t = proj_ref[0, rows, V0 + j * LANES:V0 + (j + 1) * LANES]
        c_pair = C_ref[0, j]
        n_row = n_ref[0, j:j + 1, :]
        q_c = _dot(q_t, c_pair)
        q_n = q_t * n_row
        k_bf = k_t.astype(BF16)
        v_bf = v_t.astype(BF16)
        h_parts, ws_parts, dec_parts = [], [], []
        for e in range(2):
            h = 2 * j + e
            hm = lo if e == 0 else jnp.logical_not(lo)
            m_prev = m_ref[0, h:h + 1, 0:1]
            lane_h = lane == h
            u_row = ut[h:h + 1, :]
            u_col = _rowsum(jnp.where(lane_h, u, 0.0))
            bc_col = _rowsum(jnp.where(lane_h, bc, 0.0))
            m_loc = jnp.max(jnp.where(causal, u_row, NEG), axis=1, keepdims=True)
            m_col = jnp.maximum(m_loc, m_prev)
            d = jnp.exp(jnp.where(causal, u_row - m_col, NEG))
            s = lax.dot_general(jnp.where(hm, q_t, 0.0).astype(BF16), k_bf,
                                (((1,), (1,)), ((), ())), preferred_element_type=F32)
            p = s * d
            sc = jnp.exp(m_prev - m_col)
            num = jnp.dot(p.astype(BF16), v_bf, preferred_element_type=F32) + sc * q_c
            den = _rowsum(p) + sc * _rowsum(jnp.where(hm, q_n, 0.0))
            h_parts.append(num / jnp.maximum(jnp.abs(den), jnp.exp(-(bc_col + m_col))))
            m_last = m_col[LM - 1:LM, :]
            ws_parts.append(jnp.exp(u_col - m_last))
            dec_parts.append(jnp.exp(m_prev - m_last))
            m_ref[0, h:h + 1, :] = jnp.broadcast_to(bc_col[LM - 1:LM, :] + m_last, (1, LANES))
        h_til = jnp.where(lo, h_parts[0], h_parts[1])
        ws = jnp.where(lo, ws_parts[0], ws_parts[1])
        dec = jnp.where(lo, dec_parts[0], dec_parts[1])
        upd = jnp.dot(k_t.T.astype(BF16), (ws * v_t).astype(BF16), preferred_element_type=F32)
        C_ref[0, j] = c_pair * dec + jnp.where(blockdiag, upd, 0.0)
        n_ref[0, j:j + 1, :] = n_row * dec + jnp.sum(ws * k_t, axis=0, keepdims=True)

        o_t = proj_ref[0, rows, O0 + j * LANES:O0 + (j + 1) * LANES]
        hml = jax.nn.sigmoid(o_t) * h_til
        ms = _pair_rowsum(hml * hml, lo) * (1.0 / HEAD_DIM)
        mix_ref[0, rows, sl] = hml * lax.rsqrt(ms + EPS) * mlnw_ref[:, sl]


def _neumann_inverse(n):
    eye = jnp.where(_iota(n.shape, 0) == _iota(n.shape, 1), 1.0, 0.0)
    t = eye + n
    pw = n
    for _ in range(int(math.log2(LR)) - 1):
        pw_bf = pw.astype(BF16)
        pw = jnp.dot(pw_bf, pw_bf, preferred_element_type=F32)
        t = t + jnp.dot(t.astype(BF16), pw.astype(BF16), preferred_element_type=F32)
    return t


def _rwkv_chunk(c, rs_ref, w0_ref, wup_ref, a0_ref, aup_ref, gup_ref, kk_ref, ka_ref, rk_ref,
                lnw_ref, lnb_ref, mix_ref, S_ref):
    rows = pl.ds(pl.multiple_of(c * LR, LR), LR)
    lane = _iota((1, LANES), 1)
    lo = lane < HEAD_DIM
    row = _iota((LANES, LANES), 0)
    col = _iota((LANES, LANES), 1)
    same = (row < LR) == (col < LR)
    mask_sl = jnp.logical_and(same, col < row)
    mask_li = jnp.logical_and(same, col <= row)
    tril = jnp.where(_iota((LR, LR), 1) <= _iota((LR, LR), 0), 1.0, 0.0).astype(BF16)

    wlog, a, g = _rwkv_token_prep(rs_ref[rows, RS_LORA:RS_LORA + LANES], rs_ref[rows, RS_G:RS_G + D_GATE],
                                  w0_ref[...], wup_ref[...], a0_ref[...], aup_ref[...], gup_ref[...])
    lw = _dot01_l(tril, wlog)
    e_in = jnp.exp(lw)
    e_ex = jnp.exp(lw - wlog)
    e_ng = jnp.exp(-lw)

    for j in range(PAIRS):
        sl = slice(j * LANES, (j + 1) * LANES)
        r_t = rs_ref[rows, RS_R + j * LANES:RS_R + (j + 1) * LANES]
        kr_t = rs_ref[rows, RS_K + j * LANES:RS_K + (j + 1) * LANES]
        v_t = rs_ref[rows, RS_V + j * LANES:RS_V + (j + 1) * LANES]
        a_t = a[:, sl]
        kk = kr_t * kk_ref[:, sl]
        kkn = kk / jnp.maximum(jnp.sqrt(_pair_rowsum(kk * kk, lo)), 1e-12)
        keff = kr_t * (1.0 + (a_t - 1.0) * ka_ref[:, sl])

        alpha_b = -kkn * e_ex[:, sl]
        r_b = r_t * e_in[:, sl]
        beta_t = kkn * a_t * e_ng[:, sl]
        k_tl = keff * e_ng[:, sl]
        gamma = e_in[LR - 1:LR, sl]

        lhs = jnp.concatenate([_stack2(alpha_b, lo), _stack2(r_b, lo)], axis=0).astype(BF16)
        rhs = jnp.concatenate([beta_t, beta_t, k_tl, k_tl], axis=0).astype(BF16)
        aa = lax.dot_general(lhs, rhs, (((1,), (1,)), ((), ())), preferred_element_type=F32)
        a_ab = jnp.where(mask_sl, aa[0:LANES, 0:LANES], 0.0)
        a_ak = jnp.where(mask_sl, aa[0:LANES, LANES:2 * LANES], 0.0)
        a_rb = jnp.where(mask_li, aa[LANES:2 * LANES, 0:LANES], 0.0)
        a_rk = jnp.where(mask_li, aa[LANES:2 * LANES, LANES:2 * LANES], 0.0)
        t_inv = _neumann_inverse(a_ab)

        s_pair = S_ref[0, j]
        xs = lax.dot_general(lhs, s_pair.astype(BF16), (((1,), (1,)), ((), ())),
                             preferred_element_type=F32)
        v2 = _stack2(v_t, lo)
        w2 = xs[0:LANES] + _dot(a_ak, v2)
        p2 = _dot(t_inv, w2)
        pv = jnp.concatenate([p2, v2], axis=0)
        y2 = xs[LANES:2 * LANES] + _dot(jnp.concatenate([a_rb, a_rk], axis=1), pv)
        y = y2[0:LR] + y2[LR:2 * LR]
        bk = jnp.concatenate([_stack2(beta_t, lo), _stack2(k_tl, lo)], axis=0)
        S_ref[0, j] = (s_pair + _dot(pv.T, bk)) * gamma

        bonus = _pair_rowsum(r_t * keff * rk_ref[:, sl], lo) * v_t
        yb = y + bonus
        mu = _pair_rowsum(yb, lo) * (1.0 / HEAD_DIM)
        dlt = yb - mu
        var = _pair_rowsum(dlt * dlt, lo) * (1.0 / HEAD_DIM)
        yn = dlt * lax.rsqrt(var + GN_EPS) * lnw_ref[:, sl] + lnb_ref[:, sl]
        mix_ref[0, rows, GROUP_W + j * LANES:GROUP_W + (j + 1) * LANES] = yn * g[:, sl]


def _mixer_kernel(proj_ref, convw_ref, convb_ref, ib_ref, fb_ref, mlnw_ref, mu_ref, w0_ref, wup_ref,
                  a0_ref, aup_ref, gup_ref, kk_ref, ka_ref, rk_ref, lnw_ref, lnb_ref,
                  mix_ref, C_ref, n_ref, m_ref, S_ref,
                  ext_ref, rwext_ref, q_ref, k_ref, rs_ref):
    t = pl.program_id(1)

    @pl.when(t == 0)
    def _():
        C_ref[...] = jnp.zeros_like(C_ref)
        n_ref[...] = jnp.zeros_like(n_ref)
        m_ref[...] = jnp.zeros_like(m_ref)
        S_ref[...] = jnp.zeros_like(S_ref)
        ext_ref[0:SUBLANES, :] = jnp.zeros((SUBLANES, 2 * GROUP_W), F32)
        rwext_ref[0:SUBLANES, :] = jnp.zeros((SUBLANES, RW_IN_W), F32)

    @pl.when(t > 0)
    def _():
        ext_ref[0:SUBLANES, :] = ext_ref[TB:TB + SUBLANES, :]
        rwext_ref[0:SUBLANES, :] = rwext_ref[TB:TB + SUBLANES, :]

    ext_ref[SUBLANES:TB + SUBLANES, :] = proj_ref[0, :, Q0:Q0 + 2 * GROUP_W]
    acc = convb_ref[...] + ext_ref[SUBLANES:TB + SUBLANES, :] * convw_ref[CONV_W - 1:CONV_W, :]
    for jw in range(CONV_W - 1):
        off = SUBLANES - (CONV_W - 1) + jw
        acc = acc + ext_ref[off:off + TB, :] * convw_ref[jw:jw + 1, :]
    qk = acc * jax.nn.sigmoid(acc)
    q_ref[...] = qk[:, 0:GROUP_W]
    k_ref[...] = qk[:, GROUP_W:2 * GROUP_W] * (HEAD_DIM ** -0.5)

    rw = proj_ref[0, :, RW0:RW0 + RW_IN_W]
    rwext_ref[SUBLANES:TB + SUBLANES, :] = rw
    shifted = rwext_ref[SUBLANES - 1:TB + SUBLANES - 1, :]
    rs_ref[...] = rw + mu_ref[...] * (shifted - rw)

    def ml_body(c, carry):
        _mlstm_chunk(c, proj_ref, q_ref, k_ref, ib_ref, fb_ref, mlnw_ref, mix_ref, C_ref, n_ref, m_ref)
        return carry

    lax.fori_loop(0, TB // LM, ml_body, 0)

    def rw_body(c, carry):
        _rwkv_chunk(c, rs_ref, w0_ref, wup_ref, a0_ref, aup_ref, gup_ref, kk_ref, ka_ref, rk_ref,
                    lnw_ref, lnb_ref, mix_ref, S_ref)
        return carry

    lax.fori_loop(0, TB // LR, rw_body, 0)


def _mixer(proj, p):
    bsz, seq, _ = proj.shape
    row = lambda w: pl.BlockSpec((1, w), lambda b, t: (0, 0))
    mat = lambda r, w: pl.BlockSpec((r, w), lambda b, t: (0, 0))
    state4 = pl.BlockSpec((1, PAIRS, LANES, LANES), lambda b, t: (b, 0, 0, 0))
    state3 = pl.BlockSpec((1, SUBLANES, LANES), lambda b, t: (b, 0, 0))
    return pl.pallas_call(
        _mixer_kernel,
        out_shape=(
            jax.ShapeDtypeStruct((bsz, seq, 2 * GROUP_W), F32),
            jax.ShapeDtypeStruct((bsz, PAIRS, LANES, LANES), F32),
            jax.ShapeDtypeStruct((bsz, SUBLANES, LANES), F32),
            jax.ShapeDtypeStruct((bsz, SUBLANES, LANES), F32),
            jax.ShapeDtypeStruct((bsz, PAIRS, LANES, LANES), F32),
        ),
        grid=(bsz, seq // TB),
        in_specs=[
            pl.BlockSpec((1, TB, IN_WP), lambda b, t: (b, t, 0)),
            mat(CONV_W, 2 * GROUP_W), row(2 * GROUP_W), row(LANES), row(LANES), row(GROUP_W),
            row(RW_IN_W), row(GROUP_W), mat(LANES, GROUP_W), row(GROUP_W), mat(LANES, GROUP_W),
            mat(D_GATE, GROUP_W), row(GROUP_W), row(GROUP_W), row(GROUP_W), row(GROUP_W), row(GROUP_W),
        ],
        out_specs=(
            pl.BlockSpec((1, TB, 2 * GROUP_W), lambda b, t: (b, t, 0)),
            state4, state3, state3, state4,
        ),
        scratch_shapes=[
            pltpu.VMEM((TB + SUBLANES, 2 * GROUP_W), F32),
            pltpu.VMEM((TB + SUBLANES, RW_IN_W), F32),
            pltpu.VMEM((TB, GROUP_W), F32),
            pltpu.VMEM((TB, GROUP_W), F32),
            pltpu.VMEM((TB, RW_IN_W), F32),
        ],
        compiler_params=pltpu.CompilerParams(
            dimension_semantics=("arbitrary", "arbitrary"), vmem_limit_bytes=VMEM_LIMIT),
        name="mixer",
    )(proj, p["conv_w"], p["conv_b"], p["ib_row"], p["fb_row"], p["mlnorm_w"], p["mu"], p["w0"],
      p["wup_p"], p["a0"], p["aup_p"], p["gup"], p["k_k"], p["k_a"], p["r_k"], p["ln_w"], p["ln_b"])


def _sprep_kernel(proj_ref, conv0_ref, shift0_ref, convw_ref, convb_ref, mu_ref, w0_ref, wup_ref,
                  a0_ref, aup_ref, gup_ref, qk_ref, rs_ref, wag_ref):
    acc = convb_ref[...] + proj_ref[:, Q0:Q0 + 2 * GROUP_W] * convw_ref[CONV_W - 1:CONV_W, :]
    for jw in range(CONV_W - 1):
        acc = acc + conv0_ref[jw] * convw_ref[jw:jw + 1, :]
    qk = acc * jax.nn.sigmoid(acc)
    lane = _iota((1, 2 * GROUP_W), 1)
    qk_ref[...] = jnp.where(lane < GROUP_W, qk, qk * (HEAD_DIM ** -0.5))

    rw = proj_ref[:, RW0:RW0 + RW_IN_W]
    rs = rw + mu_ref[...] * (shift0_ref[...] - rw)
    rs_ref[...] = rs
    wlog, a, g = _rwkv_token_prep(rs[:, RS_LORA:RS_LORA + LANES], rs[:, RS_G:RS_G + D_GATE],
                                  w0_ref[...], wup_ref[...], a0_ref[...], aup_ref[...], gup_ref[...])
    wag_ref[:, 0:GROUP_W] = wlog
    wag_ref[:, GROUP_W:2 * GROUP_W] = a
    wag_ref[:, 2 * GROUP_W:3 * GROUP_W] = g


def _sprep(proj_s, conv0, shift0, p):
    bsz = proj_s.shape[0]
    return pl.pallas_call(
        _sprep_kernel,
        out_shape=(
            jax.ShapeDtypeStruct((bsz, 2 * GROUP_W), F32),
            jax.ShapeDtypeStruct((bsz, RW_IN_W), F32),
            jax.ShapeDtypeStruct((bsz, 3 * GROUP_W), F32),
        ),
        compiler_params=pltpu.CompilerParams(vmem_limit_bytes=VMEM_LIMIT),
        name="sprep",
    )(proj_s, conv0, shift0, p["conv_w"], p["conv_b"], p["mu"], p["w0"], p["wup_p"], p["a0"],
      p["aup_p"], p["gup"])


def _sstep_kernel(q_ref, k_ref, v_ref, o_ref, ipre_ref, fpre_ref, ib_ref, fb_ref, m0_ref, n0_ref, C0_ref,
                  r_ref, kr_ref, vr_ref, wlog_ref, a_ref, g_ref, S0_ref,
                  mlnw_ref, kk_ref, ka_ref, rk_ref, lnw_ref, lnb_ref, erep_ref, etile_ref,
                  C_ref, n_ref, m_ref, S_ref, hml_ref, yrw_ref):
    e_rep = erep_ref[...]
    e_tile = etile_ref[...]
    rep = lambda x: _dot01_r(x, e_rep)
    tile = lambda x: _dot01_r(x, e_tile)

    q, k, v = q_ref[...], k_ref[...], v_ref[...]
    i_g = ipre_ref[...] + ib_ref[...]
    logf = _log_sigmoid(fpre_ref[...] + fb_ref[...])
    m0 = m0_ref[...]
    m_new = jnp.maximum(logf + m0, i_g)
    ws = jnp.exp(i_g - m_new)
    dec = jnp.exp(logf + m0 - m_new)
    c_new = dec * C0_ref[...] + ws * (rep(k) * tile(v))
    n_new = dec * n0_ref[...] + ws * k
    C_ref[...] = c_new
    n_ref[...] = n_new
    m_ref[...] = m_new
    num = _red3(c_new * rep(q), e_tile)
    den = _rowsum(q * n_new)
    h_til = num / jnp.maximum(jnp.abs(den), jnp.exp(-m_new))
    hml = jax.nn.sigmoid(o_ref[...]) * h_til
    hml_ref[...] = hml * lax.rsqrt(jnp.mean(hml * hml, axis=1, keepdims=True) + EPS) * mlnw_ref[...]

    r, kr, vr, a = r_ref[...], kr_ref[...], vr_ref[...], a_ref[...]
    kk = kr * kk_ref[...]
    kkn = kk / jnp.maximum(jnp.sqrt(_rowsum(kk * kk)), 1e-12)
    keff = kr * (1.0 + (a - 1.0) * ka_ref[...])
    s0 = S0_ref[...]
    sa = _red3(s0 * tile(-kkn), e_rep)
    s_new = s0 * tile(jnp.exp(wlog_ref[...])) + rep(sa) * tile(kkn * a) + rep(vr) * tile(keff)
    S_ref[...] = s_new
    y = _red3(s_new * tile(r), e_rep)
    yb = y + _rowsum(r * keff * rk_ref[...]) * vr
    mu = jnp.mean(yb, axis=1, keepdims=True)
    dlt = yb - mu
    var = jnp.mean(dlt * dlt, axis=1, keepdims=True)
    yrw_ref[...] = (dlt * lax.rsqrt(var + GN_EPS) * lnw_ref[...] + lnb_ref[...]) * g_ref[...]


def _red3(y, m01):
    h1, h2, h3 = _split3(y)
    d = lambda h: lax.dot_general(h, m01, (((1,), (1,)), ((), ())), preferred_element_type=F32)
    return d(h1) + d(h2) + d(h3)


def _sstep(rows_in, p):
    n_rows = rows_in["q"].shape[0]
    flat = HEAD_DIM * HEAD_DIM
    blk = lambda w: pl.BlockSpec((RB, w), lambda i: (i, 0))
    cst = lambda r, w: pl.BlockSpec((r, w), lambda i: (0, 0))
    names = ["q", "k", "v", "o", "ipre", "fpre", "ib", "fb", "m0", "n0", "C0",
             "r", "kr", "vr", "wlog", "a", "g", "S0"]
    widths = [64, 64, 64, 64, 1, 1, None, None, 1, 64, flat, 64, 64, 64, 64, 64, 64, flat]
    in_specs, args = [], []
    for nm, w in zip(names, widths):
        if w is None:
            in_specs.append(cst(RB, 1))
        else:
            in_specs.append(blk(w))
        args.append(rows_in[nm])
    for nm in ["mlnorm_rows", "kk_rows", "ka_rows", "rk_rows", "lnw_rows", "lnb_rows"]:
        in_specs.append(cst(RB, HEAD_DIM))
        args.append(p[nm])
    in_specs += [cst(HEAD_DIM, flat), cst(HEAD_DIM, flat)]
    args += [p["e_rep"], p["e_tile"]]
    return pl.pallas_call(
        _sstep_kernel,
        out_shape=(
            jax.ShapeDtypeStruct((n_rows, flat), F32),
            jax.ShapeDtypeStruct((n_rows, HEAD_DIM), F32),
            jax.ShapeDtypeStruct((n_rows, 1), F32),
            jax.ShapeDtypeStruct((n_rows, flat), F32),
            jax.ShapeDtypeStruct((n_rows, HEAD_DIM), F32),
            jax.ShapeDtypeStruct((n_rows, HEAD_DIM), F32),
        ),
        grid=(n_rows // RB,),
        in_specs=in_specs,
        out_specs=(blk(flat), blk(HEAD_DIM), blk(1), blk(flat), blk(HEAD_DIM), blk(HEAD_DIM)),
        compiler_params=pltpu.CompilerParams(
            dimension_semantics=("arbitrary",), vmem_limit_bytes=VMEM_LIMIT),
        name="sstep",
    )(*args)


def _layer_params(l, norm_mix_w, w_in, mlstm_conv_w, mlstm_conv_b, mlstm_i_b, mlstm_f_b, mlstm_norm_w,
                  rw_mu, rw_w0, rw_w_up, rw_a0, rw_a_up, rw_g_up, rw_k_k, rw_k_a, rw_r_k, rw_ln_w, rw_ln_b,
                  w_out, norm_mlp_w, mlp_up, mlp_down):
    ml_in = 4 * GROUP_W
    wi = w_in[l]
    zpad = jnp.zeros((D_MODEL, LANES - N_HEADS), F32)
    w_in_p = jnp.concatenate(
        [wi[:, :ml_in], wi[:, ml_in:ml_in + N_HEADS], zpad,
         wi[:, ml_in + N_HEADS:ml_in + 2 * N_HEADS], zpad, wi[:, ml_in + 2 * N_HEADS:]], axis=1)
    gate_row = lambda b: jnp.concatenate([b, jnp.zeros((LANES - N_HEADS,), F32)])[None, :]
    zl = jnp.zeros((D_LORA, GROUP_W), F32)
    rows = lambda v: jnp.tile(v.reshape(N_HEADS, HEAD_DIM), (RB // N_HEADS, 1))
    flat_idx = jnp.arange(HEAD_DIM * HEAD_DIM)
    one_hot = lambda idx: (jnp.arange(HEAD_DIM)[:, None] == idx[None, :]).astype(BF16)
    return {
        "norm_mix_w": norm_mix_w[l][None, :],
        "w_in_p": w_in_p.astype(BF16),
        "conv_w": mlstm_conv_w[l], "conv_b": mlstm_conv_b[l][None, :],
        "ib_row": gate_row(mlstm_i_b[l]), "fb_row": gate_row(mlstm_f_b[l]),
        "ib_rows": jnp.tile(mlstm_i_b[l], RB // N_HEADS)[:, None],
        "fb_rows": jnp.tile(mlstm_f_b[l], RB // N_HEADS)[:, None],
        "mlnorm_w": mlstm_norm_w[l][None, :], "mlnorm_rows": rows(mlstm_norm_w[l]),
        "mu": rw_mu[l][None, :], "w0": rw_w0[l][None, :], "a0": rw_a0[l][None, :],
        "wup_p": jnp.concatenate([rw_w_up[l], zl], axis=0).astype(BF16),
        "aup_p": jnp.concatenate([zl, rw_a_up[l]], axis=0).astype(BF16),
        "gup": rw_g_up[l].astype(BF16),
        "k_k": rw_k_k[l][None, :], "k_a": rw_k_a[l][None, :], "r_k": rw_r_k[l].reshape(1, GROUP_W),
        "ln_w": rw_ln_w[l][None, :], "ln_b": rw_ln_b[l][None, :],
        "kk_rows": rows(rw_k_k[l]), "ka_rows": rows(rw_k_a[l]), "rk_rows": rows(rw_r_k[l]),
        "lnw_rows": rows(rw_ln_w[l]), "lnb_rows": rows(rw_ln_b[l]),
        "e_rep": one_hot(flat_idx // HEAD_DIM), "e_tile": one_hot(flat_idx % HEAD_DIM),
        "w_out": w_out[l].astype(BF16), "norm_mlp_w": norm_mlp_w[l][None, :],
        "mlp_up": mlp_up[l].astype(BF16), "mlp_down": mlp_down[l].astype(BF16),
    }


def _unpair(st):
    d = HEAD_DIM
    return jnp.stack([st[:, h // 2, (h % 2) * d:(h % 2 + 1) * d, (h % 2) * d:(h % 2 + 1) * d]
                      for h in range(N_HEADS)], axis=1)


def _prompt_layer(x, p, norm_f_w):
    bsz, seq, _ = x.shape
    x2d = x.reshape(bsz * seq, D_MODEL)
    proj = _inproj(x2d, p["norm_mix_w"], p["w_in_p"]).reshape(bsz, seq, IN_WP)
    mix, c_st, n_st, m_st, s_st = _mixer(proj, p)
    y = _outmlp(x2d, mix.reshape(bsz * seq, D_MODEL), p["w_out"], p["norm_mlp_w"], p["mlp_up"],
                p["mlp_down"], norm_f_w)
    c_new = _unpair(c_st)
    n_new = n_st[:, :PAIRS, :].reshape(bsz, N_HEADS, HEAD_DIM)
    m_new = m_st[:, :, 0]
    conv_new = proj[:, seq - (CONV_W - 1):, Q0:Q0 + 2 * GROUP_W]
    s_new = _unpair(s_st)
    shift_new = proj[:, seq - 1:, RW0:RW0 + RW_IN_W]
    return y.reshape(bsz, seq, D_MODEL), (c_new, n_new, m_new, conv_new, s_new, shift_new)


def _sample_layer(x, c0, n0, m0, conv0, s0, shift0, p, norm_f_w):
    bsz = x.shape[0]
    n_rows = bsz * N_HEADS
    flat = HEAD_DIM * HEAD_DIM
    x2d = x.reshape(bsz, D_MODEL)
    proj = _inproj(x2d, p["norm_mix_w"], p["w_in_p"])
    qk, rs, wag = _sprep(proj, jnp.moveaxis(conv0, 1, 0), shift0[:, 0, :], p)
    hr = lambda a2d: a2d.reshape(n_rows, HEAD_DIM)
    rows_in = {
        "q": hr(qk[:, :GROUP_W]), "k": hr(qk[:, GROUP_W:]), "v": hr(proj[:, V0:V0 + GROUP_W]),
        "o": hr(proj[:, O0:O0 + GROUP_W]),
        "ipre": proj[:, GI0:GI0 + N_HEADS].reshape(n_rows, 1),
        "fpre": proj[:, GF0:GF0 + N_HEADS].reshape(n_rows, 1),
        "ib": p["ib_rows"], "fb": p["fb_rows"],
        "m0": m0.reshape(n_rows, 1), "n0": n0.reshape(n_rows, HEAD_DIM), "C0": c0.reshape(n_rows, flat),
        "r": hr(rs[:, RS_R:RS_R + GROUP_W]), "kr": hr(rs[:, RS_K:RS_K + GROUP_W]),
        "vr": hr(rs[:, RS_V:RS_V + GROUP_W]),
        "wlog": hr(wag[:, :GROUP_W]), "a": hr(wag[:, GROUP_W:2 * GROUP_W]), "g": hr(wag[:, 2 * GROUP_W:]),
        "S0": s0.reshape(n_rows, flat),
    }
    c_new, n_new, m_new, s_new, hml, yrw = _sstep(rows_in, p)
    mix = jnp.concatenate([hml.reshape(bsz, GROUP_W), yrw.reshape(bsz, GROUP_W)], axis=1)
    y = _outmlp(x2d, mix, p["w_out"], p["norm_mlp_w"], p["mlp_up"], p["mlp_down"], norm_f_w)
    conv_new = jnp.concatenate([conv0[:, 1:], proj[:, None, Q0:Q0 + 2 * GROUP_W]], axis=1)
    shift_new = proj[:, None, RW0:RW0 + RW_IN_W]
    return y.reshape(bsz, 1, D_MODEL), (
        c_new.reshape(bsz, N_HEADS, HEAD_DIM, HEAD_DIM), n_new.reshape(bsz, N_HEADS, HEAD_DIM),
        m_new.reshape(bsz, N_HEADS), conv_new, s_new.reshape(bsz, N_HEADS, HEAD_DIM, HEAD_DIM), shift_new)


def kernel(x_prompt, x_sample, state_mlstm_C, state_mlstm_n, state_mlstm_m, state_mlstm_conv, state_rwkv_S, state_rwkv_shift, norm_mix_w, w_in, mlstm_conv_w, mlstm_conv_b, mlstm_i_b, mlstm_f_b, mlstm_norm_w, rw_mu, rw_w0, rw_w_up, rw_a0, rw_a_up, rw_g_up, rw_k_k, rw_k_a, rw_r_k, rw_ln_w, rw_ln_b, w_out, norm_mlp_w, mlp_up, mlp_down, norm_f_w):
    depth = w_in.shape[0]
    assert depth == 1, "the final RMSNorm is fused into the layer's output kernel"
    weights = (norm_mix_w, w_in, mlstm_conv_w, mlstm_conv_b, mlstm_i_b, mlstm_f_b, mlstm_norm_w,
               rw_mu, rw_w0, rw_w_up, rw_a0, rw_a_up, rw_g_up, rw_k_k, rw_k_a, rw_r_k, rw_ln_w, rw_ln_b,
               w_out, norm_mlp_w, mlp_up, mlp_down)
    nfw = norm_f_w[None, :]
    p = _layer_params(0, *weights)
    y_p, st_p = _prompt_layer(x_prompt, p, nfw)
    y_s, st_s = _sample_layer(x_sample, state_mlstm_C[0], state_mlstm_n[0], state_mlstm_m[0],
                              state_mlstm_conv[0], state_rwkv_S[0], state_rwkv_shift[0], p, nfw)
    lead = lambda t: tuple(a[None] for a in t)
    return (y_p, y_s) + lead(st_p) + lead(st_s)
```

```python
import functools
import math

import jax
import jax.numpy as jnp
from jax import lax
from jax.experimental import pallas as pl
from jax.experimental.pallas import tpu as pltpu

F32 = jnp.float32
BF16 = jnp.bfloat16

D_MODEL = 1024
HEAD_DIM = 64
N_HEADS = 8
PAIRS = N_HEADS // 2
GROUP_W = N_HEADS * HEAD_DIM
CONV_W = 4
D_LORA = 64
D_GATE = 128
D_FF = 4 * D_MODEL
EPS = 1e-6
GN_EPS = 64e-5
LANES = 128
SUBLANES = 8

Q0, K0, V0, O0 = 0, GROUP_W, 2 * GROUP_W, 3 * GROUP_W
GI0 = 4 * GROUP_W
GF0 = GI0 + LANES
RW0 = GF0 + LANES
RW_IN_W = 3 * GROUP_W + 2 * D_LORA + D_GATE
IN_WP = RW0 + RW_IN_W
RS_R, RS_K, RS_V = 0, GROUP_W, 2 * GROUP_W
RS_LORA = 3 * GROUP_W
RS_G = RS_LORA + 2 * D_LORA

TB = 256
LM = 128
LR = 64
TM_IN = 256
TM_OUT = 256
FF_CHUNK = 1024
RB = 128
NEG = -1e30
W_SCALE = math.exp(-0.5)
VMEM_LIMIT = 56 * 1024 * 1024


def _dot(a, b):
    return jnp.dot(a.astype(BF16), b.astype(BF16), preferred_element_type=F32)


def _dot_nt(a, b):
    return lax.dot_general(a.astype(BF16), b.astype(BF16), (((1,), (1,)), ((), ())),
                           preferred_element_type=F32)


def _split3(x):
    h1 = x.astype(BF16)
    r1 = x - h1.astype(F32)
    h2 = r1.astype(BF16)
    h3 = (r1 - h2.astype(F32)).astype(BF16)
    return h1, h2, h3


def _dot01_l(m01, x):
    h1, h2, h3 = _split3(x)
    d = lambda h: jnp.dot(m01, h, preferred_element_type=F32)
    return d(h1) + d(h2) + d(h3)


def _dot01_r(x, m01):
    h1, h2, h3 = _split3(x)
    d = lambda h: jnp.dot(h, m01, preferred_element_type=F32)
    return d(h1) + d(h2) + d(h3)


def _log_sigmoid(x):
    return jnp.minimum(x, 0.0) - jnp.log(1.0 + jnp.exp(-jnp.abs(x)))


def _iota(shape, dim):
    return lax.broadcasted_iota(jnp.int32, shape, dim)


def _rowsum(x):
    return jnp.sum(x, axis=1, keepdims=True)


def _pair_rowsum(x, lo):
    s_lo = _rowsum(jnp.where(lo, x, 0.0))
    s_hi = _rowsum(jnp.where(lo, 0.0, x))
    return jnp.where(lo, s_lo, s_hi)


def _stack2(x, lo):
    return jnp.concatenate([jnp.where(lo, x, 0.0), jnp.where(lo, 0.0, x)], axis=0)


def _rms(x, w):
    ms = jnp.mean(x * x, axis=-1, keepdims=True)
    return x * lax.rsqrt(ms + EPS) * w


def _inproj_kernel(x_ref, nw_ref, w_ref, o_ref):
    xn = _rms(x_ref[...], nw_ref[...])
    o_ref[...] = jnp.dot(xn.astype(BF16), w_ref[...], preferred_element_type=F32)


def _inproj(x2d, norm_w, w_in_p):
    rows = x2d.shape[0]
    tm = min(TM_IN, rows)
    return pl.pallas_call(
        _inproj_kernel,
        out_shape=jax.ShapeDtypeStruct((rows, IN_WP), F32),
        grid=(rows // tm,),
        in_specs=[
            pl.BlockSpec((tm, D_MODEL), lambda i: (i, 0)),
            pl.BlockSpec((1, D_MODEL), lambda i: (0, 0)),
            pl.BlockSpec((D_MODEL, IN_WP), lambda i: (0, 0)),
        ],
        out_specs=pl.BlockSpec((tm, IN_WP), lambda i: (i, 0)),
        compiler_params=pltpu.CompilerParams(
            dimension_semantics=("arbitrary",), vmem_limit_bytes=VMEM_LIMIT),
        name="inproj",
    )(x2d, norm_w, w_in_p)


def _outmlp_kernel(x_ref, mix_ref, wout_ref, nmw_ref, up_ref, down_ref, nfw_ref, y_ref):
    x1 = x_ref[...] + jnp.dot(mix_ref[...].astype(BF16), wout_ref[...], preferred_element_type=F32)
    xn = _rms(x1, nmw_ref[...]).astype(BF16)
    acc = x1
    for c in range(D_FF // FF_CHUNK):
        sl = slice(c * FF_CHUNK, (c + 1) * FF_CHUNK)
        hid = jnp.maximum(jnp.dot(xn, up_ref[:, sl], preferred_element_type=F32), 0.0)
        acc = acc + jnp.dot((hid * hid).astype(BF16), down_ref[sl, :], preferred_element_type=F32)
    y_ref[...] = _rms(acc, nfw_ref[...])


def _outmlp(x2d, mix2d, w_out, norm_mlp_w, mlp_up, mlp_down, norm_f_w):
    rows = x2d.shape[0]
    tm = min(TM_OUT, rows)
    const = lambda shape: pl.BlockSpec(shape, lambda i: (0, 0), pipeline_mode=pl.Buffered(1))
    return pl.pallas_call(
        _outmlp_kernel,
        out_shape=jax.ShapeDtypeStruct((rows, D_MODEL), F32),
        grid=(rows // tm,),
        in_specs=[
            pl.BlockSpec((tm, D_MODEL), lambda i: (i, 0)),
            pl.BlockSpec((tm, D_MODEL), lambda i: (i, 0)),
            const((D_MODEL, D_MODEL)),
            const((1, D_MODEL)),
            const((D_MODEL, D_FF)),
            const((D_FF, D_MODEL)),
            const((1, D_MODEL)),
        ],
        out_specs=pl.BlockSpec((tm, D_MODEL), lambda i: (i, 0)),
        compiler_params=pltpu.CompilerParams(
            dimension_semantics=("arbitrary",), vmem_limit_bytes=VMEM_LIMIT),
        name="outmlp",
    )(x2d, mix2d, w_out, norm_mlp_w, mlp_up, mlp_down, norm_f_w)


def _rwkv_token_prep(lora_t, xg, w0, wup, a0, aup, gup):
    w_raw = w0 + jnp.dot(jnp.tanh(lora_t).astype(BF16), wup, preferred_element_type=F32)
    wlog = -W_SCALE * jax.nn.sigmoid(w_raw)
    a = jax.nn.sigmoid(a0 + jnp.dot(lora_t.astype(BF16), aup, preferred_element_type=F32))
    g = jnp.dot(jax.nn.sigmoid(xg).astype(BF16), gup, preferred_element_type=F32)
    return wlog, a, g


def _mlstm_chunk(c, proj_ref, q_ref, k_ref, ib_ref, fb_ref, mlnw_ref, mix_ref, C_ref, n_ref, m_ref):
    rows = pl.ds(pl.multiple_of(c * LM, LM), LM)
    row = _iota((LM, LM), 0)
    col = _iota((LM, LM), 1)
    causal = col <= row
    tril = jnp.where(causal, 1.0, 0.0).astype(BF16)
    blockdiag = (row < HEAD_DIM) == (col < HEAD_DIM)
    lane = _iota((1, LANES), 1)
    lo = lane < HEAD_DIM

    i_tile = proj_ref[0, rows, GI0:GI0 + LANES] + ib_ref[...]
    f_tile = _log_sigmoid(proj_ref[0, rows, GF0:GF0 + LANES] + fb_ref[...])
    bc = _dot01_l(tril, f_tile)
    u = i_tile - bc
    ut = u.T

    nt = (((1,), (1,)), ((), ()))
    pairs = range(PAIRS)
    heads = range(N_HEADS)
    tile = lambda base, j: slice(base + j * LANES, base + (j + 1) * LANES)
    hmask = lambda h: lo if h % 2 == 0 else jnp.logical_not(lo)
    q_t = [q_ref[rows, tile(0, j)] for j in pairs]
    k_t = [k_ref[rows, tile(0, j)] for j in pairs]
    v_t = [proj_ref[0, rows, tile(V0, j)] for j in pairs]
    c_pair = [C_ref[0, j] for j in pairs]
    n_row = [n_ref[0, j:j + 1, :] for j in pairs]
    k_bf = [k.astype(BF16) for k in k_t]
    v_bf = [v.astype(BF16) for v in v_t]
    s = [lax.dot_general(jnp.where(hmask(h), q_t[h // 2], 0.0).astype(BF16), k_bf[h // 2], nt,
                         preferred_element_type=F32) for h in heads]
    q_c = [_dot(q_t[j], c_pair[j]) for j in pairs]
    k_tr = [k.T for k in k_t]

    m_prev_t = m_ref[0]
    ut8 = ut[0:N_HEADS, :]
    m_last_t = jnp.maximum(jnp.max(ut8, axis=1, keepdims=True), m_prev_t)
    ws_rows = jnp.exp(ut8 - m_last_t)
    dec_t = jnp.exp(m_prev_t - m_last_t)
    m_ref[0] = bc.T[0:N_HEADS, LM - 1:LM] + m_last_t

    m_prev = [m_prev_t[h:h + 1, 0:1] for h in heads]
    u_row = [ut[h:h + 1, :] for h in heads]
    bc_col = [_rowsum(jnp.where(lane == h, bc, 0.0)) for h in heads]
    m_col = [jnp.maximum(jnp.max(jnp.where(causal, u_row[h], NEG), axis=1, keepdims=True), m_prev[h])
             for h in heads]
    p = [s[h] * jnp.exp(jnp.where(causal, u_row[h] - m_col[h], NEG)) for h in heads]
    sc = [jnp.exp(m_prev[h] - m_col[h]) for h in heads]
    pv = [jnp.dot(p[h].astype(BF16), v_bf[h // 2], preferred_element_type=F32) for h in heads]
    h_parts = []
    for h in heads:
        q_n = _rowsum(jnp.where(hmask(h), q_t[h // 2] * n_row[h // 2], 0.0))
        den = _rowsum(p[h]) + sc[h] * q_n
        num = pv[h] + sc[h] * q_c[h // 2]
        h_parts.append(num / jnp.maximum(jnp.abs(den), jnp.exp(-(bc_col[h] + m_col[h]))))

    top = _iota((LM, 1), 0) < HEAD_DIM
    kw = [k_tr[j] * jnp.where(top, ws_rows[2 * j:2 * j + 1, :], ws_rows[2 * j + 1:2 * j + 2, :])
          for j in pairs]
    upd = [jnp.dot(kw[j].astype(BF16), v_bf[j], preferred_element_type=F32) for j in pairs]
    n_upd = [jnp.dot(ws_rows.astype(BF16), k_bf[j], preferred_element_type=F32) for j in pairs]
    for j in pairs:
        dec_j = jnp.where(lo, dec_t[2 * j:2 * j + 1, :], dec_t[2 * j + 1:2 * j + 2, :])
        C_ref[0, j] = c_pair[j] * dec_j + jnp.where(blockdiag, upd[j], 0.0)
        n_ref[0, j:j + 1, :] = n_row[j] * dec_j + jnp.where(lo, n_upd[j][2 * j:2 * j + 1, :],
                                                            n_upd[j][2 * j + 1:2 * j + 2, :])

    for j in pairs:
        h_til = jnp.where(lo, h_parts[2 * j], h_parts[2 * j + 1])
        hml = jax.nn.sigmoid(proj_ref[0, rows, tile(O0, j)]) * h_til
        ms = _pair_rowsum(hml * hml, lo) * (1.0 / HEAD_DIM)
        mix_ref[0, rows, tile(0, j)] = hml * lax.rsqrt(ms + EPS) * mlnw_ref[:, tile(0, j)]


def _neumann_inverse(ns):
    shape = ns[0].shape
    eye = jnp.where(_iota(shape, 0) == _iota(shape, 1), 1.0, 0.0)
    n_sq = int(math.log2(LR)) - 1
    qs = [n.astype(BF16) for n in ns]
    ts = [eye + n for n in ns]
    sq = [jnp.dot(q, q, preferred_element_type=F32) for q in qs]
    for _ in range(n_sq - 1):
        qs = [s.astype(BF16) for s in sq]
        both = [jnp.dot(q, jnp.concatenate([q, t.astype(BF16)], axis=1), preferred_element_type=F32)
                for q, t in zip(qs, ts)]
        sq = [b[:, 0:LANES] for b in both]
        ts = [t + b[:, LANES:2 * LANES] for t, b in zip(ts, both)]
    return [t + jnp.dot(s.astype(BF16), t.astype(BF16), preferred_element_type=F32) for s, t in zip(sq, ts)]


def _rwkv_chunk(c, rs_ref, w0_ref, wup_ref, a0_ref, aup_ref, gup_ref, kk_ref, ka_ref, rk_ref,
                lnw_ref, lnb_ref, mix_ref, S_ref):
    rows = pl.ds(pl.multiple_of(c * LR, LR), LR)
    lane = _iota((1, LANES), 1)
    lo = lane < HEAD_DIM
    row = _iota((LANES, LANES), 0)
    col = _iota((LANES, LANES), 1)
    same = (row < LR) == (col < LR)
    mask_sl = jnp.logical_and(same, col < row)
    mask_li = jnp.logical_and(same, col <= row)
    tril = jnp.where(_iota((LR, LR), 1) <= _iota((LR, LR), 0), 1.0, 0.0).astype(BF16)

    wlog, a, g = _rwkv_token_prep(rs_ref[rows, RS_LORA:RS_LORA + LANES], rs_ref[rows, RS_G:RS_G + D_GATE],
                                  w0_ref[...], wup_ref[...], a0_ref[...], aup_ref[...], gup_ref[...])
    lw = _dot01_l(tril, wlog)
    e_in = jnp.exp(lw)
    e_ex = jnp.exp(lw - wlog)
    e_ng = jnp.exp(-lw)

    nt = (((1,), (1,)), ((), ()))
    pairs = range(PAIRS)
    tile = lambda base, j: slice(base + j * LANES, base + (j + 1) * LANES)
    r_t = [rs_ref[rows, tile(RS_R, j)] for j in pairs]
    kr_t = [rs_ref[rows, tile(RS_K, j)] for j in pairs]
    v_t = [rs_ref[rows, tile(RS_V, j)] for j in pairs]
    a_t = [a[:, tile(0, j)] for j in pairs]
    kk = [kr_t[j] * kk_ref[:, tile(0, j)] for j in pairs]
    kkn = [kk[j] / jnp.maximum(jnp.sqrt(_pair_rowsum(kk[j] * kk[j], lo)), 1e-12) for j in pairs]
    keff = [kr_t[j] * (1.0 + (a_t[j] - 1.0) * ka_ref[:, tile(0, j)]) for j in pairs]
    alpha_b = [-kkn[j] * e_ex[:, tile(0, j)] for j in pairs]
    r_b = [r_t[j] * e_in[:, tile(0, j)] for j in pairs]
    beta_t = [kkn[j] * a_t[j] * e_ng[:, tile(0, j)] for j in pairs]
    k_tl = [keff[j] * e_ng[:, tile(0, j)] for j in pairs]
    gamma = [e_in[LR - 1:LR, tile(0, j)] for j in pairs]

    lhs = [jnp.concatenate([_stack2(alpha_b[j], lo), _stack2(r_b[j], lo)], axis=0).astype(BF16)
           for j in pairs]
    rhs = [jnp.concatenate([beta_t[j], beta_t[j], k_tl[j], k_tl[j]], axis=0).astype(BF16) for j in pairs]
    aa = [lax.dot_general(lhs[j], rhs[j], nt, preferred_element_type=F32) for j in pairs]
    s_pair = [S_ref[0, j] for j in pairs]
    xs = [lax.dot_general(lhs[j], s_pair[j].astype(BF16), nt, preferred_element_type=F32) for j in pairs]
    a_ab = [jnp.where(mask_sl, aa[j][0:LANES, 0:LANES], 0.0) for j in pairs]
    t_inv = _neumann_inverse(a_ab)

    v2 = [_stack2(v_t[j], lo) for j in pairs]
    a_ak = [jnp.where(mask_sl, aa[j][0:LANES, LANES:2 * LANES], 0.0) for j in pairs]
    w2 = [xs[j][0:LANES] + _dot(a_ak[j], v2[j]) for j in pairs]
    p2 = [_dot(t_inv[j], w2[j]) for j in pairs]
    pv = [jnp.concatenate([p2[j], v2[j]], axis=0) for j in pairs]
    mask_li2 = jnp.concatenate([mask_li, mask_li], axis=1)
    a_r = [jnp.where(mask_li2, aa[j][LANES:2 * LANES, :], 0.0) for j in pairs]
    y2 = [xs[j][LANES:2 * LANES] + _dot(a_r[j], pv[j]) for j in pairs]
    bk = [jnp.concatenate([_stack2(beta_t[j], lo), _stack2(k_tl[j], lo)], axis=0) for j in pairs]
    upd = [_dot(pv[j].T, bk[j]) for j in pairs]
    for j in pairs:
        S_ref[0, j] = (s_pair[j] + upd[j]) * gamma[j]

    for j in pairs:
        y = y2[j][0:LR] + y2[j][LR:2 * LR]
        yb = y + _pair_rowsum(r_t[j] * keff[j] * rk_ref[:, tile(0, j)], lo) * v_t[j]
        mu = _pair_rowsum(yb, lo) * (1.0 / HEAD_DIM)
        dlt = yb - mu
        var = _pair_rowsum(dlt * dlt, lo) * (1.0 / HEAD_DIM)
        yn = dlt * lax.rsqrt(var + GN_EPS) * lnw_ref[:, tile(0, j)] + lnb_ref[:, tile(0, j)]
        mix_ref[0, rows, tile(GROUP_W, j)] = yn * g[:, tile(0, j)]


def _mixer_kernel(proj_ref, convw_ref, convb_ref, ib_ref, fb_ref, mlnw_ref, mu_ref, w0_ref, wup_ref,
                  a0_ref, aup_ref, gup_ref, kk_ref, ka_ref, rk_ref, lnw_ref, lnb_ref,
                  mix_ref, C_ref, n_ref, m_ref, S_ref,
                  ext_ref, rwext_ref, q_ref, k_ref, rs_ref):
    t = pl.program_id(1)

    @pl.when(t == 0)
    def _():
        C_ref[...] = jnp.zeros_like(C_ref)
        n_ref[...] = jnp.zeros_like(n_ref)
        m_ref[...] = jnp.zeros_like(m_ref)
        S_ref[...] = jnp.zeros_like(S_ref)
        ext_ref[0:SUBLANES, :] = jnp.zeros((SUBLANES, 2 * GROUP_W), F32)
        rwext_ref[0:SUBLANES, :] = jnp.zeros((SUBLANES, RW_IN_W), F32)

    @pl.when(t > 0)
    def _():
        ext_ref[0:SUBLANES, :] = ext_ref[TB:TB + SUBLANES, :]
        rwext_ref[0:SUBLANES, :] = rwext_ref[TB:TB + SUBLANES, :]

    ext_ref[SUBLANES:TB + SUBLANES, :] = proj_ref[0, :, Q0:Q0 + 2 * GROUP_W]
    acc = convb_ref[...] + ext_ref[SUBLANES:TB + SUBLANES, :] * convw_ref[CONV_W - 1:CONV_W, :]
    for jw in range(CONV_W - 1):
        off = SUBLANES - (CONV_W - 1) + jw
        acc = acc + ext_ref[off:off + TB, :] * convw_ref[jw:jw + 1, :]
    qk = acc * jax.nn.sigmoid(acc)
    q_ref[...] = qk[:, 0:GROUP_W]
    k_ref[...] = qk[:, GROUP_W:2 * GROUP_W] * (HEAD_DIM ** -0.5)

    rw = proj_ref[0, :, RW0:RW0 + RW_IN_W]
    rwext_ref[SUBLANES:TB + SUBLANES, :] = rw
    shifted = rwext_ref[SUBLANES - 1:TB + SUBLANES - 1, :]
    rs_ref[...] = rw + mu_ref[...] * (shifted - rw)

    def ml_body(c, carry):
        _mlstm_chunk(c, proj_ref, q_ref, k_ref, ib_ref, fb_ref, mlnw_ref, mix_ref, C_ref, n_ref, m_ref)
        return carry

    lax.fori_loop(0, TB // LM, ml_body, 0)

    def rw_body(c, carry):
        _rwkv_chunk(c, rs_ref, w0_ref, wup_ref, a0_ref, aup_ref, gup_ref, kk_ref, ka_ref, rk_ref,
                    lnw_ref, lnb_ref, mix_ref, S_ref)
        return carry

    lax.fori_loop(0, TB // LR, rw_body, 0)


def _mixer(proj, p):
    bsz, seq, _ = proj.shape
    row = lambda w: pl.BlockSpec((1, w), lambda b, t: (0, 0))
    mat = lambda r, w: pl.BlockSpec((r, w), lambda b, t: (0, 0))
    state4 = pl.BlockSpec((1, PAIRS, LANES, LANES), lambda b, t: (b, 0, 0, 0))
    state3 = pl.BlockSpec((1, SUBLANES, LANES), lambda b, t: (b, 0, 0))
    return pl.pallas_call(
        _mixer_kernel,
        out_shape=(
            jax.ShapeDtypeStruct((bsz, seq, 2 * GROUP_W), F32),
            jax.ShapeDtypeStruct((bsz, PAIRS, LANES, LANES), F32),
            jax.ShapeDtypeStruct((bsz, SUBLANES, LANES), F32),
            jax.ShapeDtypeStruct((bsz, SUBLANES, LANES), F32),
            jax.ShapeDtypeStruct((bsz, PAIRS, LANES, LANES), F32),
        ),
        grid=(bsz, seq // TB),
        in_specs=[
            pl.BlockSpec((1, TB, IN_WP), lambda b, t: (b, t, 0)),
            mat(CONV_W, 2 * GROUP_W), row(2 * GROUP_W), row(LANES), row(LANES), row(GROUP_W),
            row(RW_IN_W), row(GROUP_W), mat(LANES, GROUP_W), row(GROUP_W), mat(LANES, GROUP_W),
            mat(D_GATE, GROUP_W), row(GROUP_W), row(GROUP_W), row(GROUP_W), row(GROUP_W), row(GROUP_W),
        ],
        out_specs=(
            pl.BlockSpec((1, TB, 2 * GROUP_W), lambda b, t: (b, t, 0)),
            state4, state3, state3, state4,
        ),
        scratch_shapes=[
            pltpu.VMEM((TB + SUBLANES, 2 * GROUP_W), F32),
            pltpu.VMEM((TB + SUBLANES, RW_IN_W), F32),
            pltpu.VMEM((TB, GROUP_W), F32),
            pltpu.VMEM((TB, GROUP_W), F32),
            pltpu.VMEM((TB, RW_IN_W), F32),
        ],
        compiler_params=pltpu.CompilerParams(
            dimension_semantics=("arbitrary", "arbitrary"), vmem_limit_bytes=VMEM_LIMIT),
        name="mixer",
    )(proj, p["conv_w"], p["conv_b"], p["ib_row"], p["fb_row"], p["mlnorm_w"], p["mu"], p["w0"],
      p["wup_p"], p["a0"], p["aup_p"], p["gup"], p["k_k"], p["k_a"], p["r_k"], p["ln_w"], p["ln_b"])


def _sprep_kernel(proj_ref, conv0_ref, shift0_ref, convw_ref, convb_ref, mu_ref, w0_ref, wup_ref,
                  a0_ref, aup_ref, gup_ref, qk_ref, rs_ref, wag_ref):
    acc = convb_ref[...] + proj_ref[:, Q0:Q0 + 2 * GROUP_W] * convw_ref[CONV_W - 1:CONV_W, :]
    for jw in range(CONV_W - 1):
        acc = acc + conv0_ref[jw] * convw_ref[jw:jw + 1, :]
    qk = acc * jax.nn.sigmoid(acc)
    lane = _iota((1, 2 * GROUP_W), 1)
    qk_ref[...] = jnp.where(lane < GROUP_W, qk, qk * (HEAD_DIM ** -0.5))

    rw = proj_ref[:, RW0:RW0 + RW_IN_W]
    rs = rw + mu_ref[...] * (shift0_ref[...] - rw)
    rs_ref[...] = rs
    wlog, a, g = _rwkv_token_prep(rs[:, RS_LORA:RS_LORA + LANES], rs[:, RS_G:RS_G + D_GATE],
                                  w0_ref[...], wup_ref[...], a0_ref[...], aup_ref[...], gup_ref[...])
    wag_ref[:, 0:GROUP_W] = wlog
    wag_ref[:, GROUP_W:2 * GROUP_W] = a
    wag_ref[:, 2 * GROUP_W:3 * GROUP_W] = g


def _sprep(proj_s, conv0, shift0, p):
    bsz = proj_s.shape[0]
    return pl.pallas_call(
        _sprep_kernel,
        out_shape=(
            jax.ShapeDtypeStruct((bsz, 2 * GROUP_W), F32),
            jax.ShapeDtypeStruct((bsz, RW_IN_W), F32),
            jax.ShapeDtypeStruct((bsz, 3 * GROUP_W), F32),
        ),
        compiler_params=pltpu.CompilerParams(vmem_limit_bytes=VMEM_LIMIT),
        name="sprep",
    )(proj_s, conv0, shift0, p["conv_w"], p["conv_b"], p["mu"], p["w0"], p["wup_p"], p["a0"],
      p["aup_p"], p["gup"])


def _sstep_kernel(q_ref, k_ref, v_ref, o_ref, ipre_ref, fpre_ref, ib_ref, fb_ref, m0_ref, n0_ref, C0_ref,
                  r_ref, kr_ref, vr_ref, wlog_ref, a_ref, g_ref, S0_ref,
                  mlnw_ref, kk_ref, ka_ref, rk_ref, lnw_ref, lnb_ref, erep_ref, etile_ref,
                  C_ref, n_ref, m_ref, S_ref, hml_ref, yrw_ref):
    e_rep = erep_ref[...]
    e_tile = etile_ref[...]
    rep = lambda x: _dot01_r(x, e_rep)
    tile = lambda x: _dot01_r(x, e_tile)

    q, k, v = q_ref[...], k_ref[...], v_ref[...]
    i_g = ipre_ref[...] + ib_ref[...]
    logf = _log_sigmoid(fpre_ref[...] + fb_ref[...])
    m0 = m0_ref[...]
    m_new = jnp.maximum(logf + m0, i_g)
    ws = jnp.exp(i_g - m_new)
    dec = jnp.exp(logf + m0 - m_new)
    c_new = dec * C0_ref[...] + ws * (rep(k) * tile(v))
    n_new = dec * n0_ref[...] + ws * k
    C_ref[...] = c_new
    n_ref[...] = n_new
    m_ref[...] = m_new
    num = _red3(c_new * rep(q), e_tile)
    den = _rowsum(q * n_new)
    h_til = num / jnp.maximum(jnp.abs(den), jnp.exp(-m_new))
    hml = jax.nn.sigmoid(o_ref[...]) * h_til
    hml_ref[...] = hml * lax.rsqrt(jnp.mean(hml * hml, axis=1, keepdims=True) + EPS) * mlnw_ref[...]

    r, kr, vr, a = r_ref[...], kr_ref[...], vr_ref[...], a_ref[...]
    kk = kr * kk_ref[...]
    kkn = kk / jnp.maximum(jnp.sqrt(_rowsum(kk * kk)), 1e-12)
    keff = kr * (1.0 + (a - 1.0) * ka_ref[...])
    s0 = S0_ref[...]
    sa = _red3(s0 * tile(-kkn), e_rep)
    s_new = s0 * tile(jnp.exp(wlog_ref[...])) + rep(sa) * tile(kkn * a) + rep(vr) * tile(keff)
    S_ref[...] = s_new
    y = _red3(s_new * tile(r), e_rep)
    yb = y + _rowsum(r * keff * rk_ref[...]) * vr
    mu = jnp.mean(yb, axis=1, keepdims=True)
    dlt = yb - mu
    var = jnp.mean(dlt * dlt, axis=1, keepdims=True)
    yrw_ref[...] = (dlt * lax.rsqrt(var + GN_EPS) * lnw_ref[...] + lnb_ref[...]) * g_ref[...]


def _red3(y, m01):
    h1, h2, h3 = _split3(y)
    d = lambda h: lax.dot_general(h, m01, (((1,), (1,)), ((), ())), preferred_element_type=F32)
    return d(h1) + d(h2) + d(h3)


def _sstep(rows_in, p):
    n_rows = rows_in["q"].shape[0]
    flat = HEAD_DIM * HEAD_DIM
    blk = lambda w: pl.BlockSpec((RB, w), lambda i: (i, 0))
    cst = lambda r, w: pl.BlockSpec((r, w), lambda i: (0, 0))
    names = ["q", "k", "v", "o", "ipre", "fpre", "ib", "fb", "m0", "n0", "C0",
             "r", "kr", "vr", "wlog", "a", "g", "S0"]
    widths = [64, 64, 64, 64, 1, 1, None, None, 1, 64, flat, 64, 64, 64, 64, 64, 64, flat]
    in_specs, args = [], []
    for nm, w in zip(names, widths):
        if w is None:
            in_specs.append(cst(RB, 1))
        else:
            in_specs.append(blk(w))
        args.append(rows_in[nm])
    for nm in ["mlnorm_rows", "kk_rows", "ka_rows", "rk_rows", "lnw_rows", "lnb_rows"]:
        in_specs.append(cst(RB, HEAD_DIM))
        args.append(p[nm])
    in_specs += [cst(HEAD_DIM, flat), cst(HEAD_DIM, flat)]
    args += [p["e_rep"], p["e_tile"]]
    return pl.pallas_call(
        _sstep_kernel,
        out_shape=(
            jax.ShapeDtypeStruct((n_rows, flat), F32),
            jax.ShapeDtypeStruct((n_rows, HEAD_DIM), F32),
            jax.ShapeDtypeStruct((n_rows, 1), F32),
            jax.ShapeDtypeStruct((n_rows, flat), F32),
            jax.ShapeDtypeStruct((n_rows, HEAD_DIM), F32),
            jax.ShapeDtypeStruct((n_rows, HEAD_DIM), F32),
        ),
        grid=(n_rows // RB,),
        in_specs=in_specs,
        out_specs=(blk(flat), blk(HEAD_DIM), blk(1), blk(flat), blk(HEAD_DIM), blk(HEAD_DIM)),
        compiler_params=pltpu.CompilerParams(
            dimension_semantics=("arbitrary",), vmem_limit_bytes=VMEM_LIMIT),
        name="sstep",
    )(*args)


def _layer_params(l, norm_mix_w, w_in, mlstm_conv_w, mlstm_conv_b, mlstm_i_b, mlstm_f_b, mlstm_norm_w,
                  rw_mu, rw_w0, rw_w_up, rw_a0, rw_a_up, rw_g_up, rw_k_k, rw_k_a, rw_r_k, rw_ln_w, rw_ln_b,
                  w_out, norm_mlp_w, mlp_up, mlp_down):
    ml_in = 4 * GROUP_W
    wi = w_in[l]
    zpad = jnp.zeros((D_MODEL, LANES - N_HEADS), F32)
    w_in_p = jnp.concatenate(
        [wi[:, :ml_in], wi[:, ml_in:ml_in + N_HEADS], zpad,
         wi[:, ml_in + N_HEADS:ml_in + 2 * N_HEADS], zpad, wi[:, ml_in + 2 * N_HEADS:]], axis=1)
    gate_row = lambda b: jnp.concatenate([b, jnp.zeros((LANES - N_HEADS,), F32)])[None, :]
    zl = jnp.zeros((D_LORA, GROUP_W), F32)
    rows = lambda v: jnp.tile(v.reshape(N_HEADS, HEAD_DIM), (RB // N_HEADS, 1))
    flat_idx = jnp.arange(HEAD_DIM * HEAD_DIM)
    one_hot = lambda idx: (jnp.arange(HEAD_DIM)[:, None] == idx[None, :]).astype(BF16)
    return {
        "norm_mix_w": norm_mix_w[l][None, :],
        "w_in_p": w_in_p.astype(BF16),
        "conv_w": mlstm_conv_w[l], "conv_b": mlstm_conv_b[l][None, :],
        "ib_row": gate_row(mlstm_i_b[l]), "fb_row": gate_row(mlstm_f_b[l]),
        "ib_rows": jnp.tile(mlstm_i_b[l], RB // N_HEADS)[:, None],
        "fb_rows": jnp.tile(mlstm_f_b[l], RB // N_HEADS)[:, None],
        "mlnorm_w": mlstm_norm_w[l][None, :], "mlnorm_rows": rows(mlstm_norm_w[l]),
        "mu": rw_mu[l][None, :], "w0": rw_w0[l][None, :], "a0": rw_a0[l][None, :],
        "wup_p": jnp.concatenate([rw_w_up[l], zl], axis=0).astype(BF16),
        "aup_p": jnp.concatenate([zl, rw_a_up[l]], axis=0).astype(BF16),
        "gup": rw_g_up[l].astype(BF16),
        "k_k": rw_k_k[l][None, :], "k_a": rw_k_a[l][None, :], "r_k": rw_r_k[l].reshape(1, GROUP_W),
        "ln_w": rw_ln_w[l][None, :], "ln_b": rw_ln_b[l][None, :],
        "kk_rows": rows(rw_k_k[l]), "ka_rows": rows(rw_k_a[l]), "rk_rows": rows(rw_r_k[l]),
        "lnw_rows": rows(rw_ln_w[l]), "lnb_rows": rows(rw_ln_b[l]),
        "e_rep": one_hot(flat_idx // HEAD_DIM), "e_tile": one_hot(flat_idx % HEAD_DIM),
        "w_out": w_out[l].astype(BF16), "norm_mlp_w": norm_mlp_w[l][None, :],
        "mlp_up": mlp_up[l].astype(BF16), "mlp_down": mlp_down[l].astype(BF16),
    }


def _unpair(st):
    d = HEAD_DIM
    return jnp.stack([st[:, h // 2, (h % 2) * d:(h % 2 + 1) * d, (h % 2) * d:(h % 2 + 1) * d]
                      for h in range(N_HEADS)], axis=1)


def _prompt_layer(x, p, norm_f_w):
    bsz, seq, _ = x.shape
    x2d = x.reshape(bsz * seq, D_MODEL)
    proj = _inproj(x2d, p["norm_mix_w"], p["w_in_p"]).reshape(bsz, seq, IN_WP)
    mix, c_st, n_st, m_st, s_st = _mixer(proj, p)
    y = _outmlp(x2d, mix.reshape(bsz * seq, D_MODEL), p["w_out"], p["norm_mlp_w"], p["mlp_up"],
                p["mlp_down"], norm_f_w)
    c_new = _unpair(c_st)
    n_new = n_st[:, :PAIRS, :].reshape(bsz, N_HEADS, HEAD_DIM)
    m_new = m_st[:, :, 0]
    conv_new = proj[:, seq - (CONV_W - 1):, Q0:Q0 + 2 * GROUP_W]
    s_new = _unpair(s_st)
    shift_new = proj[:, seq - 1:, RW0:RW0 + RW_IN_W]
    return y.reshape(bsz, seq, D_MODEL), (c_new, n_new, m_new, conv_new, s_new, shift_new)


def _sample_layer(x, c0, n0, m0, conv0, s0, shift0, p, norm_f_w):
    bsz = x.shape[0]
    n_rows = bsz * N_HEADS
    flat = HEAD_DIM * HEAD_DIM
    x2d = x.reshape(bsz, D_MODEL)
    proj = _inproj(x2d, p["norm_mix_w"], p["w_in_p"])
    qk, rs, wag = _sprep(proj, jnp.moveaxis(conv0, 1, 0), shift0[:, 0, :], p)
    hr = lambda a2d: a2d.reshape(n_rows, HEAD_DIM)
    rows_in = {
        "q": hr(qk[:, :GROUP_W]), "k": hr(qk[:, GROUP_W:]), "v": hr(proj[:, V0:V0 + GROUP_W]),
        "o": hr(proj[:, O0:O0 + GROUP_W]),
        "ipre": proj[:, GI0:GI0 + N_HEADS].reshape(n_rows, 1),
        "fpre": proj[:, GF0:GF0 + N_HEADS].reshape(n_rows, 1),
        "ib": p["ib_rows"], "fb": p["fb_rows"],
        "m0": m0.reshape(n_rows, 1), "n0": n0.reshape(n_rows, HEAD_DIM), "C0": c0.reshape(n_rows, flat),
        "r": hr(rs[:, RS_R:RS_R + GROUP_W]), "kr": hr(rs[:, RS_K:RS_K + GROUP_W]),
        "vr": hr(rs[:, RS_V:RS_V + GROUP_W]),
        "wlog": hr(wag[:, :GROUP_W]), "a": hr(wag[:, GROUP_W:2 * GROUP_W]), "g": hr(wag[:, 2 * GROUP_W:]),
        "S0": s0.reshape(n_rows, flat),
    }
    c_new, n_new, m_new, s_new, hml, yrw = _sstep(rows_in, p)
    mix = jnp.concatenate([hml.reshape(bsz, GROUP_W), yrw.reshape(bsz, GROUP_W)], axis=1)
    y = _outmlp(x2d, mix, p["w_out"], p["norm_mlp_w"], p["mlp_up"], p["mlp_down"], norm_f_w)
    conv_new = jnp.concatenate([conv0[:, 1:], proj[:, None, Q0:Q0 + 2 * GROUP_W]], axis=1)
    shift_new = proj[:, None, RW0:RW0 + RW_IN_W]
    return y.reshape(bsz, 1, D_MODEL), (
        c_new.reshape(bsz, N_HEADS, HEAD_DIM, HEAD_DIM), n_new.reshape(bsz, N_HEADS, HEAD_DIM),
        m_new.reshape(bsz, N_HEADS), conv_new, s_new.reshape(bsz, N_HEADS, HEAD_DIM, HEAD_DIM), shift_new)


def kernel(x_prompt, x_sample, state_mlstm_C, state_mlstm_n, state_mlstm_m, state_mlstm_conv, state_rwkv_S, state_rwkv_shift, norm_mix_w, w_in, mlstm_conv_w, mlstm_conv_b, mlstm_i_b, mlstm_f_b, mlstm_norm_w, rw_mu, rw_w0, rw_w_up, rw_a0, rw_a_up, rw_g_up, rw_k_k, rw_k_a, rw_r_k, rw_ln_w, rw_ln_b, w_out, norm_mlp_w, mlp_up, mlp_down, norm_f_w):
    depth = w_in.shape[0]
    assert depth == 1, "the final RMSNorm is fused into the layer's output kernel"
    weights = (norm_mix_w, w_in, mlstm_conv_w, mlstm_conv_b, mlstm_i_b, mlstm_f_b, mlstm_norm_w,
               rw_mu, rw_w0, rw_w_up, rw_a0, rw_a_up, rw_g_up, rw_k_k, rw_k_a, rw_r_k, rw_ln_w, rw_ln_b,
               w_out, norm_mlp_w, mlp_up, mlp_down)
    nfw = norm_f_w[None, :]
    p = _layer_params(0, *weights)
    y_p, st_p = _prompt_layer(x_prompt, p, nfw)
    y_s, st_s = _sample_layer(x_sample, state_mlstm_C[0], state_mlstm_n[0], state_mlstm_m[0],
                              state_mlstm_conv[0], state_rwkv_S[0], state_rwkv_shift[0], p, nfw)
    lead = lambda t: tuple(a[None] for a in t)
    return (y_p, y_s) + lead(st_p) + lead(st_s)
```

```python
import functools
import math

import jax
import jax.numpy as jnp
from jax import lax
from jax.experimental import pallas as pl
from jax.experimental.pallas import tpu as pltpu

F32 = jnp.float32
BF16 = jnp.bfloat16

D_MODEL = 1024
HEAD_DIM = 64
N_HEADS = 8
PAIRS = N_HEADS // 2
GROUP_W = N_HEADS * HEAD_DIM
CONV_W = 4
D_LORA = 64
D_GATE = 128
D_FF = 4 * D_MODEL
EPS = 1e-6
GN_EPS = 64e-5
LANES = 128
SUBLANES = 8

Q0, K0, V0, O0 = 0, GROUP_W, 2 * GROUP_W, 3 * GROUP_W
GI0 = 4 * GROUP_W
GF0 = GI0 + LANES
RW0 = GF0 + LANES
RW_IN_W = 3 * GROUP_W + 2 * D_LORA + D_GATE
IN_WP = RW0 + RW_IN_W
RS_R, RS_K, RS_V = 0, GROUP_W, 2 * GROUP_W
RS_LORA = 3 * GROUP_W
RS_G = RS_LORA + 2 * D_LORA

TB = 256
LM = 128
LR = 64
ML_EVERY = 3
_DONE = object()
RW_GROUP = 2
TM_IN = 256
TM_OUT = 256
FF_CHUNK = 1024
RB = 128
NEG = -1e30
W_SCALE = math.exp(-0.5)
VMEM_LIMIT = 56 * 1024 * 1024


def _dot(a, b):
    return jnp.dot(a.astype(BF16), b.astype(BF16), preferred_element_type=F32)


def _dot_nt(a, b):
    return lax.dot_general(a.astype(BF16), b.astype(BF16), (((1,), (1,)), ((), ())),
                           preferred_element_type=F32)


def _split3(x):
    h1 = x.astype(BF16)
    r1 = x - h1.astype(F32)
    h2 = r1.astype(BF16)
    h3 = (r1 - h2.astype(F32)).astype(BF16)
    return h1, h2, h3


def _dot01_l(m01, x):
    h1, h2, h3 = _split3(x)
    d = lambda h: jnp.dot(m01, h, preferred_element_type=F32)
    return d(h1) + d(h2) + d(h3)


def _dot01_r(x, m01):
    h1, h2, h3 = _split3(x)
    d = lambda h: jnp.dot(h, m01, preferred_element_type=F32)
    return d(h1) + d(h2) + d(h3)


def _log_sigmoid(x):
    return jnp.minimum(x, 0.0) - jnp.log(1.0 + jnp.exp(-jnp.abs(x)))


def _iota(shape, dim):
    return lax.broadcasted_iota(jnp.int32, shape, dim)


def _rowsum(x):
    return jnp.sum(x, axis=1, keepdims=True)


def _pair_rowsum(x, lo):
    s_lo = _rowsum(jnp.where(lo, x, 0.0))
    s_hi = _rowsum(jnp.where(lo, 0.0, x))
    return jnp.where(lo, s_lo, s_hi)


def _stack2(x, lo):
    return jnp.concatenate([jnp.where(lo, x, 0.0), jnp.where(lo, 0.0, x)], axis=0)


def _rms(x, w):
    ms = jnp.mean(x * x, axis=-1, keepdims=True)
    return x * lax.rsqrt(ms + EPS) * w


def _inproj_kernel(x_ref, nw_ref, w_ref, o_ref):
    xn = _rms(x_ref[...], nw_ref[...])
    o_ref[...] = jnp.dot(xn.astype(BF16), w_ref[...], preferred_element_type=F32)


def _inproj(x2d, norm_w, w_in_p):
    rows = x2d.shape[0]
    tm = min(TM_IN, rows)
    return pl.pallas_call(
        _inproj_kernel,
        out_shape=jax.ShapeDtypeStruct((rows, IN_WP), F32),
        grid=(rows // tm,),
        in_specs=[
            pl.BlockSpec((tm, D_MODEL), lambda i: (i, 0)),
            pl.BlockSpec((1, D_MODEL), lambda i: (0, 0)),
            pl.BlockSpec((D_MODEL, IN_WP), lambda i: (0, 0)),
        ],
        out_specs=pl.BlockSpec((tm, IN_WP), lambda i: (i, 0)),
        compiler_params=pltpu.CompilerParams(
            dimension_semantics=("arbitrary",), vmem_limit_bytes=VMEM_LIMIT),
        name="inproj",
    )(x2d, norm_w, w_in_p)


def _outmlp_kernel(x_ref, mix_ref, wout_ref, nmw_ref, up_ref, down_ref, nfw_ref, y_ref):
    x1 = x_ref[...] + jnp.dot(mix_ref[...].astype(BF16), wout_ref[...], preferred_element_type=F32)
    xn = _rms(x1, nmw_ref[...]).astype(BF16)
    acc = x1
    for c in range(D_FF // FF_CHUNK):
        sl = slice(c * FF_CHUNK, (c + 1) * FF_CHUNK)
        hid = jnp.maximum(jnp.dot(xn, up_ref[:, sl], preferred_element_type=F32), 0.0)
        acc = acc + jnp.dot((hid * hid).astype(BF16), down_ref[sl, :], preferred_element_type=F32)
    y_ref[...] = _rms(acc, nfw_ref[...])


def _outmlp(x2d, mix2d, w_out, norm_mlp_w, mlp_up, mlp_down, norm_f_w):
    rows = x2d.shape[0]
    tm = min(TM_OUT, rows)
    const = lambda shape: pl.BlockSpec(shape, lambda i: (0, 0), pipeline_mode=pl.Buffered(1))
    return pl.pallas_call(
        _outmlp_kernel,
        out_shape=jax.ShapeDtypeStruct((rows, D_MODEL), F32),
        grid=(rows // tm,),
        in_specs=[
            pl.BlockSpec((tm, D_MODEL), lambda i: (i, 0)),
            pl.BlockSpec((tm, D_MODEL), lambda i: (i, 0)),
            const((D_MODEL, D_MODEL)),
            const((1, D_MODEL)),
            const((D_MODEL, D_FF)),
            const((D_FF, D_MODEL)),
            const((1, D_MODEL)),
        ],
        out_specs=pl.BlockSpec((tm, D_MODEL), lambda i: (i, 0)),
        compiler_params=pltpu.CompilerParams(
            dimension_semantics=("arbitrary",), vmem_limit_bytes=VMEM_LIMIT),
        name="outmlp",
    )(x2d, mix2d, w_out, norm_mlp_w, mlp_up, mlp_down, norm_f_w)


def _rwkv_token_prep(lora_t, xg, w0, wup, a0, aup, gup):
    w_raw = w0 + jnp.dot(jnp.tanh(lora_t).astype(BF16), wup, preferred_element_type=F32)
    wlog = -W_SCALE * jax.nn.sigmoid(w_raw)
    a = jax.nn.sigmoid(a0 + jnp.dot(lora_t.astype(BF16), aup, preferred_element_type=F32))
    g = jnp.dot(jax.nn.sigmoid(xg).astype(BF16), gup, preferred_element_type=F32)
    return wlog, a, g


def _mlstm_chunk(c, proj_ref, q_ref, k_ref, ib_ref, fb_ref, mlnw_ref, mix_ref, C_ref, n_ref, m_ref):
    rows = pl.ds(pl.multiple_of(c * LM, LM), LM)
    row = _iota((LM, LM), 0)
    col = _iota((LM, LM), 1)
    causal = col <= row
    tril = jnp.where(causal, 1.0, 0.0).astype(BF16)
    blockdiag = (row < HEAD_DIM) == (col < HEAD_DIM)
    lane = _iota((1, LANES), 1)
    lo = lane < HEAD_DIM

    i_tile = proj_ref[0, rows, GI0:GI0 + LANES] + ib_ref[...]
    f_tile = _log_sigmoid(proj_ref[0, rows, GF0:GF0 + LANES] + fb_ref[...])
    bc = _dot01_l(tril, f_tile)
    u = i_tile - bc
    ut = u.T
    yield

    nt = (((1,), (1,)), ((), ()))
    pairs = range(PAIRS)
    heads = range(N_HEADS)
    tile = lambda base, j: slice(base + j * LANES, base + (j + 1) * LANES)
    hmask = lambda h: lo if h % 2 == 0 else jnp.logical_not(lo)
    q_t = [q_ref[rows, tile(0, j)] for j in pairs]
    k_t = [k_ref[rows, tile(0, j)] for j in pairs]
    v_t = [proj_ref[0, rows, tile(V0, j)] for j in pairs]
    c_pair = [C_ref[0, j] for j in pairs]
    n_row = [n_ref[0, j:j + 1, :] for j in pairs]
    k_bf = [k.astype(BF16) for k in k_t]
    v_bf = [v.astype(BF16) for v in v_t]
    s = [lax.dot_general(jnp.where(hmask(h), q_t[h // 2], 0.0).astype(BF16), k_bf[h // 2], nt,
                         preferred_element_type=F32) for h in heads]
    q_c = [_dot(q_t[j], c_pair[j]) for j in pairs]
    k_tr = [k.T for k in k_t]
    yield

    m_prev_t = m_ref[0]
    ut8 = ut[0:N_HEADS, :]
    m_last_t = jnp.maximum(jnp.max(ut8, axis=1, keepdims=True), m_prev_t)
    ws_rows = jnp.exp(ut8 - m_last_t)
    dec_t = jnp.exp(m_prev_t - m_last_t)
    m_ref[0] = bc.T[0:N_HEADS, LM - 1:LM] + m_last_t

    m_prev = [m_prev_t[h:h + 1, 0:1] for h in heads]
    u_row = [ut[h:h + 1, :] for h in heads]
    bc_col = [_rowsum(jnp.where(lane == h, bc, 0.0)) for h in heads]
    m_col = [jnp.maximum(jnp.max(jnp.where(causal, u_row[h], NEG), axis=1, keepdims=True), m_prev[h])
             for h in heads]
    p = [s[h] * jnp.exp(jnp.where(causal, u_row[h] - m_col[h], NEG)) for h in heads]
    sc = [jnp.exp(m_prev[h] - m_col[h]) for h in heads]
    yield
    pv = [jnp.dot(p[h].astype(BF16), v_bf[h // 2], preferred_element_type=F32) for h in heads]
    yield
    h_parts = []
    for h in heads:
        q_n = _rowsum(jnp.where(hmask(h), q_t[h // 2] * n_row[h // 2], 0.0))
        den = _rowsum(p[h]) + sc[h] * q_n
        num = pv[h] + sc[h] * q_c[h // 2]
        h_parts.append(num / jnp.maximum(jnp.abs(den), jnp.exp(-(bc_col[h] + m_col[h]))))

    top = _iota((LM, 1), 0) < HEAD_DIM
    kw = [k_tr[j] * jnp.where(top, ws_rows[2 * j:2 * j + 1, :], ws_rows[2 * j + 1:2 * j + 2, :])
          for j in pairs]
    upd = [jnp.dot(kw[j].astype(BF16), v_bf[j], preferred_element_type=F32) for j in pairs]
    n_upd = [jnp.dot(ws_rows.astype(BF16), k_bf[j], preferred_element_type=F32) for j in pairs]
    yield
    for j in pairs:
        dec_j = jnp.where(lo, dec_t[2 * j:2 * j + 1, :], dec_t[2 * j + 1:2 * j + 2, :])
        C_ref[0, j] = c_pair[j] * dec_j + jnp.where(blockdiag, upd[j], 0.0)
        n_ref[0, j:j + 1, :] = n_row[j] * dec_j + jnp.where(lo, n_upd[j][2 * j:2 * j + 1, :],
                                                            n_upd[j][2 * j + 1:2 * j + 2, :])

    for j in pairs:
        h_til = jnp.where(lo, h_parts[2 * j], h_parts[2 * j + 1])
        hml = jax.nn.sigmoid(proj_ref[0, rows, tile(O0, j)]) * h_til
        ms = _pair_rowsum(hml * hml, lo) * (1.0 / HEAD_DIM)
        mix_ref[0, rows, tile(0, j)] = hml * lax.rsqrt(ms + EPS) * mlnw_ref[:, tile(0, j)]


def _neumann_inverse(ns):
    shape = ns[0].shape
    eye = jnp.where(_iota(shape, 0) == _iota(shape, 1), 1.0, 0.0)
    n_sq = int(math.log2(LR)) - 1
    qs = [n.astype(BF16) for n in ns]
    ts = [eye + n for n in ns]
    sq = [jnp.dot(q, q, preferred_element_type=F32) for q in qs]
    yield
    for _ in range(n_sq - 1):
        qs = [s.astype(BF16) for s in sq]
        both = [jnp.dot(q, jnp.concatenate([q, t.astype(BF16)], axis=1), preferred_element_type=F32)
                for q, t in zip(qs, ts)]
        sq = [b[:, 0:LANES] for b in both]
        ts = [t + b[:, LANES:2 * LANES] for t, b in zip(ts, both)]
        yield
    return [t + jnp.dot(s.astype(BF16), t.astype(BF16), preferred_element_type=F32) for s, t in zip(sq, ts)]


def _rwkv_chunks(c, rs_ref, w0_ref, wup_ref, a0_ref, aup_ref, gup_ref, kk_ref, ka_ref, rk_ref,
                 lnw_ref, lnb_ref, mix_ref, S_ref):
    gl = RW_GROUP * LR
    rows = pl.ds(pl.multiple_of(c * gl, gl), gl)
    lane = _iota((1, LANES), 1)
    lo = lane < HEAD_DIM
    row = _iota((LANES, LANES), 0)
    col = _iota((LANES, LANES), 1)
    same = (row < LR) == (col < LR)
    mask_sl = jnp.logical_and(same, col < row)
    mask_li = jnp.logical_and(same, col <= row)
    mask_li2 = jnp.concatenate([mask_li, mask_li], axis=1)
    grow = _iota((gl, gl), 0)
    gcol = _iota((gl, gl), 1)
    tril = jnp.where(jnp.logical_and(gcol <= grow, gcol >= (grow // LR) * LR), 1.0, 0.0).astype(BF16)
    nt = (((1,), (1,)), ((), ()))
    pairs = range(PAIRS)
    items = [(ci, j) for ci in range(RW_GROUP) for j in pairs]
    tile = lambda base, j: slice(base + j * LANES, base + (j + 1) * LANES)
    crow = lambda ci: slice(ci * LR, (ci + 1) * LR)

    wlog, a, g = _rwkv_token_prep(rs_ref[rows, RS_LORA:RS_LORA + LANES], rs_ref[rows, RS_G:RS_G + D_GATE],
                                  w0_ref[...], wup_ref[...], a0_ref[...], aup_ref[...], gup_ref[...])
    lw = _dot01_l(tril, wlog)
    e_in = jnp.exp(lw)
    e_ex = jnp.exp(lw - wlog)
    e_ng = jnp.exp(-lw)

    r_g = [rs_ref[rows, tile(RS_R, j)] for j in pairs]
    kr_g = [rs_ref[rows, tile(RS_K, j)] for j in pairs]
    v_g = [rs_ref[rows, tile(RS_V, j)] for j in pairs]
    a_g = [a[:, tile(0, j)] for j in pairs]
    kk = [kr_g[j] * kk_ref[:, tile(0, j)] for j in pairs]
    kkn = [kk[j] / jnp.maximum(jnp.sqrt(_pair_rowsum(kk[j] * kk[j], lo)), 1e-12) for j in pairs]
    keff_g = [kr_g[j] * (1.0 + (a_g[j] - 1.0) * ka_ref[:, tile(0, j)]) for j in pairs]
    alpha_g = [-kkn[j] * e_ex[:, tile(0, j)] for j in pairs]
    rb_g = [r_g[j] * e_in[:, tile(0, j)] for j in pairs]
    beta_g = [kkn[j] * a_g[j] * e_ng[:, tile(0, j)] for j in pairs]
    ktl_g = [keff_g[j] * e_ng[:, tile(0, j)] for j in pairs]
    bonus_g = [_pair_rowsum(r_g[j] * keff_g[j] * rk_ref[:, tile(0, j)], lo) * v_g[j] for j in pairs]

    lhs, rhs, v2, bk, gamma, ar = {}, {}, {}, {}, {}, {}
    for it in items:
        ci, j = it
        cr = crow(ci)
        ar[it] = jnp.concatenate([alpha_g[j][cr], rb_g[j][cr]], axis=0).astype(BF16)
        lhs[it] = jnp.concatenate([_stack2(alpha_g[j][cr], lo), _stack2(rb_g[j][cr], lo)],
                                  axis=0).astype(BF16)
        beta_c, ktl_c = beta_g[j][cr], ktl_g[j][cr]
        rhs[it] = jnp.concatenate([beta_c, beta_c, ktl_c, ktl_c], axis=0).astype(BF16)
        v2[it] = _stack2(v_g[j][cr], lo)
        bk[it] = jnp.concatenate([_stack2(beta_c, lo), _stack2(ktl_c, lo)], axis=0).astype(BF16)
        gamma[it] = e_in[(ci + 1) * LR - 1:(ci + 1) * LR, tile(0, j)]
    aa = {it: lax.dot_general(lhs[it], rhs[it], nt, preferred_element_type=F32) for it in items}
    yield
    t_list = yield from _neumann_inverse([jnp.where(mask_sl, aa[it][0:LANES, 0:LANES], 0.0) for it in items])
    t_inv = dict(zip(items, t_list))
    a_ak = {it: jnp.where(mask_sl, aa[it][0:LANES, LANES:2 * LANES], 0.0) for it in items}
    a_r = {it: jnp.where(mask_li2, aa[it][LANES:2 * LANES, :], 0.0) for it in items}
    akv = {it: _dot(a_ak[it], v2[it]) for it in items}
    yield

    s_pair = [S_ref[0, j] for j in pairs]
    for ci in range(RW_GROUP):
        its = [(ci, j) for j in pairs]
        xs = [lax.dot_general(ar[it], s_pair[it[1]].astype(BF16), nt, preferred_element_type=F32)
              for it in its]
        yield
        p2 = [_dot(t_inv[it], _stack2(xs[j][0:LR], lo) + akv[it]) for j, it in enumerate(its)]
        yield
        pv = [jnp.concatenate([p2[j], v2[it]], axis=0) for j, it in enumerate(its)]
        y2 = [_dot(a_r[it], pv[j]) for j, it in enumerate(its)]
        upd = [_dot(pv[j].T, bk[it]) for j, it in enumerate(its)]
        s_pair = [(s_pair[j] + upd[j]) * gamma[it] for j, it in enumerate(its)]
        yield
        cr = crow(ci)
        for j in pairs:
            yb = xs[j][LR:2 * LR] + y2[j][0:LR] + y2[j][LR:2 * LR] + bonus_g[j][cr]
            mu = _pair_rowsum(yb, lo) * (1.0 / HEAD_DIM)
            dlt = yb - mu
            var = _pair_rowsum(dlt * dlt, lo) * (1.0 / HEAD_DIM)
            yn = dlt * lax.rsqrt(var + GN_EPS) * lnw_ref[:, tile(0, j)] + lnb_ref[:, tile(0, j)]
            out_rows = pl.ds(pl.multiple_of(c * gl + ci * LR, LR), LR)
            mix_ref[0, out_rows, tile(GROUP_W, j)] = yn * g[cr, tile(0, j)]
    for j in pairs:
        S_ref[0, j] = s_pair[j]


def _interleave(primary, secondary, every):
    done_p = done_s = False
    k = 0
    while not (done_p and done_s):
        if not done_p:
            done_p = next(primary, _DONE) is _DONE
        k += 1
        if not done_s and (done_p or k % every == 0):
            done_s = next(secondary, _DONE) is _DONE


def _mixer_kernel(proj_ref, convw_ref, convb_ref, ib_ref, fb_ref, mlnw_ref, mu_ref, w0_ref, wup_ref,
                  a0_ref, aup_ref, gup_ref, kk_ref, ka_ref, rk_ref, lnw_ref, lnb_ref,
                  mix_ref, C_ref, n_ref, m_ref, S_ref,
                  ext_ref, rwext_ref, q_ref, k_ref, rs_ref):
    t = pl.program_id(1)

    @pl.when(t == 0)
    def _():
        C_ref[...] = jnp.zeros_like(C_ref)
        n_ref[...] = jnp.zeros_like(n_ref)
        m_ref[...] = jnp.zeros_like(m_ref)
        S_ref[...] = jnp.zeros_like(S_ref)
        ext_ref[0:SUBLANES, :] = jnp.zeros((SUBLANES, 2 * GROUP_W), F32)
        rwext_ref[0:SUBLANES, :] = jnp.zeros((SUBLANES, RW_IN_W), F32)

    @pl.when(t > 0)
    def _():
        ext_ref[0:SUBLANES, :] = ext_ref[TB:TB + SUBLANES, :]
        rwext_ref[0:SUBLANES, :] = rwext_ref[TB:TB + SUBLANES, :]

    ext_ref[SUBLANES:TB + SUBLANES, :] = proj_ref[0, :, Q0:Q0 + 2 * GROUP_W]
    acc = convb_ref[...] + ext_ref[SUBLANES:TB + SUBLANES, :] * convw_ref[CONV_W - 1:CONV_W, :]
    for jw in range(CONV_W - 1):
        off = SUBLANES - (CONV_W - 1) + jw
        acc = acc + ext_ref[off:off + TB, :] * convw_ref[jw:jw + 1, :]
    qk = acc * jax.nn.sigmoid(acc)
    q_ref[...] = qk[:, 0:GROUP_W]
    k_ref[...] = qk[:, GROUP_W:2 * GROUP_W] * (HEAD_DIM ** -0.5)

    rw = proj_ref[0, :, RW0:RW0 + RW_IN_W]
    rwext_ref[SUBLANES:TB + SUBLANES, :] = rw
    shifted = rwext_ref[SUBLANES - 1:TB + SUBLANES - 1, :]
    rs_ref[...] = rw + mu_ref[...] * (shifted - rw)

    assert RW_GROUP * LR == LM

    def body(c, carry):
        _interleave(
            _rwkv_chunks(c, rs_ref, w0_ref, wup_ref, a0_ref, aup_ref, gup_ref, kk_ref, ka_ref, rk_ref,
                         lnw_ref, lnb_ref, mix_ref, S_ref),
            _mlstm_chunk(c, proj_ref, q_ref, k_ref, ib_ref, fb_ref, mlnw_ref, mix_ref, C_ref, n_ref, m_ref),
            every=ML_EVERY)
        return carry

    lax.fori_loop(0, TB // LM, body, 0)


def _mixer(proj, p):
    bsz, seq, _ = proj.shape
    row = lambda w: pl.BlockSpec((1, w), lambda b, t: (0, 0))
    mat = lambda r, w: pl.BlockSpec((r, w), lambda b, t: (0, 0))
    state4 = pl.BlockSpec((1, PAIRS, LANES, LANES), lambda b, t: (b, 0, 0, 0))
    state3 = pl.BlockSpec((1, SUBLANES, LANES), lambda b, t: (b, 0, 0))
    return pl.pallas_call(
        _mixer_kernel,
        out_shape=(
            jax.ShapeDtypeStruct((bsz, seq, 2 * GROUP_W), F32),
            jax.ShapeDtypeStruct((bsz, PAIRS, LANES, LANES), F32),
            jax.ShapeDtypeStruct((bsz, SUBLANES, LANES), F32),
            jax.ShapeDtypeStruct((bsz, SUBLANES, LANES), F32),
            jax.ShapeDtypeStruct((bsz, PAIRS, LANES, LANES), F32),
        ),
        grid=(bsz, seq // TB),
        in_specs=[
            pl.BlockSpec((1, TB, IN_WP), lambda b, t: (b, t, 0)),
            mat(CONV_W, 2 * GROUP_W), row(2 * GROUP_W), row(LANES), row(LANES), row(GROUP_W),
            row(RW_IN_W), row(GROUP_W), mat(LANES, GROUP_W), row(GROUP_W), mat(LANES, GROUP_W),
            mat(D_GATE, GROUP_W), row(GROUP_W), row(GROUP_W), row(GROUP_W), row(GROUP_W), row(GROUP_W),
        ],
        out_specs=(
            pl.BlockSpec((1, TB, 2 * GROUP_W), lambda b, t: (b, t, 0)),
            state4, state3, state3, state4,
        ),
        scratch_shapes=[
            pltpu.VMEM((TB + SUBLANES, 2 * GROUP_W), F32),
            pltpu.VMEM((TB + SUBLANES, RW_IN_W), F32),
            pltpu.VMEM((TB, GROUP_W), F32),
            pltpu.VMEM((TB, GROUP_W), F32),
            pltpu.VMEM((TB, RW_IN_W), F32),
        ],
        compiler_params=pltpu.CompilerParams(
            dimension_semantics=("arbitrary", "arbitrary"), vmem_limit_bytes=VMEM_LIMIT),
        name="mixer",
    )(proj, p["conv_w"], p["conv_b"], p["ib_row"], p["fb_row"], p["mlnorm_w"], p["mu"], p["w0"],
      p["wup_p"], p["a0"], p["aup_p"], p["gup"], p["k_k"], p["k_a"], p["r_k"], p["ln_w"], p["ln_b"])


def _sprep_kernel(proj_ref, conv0_ref, shift0_ref, convw_ref, convb_ref, mu_ref, w0_ref, wup_ref,
                  a0_ref, aup_ref, gup_ref, qk_ref, rs_ref, wag_ref):
    acc = convb_ref[...] + proj_ref[:, Q0:Q0 + 2 * GROUP_W] * convw_ref[CONV_W - 1:CONV_W, :]
    for jw in range(CONV_W - 1):
        acc = acc + conv0_ref[jw] * convw_ref[jw:jw + 1, :]
    qk = acc * jax.nn.sigmoid(acc)
    lane = _iota((1, 2 * GROUP_W), 1)
    qk_ref[...] = jnp.where(lane < GROUP_W, qk, qk * (HEAD_DIM ** -0.5))

    rw = proj_ref[:, RW0:RW0 + RW_IN_W]
    rs = rw + mu_ref[...] * (shift0_ref[...] - rw)
    rs_ref[...] = rs
    wlog, a, g = _rwkv_token_prep(rs[:, RS_LORA:RS_LORA + LANES], rs[:, RS_G:RS_G + D_GATE],
                                  w0_ref[...], wup_ref[...], a0_ref[...], aup_ref[...], gup_ref[...])
    wag_ref[:, 0:GROUP_W] = wlog
    wag_ref[:, GROUP_W:2 * GROUP_W] = a
    wag_ref[:, 2 * GROUP_W:3 * GROUP_W] = g


def _sprep(proj_s, conv0, shift0, p):
    bsz = proj_s.shape[0]
    return pl.pallas_call(
        _sprep_kernel,
        out_shape=(
            jax.ShapeDtypeStruct((bsz, 2 * GROUP_W), F32),
            jax.ShapeDtypeStruct((bsz, RW_IN_W), F32),
            jax.ShapeDtypeStruct((bsz, 3 * GROUP_W), F32),
        ),
        compiler_params=pltpu.CompilerParams(vmem_limit_bytes=VMEM_LIMIT),
        name="sprep",
    )(proj_s, conv0, shift0, p["conv_w"], p["conv_b"], p["mu"], p["w0"], p["wup_p"], p["a0"],
      p["aup_p"], p["gup"])


def _sstep_kernel(q_ref, k_ref, v_ref, o_ref, ipre_ref, fpre_ref, ib_ref, fb_ref, m0_ref, n0_ref, C0_ref,
                  r_ref, kr_ref, vr_ref, wlog_ref, a_ref, g_ref, S0_ref,
                  mlnw_ref, kk_ref, ka_ref, rk_ref, lnw_ref, lnb_ref, erep_ref, etile_ref,
                  C_ref, n_ref, m_ref, S_ref, hml_ref, yrw_ref):
    e_rep = erep_ref[...]
    e_tile = etile_ref[...]
    rep = lambda x: _dot01_r(x, e_rep)
    tile = lambda x: _dot01_r(x, e_tile)

    q, k, v = q_ref[...], k_ref[...], v_ref[...]
    i_g = ipre_ref[...] + ib_ref[...]
    logf = _log_sigmoid(fpre_ref[...] + fb_ref[...])
    m0 = m0_ref[...]
    m_new = jnp.maximum(logf + m0, i_g)
    ws = jnp.exp(i_g - m_new)
    dec = jnp.exp(logf + m0 - m_new)
    c_new = dec * C0_ref[...] + ws * (rep(k) * tile(v))
    n_new = dec * n0_ref[...] + ws * k
    C_ref[...] = c_new
    n_ref[...] = n_new
    m_ref[...] = m_new
    num = _red3(c_new * rep(q), e_tile)
    den = _rowsum(q * n_new)
    h_til = num / jnp.maximum(jnp.abs(den), jnp.exp(-m_new))
    hml = jax.nn.sigmoid(o_ref[...]) * h_til
    hml_ref[...] = hml * lax.rsqrt(jnp.mean(hml * hml, axis=1, keepdims=True) + EPS) * mlnw_ref[...]

    r, kr, vr, a = r_ref[...], kr_ref[...], vr_ref[...], a_ref[...]
    kk = kr * kk_ref[...]
    kkn = kk / jnp.maximum(jnp.sqrt(_rowsum(kk * kk)), 1e-12)
    keff = kr * (1.0 + (a - 1.0) * ka_ref[...])
    s0 = S0_ref[...]
    sa = _red3(s0 * tile(-kkn), e_rep)
    s_new = s0 * tile(jnp.exp(wlog_ref[...])) + rep(sa) * tile(kkn * a) + rep(vr) * tile(keff)
    S_ref[...] = s_new
    y = _red3(s_new * tile(r), e_rep)
    yb = y + _rowsum(r * keff * rk_ref[...]) * vr
    mu = jnp.mean(yb, axis=1, keepdims=True)
    dlt = yb - mu
    var = jnp.mean(dlt * dlt, axis=1, keepdims=True)
    yrw_ref[...] = (dlt * lax.rsqrt(var + GN_EPS) * lnw_ref[...] + lnb_ref[...]) * g_ref[...]


def _red3(y, m01):
    h1, h2, h3 = _split3(y)
    d = lambda h: lax.dot_general(h, m01, (((1,), (1,)), ((), ())), preferred_element_type=F32)
    return d(h1) + d(h2) + d(h3)


def _sstep(rows_in, p):
    n_rows = rows_in["q"].shape[0]
    flat = HEAD_DIM * HEAD_DIM
    blk = lambda w: pl.BlockSpec((RB, w), lambda i: (i, 0))
    cst = lambda r, w: pl.BlockSpec((r, w), lambda i: (0, 0))
    names = ["q", "k", "v", "o", "ipre", "fpre", "ib", "fb", "m0", "n0", "C0",
             "r", "kr", "vr", "wlog", "a", "g", "S0"]
    widths = [64, 64, 64, 64, 1, 1, None, None, 1, 64, flat, 64, 64, 64, 64, 64, 64, flat]
    in_specs, args = [], []
    for nm, w in zip(names, widths):
        if w is None:
            in_specs.append(cst(RB, 1))
        else:
            in_specs.append(blk(w))
        args.append(rows_in[nm])
    for nm in ["mlnorm_rows", "kk_rows", "ka_rows", "rk_rows", "lnw_rows", "lnb_rows"]:
        in_specs.append(cst(RB, HEAD_DIM))
        args.append(p[nm])
    in_specs += [cst(HEAD_DIM, flat), cst(HEAD_DIM, flat)]
    args += [p["e_rep"], p["e_tile"]]
    return pl.pallas_call(
        _sstep_kernel,
        out_shape=(
            jax.ShapeDtypeStruct((n_rows, flat), F32),
            jax.ShapeDtypeStruct((n_rows, HEAD_DIM), F32),
            jax.ShapeDtypeStruct((n_rows, 1), F32),
            jax.ShapeDtypeStruct((n_rows, flat), F32),
            jax.ShapeDtypeStruct((n_rows, HEAD_DIM), F32),
            jax.ShapeDtypeStruct((n_rows, HEAD_DIM), F32),
        ),
        grid=(n_rows // RB,),
        in_specs=in_specs,
        out_specs=(blk(flat), blk(HEAD_DIM), blk(1), blk(flat), blk(HEAD_DIM), blk(HEAD_DIM)),
        compiler_params=pltpu.CompilerParams(
            dimension_semantics=("arbitrary",), vmem_limit_bytes=VMEM_LIMIT),
        name="sstep",
    )(*args)


def _layer_params(l, norm_mix_w, w_in, mlstm_conv_w, mlstm_conv_b, mlstm_i_b, mlstm_f_b, mlstm_norm_w,
                  rw_mu, rw_w0, rw_w_up, rw_a0, rw_a_up, rw_g_up, rw_k_k, rw_k_a, rw_r_k, rw_ln_w, rw_ln_b,
                  w_out, norm_mlp_w, mlp_up, mlp_down):
    ml_in = 4 * GROUP_W
    wi = w_in[l]
    zpad = jnp.zeros((D_MODEL, LANES - N_HEADS), F32)
    w_in_p = jnp.concatenate(
        [wi[:, :ml_in], wi[:, ml_in:ml_in + N_HEADS], zpad,
         wi[:, ml_in + N_HEADS:ml_in + 2 * N_HEADS], zpad, wi[:, ml_in + 2 * N_HEADS:]], axis=1)
    gate_row = lambda b: jnp.concatenate([b, jnp.zeros((LANES - N_HEADS,), F32)])[None, :]
    zl = jnp.zeros((D_LORA, GROUP_W), F32)
    rows = lambda v: jnp.tile(v.reshape(N_HEADS, HEAD_DIM), (RB // N_HEADS, 1))
    flat_idx = jnp.arange(HEAD_DIM * HEAD_DIM)
    one_hot = lambda idx: (jnp.arange(HEAD_DIM)[:, None] == idx[None, :]).astype(BF16)
    return {
        "norm_mix_w": norm_mix_w[l][None, :],
        "w_in_p": w_in_p.astype(BF16),
        "conv_w": mlstm_conv_w[l], "conv_b": mlstm_conv_b[l][None, :],
        "ib_row": gate_row(mlstm_i_b[l]), "fb_row": gate_row(mlstm_f_b[l]),
        "ib_rows": jnp.tile(mlstm_i_b[l], RB // N_HEADS)[:, None],
        "fb_rows": jnp.tile(mlstm_f_b[l], RB // N_HEADS)[:, None],
        "mlnorm_w": mlstm_norm_w[l][None, :], "mlnorm_rows": rows(mlstm_norm_w[l]),
        "mu": rw_mu[l][None, :], "w0": rw_w0[l][None, :], "a0": rw_a0[l][None, :],
        "wup_p": jnp.concatenate([rw_w_up[l], zl], axis=0).astype(BF16),
        "aup_p": jnp.concatenate([zl, rw_a_up[l]], axis=0).astype(BF16),
        "gup": rw_g_up[l].astype(BF16),
        "k_k": rw_k_k[l][None, :], "k_a": rw_k_a[l][None, :], "r_k": rw_r_k[l].reshape(1, GROUP_W),
        "ln_w": rw_ln_w[l][None, :], "ln_b": rw_ln_b[l][None, :],
        "kk_rows": rows(rw_k_k[l]), "ka_rows": rows(rw_k_a[l]), "rk_rows": rows(rw_r_k[l]),
        "lnw_rows": rows(rw_ln_w[l]), "lnb_rows": rows(rw_ln_b[l]),
        "e_rep": one_hot(flat_idx // HEAD_DIM), "e_tile": one_hot(flat_idx % HEAD_DIM),
        "w_out": w_out[l].astype(BF16), "norm_mlp_w": norm_mlp_w[l][None, :],
        "mlp_up": mlp_up[l].astype(BF16), "mlp_down": mlp_down[l].astype(BF16),
    }


def _unpair(st):
    d = HEAD_DIM
    return jnp.stack([st[:, h // 2, (h % 2) * d:(h % 2 + 1) * d, (h % 2) * d:(h % 2 + 1) * d]
                      for h in range(N_HEADS)], axis=1)


def _prompt_layer(x, p, norm_f_w):
    bsz, seq, _ = x.shape
    x2d = x.reshape(bsz * seq, D_MODEL)
    proj = _inproj(x2d, p["norm_mix_w"], p["w_in_p"]).reshape(bsz, seq, IN_WP)
    mix, c_st, n_st, m_st, s_st = _mixer(proj, p)
    y = _outmlp(x2d, mix.reshape(bsz * seq, D_MODEL), p["w_out"], p["norm_mlp_w"], p["mlp_up"],
                p["mlp_down"], norm_f_w)
    c_new = _unpair(c_st)
    n_new = n_st[:, :PAIRS, :].reshape(bsz, N_HEADS, HEAD_DIM)
    m_new = m_st[:, :, 0]
    conv_new = proj[:, seq - (CONV_W - 1):, Q0:Q0 + 2 * GROUP_W]
    s_new = _unpair(s_st)
    shift_new = proj[:, seq - 1:, RW0:RW0 + RW_IN_W]
    return y.reshape(bsz, seq, D_MODEL), (c_new, n_new, m_new, conv_new, s_new, shift_new)


def _sample_layer(x, c0, n0, m0, conv0, s0, shift0, p, norm_f_w):
    bsz = x.shape[0]
    n_rows = bsz * N_HEADS
    flat = HEAD_DIM * HEAD_DIM
    x2d = x.reshape(bsz, D_MODEL)
    proj = _inproj(x2d, p["norm_mix_w"], p["w_in_p"])
    qk, rs, wag = _sprep(proj, jnp.moveaxis(conv0, 1, 0), shift0[:, 0, :], p)
    hr = lambda a2d: a2d.reshape(n_rows, HEAD_DIM)
    rows_in = {
        "q": hr(qk[:, :GROUP_W]), "k": hr(qk[:, GROUP_W:]), "v": hr(proj[:, V0:V0 + GROUP_W]),
        "o": hr(proj[:, O0:O0 + GROUP_W]),
        "ipre": proj[:, GI0:GI0 + N_HEADS].reshape(n_rows, 1),
        "fpre": proj[:, GF0:GF0 + N_HEADS].reshape(n_rows, 1),
        "ib": p["ib_rows"], "fb": p["fb_rows"],
        "m0": m0.reshape(n_rows, 1), "n0": n0.reshape(n_rows, HEAD_DIM), "C0": c0.reshape(n_rows, flat),
        "r": hr(rs[:, RS_R:RS_R + GROUP_W]), "kr": hr(rs[:, RS_K:RS_K + GROUP_W]),
        "vr": hr(rs[:, RS_V:RS_V + GROUP_W]),
        "wlog": hr(wag[:, :GROUP_W]), "a": hr(wag[:, GROUP_W:2 * GROUP_W]), "g": hr(wag[:, 2 * GROUP_W:]),
        "S0": s0.reshape(n_rows, flat),
    }
    c_new, n_new, m_new, s_new, hml, yrw = _sstep(rows_in, p)
    mix = jnp.concatenate([hml.reshape(bsz, GROUP_W), yrw.reshape(bsz, GROUP_W)], axis=1)
    y = _outmlp(x2d, mix, p["w_out"], p["norm_mlp_w"], p["mlp_up"], p["mlp_down"], norm_f_w)
    conv_new = jnp.concatenate([conv0[:, 1:], proj[:, None, Q0:Q0 + 2 * GROUP_W]], axis=1)
    shift_new = proj[:, None, RW0:RW0 + RW_IN_W]
    return y.reshape(bsz, 1, D_MODEL), (
        c_new.reshape(bsz, N_HEADS, HEAD_DIM, HEAD_DIM), n_new.reshape(bsz, N_HEADS, HEAD_DIM),
        m_new.reshape(bsz, N_HEADS), conv_new, s_new.reshape(bsz, N_HEADS, HEAD_DIM, HEAD_DIM), shift_new)


def kernel(x_prompt, x_sample, state_mlstm_C, state_mlstm_n, state_mlstm_m, state_mlstm_conv, state_rwkv_S, state_rwkv_shift, norm_mix_w, w_in, mlstm_conv_w, mlstm_conv_b, mlstm_i_b, mlstm_f_b, mlstm_norm_w, rw_mu, rw_w0, rw_w_up, rw_a0, rw_a_up, rw_g_up, rw_k_k, rw_k_a, rw_r_k, rw_ln_w, rw_ln_b, w_out, norm_mlp_w, mlp_up, mlp_down, norm_f_w):
    depth = w_in.shape[0]
    assert depth == 1, "the final RMSNorm is fused into the layer's output kernel"
    weights = (norm_mix_w, w_in, mlstm_conv_w, mlstm_conv_b, mlstm_i_b, mlstm_f_b, mlstm_norm_w,
               rw_mu, rw_w0, rw_w_up, rw_a0, rw_a_up, rw_g_up, rw_k_k, rw_k_a, rw_r_k, rw_ln_w, rw_ln_b,
               w_out, norm_mlp_w, mlp_up, mlp_down)
    nfw = norm_f_w[None, :]
    p = _layer_params(0, *weights)
    y_p, st_p = _prompt_layer(x_prompt, p, nfw)
    y_s, st_s = _sample_layer(x_sample, state_mlstm_C[0], state_mlstm_n[0], state_mlstm_m[0],
                              state_mlstm_conv[0], state_rwkv_S[0], state_rwkv_shift[0], p, nfw)
    lead = lambda t: tuple(a[None] for a in t)
    return (y_p, y_s) + lead(st_p) + lead(st_s)
```

```python
import functools
import math

import jax
import jax.numpy as jnp
from jax import lax
from jax.experimental import pallas as pl
from jax.experimental.pallas import tpu as pltpu

F32 = jnp.float32
BF16 = jnp.bfloat16

D_MODEL = 1024
HEAD_DIM = 64
N_HEADS = 8
PAIRS = N_HEADS // 2
GROUP_W = N_HEADS * HEAD_DIM
CONV_W = 4
D_LORA = 64
D_GATE = 128
D_FF = 4 * D_MODEL
EPS = 1e-6
GN_EPS = 64e-5
LANES = 128
SUBLANES = 8

Q0, K0, V0, O0 = 0, GROUP_W, 2 * GROUP_W, 3 * GROUP_W
GI0 = 4 * GROUP_W
GF0 = GI0 + LANES
RW0 = GF0 + LANES
RW_IN_W = 3 * GROUP_W + 2 * D_LORA + D_GATE
IN_WP = RW0 + RW_IN_W
RS_R, RS_K, RS_V = 0, GROUP_W, 2 * GROUP_W
RS_LORA = 3 * GROUP_W
RS_G = RS_LORA + 2 * D_LORA

TB = 256
LM = 128
LR = 64
ML_EVERY = 3
_DONE = object()
RW_GROUP = 2
TM_IN = 256
TM_OUT = 256
FF_CHUNK = 1024
SGB = 64
SG = 8
NEG = -1e30
W_SCALE = math.exp(-0.5)
VMEM_LIMIT = 56 * 1024 * 1024


def _dot(a, b):
    return jnp.dot(a.astype(BF16), b.astype(BF16), preferred_element_type=F32)


def _dot_nt(a, b):
    return lax.dot_general(a.astype(BF16), b.astype(BF16), (((1,), (1,)), ((), ())),
                           preferred_element_type=F32)


def _split3(x):
    h1 = x.astype(BF16)
    r1 = x - h1.astype(F32)
    h2 = r1.astype(BF16)
    h3 = (r1 - h2.astype(F32)).astype(BF16)
    return h1, h2, h3


def _dot01_l(m01, x):
    h1, h2, h3 = _split3(x)
    d = lambda h: jnp.dot(m01, h, preferred_element_type=F32)
    return d(h1) + d(h2) + d(h3)


def _log_sigmoid(x):
    return jnp.minimum(x, 0.0) - jnp.log(1.0 + jnp.exp(-jnp.abs(x)))


def _iota(shape, dim):
    return lax.broadcasted_iota(jnp.int32, shape, dim)


def _rowsum(x):
    return jnp.sum(x, axis=1, keepdims=True)


def _pair_rowsum(x, lo):
    s_lo = _rowsum(jnp.where(lo, x, 0.0))
    s_hi = _rowsum(jnp.where(lo, 0.0, x))
    return jnp.where(lo, s_lo, s_hi)


def _stack2(x, lo):
    return jnp.concatenate([jnp.where(lo, x, 0.0), jnp.where(lo, 0.0, x)], axis=0)


def _rms(x, w):
    ms = jnp.mean(x * x, axis=-1, keepdims=True)
    return x * lax.rsqrt(ms + EPS) * w


def _inproj_kernel(x_ref, nw_ref, w_ref, o_ref):
    xn = _rms(x_ref[...], nw_ref[...])
    o_ref[...] = jnp.dot(xn.astype(BF16), w_ref[...], preferred_element_type=F32)


def _inproj(x2d, norm_w, w_in_p):
    rows = x2d.shape[0]
    tm = min(TM_IN, rows)
    return pl.pallas_call(
        _inproj_kernel,
        out_shape=jax.ShapeDtypeStruct((rows, IN_WP), F32),
        grid=(rows // tm,),
        in_specs=[
            pl.BlockSpec((tm, D_MODEL), lambda i: (i, 0)),
            pl.BlockSpec((1, D_MODEL), lambda i: (0, 0)),
            pl.BlockSpec((D_MODEL, IN_WP), lambda i: (0, 0)),
        ],
        out_specs=pl.BlockSpec((tm, IN_WP), lambda i: (i, 0)),
        compiler_params=pltpu.CompilerParams(
            dimension_semantics=("arbitrary",), vmem_limit_bytes=VMEM_LIMIT),
        name="inproj",
    )(x2d, norm_w, w_in_p)


def _outmlp_kernel(x_ref, mix_ref, wout_ref, nmw_ref, up_ref, down_ref, nfw_ref, y_ref):
    x1 = x_ref[...] + jnp.dot(mix_ref[...].astype(BF16), wout_ref[...], preferred_element_type=F32)
    xn = _rms(x1, nmw_ref[...]).astype(BF16)
    acc = x1
    for c in range(D_FF // FF_CHUNK):
        sl = slice(c * FF_CHUNK, (c + 1) * FF_CHUNK)
        hid = jnp.maximum(jnp.dot(xn, up_ref[:, sl], preferred_element_type=F32), 0.0)
        acc = acc + jnp.dot((hid * hid).astype(BF16), down_ref[sl, :], preferred_element_type=F32)
    y_ref[...] = _rms(acc, nfw_ref[...])


def _outmlp(x2d, mix2d, w_out, norm_mlp_w, mlp_up, mlp_down, norm_f_w):
    rows = x2d.shape[0]
    tm = min(TM_OUT, rows)
    const = lambda shape: pl.BlockSpec(shape, lambda i: (0, 0), pipeline_mode=pl.Buffered(1))
    return pl.pallas_call(
        _outmlp_kernel,
        out_shape=jax.ShapeDtypeStruct((rows, D_MODEL), F32),
        grid=(rows // tm,),
        in_specs=[
            pl.BlockSpec((tm, D_MODEL), lambda i: (i, 0)),
            pl.BlockSpec((tm, D_MODEL), lambda i: (i, 0)),
            const((D_MODEL, D_MODEL)),
            const((1, D_MODEL)),
            const((D_MODEL, D_FF)),
            const((D_FF, D_MODEL)),
            const((1, D_MODEL)),
        ],
        out_specs=pl.BlockSpec((tm, D_MODEL), lambda i: (i, 0)),
        compiler_params=pltpu.CompilerParams(
            dimension_semantics=("arbitrary",), vmem_limit_bytes=VMEM_LIMIT),
        name="outmlp",
    )(x2d, mix2d, w_out, norm_mlp_w, mlp_up, mlp_down, norm_f_w)


def _rwkv_token_prep(lora_t, xg, w0, wup, a0, aup, gup):
    w_raw = w0 + jnp.dot(jnp.tanh(lora_t).astype(BF16), wup, preferred_element_type=F32)
    wlog = -W_SCALE * jax.nn.sigmoid(w_raw)
    a = jax.nn.sigmoid(a0 + jnp.dot(lora_t.astype(BF16), aup, preferred_element_type=F32))
    g = jnp.dot(jax.nn.sigmoid(xg).astype(BF16), gup, preferred_element_type=F32)
    return wlog, a, g


def _mlstm_chunk(c, proj_ref, q_ref, k_ref, ib_ref, fb_ref, mlnw_ref, mix_ref, C_ref, n_ref, m_ref):
    rows = pl.ds(pl.multiple_of(c * LM, LM), LM)
    row = _iota((LM, LM), 0)
    col = _iota((LM, LM), 1)
    causal = col <= row
    tril = jnp.where(causal, 1.0, 0.0).astype(BF16)
    blockdiag = (row < HEAD_DIM) == (col < HEAD_DIM)
    lane = _iota((1, LANES), 1)
    lo = lane < HEAD_DIM

    i_tile = proj_ref[0, rows, GI0:GI0 + LANES] + ib_ref[...]
    f_tile = _log_sigmoid(proj_ref[0, rows, GF0:GF0 + LANES] + fb_ref[...])
    bc = _dot01_l(tril, f_tile)
    u = i_tile - bc
    ut = u.T
    yield

    nt = (((1,), (1,)), ((), ()))
    pairs = range(PAIRS)
    heads = range(N_HEADS)
    tile = lambda base, j: slice(base + j * LANES, base + (j + 1) * LANES)
    hmask = lambda h: lo if h % 2 == 0 else jnp.logical_not(lo)
    q_t = [q_ref[rows, tile(0, j)] for j in pairs]
    k_t = [k_ref[rows, tile(0, j)] for j in pairs]
    v_t = [proj_ref[0, rows, tile(V0, j)] for j in pairs]
    c_pair = [C_ref[0, j] for j in pairs]
    n_row = [n_ref[0, j:j + 1, :] for j in pairs]
    k_bf = [k.astype(BF16) for k in k_t]
    v_bf = [v.astype(BF16) for v in v_t]
    s = [lax.dot_general(jnp.where(hmask(h), q_t[h // 2], 0.0).astype(BF16), k_bf[h // 2], nt,
                         preferred_element_type=F32) for h in heads]
    q_c = [_dot(q_t[j], c_pair[j]) for j in pairs]
    k_tr = [k.T for k in k_t]
    yield

    m_prev_t = m_ref[0]
    ut8 = ut[0:N_HEADS, :]
    m_last_t = jnp.maximum(jnp.max(ut8, axis=1, keepdims=True), m_prev_t)
    ws_rows = jnp.exp(ut8 - m_last_t)
    dec_t = jnp.exp(m_prev_t - m_last_t)
    m_ref[0] = bc.T[0:N_HEADS, LM - 1:LM] + m_last_t

    m_prev = [m_prev_t[h:h + 1, 0:1] for h in heads]
    u_row = [ut[h:h + 1, :] for h in heads]
    bc_col = [_rowsum(jnp.where(lane == h, bc, 0.0)) for h in heads]
    m_col = [jnp.maximum(jnp.max(jnp.where(causal, u_row[h], NEG), axis=1, keepdims=True), m_prev[h])
             for h in heads]
    p = [s[h] * jnp.exp(jnp.where(causal, u_row[h] - m_col[h], NEG)) for h in heads]
    sc = [jnp.exp(m_prev[h] - m_col[h]) for h in heads]
    yield
    pv = [jnp.dot(p[h].astype(BF16), v_bf[h // 2], preferred_element_type=F32) for h in heads]
    yield
    h_parts = []
    for h in heads:
        q_n = _rowsum(jnp.where(hmask(h), q_t[h // 2] * n_row[h // 2], 0.0))
        den = _rowsum(p[h]) + sc[h] * q_n
        num = pv[h] + sc[h] * q_c[h // 2]
        h_parts.append(num / jnp.maximum(jnp.abs(den), jnp.exp(-(bc_col[h] + m_col[h]))))

    top = _iota((LM, 1), 0) < HEAD_DIM
    kw = [k_tr[j] * jnp.where(top, ws_rows[2 * j:2 * j + 1, :], ws_rows[2 * j + 1:2 * j + 2, :])
          for j in pairs]
    upd = [jnp.dot(kw[j].astype(BF16), v_bf[j], preferred_element_type=F32) for j in pairs]
    n_upd = [jnp.dot(ws_rows.astype(BF16), k_bf[j], preferred_element_type=F32) for j in pairs]
    yield
    for j in pairs:
        dec_j = jnp.where(lo, dec_t[2 * j:2 * j + 1, :], dec_t[2 * j + 1:2 * j + 2, :])
        C_ref[0, j] = c_pair[j] * dec_j + jnp.where(blockdiag, upd[j], 0.0)
        n_ref[0, j:j + 1, :] = n_row[j] * dec_j + jnp.where(lo, n_upd[j][2 * j:2 * j + 1, :],
                                                            n_upd[j][2 * j + 1:2 * j + 2, :])

    for j in pairs:
        h_til = jnp.where(lo, h_parts[2 * j], h_parts[2 * j + 1])
        hml = jax.nn.sigmoid(proj_ref[0, rows, tile(O0, j)]) * h_til
        ms = _pair_rowsum(hml * hml, lo) * (1.0 / HEAD_DIM)
        mix_ref[0, rows, tile(0, j)] = hml * lax.rsqrt(ms + EPS) * mlnw_ref[:, tile(0, j)]


def _neumann_inverse(ns):
    shape = ns[0].shape
    eye = jnp.where(_iota(shape, 0) == _iota(shape, 1), 1.0, 0.0)
    n_sq = int(math.log2(LR)) - 1
    qs = [n.astype(BF16) for n in ns]
    ts = [eye + n for n in ns]
    sq = [jnp.dot(q, q, preferred_element_type=F32) for q in qs]
    yield
    for _ in range(n_sq - 1):
        qs = [s.astype(BF16) for s in sq]
        both = [jnp.dot(q, jnp.concatenate([q, t.astype(BF16)], axis=1), preferred_element_type=F32)
                for q, t in zip(qs, ts)]
        sq = [b[:, 0:LANES] for b in both]
        ts = [t + b[:, LANES:2 * LANES] for t, b in zip(ts, both)]
        yield
    return [t + jnp.dot(s.astype(BF16), t.astype(BF16), preferred_element_type=F32) for s, t in zip(sq, ts)]


def _rwkv_chunks(c, rs_ref, w0_ref, wup_ref, a0_ref, aup_ref, gup_ref, kk_ref, ka_ref, rk_ref,
                 lnw_ref, lnb_ref, mix_ref, S_ref):
    gl = RW_GROUP * LR
    rows = pl.ds(pl.multiple_of(c * gl, gl), gl)
    lane = _iota((1, LANES), 1)
    lo = lane < HEAD_DIM
    row = _iota((LANES, LANES), 0)
    col = _iota((LANES, LANES), 1)
    same = (row < LR) == (col < LR)
    mask_sl = jnp.logical_and(same, col < row)
    mask_li = jnp.logical_and(same, col <= row)
    mask_li2 = jnp.concatenate([mask_li, mask_li], axis=1)
    grow = _iota((gl, gl), 0)
    gcol = _iota((gl, gl), 1)
    tril = jnp.where(jnp.logical_and(gcol <= grow, gcol >= (grow // LR) * LR), 1.0, 0.0).astype(BF16)
    nt = (((1,), (1,)), ((), ()))
    pairs = range(PAIRS)
    items = [(ci, j) for ci in range(RW_GROUP) for j in pairs]
    tile = lambda base, j: slice(base + j * LANES, base + (j + 1) * LANES)
    crow = lambda ci: slice(ci * LR, (ci + 1) * LR)

    wlog, a, g = _rwkv_token_prep(rs_ref[rows, RS_LORA:RS_LORA + LANES], rs_ref[rows, RS_G:RS_G + D_GATE],
                                  w0_ref[...], wup_ref[...], a0_ref[...], aup_ref[...], gup_ref[...])
    lw = _dot01_l(tril, wlog)
    e_in = jnp.exp(lw)
    e_ex = jnp.exp(lw - wlog)
    e_ng = jnp.exp(-lw)

    r_g = [rs_ref[rows, tile(RS_R, j)] for j in pairs]
    kr_g = [rs_ref[rows, tile(RS_K, j)] for j in pairs]
    v_g = [rs_ref[rows, tile(RS_V, j)] for j in pairs]
    a_g = [a[:, tile(0, j)] for j in pairs]
    kk = [kr_g[j] * kk_ref[:, tile(0, j)] for j in pairs]
    kkn = [kk[j] / jnp.maximum(jnp.sqrt(_pair_rowsum(kk[j] * kk[j], lo)), 1e-12) for j in pairs]
    keff_g = [kr_g[j] * (1.0 + (a_g[j] - 1.0) * ka_ref[:, tile(0, j)]) for j in pairs]
    alpha_g = [-kkn[j] * e_ex[:, tile(0, j)] for j in pairs]
    rb_g = [r_g[j] * e_in[:, tile(0, j)] for j in pairs]
    beta_g = [kkn[j] * a_g[j] * e_ng[:, tile(0, j)] for j in pairs]
    ktl_g = [keff_g[j] * e_ng[:, tile(0, j)] for j in pairs]
    bonus_g = [_pair_rowsum(r_g[j] * keff_g[j] * rk_ref[:, tile(0, j)], lo) * v_g[j] for j in pairs]

    lhs, rhs, v2, bk, gamma, ar = {}, {}, {}, {}, {}, {}
    for it in items:
        ci, j = it
        cr = crow(ci)
        ar[it] = jnp.concatenate([alpha_g[j][cr], rb_g[j][cr]], axis=0).astype(BF16)
        lhs[it] = jnp.concatenate([_stack2(alpha_g[j][cr], lo), _stack2(rb_g[j][cr], lo)],
                                  axis=0).astype(BF16)
        beta_c, ktl_c = beta_g[j][cr], ktl_g[j][cr]
        rhs[it] = jnp.concatenate([beta_c, beta_c, ktl_c, ktl_c], axis=0).astype(BF16)
        v2[it] = _stack2(v_g[j][cr], lo)
        bk[it] = jnp.concatenate([_stack2(beta_c, lo), _stack2(ktl_c, lo)], axis=0).astype(BF16)
        gamma[it] = e_in[(ci + 1) * LR - 1:(ci + 1) * LR, tile(0, j)]
    aa = {it: lax.dot_general(lhs[it], rhs[it], nt, preferred_element_type=F32) for it in items}
    yield
    t_list = yield from _neumann_inverse([jnp.where(mask_sl, aa[it][0:LANES, 0:LANES], 0.0) for it in items])
    t_inv = dict(zip(items, t_list))
    a_ak = {it: jnp.where(mask_sl, aa[it][0:LANES, LANES:2 * LANES], 0.0) for it in items}
    a_r = {it: jnp.where(mask_li2, aa[it][LANES:2 * LANES, :], 0.0) for it in items}
    akv = {it: _dot(a_ak[it], v2[it]) for it in items}
    yield

    s_pair = [S_ref[0, j] for j in pairs]
    for ci in range(RW_GROUP):
        its = [(ci, j) for j in pairs]
        xs = [lax.dot_general(ar[it], s_pair[it[1]].astype(BF16), nt, preferred_element_type=F32)
              for it in its]
        yield
        p2 = [_dot(t_inv[it], _stack2(xs[j][0:LR], lo) + akv[it]) for j, it in enumerate(its)]
        yield
        pv = [jnp.concatenate([p2[j], v2[it]], axis=0) for j, it in enumerate(its)]
        y2 = [_dot(a_r[it], pv[j]) for j, it in enumerate(its)]
        upd = [_dot(pv[j].T, bk[it]) for j, it in enumerate(its)]
        s_pair = [(s_pair[j] + upd[j]) * gamma[it] for j, it in enumerate(its)]
        yield
        cr = crow(ci)
        for j in pairs:
            yb = xs[j][LR:2 * LR] + y2[j][0:LR] + y2[j][LR:2 * LR] + bonus_g[j][cr]
            mu = _pair_rowsum(yb, lo) * (1.0 / HEAD_DIM)
            dlt = yb - mu
            var = _pair_rowsum(dlt * dlt, lo) * (1.0 / HEAD_DIM)
            yn = dlt * lax.rsqrt(var + GN_EPS) * lnw_ref[:, tile(0, j)] + lnb_ref[:, tile(0, j)]
            out_rows = pl.ds(pl.multiple_of(c * gl + ci * LR, LR), LR)
            mix_ref[0, out_rows, tile(GROUP_W, j)] = yn * g[cr, tile(0, j)]
    for j in pairs:
        S_ref[0, j] = s_pair[j]


def _interleave(primary, secondary, every):
    done_p = done_s = False
    k = 0
    while not (done_p and done_s):
        if not done_p:
            done_p = next(primary, _DONE) is _DONE
        k += 1
        if not done_s and (done_p or k % every == 0):
            done_s = next(secondary, _DONE) is _DONE


def _mixer_kernel(proj_ref, convw_ref, convb_ref, ib_ref, fb_ref, mlnw_ref, mu_ref, w0_ref, wup_ref,
                  a0_ref, aup_ref, gup_ref, kk_ref, ka_ref, rk_ref, lnw_ref, lnb_ref,
                  mix_ref, C_ref, n_ref, m_ref, S_ref,
                  ext_ref, rwext_ref, q_ref, k_ref, rs_ref):
    t = pl.program_id(1)

    @pl.when(t == 0)
    def _():
        C_ref[...] = jnp.zeros_like(C_ref)
        n_ref[...] = jnp.zeros_like(n_ref)
        m_ref[...] = jnp.zeros_like(m_ref)
        S_ref[...] = jnp.zeros_like(S_ref)
        ext_ref[0:SUBLANES, :] = jnp.zeros((SUBLANES, 2 * GROUP_W), F32)
        rwext_ref[0:SUBLANES, :] = jnp.zeros((SUBLANES, RW_IN_W), F32)

    @pl.when(t > 0)
    def _():
        ext_ref[0:SUBLANES, :] = ext_ref[TB:TB + SUBLANES, :]
        rwext_ref[0:SUBLANES, :] = rwext_ref[TB:TB + SUBLANES, :]

    ext_ref[SUBLANES:TB + SUBLANES, :] = proj_ref[0, :, Q0:Q0 + 2 * GROUP_W]
    acc = convb_ref[...] + ext_ref[SUBLANES:TB + SUBLANES, :] * convw_ref[CONV_W - 1:CONV_W, :]
    for jw in range(CONV_W - 1):
        off = SUBLANES - (CONV_W - 1) + jw
        acc = acc + ext_ref[off:off + TB, :] * convw_ref[jw:jw + 1, :]
    qk = acc * jax.nn.sigmoid(acc)
    q_ref[...] = qk[:, 0:GROUP_W]
    k_ref[...] = qk[:, GROUP_W:2 * GROUP_W] * (HEAD_DIM ** -0.5)

    rw = proj_ref[0, :, RW0:RW0 + RW_IN_W]
    rwext_ref[SUBLANES:TB + SUBLANES, :] = rw
    shifted = rwext_ref[SUBLANES - 1:TB + SUBLANES - 1, :]
    rs_ref[...] = rw + mu_ref[...] * (shifted - rw)

    assert RW_GROUP * LR == LM

    def body(c, carry):
        _interleave(
            _rwkv_chunks(c, rs_ref, w0_ref, wup_ref, a0_ref, aup_ref, gup_ref, kk_ref, ka_ref, rk_ref,
                         lnw_ref, lnb_ref, mix_ref, S_ref),
            _mlstm_chunk(c, proj_ref, q_ref, k_ref, ib_ref, fb_ref, mlnw_ref, mix_ref, C_ref, n_ref, m_ref),
            every=ML_EVERY)
        return carry

    lax.fori_loop(0, TB // LM, body, 0)


def _mixer(proj, p):
    bsz, seq, _ = proj.shape
    row = lambda w: pl.BlockSpec((1, w), lambda b, t: (0, 0))
    mat = lambda r, w: pl.BlockSpec((r, w), lambda b, t: (0, 0))
    state4 = pl.BlockSpec((1, PAIRS, LANES, LANES), lambda b, t: (b, 0, 0, 0))
    state3 = pl.BlockSpec((1, SUBLANES, LANES), lambda b, t: (b, 0, 0))
    return pl.pallas_call(
        _mixer_kernel,
        out_shape=(
            jax.ShapeDtypeStruct((bsz, seq, 2 * GROUP_W), F32),
            jax.ShapeDtypeStruct((bsz, PAIRS, LANES, LANES), F32),
            jax.ShapeDtypeStruct((bsz, SUBLANES, LANES), F32),
            jax.ShapeDtypeStruct((bsz, SUBLANES, LANES), F32),
            jax.ShapeDtypeStruct((bsz, PAIRS, LANES, LANES), F32),
        ),
        grid=(bsz, seq // TB),
        in_specs=[
            pl.BlockSpec((1, TB, IN_WP), lambda b, t: (b, t, 0)),
            mat(CONV_W, 2 * GROUP_W), row(2 * GROUP_W), row(LANES), row(LANES), row(GROUP_W),
            row(RW_IN_W), row(GROUP_W), mat(LANES, GROUP_W), row(GROUP_W), mat(LANES, GROUP_W),
            mat(D_GATE, GROUP_W), row(GROUP_W), row(GROUP_W), row(GROUP_W), row(GROUP_W), row(GROUP_W),
        ],
        out_specs=(
            pl.BlockSpec((1, TB, 2 * GROUP_W), lambda b, t: (b, t, 0)),
            state4, state3, state3, state4,
        ),
        scratch_shapes=[
            pltpu.VMEM((TB + SUBLANES, 2 * GROUP_W), F32),
            pltpu.VMEM((TB + SUBLANES, RW_IN_W), F32),
            pltpu.VMEM((TB, GROUP_W), F32),
            pltpu.VMEM((TB, GROUP_W), F32),
            pltpu.VMEM((TB, RW_IN_W), F32),
        ],
        compiler_params=pltpu.CompilerParams(
            dimension_semantics=("arbitrary", "arbitrary"), vmem_limit_bytes=VMEM_LIMIT),
        name="mixer",
    )(proj, p["conv_w"], p["conv_b"], p["ib_row"], p["fb_row"], p["mlnorm_w"], p["mu"], p["w0"],
      p["wup_p"], p["a0"], p["aup_p"], p["gup"], p["k_k"], p["k_a"], p["r_k"], p["ln_w"], p["ln_b"])


def _sprep_kernel(proj_ref, conv0_ref, shift0_ref, convw_ref, convb_ref, mu_ref, w0_ref, wup_ref,
                  a0_ref, aup_ref, gup_ref, qk_ref, rs_ref, wag_ref):
    acc = convb_ref[...] + proj_ref[:, Q0:Q0 + 2 * GROUP_W] * convw_ref[CONV_W - 1:CONV_W, :]
    for jw in range(CONV_W - 1):
        acc = acc + conv0_ref[jw] * convw_ref[jw:jw + 1, :]
    qk = acc * jax.nn.sigmoid(acc)
    lane = _iota((1, 2 * GROUP_W), 1)
    qk_ref[...] = jnp.where(lane < GROUP_W, qk, qk * (HEAD_DIM ** -0.5))

    rw = proj_ref[:, RW0:RW0 + RW_IN_W]
    rs = rw + mu_ref[...] * (shift0_ref[...] - rw)
    rs_ref[...] = rs
    wlog, a, g = _rwkv_token_prep(rs[:, RS_LORA:RS_LORA + LANES], rs[:, RS_G:RS_G + D_GATE],
                                  w0_ref[...], wup_ref[...], a0_ref[...], aup_ref[...], gup_ref[...])
    wag_ref[:, 0:GROUP_W] = wlog
    wag_ref[:, GROUP_W:2 * GROUP_W] = a
    wag_ref[:, 2 * GROUP_W:3 * GROUP_W] = g


def _sprep(proj_s, conv0, shift0, p):
    bsz = proj_s.shape[0]
    return pl.pallas_call(
        _sprep_kernel,
        out_shape=(
            jax.ShapeDtypeStruct((bsz, 2 * GROUP_W), F32),
            jax.ShapeDtypeStruct((bsz, RW_IN_W), F32),
            jax.ShapeDtypeStruct((bsz, 3 * GROUP_W), F32),
        ),
        compiler_params=pltpu.CompilerParams(vmem_limit_bytes=VMEM_LIMIT),
        name="sprep",
    )(proj_s, conv0, shift0, p["conv_w"], p["conv_b"], p["mu"], p["w0"], p["wup_p"], p["a0"],
      p["aup_p"], p["gup"])


(SR_Q, SR_K, SR_V, SR_O, SR_IPRE, SR_FPRE, SR_M0, SR_N0, SR_R, SR_KR, SR_VR, SR_WLOG, SR_A, SR_G,
 SR_IB, SR_FB, SR_MLNW, SR_KK, SR_KA, SR_RK, SR_LNW, SR_LNB) = range(22)
SROWS_IN = 24
SO_N, SO_M, SO_HML, SO_YRW = range(4)
SROWS_OUT = 8


def _sstep_kernel(x_ref, C0_ref, S0_ref, C_ref, S_ref, y_ref):
    eye = _iota((HEAD_DIM, HEAD_DIM), 0) == _iota((HEAD_DIM, HEAD_DIM), 1)
    lanesum = lambda t: jnp.sum(t, axis=2, keepdims=True)
    to_col = lambda r: lanesum(jnp.where(eye, r, 0.0))
    to_row = lambda c: jnp.sum(jnp.where(eye, c, 0.0), axis=1, keepdims=True)

    def body(i, carry):
        gs = pl.ds(pl.multiple_of(i * SG, SG), SG)
        row = lambda idx: x_ref[gs, idx:idx + 1, :]

        q, k, v = row(SR_Q), row(SR_K), row(SR_V)
        i_g = row(SR_IPRE) + row(SR_IB)
        logf = _log_sigmoid(row(SR_FPRE) + row(SR_FB))
        m0 = row(SR_M0)
        m_new = jnp.maximum(logf + m0, i_g)
        ws = jnp.exp(i_g - m_new)
        dec = jnp.exp(logf + m0 - m_new)
        c_new = dec * C0_ref[gs] + to_col(ws * k) * v
        n_new = dec * row(SR_N0) + ws * k
        C_ref[gs] = c_new
        num = jnp.sum(c_new * to_col(q), axis=1, keepdims=True)
        den = lanesum(q * n_new)
        h_til = num / jnp.maximum(jnp.abs(den), jnp.exp(-m_new))
        hml = jax.nn.sigmoid(row(SR_O)) * h_til
        hml = hml * lax.rsqrt(lanesum(hml * hml) * (1.0 / HEAD_DIM) + EPS) * row(SR_MLNW)

        r, kr, vr, a = row(SR_R), row(SR_KR), row(SR_VR), row(SR_A)
        kk = kr * row(SR_KK)
        kkn = kk / jnp.maximum(jnp.sqrt(lanesum(kk * kk)), 1e-12)
        keff = kr * (1.0 + (a - 1.0) * row(SR_KA))
        s0 = S0_ref[gs]
        sa = lanesum(s0 * (-kkn))
        s_new = s0 * jnp.exp(row(SR_WLOG)) + sa * (kkn * a) + to_col(vr) * keff
        S_ref[gs] = s_new
        yb = to_row(lanesum(s_new * r)) + lanesum(r * keff * row(SR_RK)) * vr
        mu = lanesum(yb) * (1.0 / HEAD_DIM)
        dlt = yb - mu
        var = lanesum(dlt * dlt) * (1.0 / HEAD_DIM)
        yrw = (dlt * lax.rsqrt(var + GN_EPS) * row(SR_LNW) + row(SR_LNB)) * row(SR_G)

        y_ref[gs, SO_N:SO_N + 1, :] = n_new
        y_ref[gs, SO_M:SO_M + 1, :] = m_new
        y_ref[gs, SO_HML:SO_HML + 1, :] = hml
        y_ref[gs, SO_YRW:SO_YRW + 1, :] = yrw
        y_ref[gs, SO_YRW + 1:SROWS_OUT, :] = jnp.zeros((SG, SROWS_OUT - SO_YRW - 1, HEAD_DIM), F32)
        return carry

    lax.fori_loop(0, SGB // SG, body, 0)


def _sstep(x_rows, c0, s0):
    n_rows = x_rows.shape[0]
    tile3 = pl.BlockSpec((SGB, HEAD_DIM, HEAD_DIM), lambda i: (i, 0, 0))
    return pl.pallas_call(
        _sstep_kernel,
        out_shape=(
            jax.ShapeDtypeStruct((n_rows, HEAD_DIM, HEAD_DIM), F32),
            jax.ShapeDtypeStruct((n_rows, HEAD_DIM, HEAD_DIM), F32),
            jax.ShapeDtypeStruct((n_rows, SROWS_OUT, HEAD_DIM), F32),
        ),
        grid=(n_rows // SGB,),
        in_specs=[pl.BlockSpec((SGB, SROWS_IN, HEAD_DIM), lambda i: (i, 0, 0)), tile3, tile3],
        out_specs=(tile3, tile3, pl.BlockSpec((SGB, SROWS_OUT, HEAD_DIM), lambda i: (i, 0, 0))),
        compiler_params=pltpu.CompilerParams(
            dimension_semantics=("arbitrary",), vmem_limit_bytes=VMEM_LIMIT),
        name="sstep",
    )(x_rows, c0, s0)


def _layer_params(l, norm_mix_w, w_in, mlstm_conv_w, mlstm_conv_b, mlstm_i_b, mlstm_f_b, mlstm_norm_w,
                  rw_mu, rw_w0, rw_w_up, rw_a0, rw_a_up, rw_g_up, rw_k_k, rw_k_a, rw_r_k, rw_ln_w, rw_ln_b,
                  w_out, norm_mlp_w, mlp_up, mlp_down):
    ml_in = 4 * GROUP_W
    wi = w_in[l].astype(BF16)
    zpad = jnp.zeros((D_MODEL, LANES - N_HEADS), BF16)
    w_in_p = jnp.concatenate(
        [wi[:, :ml_in], wi[:, ml_in:ml_in + N_HEADS], zpad,
         wi[:, ml_in + N_HEADS:ml_in + 2 * N_HEADS], zpad, wi[:, ml_in + 2 * N_HEADS:]], axis=1)
    gate_row = lambda b: jnp.concatenate([b, jnp.zeros((LANES - N_HEADS,), F32)])[None, :]
    zl = jnp.zeros((D_LORA, GROUP_W), F32)
    return {
        "norm_mix_w": norm_mix_w[l][None, :],
        "w_in_p": w_in_p,
        "conv_w": mlstm_conv_w[l], "conv_b": mlstm_conv_b[l][None, :],
        "ib_row": gate_row(mlstm_i_b[l]), "fb_row": gate_row(mlstm_f_b[l]),
        "i_b": mlstm_i_b[l], "f_b": mlstm_f_b[l],
        "mlnorm_w": mlstm_norm_w[l][None, :],
        "mu": rw_mu[l][None, :], "w0": rw_w0[l][None, :], "a0": rw_a0[l][None, :],
        "wup_p": jnp.concatenate([rw_w_up[l], zl], axis=0).astype(BF16),
        "aup_p": jnp.concatenate([zl, rw_a_up[l]], axis=0).astype(BF16),
        "gup": rw_g_up[l].astype(BF16),
        "k_k": rw_k_k[l][None, :], "k_a": rw_k_a[l][None, :], "r_k": rw_r_k[l].reshape(1, GROUP_W),
        "ln_w": rw_ln_w[l][None, :], "ln_b": rw_ln_b[l][None, :],
        "w_out": w_out[l].astype(BF16), "norm_mlp_w": norm_mlp_w[l][None, :],
        "mlp_up": mlp_up[l].astype(BF16), "mlp_down": mlp_down[l].astype(BF16),
    }


def _unpair(st):
    d = HEAD_DIM
    return jnp.stack([st[:, h // 2, (h % 2) * d:(h % 2 + 1) * d, (h % 2) * d:(h % 2 + 1) * d]
                      for h in range(N_HEADS)], axis=1)


def _prompt_layer(x, p, norm_f_w):
    bsz, seq, _ = x.shape
    x2d = x.reshape(bsz * seq, D_MODEL)
    proj = _inproj(x2d, p["norm_mix_w"], p["w_in_p"]).reshape(bsz, seq, IN_WP)
    mix, c_st, n_st, m_st, s_st = _mixer(proj, p)
    y = _outmlp(x2d, mix.reshape(bsz * seq, D_MODEL), p["w_out"], p["norm_mlp_w"], p["mlp_up"],
                p["mlp_down"], norm_f_w)
    c_new = _unpair(c_st)
    n_new = n_st[:, :PAIRS, :].reshape(bsz, N_HEADS, HEAD_DIM)
    m_new = m_st[:, :, 0]
    conv_new = proj[:, seq - (CONV_W - 1):, Q0:Q0 + 2 * GROUP_W]
    s_new = _unpair(s_st)
    shift_new = proj[:, seq - 1:, RW0:RW0 + RW_IN_W]
    return y.reshape(bsz, seq, D_MODEL), (c_new, n_new, m_new, conv_new, s_new, shift_new)


def _sample_layer(x, c0, n0, m0, conv0, s0, shift0, p, norm_f_w):
    bsz = x.shape[0]
    n_rows = bsz * N_HEADS
    x2d = x.reshape(bsz, D_MODEL)
    proj = _inproj(x2d, p["norm_mix_w"], p["w_in_p"])
    qk, rs, wag = _sprep(proj, jnp.moveaxis(conv0, 1, 0), shift0[:, 0, :], p)
    hr = lambda a2d: a2d.reshape(n_rows, HEAD_DIM)
    rep = lambda a2d: jnp.broadcast_to(a2d.reshape(n_rows, 1), (n_rows, HEAD_DIM))
    per_head = lambda v: jnp.tile(v.reshape(N_HEADS, HEAD_DIM), (bsz, 1))
    per_head_s = lambda v: jnp.broadcast_to(jnp.tile(v, bsz)[:, None], (n_rows, HEAD_DIM))
    packed = [None] * SROWS_IN
    packed[SR_Q], packed[SR_K] = hr(qk[:, :GROUP_W]), hr(qk[:, GROUP_W:])
    packed[SR_V], packed[SR_O] = hr(proj[:, V0:V0 + GROUP_W]), hr(proj[:, O0:O0 + GROUP_W])
    packed[SR_IPRE], packed[SR_FPRE] = rep(proj[:, GI0:GI0 + N_HEADS]), rep(proj[:, GF0:GF0 + N_HEADS])
    packed[SR_M0], packed[SR_N0] = rep(m0), n0.reshape(n_rows, HEAD_DIM)
    packed[SR_R], packed[SR_KR] = hr(rs[:, RS_R:RS_R + GROUP_W]), hr(rs[:, RS_K:RS_K + GROUP_W])
    packed[SR_VR] = hr(rs[:, RS_V:RS_V + GROUP_W])
    packed[SR_WLOG], packed[SR_A], packed[SR_G] = (hr(wag[:, :GROUP_W]), hr(wag[:, GROUP_W:2 * GROUP_W]),
                                                   hr(wag[:, 2 * GROUP_W:]))
    packed[SR_IB], packed[SR_FB] = per_head_s(p["i_b"]), per_head_s(p["f_b"])
    packed[SR_MLNW], packed[SR_KK], packed[SR_KA] = per_head(p["mlnorm_w"]), per_head(p["k_k"]), per_head(p["k_a"])
    packed[SR_RK], packed[SR_LNW], packed[SR_LNB] = per_head(p["r_k"]), per_head(p["ln_w"]), per_head(p["ln_b"])
    zero = jnp.zeros((n_rows, HEAD_DIM), F32)
    x_rows = jnp.stack([zero if a is None else a for a in packed], axis=1)
    c_new, s_new, y_rows = _sstep(x_rows, c0.reshape(n_rows, HEAD_DIM, HEAD_DIM),
                                  s0.reshape(n_rows, HEAD_DIM, HEAD_DIM))
    mix = jnp.concatenate([y_rows[:, SO_HML].reshape(bsz, GROUP_W), y_rows[:, SO_YRW].reshape(bsz, GROUP_W)],
                          axis=1)
    y = _outmlp(x2d, mix, p["w_out"], p["norm_mlp_w"], p["mlp_up"], p["mlp_down"], norm_f_w)
    conv_new = jnp.concatenate([conv0[:, 1:], proj[:, None, Q0:Q0 + 2 * GROUP_W]], axis=1)
    shift_new = proj[:, None, RW0:RW0 + RW_IN_W]
    return y.reshape(bsz, 1, D_MODEL), (
        c_new.reshape(bsz, N_HEADS, HEAD_DIM, HEAD_DIM), y_rows[:, SO_N].reshape(bsz, N_HEADS, HEAD_DIM),
        y_rows[:, SO_M, 0].reshape(bsz, N_HEADS), conv_new,
        s_new.reshape(bsz, N_HEADS, HEAD_DIM, HEAD_DIM), shift_new)


def kernel(x_prompt, x_sample, state_mlstm_C, state_mlstm_n, state_mlstm_m, state_mlstm_conv, state_rwkv_S, state_rwkv_shift, norm_mix_w, w_in, mlstm_conv_w, mlstm_conv_b, mlstm_i_b, mlstm_f_b, mlstm_norm_w, rw_mu, rw_w0, rw_w_up, rw_a0, rw_a_up, rw_g_up, rw_k_k, rw_k_a, rw_r_k, rw_ln_w, rw_ln_b, w_out, norm_mlp_w, mlp_up, mlp_down, norm_f_w):
    depth = w_in.shape[0]
    assert depth == 1, "the final RMSNorm is fused into the layer's output kernel"
    weights = (norm_mix_w, w_in, mlstm_conv_w, mlstm_conv_b, mlstm_i_b, mlstm_f_b, mlstm_norm_w,
               rw_mu, rw_w0, rw_w_up, rw_a0, rw_a_up, rw_g_up, rw_k_k, rw_k_a, rw_r_k, rw_ln_w, rw_ln_b,
               w_out, norm_mlp_w, mlp_up, mlp_down)
    nfw = norm_f_w[None, :]
    p = _layer_params(0, *weights)
    y_p, st_p = _prompt_layer(x_prompt, p, nfw)
    y_s, st_s = _sample_layer(x_sample, state_mlstm_C[0], state_mlstm_n[0], state_mlstm_m[0],
                              state_mlstm_conv[0], state_rwkv_S[0], state_rwkv_shift[0], p, nfw)
    lead = lambda t: tuple(a[None] for a in t)
    return (y_p, y_s) + lead(st_p) + lead(st_s)
```

```python
import functools
import math

import jax
import jax.numpy as jnp
from jax import lax
from jax.experimental import pallas as pl
from jax.experimental.pallas import tpu as pltpu

F32 = jnp.float32
BF16 = jnp.bfloat16

D_MODEL = 1024
HEAD_DIM = 64
N_HEADS = 8
PAIRS = N_HEADS // 2
GROUP_W = N_HEADS * HEAD_DIM
CONV_W = 4
D_LORA = 64
D_GATE = 128
D_FF = 4 * D_MODEL
EPS = 1e-6
GN_EPS = 64e-5
LANES = 128
SUBLANES = 8

Q0, K0, V0, O0 = 0, GROUP_W, 2 * GROUP_W, 3 * GROUP_W
GI0 = 4 * GROUP_W
GF0 = GI0 + LANES
RW0 = GF0 + LANES
RW_IN_W = 3 * GROUP_W + 2 * D_LORA + D_GATE
IN_WP = RW0 + RW_IN_W
RS_R, RS_K, RS_V = 0, GROUP_W, 2 * GROUP_W
RS_LORA = 3 * GROUP_W
RS_G = RS_LORA + 2 * D_LORA

TB = 256
LM = 128
LR = 64
ML_EVERY = 3
_DONE = object()
RW_GROUP = 2
TM_IN = 256
TM_OUT = 256
FF_CHUNK = 1024
SAMPLE_UNROLL = 8
NEG = -1e30
W_SCALE = math.exp(-0.5)
VMEM_LIMIT = 56 * 1024 * 1024


def _dot(a, b):
    return jnp.dot(a.astype(BF16), b.astype(BF16), preferred_element_type=F32)


def _dot_nt(a, b):
    return lax.dot_general(a.astype(BF16), b.astype(BF16), (((1,), (1,)), ((), ())),
                           preferred_element_type=F32)


def _split3(x):
    h1 = x.astype(BF16)
    r1 = x - h1.astype(F32)
    h2 = r1.astype(BF16)
    h3 = (r1 - h2.astype(F32)).astype(BF16)
    return h1, h2, h3


def _dot01_l(m01, x):
    h1, h2, h3 = _split3(x)
    d = lambda h: jnp.dot(m01, h, preferred_element_type=F32)
    return d(h1) + d(h2) + d(h3)


def _log_sigmoid(x):
    return jnp.minimum(x, 0.0) - jnp.log(1.0 + jnp.exp(-jnp.abs(x)))


def _iota(shape, dim):
    return lax.broadcasted_iota(jnp.int32, shape, dim)


def _rowsum(x):
    return jnp.sum(x, axis=1, keepdims=True)


def _pair_rowsum(x, lo):
    s_lo = _rowsum(jnp.where(lo, x, 0.0))
    s_hi = _rowsum(jnp.where(lo, 0.0, x))
    return jnp.where(lo, s_lo, s_hi)


def _stack2(x, lo):
    return jnp.concatenate([jnp.where(lo, x, 0.0), jnp.where(lo, 0.0, x)], axis=0)


def _rms(x, w):
    ms = jnp.mean(x * x, axis=-1, keepdims=True)
    return x * lax.rsqrt(ms + EPS) * w


def _inproj_kernel(x_ref, nw_ref, w_ref, o_ref):
    xn = _rms(x_ref[...], nw_ref[...])
    o_ref[...] = jnp.dot(xn.astype(BF16), w_ref[...], preferred_element_type=F32)


def _inproj(x2d, norm_w, w_in_p):
    rows = x2d.shape[0]
    tm = min(TM_IN, rows)
    return pl.pallas_call(
        _inproj_kernel,
        out_shape=jax.ShapeDtypeStruct((rows, IN_WP), F32),
        grid=(rows // tm,),
        in_specs=[
            pl.BlockSpec((tm, D_MODEL), lambda i: (i, 0)),
            pl.BlockSpec((1, D_MODEL), lambda i: (0, 0)),
            pl.BlockSpec((D_MODEL, IN_WP), lambda i: (0, 0)),
        ],
        out_specs=pl.BlockSpec((tm, IN_WP), lambda i: (i, 0)),
        compiler_params=pltpu.CompilerParams(
            dimension_semantics=("arbitrary",), vmem_limit_bytes=VMEM_LIMIT),
        name="inproj",
    )(x2d, norm_w, w_in_p)


def _outmlp_kernel(x_ref, mix_ref, wout_ref, nmw_ref, up_ref, down_ref, nfw_ref, y_ref):
    x1 = x_ref[...] + jnp.dot(mix_ref[...].astype(BF16), wout_ref[...], preferred_element_type=F32)
    xn = _rms(x1, nmw_ref[...]).astype(BF16)
    acc = x1
    for c in range(D_FF // FF_CHUNK):
        sl = slice(c * FF_CHUNK, (c + 1) * FF_CHUNK)
        hid = jnp.maximum(jnp.dot(xn, up_ref[:, sl], preferred_element_type=F32), 0.0)
        acc = acc + jnp.dot((hid * hid).astype(BF16), down_ref[sl, :], preferred_element_type=F32)
    y_ref[...] = _rms(acc, nfw_ref[...])


def _outmlp(x2d, mix2d, w_out, norm_mlp_w, mlp_up, mlp_down, norm_f_w):
    rows = x2d.shape[0]
    tm = min(TM_OUT, rows)
    const = lambda shape: pl.BlockSpec(shape, lambda i: (0, 0), pipeline_mode=pl.Buffered(1))
    return pl.pallas_call(
        _outmlp_kernel,
        out_shape=jax.ShapeDtypeStruct((rows, D_MODEL), F32),
        grid=(rows // tm,),
        in_specs=[
            pl.BlockSpec((tm, D_MODEL), lambda i: (i, 0)),
            pl.BlockSpec((tm, D_MODEL), lambda i: (i, 0)),
            const((D_MODEL, D_MODEL)),
            const((1, D_MODEL)),
            const((D_MODEL, D_FF)),
            const((D_FF, D_MODEL)),
            const((1, D_MODEL)),
        ],
        out_specs=pl.BlockSpec((tm, D_MODEL), lambda i: (i, 0)),
        compiler_params=pltpu.CompilerParams(
            dimension_semantics=("arbitrary",), vmem_limit_bytes=VMEM_LIMIT),
        name="outmlp",
    )(x2d, mix2d, w_out, norm_mlp_w, mlp_up, mlp_down, norm_f_w)


def _rwkv_token_prep(lora_t, xg, w0, wup, a0, aup, gup):
    w_raw = w0 + jnp.dot(jnp.tanh(lora_t).astype(BF16), wup, preferred_element_type=F32)
    wlog = -W_SCALE * jax.nn.sigmoid(w_raw)
    a = jax.nn.sigmoid(a0 + jnp.dot(lora_t.astype(BF16), aup, preferred_element_type=F32))
    g = jnp.dot(jax.nn.sigmoid(xg).astype(BF16), gup, preferred_element_type=F32)
    return wlog, a, g


def _mlstm_chunk(c, proj_ref, q_ref, k_ref, ib_ref, fb_ref, mlnw_ref, mix_ref, C_ref, n_ref, m_ref):
    rows = pl.ds(pl.multiple_of(c * LM, LM), LM)
    row = _iota((LM, LM), 0)
    col = _iota((LM, LM), 1)
    causal = col <= row
    tril = jnp.where(causal, 1.0, 0.0).astype(BF16)
    blockdiag = (row < HEAD_DIM) == (col < HEAD_DIM)
    lane = _iota((1, LANES), 1)
    lo = lane < HEAD_DIM

    i_tile = proj_ref[0, rows, GI0:GI0 + LANES] + ib_ref[...]
    f_tile = _log_sigmoid(proj_ref[0, rows, GF0:GF0 + LANES] + fb_ref[...])
    bc = _dot01_l(tril, f_tile)
    u = i_tile - bc
    ut = u.T
    yield

    nt = (((1,), (1,)), ((), ()))
    pairs = range(PAIRS)
    heads = range(N_HEADS)
    tile = lambda base, j: slice(base + j * LANES, base + (j + 1) * LANES)
    hmask = lambda h: lo if h % 2 == 0 else jnp.logical_not(lo)
    q_t = [q_ref[rows, tile(0, j)] for j in pairs]
    k_t = [k_ref[rows, tile(0, j)] for j in pairs]
    v_t = [proj_ref[0, rows, tile(V0, j)] for j in pairs]
    c_pair = [C_ref[0, j] for j in pairs]
    n_row = [n_ref[0, j:j + 1, :] for j in pairs]
    k_bf = [k.astype(BF16) for k in k_t]
    v_bf = [v.astype(BF16) for v in v_t]
    s = [lax.dot_general(jnp.where(hmask(h), q_t[h // 2], 0.0).astype(BF16), k_bf[h // 2], nt,
                         preferred_element_type=F32) for h in heads]
    q_c = [_dot(q_t[j], c_pair[j]) for j in pairs]
    k_tr = [k.T for k in k_t]
    yield

    m_prev_t = m_ref[0]
    ut8 = ut[0:N_HEADS, :]
    m_last_t = jnp.maximum(jnp.max(ut8, axis=1, keepdims=True), m_prev_t)
    ws_rows = jnp.exp(ut8 - m_last_t)
    dec_t = jnp.exp(m_prev_t - m_last_t)
    m_ref[0] = bc.T[0:N_HEADS, LM - 1:LM] + m_last_t

    m_prev = [m_prev_t[h:h + 1, 0:1] for h in heads]
    u_row = [ut[h:h + 1, :] for h in heads]
    bc_col = [_rowsum(jnp.where(lane == h, bc, 0.0)) for h in heads]
    m_col = [jnp.maximum(jnp.max(jnp.where(causal, u_row[h], NEG), axis=1, keepdims=True), m_prev[h])
             for h in heads]
    p = [s[h] * jnp.exp(jnp.where(causal, u_row[h] - m_col[h], NEG)) for h in heads]
    sc = [jnp.exp(m_prev[h] - m_col[h]) for h in heads]
    yield
    pv = [jnp.dot(p[h].astype(BF16), v_bf[h // 2], preferred_element_type=F32) for h in heads]
    yield
    h_parts = []
    for h in heads:
        q_n = _rowsum(jnp.where(hmask(h), q_t[h // 2] * n_row[h // 2], 0.0))
        den = _rowsum(p[h]) + sc[h] * q_n
        num = pv[h] + sc[h] * q_c[h // 2]
        h_parts.append(num / jnp.maximum(jnp.abs(den), jnp.exp(-(bc_col[h] + m_col[h]))))

    top = _iota((LM, 1), 0) < HEAD_DIM
    kw = [k_tr[j] * jnp.where(top, ws_rows[2 * j:2 * j + 1, :], ws_rows[2 * j + 1:2 * j + 2, :])
          for j in pairs]
    upd = [jnp.dot(kw[j].astype(BF16), v_bf[j], preferred_element_type=F32) for j in pairs]
    n_upd = [jnp.dot(ws_rows.astype(BF16), k_bf[j], preferred_element_type=F32) for j in pairs]
    yield
    for j in pairs:
        dec_j = jnp.where(lo, dec_t[2 * j:2 * j + 1, :], dec_t[2 * j + 1:2 * j + 2, :])
        C_ref[0, j] = c_pair[j] * dec_j + jnp.where(blockdiag, upd[j], 0.0)
        n_ref[0, j:j + 1, :] = n_row[j] * dec_j + jnp.where(lo, n_upd[j][2 * j:2 * j + 1, :],
                                                            n_upd[j][2 * j + 1:2 * j + 2, :])

    for j in pairs:
        h_til = jnp.where(lo, h_parts[2 * j], h_parts[2 * j + 1])
        hml = jax.nn.sigmoid(proj_ref[0, rows, tile(O0, j)]) * h_til
        ms = _pair_rowsum(hml * hml, lo) * (1.0 / HEAD_DIM)
        mix_ref[0, rows, tile(0, j)] = hml * lax.rsqrt(ms + EPS) * mlnw_ref[:, tile(0, j)]


def _neumann_inverse(ns):
    shape = ns[0].shape
    eye = jnp.where(_iota(shape, 0) == _iota(shape, 1), 1.0, 0.0)
    n_sq = int(math.log2(LR)) - 1
    qs = [n.astype(BF16) for n in ns]
    ts = [eye + n for n in ns]
    sq = [jnp.dot(q, q, preferred_element_type=F32) for q in qs]
    yield
    for _ in range(n_sq - 1):
        qs = [s.astype(BF16) for s in sq]
        both = [jnp.dot(q, jnp.concatenate([q, t.astype(BF16)], axis=1), preferred_element_type=F32)
                for q, t in zip(qs, ts)]
        sq = [b[:, 0:LANES] for b in both]
        ts = [t + b[:, LANES:2 * LANES] for t, b in zip(ts, both)]
        yield
    return [t + jnp.dot(s.astype(BF16), t.astype(BF16), preferred_element_type=F32) for s, t in zip(sq, ts)]


def _rwkv_chunks(c, rs_ref, w0_ref, wup_ref, a0_ref, aup_ref, gup_ref, kk_ref, ka_ref, rk_ref,
                 lnw_ref, lnb_ref, mix_ref, S_ref):
    gl = RW_GROUP * LR
    rows = pl.ds(pl.multiple_of(c * gl, gl), gl)
    lane = _iota((1, LANES), 1)
    lo = lane < HEAD_DIM
    row = _iota((LANES, LANES), 0)
    col = _iota((LANES, LANES), 1)
    same = (row < LR) == (col < LR)
    mask_sl = jnp.logical_and(same, col < row)
    mask_li = jnp.logical_and(same, col <= row)
    mask_li2 = jnp.concatenate([mask_li, mask_li], axis=1)
    grow = _iota((gl, gl), 0)
    gcol = _iota((gl, gl), 1)
    tril = jnp.where(jnp.logical_and(gcol <= grow, gcol >= (grow // LR) * LR), 1.0, 0.0).astype(BF16)
    nt = (((1,), (1,)), ((), ()))
    pairs = range(PAIRS)
    items = [(ci, j) for ci in range(RW_GROUP) for j in pairs]
    tile = lambda base, j: slice(base + j * LANES, base + (j + 1) * LANES)
    crow = lambda ci: slice(ci * LR, (ci + 1) * LR)

    wlog, a, g = _rwkv_token_prep(rs_ref[rows, RS_LORA:RS_LORA + LANES], rs_ref[rows, RS_G:RS_G + D_GATE],
                                  w0_ref[...], wup_ref[...], a0_ref[...], aup_ref[...], gup_ref[...])
    lw = _dot01_l(tril, wlog)
    e_in = jnp.exp(lw)
    e_ex = jnp.exp(lw - wlog)
    e_ng = jnp.exp(-lw)

    r_g = [rs_ref[rows, tile(RS_R, j)] for j in pairs]
    kr_g = [rs_ref[rows, tile(RS_K, j)] for j in pairs]
    v_g = [rs_ref[rows, tile(RS_V, j)] for j in pairs]
    a_g = [a[:, tile(0, j)] for j in pairs]
    kk = [kr_g[j] * kk_ref[:, tile(0, j)] for j in pairs]
    kkn = [kk[j] / jnp.maximum(jnp.sqrt(_pair_rowsum(kk[j] * kk[j], lo)), 1e-12) for j in pairs]
    keff_g = [kr_g[j] * (1.0 + (a_g[j] - 1.0) * ka_ref[:, tile(0, j)]) for j in pairs]
    alpha_g = [-kkn[j] * e_ex[:, tile(0, j)] for j in pairs]
    rb_g = [r_g[j] * e_in[:, tile(0, j)] for j in pairs]
    beta_g = [kkn[j] * a_g[j] * e_ng[:, tile(0, j)] for j in pairs]
    ktl_g = [keff_g[j] * e_ng[:, tile(0, j)] for j in pairs]
    bonus_g = [_pair_rowsum(r_g[j] * keff_g[j] * rk_ref[:, tile(0, j)], lo) * v_g[j] for j in pairs]

    lhs, rhs, v2, bk, gamma, ar = {}, {}, {}, {}, {}, {}
    for it in items:
        ci, j = it
        cr = crow(ci)
        ar[it] = jnp.concatenate([alpha_g[j][cr], rb_g[j][cr]], axis=0).astype(BF16)
        lhs[it] = jnp.concatenate([_stack2(alpha_g[j][cr], lo), _stack2(rb_g[j][cr], lo)],
                                  axis=0).astype(BF16)
        beta_c, ktl_c = beta_g[j][cr], ktl_g[j][cr]
        rhs[it] = jnp.concatenate([beta_c, beta_c, ktl_c, ktl_c], axis=0).astype(BF16)
        v2[it] = _stack2(v_g[j][cr], lo)
        bk[it] = jnp.concatenate([_stack2(beta_c, lo), _stack2(ktl_c, lo)], axis=0).astype(BF16)
        gamma[it] = e_in[(ci + 1) * LR - 1:(ci + 1) * LR, tile(0, j)]
    aa = {it: lax.dot_general(lhs[it], rhs[it], nt, preferred_element_type=F32) for it in items}
    yield
    t_list = yield from _neumann_inverse([jnp.where(mask_sl, aa[it][0:LANES, 0:LANES], 0.0) for it in items])
    t_inv = dict(zip(items, t_list))
    a_ak = {it: jnp.where(mask_sl, aa[it][0:LANES, LANES:2 * LANES], 0.0) for it in items}
    a_r = {it: jnp.where(mask_li2, aa[it][LANES:2 * LANES, :], 0.0) for it in items}
    akv = {it: _dot(a_ak[it], v2[it]) for it in items}
    yield

    s_pair = [S_ref[0, j] for j in pairs]
    for ci in range(RW_GROUP):
        its = [(ci, j) for j in pairs]
        xs = [lax.dot_general(ar[it], s_pair[it[1]].astype(BF16), nt, preferred_element_type=F32)
              for it in its]
        yield
        p2 = [_dot(t_inv[it], _stack2(xs[j][0:LR], lo) + akv[it]) for j, it in enumerate(its)]
        yield
        pv = [jnp.concatenate([p2[j], v2[it]], axis=0) for j, it in enumerate(its)]
        y2 = [_dot(a_r[it], pv[j]) for j, it in enumerate(its)]
        upd = [_dot(pv[j].T, bk[it]) for j, it in enumerate(its)]
        s_pair = [(s_pair[j] + upd[j]) * gamma[it] for j, it in enumerate(its)]
        yield
        cr = crow(ci)
        for j in pairs:
            yb = xs[j][LR:2 * LR] + y2[j][0:LR] + y2[j][LR:2 * LR] + bonus_g[j][cr]
            mu = _pair_rowsum(yb, lo) * (1.0 / HEAD_DIM)
            dlt = yb - mu
            var = _pair_rowsum(dlt * dlt, lo) * (1.0 / HEAD_DIM)
            yn = dlt * lax.rsqrt(var + GN_EPS) * lnw_ref[:, tile(0, j)] + lnb_ref[:, tile(0, j)]
            out_rows = pl.ds(pl.multiple_of(c * gl + ci * LR, LR), LR)
            mix_ref[0, out_rows, tile(GROUP_W, j)] = yn * g[cr, tile(0, j)]
    for j in pairs:
        S_ref[0, j] = s_pair[j]


def _interleave(primary, secondary, every):
    done_p = done_s = False
    k = 0
    while not (done_p and done_s):
        if not done_p:
            done_p = next(primary, _DONE) is _DONE
        k += 1
        if not done_s and (done_p or k % every == 0):
            done_s = next(secondary, _DONE) is _DONE


def _mixer_kernel(proj_ref, convw_ref, convb_ref, ib_ref, fb_ref, mlnw_ref, mu_ref, w0_ref, wup_ref,
                  a0_ref, aup_ref, gup_ref, kk_ref, ka_ref, rk_ref, lnw_ref, lnb_ref,
                  mix_ref, C_ref, n_ref, m_ref, S_ref,
                  ext_ref, rwext_ref, q_ref, k_ref, rs_ref):
    t = pl.program_id(1)

    @pl.when(t == 0)
    def _():
        C_ref[...] = jnp.zeros_like(C_ref)
        n_ref[...] = jnp.zeros_like(n_ref)
        m_ref[...] = jnp.zeros_like(m_ref)
        S_ref[...] = jnp.zeros_like(S_ref)
        ext_ref[0:SUBLANES, :] = jnp.zeros((SUBLANES, 2 * GROUP_W), F32)
        rwext_ref[0:SUBLANES, :] = jnp.zeros((SUBLANES, RW_IN_W), F32)

    @pl.when(t > 0)
    def _():
        ext_ref[0:SUBLANES, :] = ext_ref[TB:TB + SUBLANES, :]
        rwext_ref[0:SUBLANES, :] = rwext_ref[TB:TB + SUBLANES, :]

    ext_ref[SUBLANES:TB + SUBLANES, :] = proj_ref[0, :, Q0:Q0 + 2 * GROUP_W]
    acc = convb_ref[...] + ext_ref[SUBLANES:TB + SUBLANES, :] * convw_ref[CONV_W - 1:CONV_W, :]
    for jw in range(CONV_W - 1):
        off = SUBLANES - (CONV_W - 1) + jw
        acc = acc + ext_ref[off:off + TB, :] * convw_ref[jw:jw + 1, :]
    qk = acc * jax.nn.sigmoid(acc)
    q_ref[...] = qk[:, 0:GROUP_W]
    k_ref[...] = qk[:, GROUP_W:2 * GROUP_W] * (HEAD_DIM ** -0.5)

    rw = proj_ref[0, :, RW0:RW0 + RW_IN_W]
    rwext_ref[SUBLANES:TB + SUBLANES, :] = rw
    shifted = rwext_ref[SUBLANES - 1:TB + SUBLANES - 1, :]
    rs_ref[...] = rw + mu_ref[...] * (shifted - rw)

    assert RW_GROUP * LR == LM

    def body(c, carry):
        _interleave(
            _rwkv_chunks(c, rs_ref, w0_ref, wup_ref, a0_ref, aup_ref, gup_ref, kk_ref, ka_ref, rk_ref,
                         lnw_ref, lnb_ref, mix_ref, S_ref),
            _mlstm_chunk(c, proj_ref, q_ref, k_ref, ib_ref, fb_ref, mlnw_ref, mix_ref, C_ref, n_ref, m_ref),
            every=ML_EVERY)
        return carry

    lax.fori_loop(0, TB // LM, body, 0)


def _mixer(proj, p):
    bsz, seq, _ = proj.shape
    row = lambda w: pl.BlockSpec((1, w), lambda b, t: (0, 0))
    mat = lambda r, w: pl.BlockSpec((r, w), lambda b, t: (0, 0))
    state4 = pl.BlockSpec((1, PAIRS, LANES, LANES), lambda b, t: (b, 0, 0, 0))
    state3 = pl.BlockSpec((1, SUBLANES, LANES), lambda b, t: (b, 0, 0))
    return pl.pallas_call(
        _mixer_kernel,
        out_shape=(
            jax.ShapeDtypeStruct((bsz, seq, 2 * GROUP_W), F32),
            jax.ShapeDtypeStruct((bsz, PAIRS, LANES, LANES), F32),
            jax.ShapeDtypeStruct((bsz, SUBLANES, LANES), F32),
            jax.ShapeDtypeStruct((bsz, SUBLANES, LANES), F32),
            jax.ShapeDtypeStruct((bsz, PAIRS, LANES, LANES), F32),
        ),
        grid=(bsz, seq // TB),
        in_specs=[
            pl.BlockSpec((1, TB, IN_WP), lambda b, t: (b, t, 0)),
            mat(CONV_W, 2 * GROUP_W), row(2 * GROUP_W), row(LANES), row(LANES), row(GROUP_W),
            row(RW_IN_W), row(GROUP_W), mat(LANES, GROUP_W), row(GROUP_W), mat(LANES, GROUP_W),
            mat(D_GATE, GROUP_W), row(GROUP_W), row(GROUP_W), row(GROUP_W), row(GROUP_W), row(GROUP_W),
        ],
        out_specs=(
            pl.BlockSpec((1, TB, 2 * GROUP_W), lambda b, t: (b, t, 0)),
            state4, state3, state3, state4,
        ),
        scratch_shapes=[
            pltpu.VMEM((TB + SUBLANES, 2 * GROUP_W), F32),
            pltpu.VMEM((TB + SUBLANES, RW_IN_W), F32),
            pltpu.VMEM((TB, GROUP_W), F32),
            pltpu.VMEM((TB, GROUP_W), F32),
            pltpu.VMEM((TB, RW_IN_W), F32),
        ],
        compiler_params=pltpu.CompilerParams(
            dimension_semantics=("arbitrary", "arbitrary"), vmem_limit_bytes=VMEM_LIMIT),
        name="mixer",
    )(proj, p["conv_w"], p["conv_b"], p["ib_row"], p["fb_row"], p["mlnorm_w"], p["mu"], p["w0"],
      p["wup_p"], p["a0"], p["aup_p"], p["gup"], p["k_k"], p["k_a"], p["r_k"], p["ln_w"], p["ln_b"])


def _inproj_t_kernel(x_ref, nw_ref, w_ref, o_ref, ot_ref):
    xn = _rms(x_ref[...], nw_ref[...])
    o = jnp.dot(xn.astype(BF16), w_ref[...], preferred_element_type=F32)
    o_ref[...] = o
    ot_ref[...] = o.T


def _inproj_t(x2d, norm_w, w_in_p):
    rows = x2d.shape[0]
    return pl.pallas_call(
        _inproj_t_kernel,
        out_shape=(jax.ShapeDtypeStruct((rows, IN_WP), F32), jax.ShapeDtypeStruct((IN_WP, rows), F32)),
        compiler_params=pltpu.CompilerParams(vmem_limit_bytes=VMEM_LIMIT),
        name="inproj_t",
    )(x2d, norm_w, w_in_p)


(PC_W0, PC_A0, PC_MLNW, PC_KK, PC_KA, PC_RK, PC_LNW, PC_LNB) = range(8)
PC_CONVB = CONV_W


def _sample_kernel(pt_ref, conv0_ref, shift0_ref, m0_ref, n0_ref, C0_ref, S0_ref,
                   p1024_ref, p512_ref, mu_ref, gb_ref, wup_ref, aup_ref, gup_ref,
                   hml_ref, yrw_ref, C_ref, n_ref, m_ref, S_ref,
                   wk_s, q_s, vr_s, y_s):
    h = pl.program_id(0)
    hrow = lambda base, n=HEAD_DIM: pl.ds(pl.multiple_of(base + h * n, n), n)
    pcol = lambda j: p512_ref[hrow(0), j:j + 1]
    colsum = lambda t: jnp.sum(t, axis=0, keepdims=True)

    def conv_silu(base):
        rows = hrow(base)
        acc = p1024_ref[rows, PC_CONVB:PC_CONVB + 1] + pt_ref[rows, :] * p1024_ref[rows, CONV_W - 1:CONV_W]
        for jw in range(CONV_W - 1):
            acc = acc + conv0_ref[jw, rows, :] * p1024_ref[rows, jw:jw + 1]
        return acc * jax.nn.sigmoid(acc)

    q = conv_silu(Q0)
    k = conv_silu(K0) * (HEAD_DIM ** -0.5)
    v = pt_ref[hrow(V0), :]
    i_g = pt_ref[pl.ds(GI0 + h, 1), :] + gb_ref[pl.ds(h, 1), 0:1]
    logf = _log_sigmoid(pt_ref[pl.ds(GF0 + h, 1), :] + gb_ref[pl.ds(h, 1), 1:2])
    m0 = m0_ref[pl.ds(h, 1), :]
    m_new = jnp.maximum(logf + m0, i_g)
    ws = jnp.exp(i_g - m_new)
    dec = jnp.exp(logf + m0 - m_new)
    wk = ws * k
    n_new = dec * n0_ref[0] + wk
    n_ref[0] = n_new
    m_ref[pl.ds(h, 1), :] = m_new
    wk_s[...] = wk
    q_s[...] = q

    def c_body(kk_, num):
        ck = dec * C0_ref[0, kk_] + wk_s[pl.ds(kk_, 1), :] * v
        C_ref[0, kk_] = ck
        return num + q_s[pl.ds(kk_, 1), :] * ck

    num = lax.fori_loop(0, HEAD_DIM, c_body, jnp.zeros((HEAD_DIM, pt_ref.shape[1]), F32), unroll=SAMPLE_UNROLL)
    den = colsum(q * n_new)
    hml = jax.nn.sigmoid(pt_ref[hrow(O0), :]) * (num / jnp.maximum(jnp.abs(den), jnp.exp(-m_new)))
    hml_ref[...] = hml * lax.rsqrt(colsum(hml * hml) * (1.0 / HEAD_DIM) + EPS) * pcol(PC_MLNW)

    def shifted(base, n=HEAD_DIM):
        rw = pt_ref[hrow(RW0 + base, n) if n == HEAD_DIM else pl.ds(RW0 + base, n), :]
        rows = hrow(base, n) if n == HEAD_DIM else pl.ds(base, n)
        return rw + mu_ref[rows, :] * (shift0_ref[rows, :] - rw)

    r, kr, vr = shifted(RS_R), shifted(RS_K), shifted(RS_V)
    lora = shifted(RS_LORA, LANES)
    xg = shifted(RS_G, D_GATE)
    hw = hrow(0)
    w_raw = pcol(PC_W0) + jnp.dot(wup_ref[hw, :], jnp.tanh(lora).astype(BF16), preferred_element_type=F32)
    w = jnp.exp(-W_SCALE * jax.nn.sigmoid(w_raw))
    a = jax.nn.sigmoid(pcol(PC_A0) + jnp.dot(aup_ref[hw, :], lora.astype(BF16), preferred_element_type=F32))
    g = jnp.dot(gup_ref[hw, :], jax.nn.sigmoid(xg).astype(BF16), preferred_element_type=F32)
    kk = kr * pcol(PC_KK)
    kkn = kk / jnp.maximum(jnp.sqrt(colsum(kk * kk)), 1e-12)
    keff = kr * (1.0 + (a - 1.0) * pcol(PC_KA))
    nkk = -kkn
    ka = kkn * a
    vr_s[...] = vr

    def s_body(v_, carry):
        sv = S0_ref[0, v_]
        sn = sv * w + colsum(sv * nkk) * ka + vr_s[pl.ds(v_, 1), :] * keff
        S_ref[0, v_] = sn
        y_s[pl.ds(v_, 1), :] = colsum(sn * r)
        return carry

    lax.fori_loop(0, HEAD_DIM, s_body, 0, unroll=SAMPLE_UNROLL)
    yb = y_s[...] + colsum(r * keff * pcol(PC_RK)) * vr
    dlt = yb - colsum(yb) * (1.0 / HEAD_DIM)
    var = colsum(dlt * dlt) * (1.0 / HEAD_DIM)
    yrw_ref[...] = (dlt * lax.rsqrt(var + GN_EPS) * pcol(PC_LNW) + pcol(PC_LNB)) * g


def _sample(proj_t, conv0_t, shift0_t, m0_t, n0_t, c0_t, s0_t, p):
    bsz = proj_t.shape[1]
    whole = lambda a: pl.BlockSpec(a.shape, lambda h: (0,) * a.ndim)
    state4 = pl.BlockSpec((1, HEAD_DIM, HEAD_DIM, bsz), lambda h: (h, 0, 0, 0))
    head2 = pl.BlockSpec((HEAD_DIM, bsz), lambda h: (h, 0))
    head3 = pl.BlockSpec((1, HEAD_DIM, bsz), lambda h: (h, 0, 0))
    consts = [p["p1024"], p["p512"], p["mu_col"], p["gate_b"], p["wup_t"], p["aup_t"], p["gup_t"]]
    return pl.pallas_call(
        _sample_kernel,
        out_shape=(
            jax.ShapeDtypeStruct((GROUP_W, bsz), F32),
            jax.ShapeDtypeStruct((GROUP_W, bsz), F32),
            jax.ShapeDtypeStruct((N_HEADS, HEAD_DIM, HEAD_DIM, bsz), F32),
            jax.ShapeDtypeStruct((N_HEADS, HEAD_DIM, bsz), F32),
            jax.ShapeDtypeStruct((N_HEADS, bsz), F32),
            jax.ShapeDtypeStruct((N_HEADS, HEAD_DIM, HEAD_DIM, bsz), F32),
        ),
        grid=(N_HEADS,),
        in_specs=[whole(proj_t), whole(conv0_t), whole(shift0_t), whole(m0_t), head3, state4, state4]
                 + [whole(c) for c in consts],
        out_specs=(head2, head2, state4, head3, pl.BlockSpec((N_HEADS, bsz), lambda h: (0, 0)), state4),
        scratch_shapes=[pltpu.VMEM((HEAD_DIM, bsz), F32) for _ in range(4)],
        compiler_params=pltpu.CompilerParams(
            dimension_semantics=("arbitrary",), vmem_limit_bytes=VMEM_LIMIT),
        name="sample",
    )(proj_t, conv0_t, shift0_t, m0_t, n0_t, c0_t, s0_t, *consts)


def _layer_params(l, norm_mix_w, w_in, mlstm_conv_w, mlstm_conv_b, mlstm_i_b, mlstm_f_b, mlstm_norm_w,
                  rw_mu, rw_w0, rw_w_up, rw_a0, rw_a_up, rw_g_up, rw_k_k, rw_k_a, rw_r_k, rw_ln_w, rw_ln_b,
                  w_out, norm_mlp_w, mlp_up, mlp_down):
    ml_in = 4 * GROUP_W
    wi = w_in[l].astype(BF16)
    zpad = jnp.zeros((D_MODEL, LANES - N_HEADS), BF16)
    w_in_p = jnp.concatenate(
        [wi[:, :ml_in], wi[:, ml_in:ml_in + N_HEADS], zpad,
         wi[:, ml_in + N_HEADS:ml_in + 2 * N_HEADS], zpad, wi[:, ml_in + 2 * N_HEADS:]], axis=1)
    gate_row = lambda b: jnp.concatenate([b, jnp.zeros((LANES - N_HEADS,), F32)])[None, :]
    zl = jnp.zeros((D_LORA, GROUP_W), F32)
    return {
        "norm_mix_w": norm_mix_w[l][None, :],
        "w_in_p": w_in_p,
        "conv_w": mlstm_conv_w[l], "conv_b": mlstm_conv_b[l][None, :],
        "ib_row": gate_row(mlstm_i_b[l]), "fb_row": gate_row(mlstm_f_b[l]),
        "gate_b": jnp.stack([mlstm_i_b[l], mlstm_f_b[l]], axis=1),
        "p1024": jnp.concatenate([mlstm_conv_w[l].T, mlstm_conv_b[l][:, None],
                                  jnp.zeros((2 * GROUP_W, SUBLANES - CONV_W - 1), F32)], axis=1),
        "p512": jnp.stack([rw_w0[l], rw_a0[l], mlstm_norm_w[l], rw_k_k[l], rw_k_a[l], rw_r_k[l].reshape(GROUP_W),
                           rw_ln_w[l], rw_ln_b[l]], axis=1),
        "mu_col": rw_mu[l][:, None],
        "mlnorm_w": mlstm_norm_w[l][None, :],
        "mu": rw_mu[l][None, :], "w0": rw_w0[l][None, :], "a0": rw_a0[l][None, :],
        "wup_p": jnp.concatenate([rw_w_up[l], zl], axis=0).astype(BF16),
        "aup_p": jnp.concatenate([zl, rw_a_up[l]], axis=0).astype(BF16),
        "gup": rw_g_up[l].astype(BF16),
        "wup_t": jnp.concatenate([rw_w_up[l], zl], axis=0).T.astype(BF16),
        "aup_t": jnp.concatenate([zl, rw_a_up[l]], axis=0).T.astype(BF16),
        "gup_t": rw_g_up[l].T.astype(BF16),
        "k_k": rw_k_k[l][None, :], "k_a": rw_k_a[l][None, :], "r_k": rw_r_k[l].reshape(1, GROUP_W),
        "ln_w": rw_ln_w[l][None, :], "ln_b": rw_ln_b[l][None, :],
        "w_out": w_out[l].astype(BF16), "norm_mlp_w": norm_mlp_w[l][None, :],
        "mlp_up": mlp_up[l].astype(BF16), "mlp_down": mlp_down[l].astype(BF16),
    }


def _unpair(st):
    d = HEAD_DIM
    return jnp.stack([st[:, h // 2, (h % 2) * d:(h % 2 + 1) * d, (h % 2) * d:(h % 2 + 1) * d]
                      for h in range(N_HEADS)], axis=1)


def _prompt_layer(x, p, norm_f_w):
    bsz, seq, _ = x.shape
    x2d = x.reshape(bsz * seq, D_MODEL)
    proj = _inproj(x2d, p["norm_mix_w"], p["w_in_p"]).reshape(bsz, seq, IN_WP)
    mix, c_st, n_st, m_st, s_st = _mixer(proj, p)
    y = _outmlp(x2d, mix.reshape(bsz * seq, D_MODEL), p["w_out"], p["norm_mlp_w"], p["mlp_up"],
                p["mlp_down"], norm_f_w)
    c_new = _unpair(c_st)
    n_new = n_st[:, :PAIRS, :].reshape(bsz, N_HEADS, HEAD_DIM)
    m_new = m_st[:, :, 0]
    conv_new = proj[:, seq - (CONV_W - 1):, Q0:Q0 + 2 * GROUP_W]
    s_new = _unpair(s_st)
    shift_new = proj[:, seq - 1:, RW0:RW0 + RW_IN_W]
    return y.reshape(bsz, seq, D_MODEL), (c_new, n_new, m_new, conv_new, s_new, shift_new)


def _sample_layer(x, c0, n0, m0, conv0, s0, shift0, p, norm_f_w):
    bsz = x.shape[0]
    x2d = x.reshape(bsz, D_MODEL)
    proj, proj_t = _inproj_t(x2d, p["norm_mix_w"], p["w_in_p"])
    hml_t, yrw_t, c_t, n_t, m_t, s_t = _sample(
        proj_t, jnp.transpose(conv0, (1, 2, 0)), shift0[:, 0, :].T, m0.T, jnp.transpose(n0, (1, 2, 0)),
        jnp.transpose(c0, (1, 2, 3, 0)), jnp.transpose(s0, (1, 2, 3, 0)), p)
    mix = jnp.concatenate([hml_t, yrw_t], axis=0).T
    y = _outmlp(x2d, mix, p["w_out"], p["norm_mlp_w"], p["mlp_up"], p["mlp_down"], norm_f_w)
    conv_new = jnp.concatenate([conv0[:, 1:], proj[:, None, Q0:Q0 + 2 * GROUP_W]], axis=1)
    shift_new = proj[:, None, RW0:RW0 + RW_IN_W]
    return y.reshape(bsz, 1, D_MODEL), (
        jnp.transpose(c_t, (3, 0, 1, 2)), jnp.transpose(n_t, (2, 0, 1)), m_t.T, conv_new,
        jnp.transpose(s_t, (3, 0, 1, 2)), shift_new)


def kernel(x_prompt, x_sample, state_mlstm_C, state_mlstm_n, state_mlstm_m, state_mlstm_conv, state_rwkv_S, state_rwkv_shift, norm_mix_w, w_in, mlstm_conv_w, mlstm_conv_b, mlstm_i_b, mlstm_f_b, mlstm_norm_w, rw_mu, rw_w0, rw_w_up, rw_a0, rw_a_up, rw_g_up, rw_k_k, rw_k_a, rw_r_k, rw_ln_w, rw_ln_b, w_out, norm_mlp_w, mlp_up, mlp_down, norm_f_w):
    depth = w_in.shape[0]
    assert depth == 1, "the final RMSNorm is fused into the layer's output kernel"
    weights = (norm_mix_w, w_in, mlstm_conv_w, mlstm_conv_b, mlstm_i_b, mlstm_f_b, mlstm_norm_w,
               rw_mu, rw_w0, rw_w_up, rw_a0, rw_a_up, rw_g_up, rw_k_k, rw_k_a, rw_r_k, rw_ln_w, rw_ln_b,
               w_out, norm_mlp_w, mlp_up, mlp_down)
    nfw = norm_f_w[None, :]
    p = _layer_params(0, *weights)
    y_p, st_p = _prompt_layer(x_prompt, p, nfw)
    y_s, st_s = _sample_layer(x_sample, state_mlstm_C[0], state_mlstm_n[0], state_mlstm_m[0],
                              state_mlstm_conv[0], state_rwkv_S[0], state_rwkv_shift[0], p, nfw)
    lead = lambda t: tuple(a[None] for a in t)
    return (y_p, y_s) + lead(st_p) + lead(st_s)
```

```python
import functools
import math

import jax
import jax.numpy as jnp
from jax import lax
from jax.experimental import pallas as pl
from jax.experimental.pallas import tpu as pltpu

F32 = jnp.float32
BF16 = jnp.bfloat16

D_MODEL = 1024
HEAD_DIM = 64
N_HEADS = 8
PAIRS = N_HEADS // 2
GROUP_W = N_HEADS * HEAD_DIM
CONV_W = 4
D_LORA = 64
D_GATE = 128
D_FF = 4 * D_MODEL
EPS = 1e-6
GN_EPS = 64e-5
LANES = 128
SUBLANES = 8

Q0, K0, V0, O0 = 0, GROUP_W, 2 * GROUP_W, 3 * GROUP_W
GI0 = 4 * GROUP_W
GF0 = GI0 + LANES
RW0 = GF0 + LANES
RW_IN_W = 3 * GROUP_W + 2 * D_LORA + D_GATE
IN_WP = RW0 + RW_IN_W
RS_R, RS_K, RS_V = 0, GROUP_W, 2 * GROUP_W
RS_LORA = 3 * GROUP_W
RS_G = RS_LORA + 2 * D_LORA

TB = 256
LM = 128
LR = 64
ML_EVERY = 3
_DONE = object()
RW_GROUP = 2
TM_IN = 512
TM_OUT = 512
FF_CHUNK = 1024
SAMPLE_UNROLL = 8
NEG = -1e30
W_SCALE = math.exp(-0.5)
VMEM_LIMIT = 56 * 1024 * 1024


def _dot(a, b):
    return jnp.dot(a.astype(BF16), b.astype(BF16), preferred_element_type=F32)


def _dot_nt(a, b):
    return lax.dot_general(a.astype(BF16), b.astype(BF16), (((1,), (1,)), ((), ())),
                           preferred_element_type=F32)


def _split3(x):
    h1 = x.astype(BF16)
    r1 = x - h1.astype(F32)
    h2 = r1.astype(BF16)
    h3 = (r1 - h2.astype(F32)).astype(BF16)
    return h1, h2, h3


def _dot01_l(m01, x):
    h1, h2, h3 = _split3(x)
    d = lambda h: jnp.dot(m01, h, preferred_element_type=F32)
    return d(h1) + d(h2) + d(h3)


def _log_sigmoid(x):
    return jnp.minimum(x, 0.0) - jnp.log(1.0 + jnp.exp(-jnp.abs(x)))


def _iota(shape, dim):
    return lax.broadcasted_iota(jnp.int32, shape, dim)


def _rowsum(x):
    return jnp.sum(x, axis=1, keepdims=True)


def _pair_rowsum(x, lo):
    s_lo = _rowsum(jnp.where(lo, x, 0.0))
    s_hi = _rowsum(jnp.where(lo, 0.0, x))
    return jnp.where(lo, s_lo, s_hi)


def _stack2(x, lo):
    return jnp.concatenate([jnp.where(lo, x, 0.0), jnp.where(lo, 0.0, x)], axis=0)


def _rms(x, w):
    ms = jnp.mean(x * x, axis=-1, keepdims=True)
    return x * lax.rsqrt(ms + EPS) * w


def _inproj_kernel(x_ref, nw_ref, w_ref, o_ref):
    xn = _rms(x_ref[...], nw_ref[...])
    o_ref[...] = jnp.dot(xn.astype(BF16), w_ref[...], preferred_element_type=F32)


def _inproj(x2d, norm_w, w_in_p):
    rows = x2d.shape[0]
    tm = min(TM_IN, rows)
    return pl.pallas_call(
        _inproj_kernel,
        out_shape=jax.ShapeDtypeStruct((rows, IN_WP), F32),
        grid=(rows // tm,),
        in_specs=[
            pl.BlockSpec((tm, D_MODEL), lambda i: (i, 0)),
            pl.BlockSpec((1, D_MODEL), lambda i: (0, 0)),
            pl.BlockSpec((D_MODEL, IN_WP), lambda i: (0, 0)),
        ],
        out_specs=pl.BlockSpec((tm, IN_WP), lambda i: (i, 0)),
        compiler_params=pltpu.CompilerParams(
            dimension_semantics=("arbitrary",), vmem_limit_bytes=VMEM_LIMIT),
        name="inproj",
    )(x2d, norm_w, w_in_p)


def _outmlp_kernel(x_ref, mix_ref, wout_ref, nmw_ref, up_ref, down_ref, nfw_ref, y_ref):
    x1 = x_ref[...] + jnp.dot(mix_ref[...].astype(BF16), wout_ref[...], preferred_element_type=F32)
    xn = _rms(x1, nmw_ref[...]).astype(BF16)
    acc = x1
    for c in range(D_FF // FF_CHUNK):
        sl = slice(c * FF_CHUNK, (c + 1) * FF_CHUNK)
        hid = jnp.maximum(jnp.dot(xn, up_ref[:, sl], preferred_element_type=F32), 0.0)
        acc = acc + jnp.dot((hid * hid).astype(BF16), down_ref[sl, :], preferred_element_type=F32)
    y_ref[...] = _rms(acc, nfw_ref[...])


def _outmlp(x2d, mix2d, w_out, norm_mlp_w, mlp_up, mlp_down, norm_f_w):
    rows = x2d.shape[0]
    tm = min(TM_OUT, rows)
    const = lambda shape: pl.BlockSpec(shape, lambda i: (0, 0), pipeline_mode=pl.Buffered(1))
    return pl.pallas_call(
        _outmlp_kernel,
        out_shape=jax.ShapeDtypeStruct((rows, D_MODEL), F32),
        grid=(rows // tm,),
        in_specs=[
            pl.BlockSpec((tm, D_MODEL), lambda i: (i, 0)),
            pl.BlockSpec((tm, D_MODEL), lambda i: (i, 0)),
            const((D_MODEL, D_MODEL)),
            const((1, D_MODEL)),
            const((D_MODEL, D_FF)),
            const((D_FF, D_MODEL)),
            const((1, D_MODEL)),
        ],
        out_specs=pl.BlockSpec((tm, D_MODEL), lambda i: (i, 0)),
        compiler_params=pltpu.CompilerParams(
            dimension_semantics=("arbitrary",), vmem_limit_bytes=VMEM_LIMIT),
        name="outmlp",
    )(x2d, mix2d, w_out, norm_mlp_w, mlp_up, mlp_down, norm_f_w)


def _rwkv_token_prep(lora_t, xg, w0, wup, a0, aup, gup):
    w_raw = w0 + jnp.dot(jnp.tanh(lora_t).astype(BF16), wup, preferred_element_type=F32)
    wlog = -W_SCALE * jax.nn.sigmoid(w_raw)
    a = jax.nn.sigmoid(a0 + jnp.dot(lora_t.astype(BF16), aup, preferred_element_type=F32))
    g = jnp.dot(jax.nn.sigmoid(xg).astype(BF16), gup, preferred_element_type=F32)
    return wlog, a, g


def _mlstm_chunk(c, proj_ref, q_ref, k_ref, ib_ref, fb_ref, mlnw_ref, mix_ref, C_ref, n_ref, m_ref):
    rows = pl.ds(pl.multiple_of(c * LM, LM), LM)
    row = _iota((LM, LM), 0)
    col = _iota((LM, LM), 1)
    causal = col <= row
    tril = jnp.where(causal, 1.0, 0.0).astype(BF16)
    blockdiag = (row < HEAD_DIM) == (col < HEAD_DIM)
    lane = _iota((1, LANES), 1)
    lo = lane < HEAD_DIM

    i_tile = proj_ref[0, rows, GI0:GI0 + LANES] + ib_ref[...]
    f_tile = _log_sigmoid(proj_ref[0, rows, GF0:GF0 + LANES] + fb_ref[...])
    bc = _dot01_l(tril, f_tile)
    u = i_tile - bc
    ut = u.T
    yield

    nt = (((1,), (1,)), ((), ()))
    pairs = range(PAIRS)
    heads = range(N_HEADS)
    tile = lambda base, j: slice(base + j * LANES, base + (j + 1) * LANES)
    hmask = lambda h: lo if h % 2 == 0 else jnp.logical_not(lo)
    q_t = [q_ref[rows, tile(0, j)] for j in pairs]
    k_t = [k_ref[rows, tile(0, j)] for j in pairs]
    v_t = [proj_ref[0, rows, tile(V0, j)] for j in pairs]
    c_pair = [C_ref[0, j] for j in pairs]
    n_row = [n_ref[0, j:j + 1, :] for j in pairs]
    k_bf = [k.astype(BF16) for k in k_t]
    v_bf = [v.astype(BF16) for v in v_t]
    s = [lax.dot_general(jnp.where(hmask(h), q_t[h // 2], 0.0).astype(BF16), k_bf[h // 2], nt,
                         preferred_element_type=F32) for h in heads]
    q_c = [_dot(q_t[j], c_pair[j]) for j in pairs]
    k_tr = [k.T for k in k_t]
    yield

    m_prev_t = m_ref[0]
    ut8 = ut[0:N_HEADS, :]
    m_last_t = jnp.maximum(jnp.max(ut8, axis=1, keepdims=True), m_prev_t)
    ws_rows = jnp.exp(ut8 - m_last_t)
    dec_t = jnp.exp(m_prev_t - m_last_t)
    m_ref[0] = bc.T[0:N_HEADS, LM - 1:LM] + m_last_t

    m_prev = [m_prev_t[h:h + 1, 0:1] for h in heads]
    u_row = [ut[h:h + 1, :] for h in heads]
    bc_col = [_rowsum(jnp.where(lane == h, bc, 0.0)) for h in heads]
    m_col = [jnp.maximum(jnp.max(jnp.where(causal, u_row[h], NEG), axis=1, keepdims=True), m_prev[h])
             for h in heads]
    p = [s[h] * jnp.exp(jnp.where(causal, u_row[h] - m_col[h], NEG)) for h in heads]
    sc = [jnp.exp(m_prev[h] - m_col[h]) for h in heads]
    yield
    pv = [jnp.dot(p[h].astype(BF16), v_bf[h // 2], preferred_element_type=F32) for h in heads]
    yield
    h_parts = []
    for h in heads:
        q_n = _rowsum(jnp.where(hmask(h), q_t[h // 2] * n_row[h // 2], 0.0))
        den = _rowsum(p[h]) + sc[h] * q_n
        num = pv[h] + sc[h] * q_c[h // 2]
        h_parts.append(num / jnp.maximum(jnp.abs(den), jnp.exp(-(bc_col[h] + m_col[h]))))

    top = _iota((LM, 1), 0) < HEAD_DIM
    kw = [k_tr[j] * jnp.where(top, ws_rows[2 * j:2 * j + 1, :], ws_rows[2 * j + 1:2 * j + 2, :])
          for j in pairs]
    upd = [jnp.dot(kw[j].astype(BF16), v_bf[j], preferred_element_type=F32) for j in pairs]
    n_upd = [jnp.dot(ws_rows.astype(BF16), k_bf[j], preferred_element_type=F32) for j in pairs]
    yield
    for j in pairs:
        dec_j = jnp.where(lo, dec_t[2 * j:2 * j + 1, :], dec_t[2 * j + 1:2 * j + 2, :])
        C_ref[0, j] = c_pair[j] * dec_j + jnp.where(blockdiag, upd[j], 0.0)
        n_ref[0, j:j + 1, :] = n_row[j] * dec_j + jnp.where(lo, n_upd[j][2 * j:2 * j + 1, :],
                                                            n_upd[j][2 * j + 1:2 * j + 2, :])

    for j in pairs:
        h_til = jnp.where(lo, h_parts[2 * j], h_parts[2 * j + 1])
        hml = jax.nn.sigmoid(proj_ref[0, rows, tile(O0, j)]) * h_til
        ms = _pair_rowsum(hml * hml, lo) * (1.0 / HEAD_DIM)
        mix_ref[0, rows, tile(0, j)] = hml * lax.rsqrt(ms + EPS) * mlnw_ref[:, tile(0, j)]


def _neumann_inverse(ns):
    shape = ns[0].shape
    eye = jnp.where(_iota(shape, 0) == _iota(shape, 1), 1.0, 0.0)
    n_sq = int(math.log2(LR)) - 1
    qs = [n.astype(BF16) for n in ns]
    ts = [eye + n for n in ns]
    sq = [jnp.dot(q, q, preferred_element_type=F32) for q in qs]
    yield
    for _ in range(n_sq - 1):
        qs = [s.astype(BF16) for s in sq]
        both = [jnp.dot(q, jnp.concatenate([q, t.astype(BF16)], axis=1), preferred_element_type=F32)
                for q, t in zip(qs, ts)]
        sq = [b[:, 0:LANES] for b in both]
        ts = [t + b[:, LANES:2 * LANES] for t, b in zip(ts, both)]
        yield
    return [t + jnp.dot(s.astype(BF16), t.astype(BF16), preferred_element_type=F32) for s, t in zip(sq, ts)]


def _rwkv_chunks(c, rs_ref, w0_ref, wup_ref, a0_ref, aup_ref, gup_ref, kk_ref, ka_ref, rk_ref,
                 lnw_ref, lnb_ref, mix_ref, S_ref):
    gl = RW_GROUP * LR
    rows = pl.ds(pl.multiple_of(c * gl, gl), gl)
    lane = _iota((1, LANES), 1)
    lo = lane < HEAD_DIM
    row = _iota((LANES, LANES), 0)
    col = _iota((LANES, LANES), 1)
    same = (row < LR) == (col < LR)
    mask_sl = jnp.logical_and(same, col < row)
    mask_li = jnp.logical_and(same, col <= row)
    mask_li2 = jnp.concatenate([mask_li, mask_li], axis=1)
    grow = _iota((gl, gl), 0)
    gcol = _iota((gl, gl), 1)
    tril = jnp.where(jnp.logical_and(gcol <= grow, gcol >= (grow // LR) * LR), 1.0, 0.0).astype(BF16)
    nt = (((1,), (1,)), ((), ()))
    pairs = range(PAIRS)
    items = [(ci, j) for ci in range(RW_GROUP) for j in pairs]
    tile = lambda base, j: slice(base + j * LANES, base + (j + 1) * LANES)
    crow = lambda ci: slice(ci * LR, (ci + 1) * LR)

    wlog, a, g = _rwkv_token_prep(rs_ref[rows, RS_LORA:RS_LORA + LANES], rs_ref[rows, RS_G:RS_G + D_GATE],
                                  w0_ref[...], wup_ref[...], a0_ref[...], aup_ref[...], gup_ref[...])
    lw = _dot01_l(tril, wlog)
    e_in = jnp.exp(lw)
    e_ex = jnp.exp(lw - wlog)
    e_ng = jnp.exp(-lw)

    r_g = [rs_ref[rows, tile(RS_R, j)] for j in pairs]
    kr_g = [rs_ref[rows, tile(RS_K, j)] for j in pairs]
    v_g = [rs_ref[rows, tile(RS_V, j)] for j in pairs]
    a_g = [a[:, tile(0, j)] for j in pairs]
    kk = [kr_g[j] * kk_ref[:, tile(0, j)] for j in pairs]
    kkn = [kk[j] / jnp.maximum(jnp.sqrt(_pair_rowsum(kk[j] * kk[j], lo)), 1e-12) for j in pairs]
    keff_g = [kr_g[j] * (1.0 + (a_g[j] - 1.0) * ka_ref[:, tile(0, j)]) for j in pairs]
    alpha_g = [-kkn[j] * e_ex[:, tile(0, j)] for j in pairs]
    rb_g = [r_g[j] * e_in[:, tile(0, j)] for j in pairs]
    beta_g = [kkn[j] * a_g[j] * e_ng[:, tile(0, j)] for j in pairs]
    ktl_g = [keff_g[j] * e_ng[:, tile(0, j)] for j in pairs]
    bonus_g = [_pair_rowsum(r_g[j] * keff_g[j] * rk_ref[:, tile(0, j)], lo) * v_g[j] for j in pairs]

    lhs, rhs, v2, bk, gamma, ar = {}, {}, {}, {}, {}, {}
    for it in items:
        ci, j = it
        cr = crow(ci)
        ar[it] = jnp.concatenate([alpha_g[j][cr], rb_g[j][cr]], axis=0).astype(BF16)
        lhs[it] = jnp.concatenate([_stack2(alpha_g[j][cr], lo), _stack2(rb_g[j][cr], lo)],
                                  axis=0).astype(BF16)
        beta_c, ktl_c = beta_g[j][cr], ktl_g[j][cr]
        rhs[it] = jnp.concatenate([beta_c, beta_c, ktl_c, ktl_c], axis=0).astype(BF16)
        v2[it] = _stack2(v_g[j][cr], lo)
        bk[it] = jnp.concatenate([_stack2(beta_c, lo), _stack2(ktl_c, lo)], axis=0).astype(BF16)
        gamma[it] = e_in[(ci + 1) * LR - 1:(ci + 1) * LR, tile(0, j)]
    aa = {it: lax.dot_general(lhs[it], rhs[it], nt, preferred_element_type=F32) for it in items}
    yield
    t_list = yield from _neumann_inverse([jnp.where(mask_sl, aa[it][0:LANES, 0:LANES], 0.0) for it in items])
    t_inv = dict(zip(items, t_list))
    a_ak = {it: jnp.where(mask_sl, aa[it][0:LANES, LANES:2 * LANES], 0.0) for it in items}
    a_r = {it: jnp.where(mask_li2, aa[it][LANES:2 * LANES, :], 0.0) for it in items}
    akv = {it: _dot(a_ak[it], v2[it]) for it in items}
    yield

    s_pair = [S_ref[0, j] for j in pairs]
    for ci in range(RW_GROUP):
        its = [(ci, j) for j in pairs]
        xs = [lax.dot_general(ar[it], s_pair[it[1]].astype(BF16), nt, preferred_element_type=F32)
              for it in its]
        yield
        p2 = [_dot(t_inv[it], _stack2(xs[j][0:LR], lo) + akv[it]) for j, it in enumerate(its)]
        yield
        pv = [jnp.concatenate([p2[j], v2[it]], axis=0) for j, it in enumerate(its)]
        y2 = [_dot(a_r[it], pv[j]) for j, it in enumerate(its)]
        upd = [_dot(pv[j].T, bk[it]) for j, it in enumerate(its)]
        s_pair = [(s_pair[j] + upd[j]) * gamma[it] for j, it in enumerate(its)]
        yield
        cr = crow(ci)
        for j in pairs:
            yb = xs[j][LR:2 * LR] + y2[j][0:LR] + y2[j][LR:2 * LR] + bonus_g[j][cr]
            mu = _pair_rowsum(yb, lo) * (1.0 / HEAD_DIM)
            dlt = yb - mu
            var = _pair_rowsum(dlt * dlt, lo) * (1.0 / HEAD_DIM)
            yn = dlt * lax.rsqrt(var + GN_EPS) * lnw_ref[:, tile(0, j)] + lnb_ref[:, tile(0, j)]
            out_rows = pl.ds(pl.multiple_of(c * gl + ci * LR, LR), LR)
            mix_ref[0, out_rows, tile(GROUP_W, j)] = yn * g[cr, tile(0, j)]
    for j in pairs:
        S_ref[0, j] = s_pair[j]


def _shift_rows(x, tail, j):
    rolled = pltpu.roll(x, j, axis=0)
    head = jnp.where(_iota((SUBLANES, 1), 0) < j, pltpu.roll(tail, j, axis=0), rolled[0:SUBLANES])
    return jnp.concatenate([head, rolled[SUBLANES:]], axis=0)


def _interleave(primary, secondary, every):
    done_p = done_s = False
    k = 0
    while not (done_p and done_s):
        if not done_p:
            done_p = next(primary, _DONE) is _DONE
        k += 1
        if not done_s and (done_p or k % every == 0):
            done_s = next(secondary, _DONE) is _DONE


def _mixer_kernel(proj_ref, convw_ref, convb_ref, ib_ref, fb_ref, mlnw_ref, mu_ref, w0_ref, wup_ref,
                  a0_ref, aup_ref, gup_ref, kk_ref, ka_ref, rk_ref, lnw_ref, lnb_ref,
                  mix_ref, C_ref, n_ref, m_ref, S_ref,
                  qk_tail_ref, rw_tail_ref, q_ref, k_ref, rs_ref):
    t = pl.program_id(1)

    @pl.when(t == 0)
    def _():
        C_ref[...] = jnp.zeros_like(C_ref)
        n_ref[...] = jnp.zeros_like(n_ref)
        m_ref[...] = jnp.zeros_like(m_ref)
        S_ref[...] = jnp.zeros_like(S_ref)
        qk_tail_ref[...] = jnp.zeros_like(qk_tail_ref)
        rw_tail_ref[...] = jnp.zeros_like(rw_tail_ref)

    u = proj_ref[0, :, Q0:Q0 + 2 * GROUP_W]
    qk_tail = qk_tail_ref[...]
    acc = convb_ref[...] + u * convw_ref[CONV_W - 1:CONV_W, :]
    for jw in range(CONV_W - 1):
        acc = acc + _shift_rows(u, qk_tail, CONV_W - 1 - jw) * convw_ref[jw:jw + 1, :]
    qk_tail_ref[...] = u[TB - SUBLANES:TB]
    qk = acc * jax.nn.sigmoid(acc)
    q_ref[...] = qk[:, 0:GROUP_W]
    k_ref[...] = qk[:, GROUP_W:2 * GROUP_W] * (HEAD_DIM ** -0.5)

    rw = proj_ref[0, :, RW0:RW0 + RW_IN_W]
    rs_ref[...] = rw + mu_ref[...] * (_shift_rows(rw, rw_tail_ref[...], 1) - rw)
    rw_tail_ref[...] = rw[TB - SUBLANES:TB]

    assert RW_GROUP * LR == LM

    def body(c, carry):
        _interleave(
            _rwkv_chunks(c, rs_ref, w0_ref, wup_ref, a0_ref, aup_ref, gup_ref, kk_ref, ka_ref, rk_ref,
                         lnw_ref, lnb_ref, mix_ref, S_ref),
            _mlstm_chunk(c, proj_ref, q_ref, k_ref, ib_ref, fb_ref, mlnw_ref, mix_ref, C_ref, n_ref, m_ref),
            every=ML_EVERY)
        return carry

    lax.fori_loop(0, TB // LM, body, 0)


def _mixer(proj, p):
    bsz, seq, _ = proj.shape
    row = lambda w: pl.BlockSpec((1, w), lambda b, t: (0, 0))
    mat = lambda r, w: pl.BlockSpec((r, w), lambda b, t: (0, 0))
    state4 = pl.BlockSpec((1, PAIRS, LANES, LANES), lambda b, t: (b, 0, 0, 0))
    state3 = pl.BlockSpec((1, SUBLANES, LANES), lambda b, t: (b, 0, 0))
    return pl.pallas_call(
        _mixer_kernel,
        out_shape=(
            jax.ShapeDtypeStruct((bsz, seq, 2 * GROUP_W), F32),
            jax.ShapeDtypeStruct((bsz, PAIRS, LANES, LANES), F32),
            jax.ShapeDtypeStruct((bsz, SUBLANES, LANES), F32),
            jax.ShapeDtypeStruct((bsz, SUBLANES, LANES), F32),
            jax.ShapeDtypeStruct((bsz, PAIRS, LANES, LANES), F32),
        ),
        grid=(bsz, seq // TB),
        in_specs=[
            pl.BlockSpec((1, TB, IN_WP), lambda b, t: (b, t, 0)),
            mat(CONV_W, 2 * GROUP_W), row(2 * GROUP_W), row(LANES), row(LANES), row(GROUP_W),
            row(RW_IN_W), row(GROUP_W), mat(LANES, GROUP_W), row(GROUP_W), mat(LANES, GROUP_W),
            mat(D_GATE, GROUP_W), row(GROUP_W), row(GROUP_W), row(GROUP_W), row(GROUP_W), row(GROUP_W),
        ],
        out_specs=(
            pl.BlockSpec((1, TB, 2 * GROUP_W), lambda b, t: (b, t, 0)),
            state4, state3, state3, state4,
        ),
        scratch_shapes=[
            pltpu.VMEM((SUBLANES, 2 * GROUP_W), F32),
            pltpu.VMEM((SUBLANES, RW_IN_W), F32),
            pltpu.VMEM((TB, GROUP_W), F32),
            pltpu.VMEM((TB, GROUP_W), F32),
            pltpu.VMEM((TB, RW_IN_W), F32),
        ],
        compiler_params=pltpu.CompilerParams(
            dimension_semantics=("arbitrary", "arbitrary"), vmem_limit_bytes=VMEM_LIMIT),
        name="mixer",
    )(proj, p["conv_w"], p["conv_b"], p["ib_row"], p["fb_row"], p["mlnorm_w"], p["mu"], p["w0"],
      p["wup_p"], p["a0"], p["aup_p"], p["gup"], p["k_k"], p["k_a"], p["r_k"], p["ln_w"], p["ln_b"])


def _inproj_t_kernel(x_ref, nw_ref, w_ref, o_ref, ot_ref):
    xn = _rms(x_ref[...], nw_ref[...])
    o = jnp.dot(xn.astype(BF16), w_ref[...], preferred_element_type=F32)
    o_ref[...] = o
    ot_ref[...] = o.T


def _inproj_t(x2d, norm_w, w_in_p):
    rows = x2d.shape[0]
    return pl.pallas_call(
        _inproj_t_kernel,
        out_shape=(jax.ShapeDtypeStruct((rows, IN_WP), F32), jax.ShapeDtypeStruct((IN_WP, rows), F32)),
        compiler_params=pltpu.CompilerParams(vmem_limit_bytes=VMEM_LIMIT),
        name="inproj_t",
    )(x2d, norm_w, w_in_p)


(PC_W0, PC_A0, PC_MLNW, PC_KK, PC_KA, PC_RK, PC_LNW, PC_LNB) = range(8)
PC_CONVB = CONV_W


def _sample_kernel(pt_ref, conv0_ref, shift0_ref, m0_ref, n0_ref, C0_ref, S0_ref,
                   p1024_ref, p512_ref, mu_ref, gb_ref, wup_ref, aup_ref, gup_ref,
                   hml_ref, yrw_ref, C_ref, n_ref, m_ref, S_ref,
                   wk_s, q_s, vr_s, y_s):
    h = pl.program_id(0)
    hrow = lambda base, n=HEAD_DIM: pl.ds(pl.multiple_of(base + h * n, n), n)
    pcol = lambda j: p512_ref[hrow(0), j:j + 1]
    colsum = lambda t: jnp.sum(t, axis=0, keepdims=True)

    def conv_silu(base):
        rows = hrow(base)
        acc = p1024_ref[rows, PC_CONVB:PC_CONVB + 1] + pt_ref[rows, :] * p1024_ref[rows, CONV_W - 1:CONV_W]
        for jw in range(CONV_W - 1):
            acc = acc + conv0_ref[jw, rows, :] * p1024_ref[rows, jw:jw + 1]
        return acc * jax.nn.sigmoid(acc)

    q = conv_silu(Q0)
    k = conv_silu(K0) * (HEAD_DIM ** -0.5)
    v = pt_ref[hrow(V0), :]
    i_g = pt_ref[pl.ds(GI0 + h, 1), :] + gb_ref[pl.ds(h, 1), 0:1]
    logf = _log_sigmoid(pt_ref[pl.ds(GF0 + h, 1), :] + gb_ref[pl.ds(h, 1), 1:2])
    m0 = m0_ref[pl.ds(h, 1), :]
    m_new = jnp.maximum(logf + m0, i_g)
    ws = jnp.exp(i_g - m_new)
    dec = jnp.exp(logf + m0 - m_new)
    wk = ws * k
    n_new = dec * n0_ref[0] + wk
    n_ref[0] = n_new
    m_ref[pl.ds(h, 1), :] = m_new
    wk_s[...] = wk
    q_s[...] = q

    def c_body(kk_, num):
        ck = dec * C0_ref[0, kk_] + wk_s[pl.ds(kk_, 1), :] * v
        C_ref[0, kk_] = ck
        return num + q_s[pl.ds(kk_, 1), :] * ck

    num = lax.fori_loop(0, HEAD_DIM, c_body, jnp.zeros((HEAD_DIM, pt_ref.shape[1]), F32), unroll=SAMPLE_UNROLL)
    den = colsum(q * n_new)
    hml = jax.nn.sigmoid(pt_ref[hrow(O0), :]) * (num / jnp.maximum(jnp.abs(den), jnp.exp(-m_new)))
    hml_ref[...] = hml * lax.rsqrt(colsum(hml * hml) * (1.0 / HEAD_DIM) + EPS) * pcol(PC_MLNW)

    def shifted(base, n=HEAD_DIM):
        rw = pt_ref[hrow(RW0 + base, n) if n == HEAD_DIM else pl.ds(RW0 + base, n), :]
        rows = hrow(base, n) if n == HEAD_DIM else pl.ds(base, n)
        return rw + mu_ref[rows, :] * (shift0_ref[rows, :] - rw)

    r, kr, vr = shifted(RS_R), shifted(RS_K), shifted(RS_V)
    lora = shifted(RS_LORA, LANES)
    xg = shifted(RS_G, D_GATE)
    hw = hrow(0)
    w_raw = pcol(PC_W0) + jnp.dot(wup_ref[hw, :], jnp.tanh(lora).astype(BF16), preferred_element_type=F32)
    w = jnp.exp(-W_SCALE * jax.nn.sigmoid(w_raw))
    a = jax.nn.sigmoid(pcol(PC_A0) + jnp.dot(aup_ref[hw, :], lora.astype(BF16), preferred_element_type=F32))
    g = jnp.dot(gup_ref[hw, :], jax.nn.sigmoid(xg).astype(BF16), preferred_element_type=F32)
    kk = kr * pcol(PC_KK)
    kkn = kk / jnp.maximum(jnp.sqrt(colsum(kk * kk)), 1e-12)
    keff = kr * (1.0 + (a - 1.0) * pcol(PC_KA))
    nkk = -kkn
    ka = kkn * a
    vr_s[...] = vr

    def s_body(v_, carry):
        sv = S0_ref[0, v_]
        sn = sv * w + colsum(sv * nkk) * ka + vr_s[pl.ds(v_, 1), :] * keff
        S_ref[0, v_] = sn
        y_s[pl.ds(v_, 1), :] = colsum(sn * r)
        return carry

    lax.fori_loop(0, HEAD_DIM, s_body, 0, unroll=SAMPLE_UNROLL)
    yb = y_s[...] + colsum(r * keff * pcol(PC_RK)) * vr
    dlt = yb - colsum(yb) * (1.0 / HEAD_DIM)
    var = colsum(dlt * dlt) * (1.0 / HEAD_DIM)
    yrw_ref[...] = (dlt * lax.rsqrt(var + GN_EPS) * pcol(PC_LNW) + pcol(PC_LNB)) * g


def _sample(proj_t, conv0_t, shift0_t, m0_t, n0_t, c0_t, s0_t, p):
    bsz = proj_t.shape[1]
    whole = lambda a: pl.BlockSpec(a.shape, lambda h: (0,) * a.ndim)
    state4 = pl.BlockSpec((1, HEAD_DIM, HEAD_DIM, bsz), lambda h: (h, 0, 0, 0))
    head2 = pl.BlockSpec((HEAD_DIM, bsz), lambda h: (h, 0))
    head3 = pl.BlockSpec((1, HEAD_DIM, bsz), lambda h: (h, 0, 0))
    consts = [p["p1024"], p["p512"], p["mu_col"], p["gate_b"], p["wup_t"], p["aup_t"], p["gup_t"]]
    return pl.pallas_call(
        _sample_kernel,
        out_shape=(
            jax.ShapeDtypeStruct((GROUP_W, bsz), F32),
            jax.ShapeDtypeStruct((GROUP_W, bsz), F32),
            jax.ShapeDtypeStruct((N_HEADS, HEAD_DIM, HEAD_DIM, bsz), F32),
            jax.ShapeDtypeStruct((N_HEADS, HEAD_DIM, bsz), F32),
            jax.ShapeDtypeStruct((N_HEADS, bsz), F32),
            jax.ShapeDtypeStruct((N_HEADS, HEAD_DIM, HEAD_DIM, bsz), F32),
        ),
        grid=(N_HEADS,),
        in_specs=[whole(proj_t), whole(conv0_t), whole(shift0_t), whole(m0_t), head3, state4, state4]
                 + [whole(c) for c in consts],
        out_specs=(head2, head2, state4, head3, pl.BlockSpec((N_HEADS, bsz), lambda h: (0, 0)), state4),
        scratch_shapes=[pltpu.VMEM((HEAD_DIM, bsz), F32) for _ in range(4)],
        compiler_params=pltpu.CompilerParams(
            dimension_semantics=("arbitrary",), vmem_limit_bytes=VMEM_LIMIT),
        name="sample",
    )(proj_t, conv0_t, shift0_t, m0_t, n0_t, c0_t, s0_t, *consts)


def _layer_params(l, norm_mix_w, w_in, mlstm_conv_w, mlstm_conv_b, mlstm_i_b, mlstm_f_b, mlstm_norm_w,
                  rw_mu, rw_w0, rw_w_up, rw_a0, rw_a_up, rw_g_up, rw_k_k, rw_k_a, rw_r_k, rw_ln_w, rw_ln_b,
                  w_out, norm_mlp_w, mlp_up, mlp_down):
    ml_in = 4 * GROUP_W
    wi = w_in[l].astype(BF16)
    zpad = jnp.zeros((D_MODEL, LANES - N_HEADS), BF16)
    w_in_p = jnp.concatenate(
        [wi[:, :ml_in], wi[:, ml_in:ml_in + N_HEADS], zpad,
         wi[:, ml_in + N_HEADS:ml_in + 2 * N_HEADS], zpad, wi[:, ml_in + 2 * N_HEADS:]], axis=1)
    gate_row = lambda b: jnp.concatenate([b, jnp.zeros((LANES - N_HEADS,), F32)])[None, :]
    zl = jnp.zeros((D_LORA, GROUP_W), F32)
    return {
        "norm_mix_w": norm_mix_w[l][None, :],
        "w_in_p": w_in_p,
        "conv_w": mlstm_conv_w[l], "conv_b": mlstm_conv_b[l][None, :],
        "ib_row": gate_row(mlstm_i_b[l]), "fb_row": gate_row(mlstm_f_b[l]),
        "gate_b": jnp.stack([mlstm_i_b[l], mlstm_f_b[l]], axis=1),
        "p1024": jnp.concatenate([mlstm_conv_w[l].T, mlstm_conv_b[l][:, None],
                                  jnp.zeros((2 * GROUP_W, SUBLANES - CONV_W - 1), F32)], axis=1),
        "p512": jnp.stack([rw_w0[l], rw_a0[l], mlstm_norm_w[l], rw_k_k[l], rw_k_a[l], rw_r_k[l].reshape(GROUP_W),
                           rw_ln_w[l], rw_ln_b[l]], axis=1),
        "mu_col": rw_mu[l][:, None],
        "mlnorm_w": mlstm_norm_w[l][None, :],
        "mu": rw_mu[l][None, :], "w0": rw_w0[l][None, :], "a0": rw_a0[l][None, :],
        "wup_p": jnp.concatenate([rw_w_up[l], zl], axis=0).astype(BF16),
        "aup_p": jnp.concatenate([zl, rw_a_up[l]], axis=0).astype(BF16),
        "gup": rw_g_up[l].astype(BF16),
        "wup_t": jnp.concatenate([rw_w_up[l], zl], axis=0).T.astype(BF16),
        "aup_t": jnp.concatenate([zl, rw_a_up[l]], axis=0).T.astype(BF16),
        "gup_t": rw_g_up[l].T.astype(BF16),
        "k_k": rw_k_k[l][None, :], "k_a": rw_k_a[l][None, :], "r_k": rw_r_k[l].reshape(1, GROUP_W),
        "ln_w": rw_ln_w[l][None, :], "ln_b": rw_ln_b[l][None, :],
        "w_out": w_out[l].astype(BF16), "norm_mlp_w": norm_mlp_w[l][None, :],
        "mlp_up": mlp_up[l].astype(BF16), "mlp_down": mlp_down[l].astype(BF16),
    }


def _unpair(st):
    d = HEAD_DIM
    return jnp.stack([st[:, h // 2, (h % 2) * d:(h % 2 + 1) * d, (h % 2) * d:(h % 2 + 1) * d]
                      for h in range(N_HEADS)], axis=1)


def _prompt_layer(x, p, norm_f_w):
    bsz, seq, _ = x.shape
    x2d = x.reshape(bsz * seq, D_MODEL)
    proj = _inproj(x2d, p["norm_mix_w"], p["w_in_p"]).reshape(bsz, seq, IN_WP)
    mix, c_st, n_st, m_st, s_st = _mixer(proj, p)
    y = _outmlp(x2d, mix.reshape(bsz * seq, D_MODEL), p["w_out"], p["norm_mlp_w"], p["mlp_up"],
                p["mlp_down"], norm_f_w)
    c_new = _unpair(c_st)
    n_new = n_st[:, :PAIRS, :].reshape(bsz, N_HEADS, HEAD_DIM)
    m_new = m_st[:, :, 0]
    conv_new = proj[:, seq - (CONV_W - 1):, Q0:Q0 + 2 * GROUP_W]
    s_new = _unpair(s_st)
    shift_new = proj[:, seq - 1:, RW0:RW0 + RW_IN_W]
    return y.reshape(bsz, seq, D_MODEL), (c_new, n_new, m_new, conv_new, s_new, shift_new)


def _sample_layer(x, c0, n0, m0, conv0, s0, shift0, p, norm_f_w):
    bsz = x.shape[0]
    x2d = x.reshape(bsz, D_MODEL)
    proj, proj_t = _inproj_t(x2d, p["norm_mix_w"], p["w_in_p"])
    hml_t, yrw_t, c_t, n_t, m_t, s_t = _sample(
        proj_t, jnp.transpose(conv0, (1, 2, 0)), shift0[:, 0, :].T, m0.T, jnp.transpose(n0, (1, 2, 0)),
        jnp.transpose(c0, (1, 2, 3, 0)), jnp.transpose(s0, (1, 2, 3, 0)), p)
    mix = jnp.concatenate([hml_t, yrw_t], axis=0).T
    y = _outmlp(x2d, mix, p["w_out"], p["norm_mlp_w"], p["mlp_up"], p["mlp_down"], norm_f_w)
    conv_new = jnp.concatenate([conv0[:, 1:], proj[:, None, Q0:Q0 + 2 * GROUP_W]], axis=1)
    shift_new = proj[:, None, RW0:RW0 + RW_IN_W]
    return y.reshape(bsz, 1, D_MODEL), (
        jnp.transpose(c_t, (3, 0, 1, 2)), jnp.transpose(n_t, (2, 0, 1)), m_t.T, conv_new,
        jnp.transpose(s_t, (3, 0, 1, 2)), shift_new)


def kernel(x_prompt, x_sample, state_mlstm_C, state_mlstm_n, state_mlstm_m, state_mlstm_conv, state_rwkv_S, state_rwkv_shift, norm_mix_w, w_in, mlstm_conv_w, mlstm_conv_b, mlstm_i_b, mlstm_f_b, mlstm_norm_w, rw_mu, rw_w0, rw_w_up, rw_a0, rw_a_up, rw_g_up, rw_k_k, rw_k_a, rw_r_k, rw_ln_w, rw_ln_b, w_out, norm_mlp_w, mlp_up, mlp_down, norm_f_w):
    depth = w_in.shape[0]
    assert depth == 1, "the final RMSNorm is fused into the layer's output kernel"
    weights = (norm_mix_w, w_in, mlstm_conv_w, mlstm_conv_b, mlstm_i_b, mlstm_f_b, mlstm_norm_w,
               rw_mu, rw_w0, rw_w_up, rw_a0, rw_a_up, rw_g_up, rw_k_k, rw_k_a, rw_r_k, rw_ln_w, rw_ln_b,
               w_out, norm_mlp_w, mlp_up, mlp_down)
    nfw = norm_f_w[None, :]
    p = _layer_params(0, *weights)
    y_p, st_p = _prompt_layer(x_prompt, p, nfw)
    y_s, st_s = _sample_layer(x_sample, state_mlstm_C[0], state_mlstm_n[0], state_mlstm_m[0],
                              state_mlstm_conv[0], state_rwkv_S[0], state_rwkv_shift[0], p, nfw)
    lead = lambda t: tuple(a[None] for a in t)
    return (y_p, y_s) + lead(st_p) + lead(st_s)
```

```python
import itertools
import math

import jax
import jax.numpy as jnp
from jax import lax
from jax.experimental import pallas as pl
from jax.experimental.pallas import tpu as pltpu

F32 = jnp.float32
BF16 = jnp.bfloat16

D_MODEL = 1024
HEAD_DIM = 64
N_HEADS = 8
PAIRS = N_HEADS // 2
GROUP_W = N_HEADS * HEAD_DIM
CONV_W = 4
D_LORA = 64
D_GATE = 128
D_FF = 4 * D_MODEL
EPS = 1e-6
GN_EPS = 64e-5
LANES = 128
SUBLANES = 8

Q0, K0, V0, O0 = 0, GROUP_W, 2 * GROUP_W, 3 * GROUP_W
GI0 = 4 * GROUP_W
GF0 = GI0 + LANES
RW0 = GF0 + LANES
RW_IN_W = 3 * GROUP_W + 2 * D_LORA + D_GATE
IN_WP = RW0 + RW_IN_W
RS_R, RS_K, RS_V = 0, GROUP_W, 2 * GROUP_W
RS_LORA = 3 * GROUP_W
RS_G = RS_LORA + 2 * D_LORA

TB = 256
LM = 128
LR = 64
ML_EVERY = 3
_DONE = object()
_STATE = object()
RW_GROUP = 2
TM_IN = 512
TM_OUT = 512
FF_CHUNK = 1024
SAMPLE_UNROLL = 8
NEG = -1e30
W_SCALE = math.exp(-0.5)
VMEM_LIMIT = 56 * 1024 * 1024


def _dot(a, b):
    return jnp.dot(a.astype(BF16), b.astype(BF16), preferred_element_type=F32)


def _dot_nt(a, b):
    return lax.dot_general(a.astype(BF16), b.astype(BF16), (((1,), (1,)), ((), ())),
                           preferred_element_type=F32)


def _split3(x):
    h1 = x.astype(BF16)
    r1 = x - h1.astype(F32)
    h2 = r1.astype(BF16)
    h3 = (r1 - h2.astype(F32)).astype(BF16)
    return h1, h2, h3


def _dot01_l(m01, x):
    h1, h2, h3 = _split3(x)
    d = lambda h: jnp.dot(m01, h, preferred_element_type=F32)
    return d(h1) + d(h2) + d(h3)


def _log_sigmoid(x):
    return jnp.minimum(x, 0.0) - jnp.log(1.0 + jnp.exp(-jnp.abs(x)))


def _tanh(x):
    e = jnp.exp(-2.0 * jnp.abs(x))
    t = (1.0 - e) / (1.0 + e)
    return jnp.where(x < 0.0, -t, t)


def _iota(shape, dim):
    return lax.broadcasted_iota(jnp.int32, shape, dim)


def _rowsum(x):
    return jnp.sum(x, axis=1, keepdims=True)


def _pair_rowsum(x, lo):
    s_lo = _rowsum(jnp.where(lo, x, 0.0))
    s_hi = _rowsum(jnp.where(lo, 0.0, x))
    return jnp.where(lo, s_lo, s_hi)


def _stack2(x, lo):
    zero = jnp.zeros_like(x)
    return jnp.concatenate([jnp.where(lo, x, zero), jnp.where(lo, zero, x)], axis=0)


def _rms(x, w):
    ms = jnp.mean(x * x, axis=-1, keepdims=True)
    return x * lax.rsqrt(ms + EPS) * w


def _inproj_kernel(x_ref, nw_ref, w_ref, o_ref):
    xn = _rms(x_ref[...], nw_ref[...])
    o_ref[...] = jnp.dot(xn.astype(BF16), w_ref[...], preferred_element_type=F32)


def _inproj(x2d, norm_w, w_in_p):
    rows = x2d.shape[0]
    tm = min(TM_IN, rows)
    return pl.pallas_call(
        _inproj_kernel,
        out_shape=jax.ShapeDtypeStruct((rows, IN_WP), F32),
        grid=(rows // tm,),
        in_specs=[
            pl.BlockSpec((tm, D_MODEL), lambda i: (i, 0)),
            pl.BlockSpec((1, D_MODEL), lambda i: (0, 0)),
            pl.BlockSpec((D_MODEL, IN_WP), lambda i: (0, 0)),
        ],
        out_specs=pl.BlockSpec((tm, IN_WP), lambda i: (i, 0)),
        compiler_params=pltpu.CompilerParams(
            dimension_semantics=("arbitrary",), vmem_limit_bytes=VMEM_LIMIT),
        name="inproj",
    )(x2d, norm_w, w_in_p)


def _outmlp_kernel(x_ref, mix_ref, wout_ref, nmw_ref, up_ref, down_ref, nfw_ref, y_ref):
    x1 = x_ref[...] + jnp.dot(mix_ref[...].astype(BF16), wout_ref[...], preferred_element_type=F32)
    xn = _rms(x1, nmw_ref[...]).astype(BF16)
    acc = x1
    for c in range(D_FF // FF_CHUNK):
        sl = slice(c * FF_CHUNK, (c + 1) * FF_CHUNK)
        hid = jnp.maximum(jnp.dot(xn, up_ref[:, sl], preferred_element_type=F32), 0.0)
        acc = acc + jnp.dot((hid * hid).astype(BF16), down_ref[sl, :], preferred_element_type=F32)
    y_ref[...] = _rms(acc, nfw_ref[...])


def _outmlp(x2d, mix2d, w_out, norm_mlp_w, mlp_up, mlp_down, norm_f_w):
    rows = x2d.shape[0]
    tm = min(TM_OUT, rows)
    const = lambda shape: pl.BlockSpec(shape, lambda i: (0, 0), pipeline_mode=pl.Buffered(1))
    return pl.pallas_call(
        _outmlp_kernel,
        out_shape=jax.ShapeDtypeStruct((rows, D_MODEL), F32),
        grid=(rows // tm,),
        in_specs=[
            pl.BlockSpec((tm, D_MODEL), lambda i: (i, 0)),
            pl.BlockSpec((tm, D_MODEL), lambda i: (i, 0)),
            const((D_MODEL, D_MODEL)),
            const((1, D_MODEL)),
            const((D_MODEL, D_FF)),
            const((D_FF, D_MODEL)),
            const((1, D_MODEL)),
        ],
        out_specs=pl.BlockSpec((tm, D_MODEL), lambda i: (i, 0)),
        compiler_params=pltpu.CompilerParams(
            dimension_semantics=("arbitrary",), vmem_limit_bytes=VMEM_LIMIT),
        name="outmlp",
    )(x2d, mix2d, w_out, norm_mlp_w, mlp_up, mlp_down, norm_f_w)


def _rwkv_token_prep(lora_t, xg, w0, wup, a0, aup, gup):
    w_raw = w0 + jnp.dot(_tanh(lora_t).astype(BF16), wup, preferred_element_type=F32)
    wlog = -W_SCALE * jax.nn.sigmoid(w_raw)
    a = jax.nn.sigmoid(a0 + jnp.dot(lora_t.astype(BF16), aup, preferred_element_type=F32))
    g = jnp.dot(jax.nn.sigmoid(xg).astype(BF16), gup, preferred_element_type=F32)
    return wlog, a, g


def _mlstm_chunk(c, proj_ref, q_ref, k_ref, ib_ref, fb_ref, mlnw_ref, mix_ref, C_ref, n_ref, m_ref):
    rows = pl.ds(pl.multiple_of(c * LM, LM), LM)
    row = _iota((LM, LM), 0)
    col = _iota((LM, LM), 1)
    causal = col <= row
    tril = jnp.where(causal, 1.0, 0.0).astype(BF16)
    blockdiag = (row < HEAD_DIM) == (col < HEAD_DIM)
    lane = _iota((1, LANES), 1)
    lo = lane < HEAD_DIM

    i_tile = proj_ref[0, rows, GI0:GI0 + LANES] + ib_ref[...]
    f_tile = _log_sigmoid(proj_ref[0, rows, GF0:GF0 + LANES] + fb_ref[...])
    bc = _dot01_l(tril, f_tile)
    u = i_tile - bc
    ut = u.T
    yield

    nt = (((1,), (1,)), ((), ()))
    pairs = range(PAIRS)
    heads = range(N_HEADS)
    tile = lambda base, j: slice(base + j * LANES, base + (j + 1) * LANES)
    hmask = lambda h: lo if h % 2 == 0 else jnp.logical_not(lo)
    q_t = [q_ref[rows, tile(0, j)] for j in pairs]
    k_t = [k_ref[rows, tile(0, j)] for j in pairs]
    v_t = [proj_ref[0, rows, tile(V0, j)] for j in pairs]
    c_pair = [C_ref[0, j] for j in pairs]
    n_row = [n_ref[0, j:j + 1, :] for j in pairs]
    k_bf = [k.astype(BF16) for k in k_t]
    v_bf = [v.astype(BF16) for v in v_t]
    s = [lax.dot_general(jnp.where(hmask(h), q_t[h // 2], 0.0).astype(BF16), k_bf[h // 2], nt,
                         preferred_element_type=F32) for h in heads]
    q_c = [_dot(q_t[j], c_pair[j]) for j in pairs]
    k_tr = [k.T for k in k_t]
    yield

    m_prev_t = m_ref[0]
    ut8 = ut[0:N_HEADS, :]
    m_last_t = jnp.maximum(jnp.max(ut8, axis=1, keepdims=True), m_prev_t)
    ws_rows = jnp.exp(ut8 - m_last_t)
    dec_t = jnp.exp(m_prev_t - m_last_t)
    m_ref[0] = bc.T[0:N_HEADS, LM - 1:LM] + m_last_t

    m_prev = [m_prev_t[h:h + 1, 0:1] for h in heads]
    u_row = [ut[h:h + 1, :] for h in heads]
    bc_col = [_rowsum(jnp.where(lane == h, bc, 0.0)) for h in heads]
    m_col = [jnp.maximum(jnp.max(jnp.where(causal, u_row[h], NEG), axis=1, keepdims=True), m_prev[h])
             for h in heads]
    p = [s[h] * jnp.exp(jnp.where(causal, u_row[h] - m_col[h], NEG)) for h in heads]
    sc = [jnp.exp(m_prev[h] - m_col[h]) for h in heads]
    yield
    pv = [jnp.dot(p[h].astype(BF16), v_bf[h // 2], preferred_element_type=F32) for h in heads]
    yield
    h_parts = []
    for h in heads:
        q_n = _rowsum(jnp.where(hmask(h), q_t[h // 2] * n_row[h // 2], 0.0))
        den = _rowsum(p[h]) + sc[h] * q_n
        num = pv[h] + sc[h] * q_c[h // 2]
        h_parts.append(num / jnp.maximum(jnp.abs(den), jnp.exp(-(bc_col[h] + m_col[h]))))

    top = _iota((LM, 1), 0) < HEAD_DIM
    kw = [k_tr[j] * jnp.where(top, ws_rows[2 * j:2 * j + 1, :], ws_rows[2 * j + 1:2 * j + 2, :])
          for j in pairs]
    upd = [jnp.dot(kw[j].astype(BF16), v_bf[j], preferred_element_type=F32) for j in pairs]
    n_upd = [jnp.dot(ws_rows.astype(BF16), k_bf[j], preferred_element_type=F32) for j in pairs]
    yield
    for j in pairs:
        dec_j = jnp.where(lo, dec_t[2 * j:2 * j + 1, :], dec_t[2 * j + 1:2 * j + 2, :])
        C_ref[0, j] = c_pair[j] * dec_j + jnp.where(blockdiag, upd[j], 0.0)
        n_ref[0, j:j + 1, :] = n_row[j] * dec_j + jnp.where(lo, n_upd[j][2 * j:2 * j + 1, :],
                                                            n_upd[j][2 * j + 1:2 * j + 2, :])

    for j in pairs:
        h_til = jnp.where(lo, h_parts[2 * j], h_parts[2 * j + 1])
        hml = jax.nn.sigmoid(proj_ref[0, rows, tile(O0, j)]) * h_til
        ms = _pair_rowsum(hml * hml, lo) * (1.0 / HEAD_DIM)
        mix_ref[0, rows, tile(0, j)] = hml * lax.rsqrt(ms + EPS) * mlnw_ref[:, tile(0, j)]


def _neumann_inverse(ns):
    shape = ns[0].shape
    eye = jnp.where(_iota(shape, 0) == _iota(shape, 1), 1.0, 0.0)
    n_sq = int(math.log2(LR)) - 1
    qs = [n.astype(BF16) for n in ns]
    ts = [eye + n for n in ns]
    sq = [jnp.dot(q, q, preferred_element_type=F32) for q in qs]
    yield
    for _ in range(n_sq - 1):
        qs = [s.astype(BF16) for s in sq]
        both = [jnp.dot(q, jnp.concatenate([q, t.astype(BF16)], axis=1), preferred_element_type=F32)
                for q, t in zip(qs, ts)]
        sq = [b[:, 0:LANES] for b in both]
        ts = [t + b[:, LANES:2 * LANES] for t, b in zip(ts, both)]
        yield
    return [t + jnp.dot(s.astype(BF16), t.astype(BF16), preferred_element_type=F32) for s, t in zip(sq, ts)]


def _rwkv_chunks(c, rs_ref, w0_ref, wup_ref, a0_ref, aup_ref, gup_ref, kk_ref, ka_ref, rk_ref,
                 lnw_ref, lnb_ref, mix_ref, S_ref):
    gl = RW_GROUP * LR
    rows = pl.ds(pl.multiple_of(c * gl, gl), gl)
    lane = _iota((1, LANES), 1)
    lo = lane < HEAD_DIM
    row = _iota((LANES, LANES), 0)
    col = _iota((LANES, LANES), 1)
    same = (row < LR) == (col < LR)
    mask_sl = jnp.logical_and(same, col < row)
    mask_li = jnp.logical_and(same, col <= row)
    mask_li2 = jnp.concatenate([mask_li, mask_li], axis=1)
    grow = _iota((gl, gl), 0)
    gcol = _iota((gl, gl), 1)
    tril = jnp.where(jnp.logical_and(gcol <= grow, gcol >= (grow // LR) * LR), 1.0, 0.0).astype(BF16)
    nt = (((1,), (1,)), ((), ()))
    pairs = range(PAIRS)
    items = [(ci, j) for ci in range(RW_GROUP) for j in pairs]
    tile = lambda base, j: slice(base + j * LANES, base + (j + 1) * LANES)
    crow = lambda ci: slice(ci * LR, (ci + 1) * LR)

    wlog, a, g = _rwkv_token_prep(rs_ref[rows, RS_LORA:RS_LORA + LANES], rs_ref[rows, RS_G:RS_G + D_GATE],
                                  w0_ref[...], wup_ref[...], a0_ref[...], aup_ref[...], gup_ref[...])
    lw = _dot01_l(tril, wlog)
    e_in = jnp.exp(lw)
    e_ex = jnp.exp(lw - wlog)
    e_ng = jnp.exp(-lw)

    r_g = [rs_ref[rows, tile(RS_R, j)] for j in pairs]
    kr_g = [rs_ref[rows, tile(RS_K, j)] for j in pairs]
    v_g = [rs_ref[rows, tile(RS_V, j)] for j in pairs]
    a_g = [a[:, tile(0, j)] for j in pairs]
    kk = [kr_g[j] * kk_ref[:, tile(0, j)] for j in pairs]
    kkn = [kk[j] * lax.rsqrt(jnp.maximum(_pair_rowsum(kk[j] * kk[j], lo), 1e-24)) for j in pairs]
    keff_g = [kr_g[j] * (1.0 + (a_g[j] - 1.0) * ka_ref[:, tile(0, j)]) for j in pairs]
    alpha_g = [-kkn[j] * e_ex[:, tile(0, j)] for j in pairs]
    rb_g = [r_g[j] * e_in[:, tile(0, j)] for j in pairs]
    beta_g = [kkn[j] * a_g[j] * e_ng[:, tile(0, j)] for j in pairs]
    ktl_g = [keff_g[j] * e_ng[:, tile(0, j)] for j in pairs]
    bonus_g = [_pair_rowsum(r_g[j] * keff_g[j] * rk_ref[:, tile(0, j)], lo) * v_g[j] for j in pairs]

    lhs, rhs, v2, bk, gamma, ar = {}, {}, {}, {}, {}, {}
    for it in items:
        ci, j = it
        cr = crow(ci)
        ar[it] = jnp.concatenate([alpha_g[j][cr], rb_g[j][cr]], axis=0).astype(BF16)
        lhs[it] = jnp.concatenate([_stack2(ar[it][0:LR], lo), _stack2(ar[it][LR:2 * LR], lo)], axis=0)
        beta_c, ktl_c = beta_g[j][cr].astype(BF16), ktl_g[j][cr].astype(BF16)
        rhs[it] = jnp.concatenate([beta_c, beta_c, ktl_c, ktl_c], axis=0)
        v2[it] = _stack2(v_g[j][cr].astype(BF16), lo)
        bk[it] = jnp.concatenate([_stack2(beta_c, lo), _stack2(ktl_c, lo)], axis=0)
        gamma[it] = e_in[(ci + 1) * LR - 1:(ci + 1) * LR, tile(0, j)]
    aa = {it: lax.dot_general(lhs[it], rhs[it], nt, preferred_element_type=F32) for it in items}
    yield
    t_list = yield from _neumann_inverse([jnp.where(mask_sl, aa[it][0:LANES, 0:LANES], 0.0) for it in items])
    t_inv = dict(zip(items, t_list))
    a_ak = {it: jnp.where(mask_sl, aa[it][0:LANES, LANES:2 * LANES], 0.0) for it in items}
    a_r = {it: jnp.where(mask_li2, aa[it][LANES:2 * LANES, :], 0.0) for it in items}
    akv = {it: _dot(a_ak[it], v2[it]) for it in items}
    yield _STATE

    s_pair = [S_ref[0, j] for j in pairs]
    for ci in range(RW_GROUP):
        its = [(ci, j) for j in pairs]
        xs = [lax.dot_general(ar[it], s_pair[it[1]].astype(BF16), nt, preferred_element_type=F32)
              for it in its]
        yield
        p2 = [_dot(t_inv[it], _stack2(xs[j][0:LR], lo) + akv[it]) for j, it in enumerate(its)]
        yield
        pv = [jnp.concatenate([p2[j].astype(BF16), v2[it]], axis=0) for j, it in enumerate(its)]
        y2 = [_dot(a_r[it], pv[j]) for j, it in enumerate(its)]
        upd = [lax.dot_general(pv[j], bk[it], (((0,), (0,)), ((), ())), preferred_element_type=F32)
               for j, it in enumerate(its)]
        s_pair = [(s_pair[j] + upd[j]) * gamma[it] for j, it in enumerate(its)]
        yield
        cr = crow(ci)
        for j in pairs:
            yb = xs[j][LR:2 * LR] + y2[j][0:LR] + y2[j][LR:2 * LR] + bonus_g[j][cr]
            mu = _pair_rowsum(yb, lo) * (1.0 / HEAD_DIM)
            dlt = yb - mu
            var = _pair_rowsum(dlt * dlt, lo) * (1.0 / HEAD_DIM)
            yn = dlt * lax.rsqrt(var + GN_EPS) * lnw_ref[:, tile(0, j)] + lnb_ref[:, tile(0, j)]
            out_rows = pl.ds(pl.multiple_of(c * gl + ci * LR, LR), LR)
            mix_ref[0, out_rows, tile(GROUP_W, j)] = yn * g[cr, tile(0, j)]
    for j in pairs:
        S_ref[0, j] = s_pair[j]


def _shift_rows(x, tail, j):
    rolled = pltpu.roll(x, j, axis=0)
    head = jnp.where(_iota((SUBLANES, 1), 0) < j, pltpu.roll(tail, j, axis=0), rolled[0:SUBLANES])
    return jnp.concatenate([head, rolled[SUBLANES:]], axis=0)


def _until_state(gen):
    for tok in gen:
        if tok is _STATE:
            return
        yield


def _drain(gen):
    for _ in gen:
        yield


def _interleave(primary, secondary, every):
    done_p = done_s = False
    k = 0
    while not (done_p and done_s):
        if not done_p:
            done_p = next(primary, _DONE) is _DONE
        k += 1
        if not done_s and (done_p or k % every == 0):
            done_s = next(secondary, _DONE) is _DONE


def _mixer_kernel(proj_ref, convw_ref, convb_ref, ib_ref, fb_ref, mlnw_ref, mu_ref, w0_ref, wup_ref,
                  a0_ref, aup_ref, gup_ref, kk_ref, ka_ref, rk_ref, lnw_ref, lnb_ref,
                  mix_ref, C_ref, n_ref, m_ref, S_ref,
                  qk_tail_ref, rw_tail_ref, q_ref, k_ref, rs_ref):
    t = pl.program_id(1)

    @pl.when(t == 0)
    def _():
        C_ref[...] = jnp.zeros_like(C_ref)
        n_ref[...] = jnp.zeros_like(n_ref)
        m_ref[...] = jnp.zeros_like(m_ref)
        S_ref[...] = jnp.zeros_like(S_ref)
        qk_tail_ref[...] = jnp.zeros_like(qk_tail_ref)
        rw_tail_ref[...] = jnp.zeros_like(rw_tail_ref)

    u = proj_ref[0, :, Q0:Q0 + 2 * GROUP_W]
    qk_tail = qk_tail_ref[...]
    acc = convb_ref[...] + u * convw_ref[CONV_W - 1:CONV_W, :]
    for jw in range(CONV_W - 1):
        acc = acc + _shift_rows(u, qk_tail, CONV_W - 1 - jw) * convw_ref[jw:jw + 1, :]
    qk_tail_ref[...] = u[TB - SUBLANES:TB]
    qk = acc * jax.nn.sigmoid(acc)
    q_ref[...] = qk[:, 0:GROUP_W]
    k_ref[...] = qk[:, GROUP_W:2 * GROUP_W] * (HEAD_DIM ** -0.5)

    rw = proj_ref[0, :, RW0:RW0 + RW_IN_W]
    rs_ref[...] = rw + mu_ref[...] * (_shift_rows(rw, rw_tail_ref[...], 1) - rw)
    rw_tail_ref[...] = rw[TB - SUBLANES:TB]

    assert RW_GROUP * LR == LM
    rwkv = lambda c: _rwkv_chunks(c, rs_ref, w0_ref, wup_ref, a0_ref, aup_ref, gup_ref, kk_ref, ka_ref, rk_ref,
                                  lnw_ref, lnb_ref, mix_ref, S_ref)
    mlstm = lambda c: _mlstm_chunk(c, proj_ref, q_ref, k_ref, ib_ref, fb_ref, mlnw_ref, mix_ref, C_ref, n_ref, m_ref)
    n_groups = TB // LM
    gens = [rwkv(c) for c in range(n_groups)]
    _interleave(_until_state(gens[0]), mlstm(0), every=ML_EVERY)
    for c in range(1, n_groups):
        _interleave(_until_state(gens[c]), _drain(gens[c - 1]), every=1)
    _interleave(_drain(gens[-1]), itertools.chain(*[mlstm(c) for c in range(1, n_groups)]), every=1)


def _mixer(proj, p):
    bsz, seq, _ = proj.shape
    row = lambda w: pl.BlockSpec((1, w), lambda b, t: (0, 0))
    mat = lambda r, w: pl.BlockSpec((r, w), lambda b, t: (0, 0))
    state4 = pl.BlockSpec((1, PAIRS, LANES, LANES), lambda b, t: (b, 0, 0, 0))
    state3 = pl.BlockSpec((1, SUBLANES, LANES), lambda b, t: (b, 0, 0))
    return pl.pallas_call(
        _mixer_kernel,
        out_shape=(
            jax.ShapeDtypeStruct((bsz, seq, 2 * GROUP_W), F32),
            jax.ShapeDtypeStruct((bsz, PAIRS, LANES, LANES), F32),
            jax.ShapeDtypeStruct((bsz, SUBLANES, LANES), F32),
            jax.ShapeDtypeStruct((bsz, SUBLANES, LANES), F32),
            jax.ShapeDtypeStruct((bsz, PAIRS, LANES, LANES), F32),
        ),
        grid=(bsz, seq // TB),
        in_specs=[
            pl.BlockSpec((1, TB, IN_WP), lambda b, t: (b, t, 0)),
            mat(CONV_W, 2 * GROUP_W), row(2 * GROUP_W), row(LANES), row(LANES), row(GROUP_W),
            row(RW_IN_W), row(GROUP_W), mat(LANES, GROUP_W), row(GROUP_W), mat(LANES, GROUP_W),
            mat(D_GATE, GROUP_W), row(GROUP_W), row(GROUP_W), row(GROUP_W), row(GROUP_W), row(GROUP_W),
        ],
        out_specs=(
            pl.BlockSpec((1, TB, 2 * GROUP_W), lambda b, t: (b, t, 0)),
            state4, state3, state3, state4,
        ),
        scratch_shapes=[
            pltpu.VMEM((SUBLANES, 2 * GROUP_W), F32),
            pltpu.VMEM((SUBLANES, RW_IN_W), F32),
            pltpu.VMEM((TB, GROUP_W), F32),
            pltpu.VMEM((TB, GROUP_W), F32),
            pltpu.VMEM((TB, RW_IN_W), F32),
        ],
        compiler_params=pltpu.CompilerParams(
            dimension_semantics=("arbitrary", "arbitrary"), vmem_limit_bytes=VMEM_LIMIT),
        name="mixer",
    )(proj, p["conv_w"], p["conv_b"], p["ib_row"], p["fb_row"], p["mlnorm_w"], p["mu"], p["w0"],
      p["wup_p"], p["a0"], p["aup_p"], p["gup"], p["k_k"], p["k_a"], p["r_k"], p["ln_w"], p["ln_b"])


def _inproj_t_kernel(x_ref, nw_ref, w_ref, o_ref, ot_ref):
    xn = _rms(x_ref[...], nw_ref[...])
    o = jnp.dot(xn.astype(BF16), w_ref[...], preferred_element_type=F32)
    o_ref[...] = o
    ot_ref[...] = o.T


def _inproj_t(x2d, norm_w, w_in_p):
    rows = x2d.shape[0]
    return pl.pallas_call(
        _inproj_t_kernel,
        out_shape=(jax.ShapeDtypeStruct((rows, IN_WP), F32), jax.ShapeDtypeStruct((IN_WP, rows), F32)),
        compiler_params=pltpu.CompilerParams(vmem_limit_bytes=VMEM_LIMIT),
        name="inproj_t",
    )(x2d, norm_w, w_in_p)


(PC_W0, PC_A0, PC_MLNW, PC_KK, PC_KA, PC_RK, PC_LNW, PC_LNB) = range(8)
PC_CONVB = CONV_W


def _sample_kernel(pt_ref, conv0_ref, shift0_ref, m0_ref, n0_ref, C0_ref, S0_ref,
                   p1024_ref, p512_ref, mu_ref, gb_ref, wup_ref, aup_ref, gup_ref,
                   hml_ref, yrw_ref, C_ref, n_ref, m_ref, S_ref,
                   wk_s, q_s, vr_s, y_s):
    h = pl.program_id(0)
    hrow = lambda base, n=HEAD_DIM: pl.ds(pl.multiple_of(base + h * n, n), n)
    pcol = lambda j: p512_ref[hrow(0), j:j + 1]
    colsum = lambda t: jnp.sum(t, axis=0, keepdims=True)

    def conv_silu(base):
        rows = hrow(base)
        acc = p1024_ref[rows, PC_CONVB:PC_CONVB + 1] + pt_ref[rows, :] * p1024_ref[rows, CONV_W - 1:CONV_W]
        for jw in range(CONV_W - 1):
            acc = acc + conv0_ref[jw, rows, :] * p1024_ref[rows, jw:jw + 1]
        return acc * jax.nn.sigmoid(acc)

    q = conv_silu(Q0)
    k = conv_silu(K0) * (HEAD_DIM ** -0.5)
    v = pt_ref[hrow(V0), :]
    i_g = pt_ref[pl.ds(GI0 + h, 1), :] + gb_ref[pl.ds(h, 1), 0:1]
    logf = _log_sigmoid(pt_ref[pl.ds(GF0 + h, 1), :] + gb_ref[pl.ds(h, 1), 1:2])
    m0 = m0_ref[pl.ds(h, 1), :]
    m_new = jnp.maximum(logf + m0, i_g)
    ws = jnp.exp(i_g - m_new)
    dec = jnp.exp(logf + m0 - m_new)
    wk = ws * k
    n_new = dec * n0_ref[0] + wk
    n_ref[0] = n_new
    m_ref[pl.ds(h, 1), :] = m_new
    wk_s[...] = wk
    q_s[...] = q

    def c_body(kk_, num):
        ck = dec * C0_ref[0, kk_] + wk_s[pl.ds(kk_, 1), :] * v
        C_ref[0, kk_] = ck
        return num + q_s[pl.ds(kk_, 1), :] * ck

    num = lax.fori_loop(0, HEAD_DIM, c_body, jnp.zeros((HEAD_DIM, pt_ref.shape[1]), F32), unroll=SAMPLE_UNROLL)
    den = colsum(q * n_new)
    hml = jax.nn.sigmoid(pt_ref[hrow(O0), :]) * (num / jnp.maximum(jnp.abs(den), jnp.exp(-m_new)))
    hml_ref[...] = hml * lax.rsqrt(colsum(hml * hml) * (1.0 / HEAD_DIM) + EPS) * pcol(PC_MLNW)

    def shifted(base, n=HEAD_DIM):
        rw = pt_ref[hrow(RW0 + base, n) if n == HEAD_DIM else pl.ds(RW0 + base, n), :]
        rows = hrow(base, n) if n == HEAD_DIM else pl.ds(base, n)
        return rw + mu_ref[rows, :] * (shift0_ref[rows, :] - rw)

    r, kr, vr = shifted(RS_R), shifted(RS_K), shifted(RS_V)
    lora = shifted(RS_LORA, LANES)
    xg = shifted(RS_G, D_GATE)
    hw = hrow(0)
    w_raw = pcol(PC_W0) + jnp.dot(wup_ref[hw, :], jnp.tanh(lora).astype(BF16), preferred_element_type=F32)
    w = jnp.exp(-W_SCALE * jax.nn.sigmoid(w_raw))
    a = jax.nn.sigmoid(pcol(PC_A0) + jnp.dot(aup_ref[hw, :], lora.astype(BF16), preferred_element_type=F32))
    g = jnp.dot(gup_ref[hw, :], jax.nn.sigmoid(xg).astype(BF16), preferred_element_type=F32)
    kk = kr * pcol(PC_KK)
    kkn = kk / jnp.maximum(jnp.sqrt(colsum(kk * kk)), 1e-12)
    keff = kr * (1.0 + (a - 1.0) * pcol(PC_KA))
    nkk = -kkn
    ka = kkn * a
    vr_s[...] = vr

    def s_body(v_, carry):
        sv = S0_ref[0, v_]
        sn = sv * w + colsum(sv * nkk) * ka + vr_s[pl.ds(v_, 1), :] * keff
        S_ref[0, v_] = sn
        y_s[pl.ds(v_, 1), :] = colsum(sn * r)
        return carry

    lax.fori_loop(0, HEAD_DIM, s_body, 0, unroll=SAMPLE_UNROLL)
    yb = y_s[...] + colsum(r * keff * pcol(PC_RK)) * vr
    dlt = yb - colsum(yb) * (1.0 / HEAD_DIM)
    var = colsum(dlt * dlt) * (1.0 / HEAD_DIM)
    yrw_ref[...] = (dlt * lax.rsqrt(var + GN_EPS) * pcol(PC_LNW) + pcol(PC_LNB)) * g


def _sample(proj_t, conv0_t, shift0_t, m0_t, n0_t, c0_t, s0_t, p):
    bsz = proj_t.shape[1]
    whole = lambda a: pl.BlockSpec(a.shape, lambda h: (0,) * a.ndim)
    state4 = pl.BlockSpec((1, HEAD_DIM, HEAD_DIM, bsz), lambda h: (h, 0, 0, 0))
    head2 = pl.BlockSpec((HEAD_DIM, bsz), lambda h: (h, 0))
    head3 = pl.BlockSpec((1, HEAD_DIM, bsz), lambda h: (h, 0, 0))
    consts = [p["p1024"], p["p512"], p["mu_col"], p["gate_b"], p["wup_t"], p["aup_t"], p["gup_t"]]
    return pl.pallas_call(
        _sample_kernel,
        out_shape=(
            jax.ShapeDtypeStruct((GROUP_W, bsz), F32),
            jax.ShapeDtypeStruct((GROUP_W, bsz), F32),
            jax.ShapeDtypeStruct((N_HEADS, HEAD_DIM, HEAD_DIM, bsz), F32),
            jax.ShapeDtypeStruct((N_HEADS, HEAD_DIM, bsz), F32),
            jax.ShapeDtypeStruct((N_HEADS, bsz), F32),
            jax.ShapeDtypeStruct((N_HEADS, HEAD_DIM, HEAD_DIM, bsz), F32),
        ),
        grid=(N_HEADS,),
        in_specs=[whole(proj_t), whole(conv0_t), whole(shift0_t), whole(m0_t), head3, state4, state4]
                 + [whole(c) for c in consts],
        out_specs=(head2, head2, state4, head3, pl.BlockSpec((N_HEADS, bsz), lambda h: (0, 0)), state4),
        scratch_shapes=[pltpu.VMEM((HEAD_DIM, bsz), F32) for _ in range(4)],
        compiler_params=pltpu.CompilerParams(
            dimension_semantics=("arbitrary",), vmem_limit_bytes=VMEM_LIMIT),
        name="sample",
    )(proj_t, conv0_t, shift0_t, m0_t, n0_t, c0_t, s0_t, *consts)


def _layer_params(l, norm_mix_w, w_in, mlstm_conv_w, mlstm_conv_b, mlstm_i_b, mlstm_f_b, mlstm_norm_w,
                  rw_mu, rw_w0, rw_w_up, rw_a0, rw_a_up, rw_g_up, rw_k_k, rw_k_a, rw_r_k, rw_ln_w, rw_ln_b,
                  w_out, norm_mlp_w, mlp_up, mlp_down):
    ml_in = 4 * GROUP_W
    wi = w_in[l].astype(BF16)
    zpad = jnp.zeros((D_MODEL, LANES - N_HEADS), BF16)
    w_in_p = jnp.concatenate(
        [wi[:, :ml_in], wi[:, ml_in:ml_in + N_HEADS], zpad,
         wi[:, ml_in + N_HEADS:ml_in + 2 * N_HEADS], zpad, wi[:, ml_in + 2 * N_HEADS:]], axis=1)
    gate_row = lambda b: jnp.concatenate([b, jnp.zeros((LANES - N_HEADS,), F32)])[None, :]
    zl = jnp.zeros((D_LORA, GROUP_W), F32)
    return {
        "norm_mix_w": norm_mix_w[l][None, :],
        "w_in_p": w_in_p,
        "conv_w": mlstm_conv_w[l], "conv_b": mlstm_conv_b[l][None, :],
        "ib_row": gate_row(mlstm_i_b[l]), "fb_row": gate_row(mlstm_f_b[l]),
        "gate_b": jnp.stack([mlstm_i_b[l], mlstm_f_b[l]], axis=1),
        "p1024": jnp.concatenate([mlstm_conv_w[l].T, mlstm_conv_b[l][:, None],
                                  jnp.zeros((2 * GROUP_W, SUBLANES - CONV_W - 1), F32)], axis=1),
        "p512": jnp.stack([rw_w0[l], rw_a0[l], mlstm_norm_w[l], rw_k_k[l], rw_k_a[l], rw_r_k[l].reshape(GROUP_W),
                           rw_ln_w[l], rw_ln_b[l]], axis=1),
        "mu_col": rw_mu[l][:, None],
        "mlnorm_w": mlstm_norm_w[l][None, :],
        "mu": rw_mu[l][None, :], "w0": rw_w0[l][None, :], "a0": rw_a0[l][None, :],
        "wup_p": jnp.concatenate([rw_w_up[l], zl], axis=0).astype(BF16),
        "aup_p": jnp.concatenate([zl, rw_a_up[l]], axis=0).astype(BF16),
        "gup": rw_g_up[l].astype(BF16),
        "wup_t": jnp.concatenate([rw_w_up[l], zl], axis=0).T.astype(BF16),
        "aup_t": jnp.concatenate([zl, rw_a_up[l]], axis=0).T.astype(BF16),
        "gup_t": rw_g_up[l].T.astype(BF16),
        "k_k": rw_k_k[l][None, :], "k_a": rw_k_a[l][None, :], "r_k": rw_r_k[l].reshape(1, GROUP_W),
        "ln_w": rw_ln_w[l][None, :], "ln_b": rw_ln_b[l][None, :],
        "w_out": w_out[l].astype(BF16), "norm_mlp_w": norm_mlp_w[l][None, :],
        "mlp_up": mlp_up[l].astype(BF16), "mlp_down": mlp_down[l].astype(BF16),
    }


def _unpair(st):
    d = HEAD_DIM
    return jnp.stack([st[:, h // 2, (h % 2) * d:(h % 2 + 1) * d, (h % 2) * d:(h % 2 + 1) * d]
                      for h in range(N_HEADS)], axis=1)


def _prompt_layer(x, p, norm_f_w):
    bsz, seq, _ = x.shape
    x2d = x.reshape(bsz * seq, D_MODEL)
    proj = _inproj(x2d, p["norm_mix_w"], p["w_in_p"]).reshape(bsz, seq, IN_WP)
    mix, c_st, n_st, m_st, s_st = _mixer(proj, p)
    y = _outmlp(x2d, mix.reshape(bsz * seq, D_MODEL), p["w_out"], p["norm_mlp_w"], p["mlp_up"],
                p["mlp_down"], norm_f_w)
    c_new = _unpair(c_st)
    n_new = n_st[:, :PAIRS, :].reshape(bsz, N_HEADS, HEAD_DIM)
    m_new = m_st[:, :, 0]
    conv_new = proj[:, seq - (CONV_W - 1):, Q0:Q0 + 2 * GROUP_W]
    s_new = _unpair(s_st)
    shift_new = proj[:, seq - 1:, RW0:RW0 + RW_IN_W]
    return y.reshape(bsz, seq, D_MODEL), (c_new, n_new, m_new, conv_new, s_new, shift_new)


def _sample_layer(x, c0, n0, m0, conv0, s0, shift0, p, norm_f_w):
    bsz = x.shape[0]
    x2d = x.reshape(bsz, D_MODEL)
    proj, proj_t = _inproj_t(x2d, p["norm_mix_w"], p["w_in_p"])
    hml_t, yrw_t, c_t, n_t, m_t, s_t = _sample(
        proj_t, jnp.transpose(conv0, (1, 2, 0)), shift0[:, 0, :].T, m0.T, jnp.transpose(n0, (1, 2, 0)),
        jnp.transpose(c0, (1, 2, 3, 0)), jnp.transpose(s0, (1, 2, 3, 0)), p)
    mix = jnp.concatenate([hml_t, yrw_t], axis=0).T
    y = _outmlp(x2d, mix, p["w_out"], p["norm_mlp_w"], p["mlp_up"], p["mlp_down"], norm_f_w)
    conv_new = jnp.concatenate([conv0[:, 1:], proj[:, None, Q0:Q0 + 2 * GROUP_W]], axis=1)
    shift_new = proj[:, None, RW0:RW0 + RW_IN_W]
    return y.reshape(bsz, 1, D_MODEL), (
        jnp.transpose(c_t, (3, 0, 1, 2)), jnp.transpose(n_t, (2, 0, 1)), m_t.T, conv_new,
        jnp.transpose(s_t, (3, 0, 1, 2)), shift_new)


def kernel(x_prompt, x_sample, state_mlstm_C, state_mlstm_n, state_mlstm_m, state_mlstm_conv, state_rwkv_S, state_rwkv_shift, norm_mix_w, w_in, mlstm_conv_w, mlstm_conv_b, mlstm_i_b, mlstm_f_b, mlstm_norm_w, rw_mu, rw_w0, rw_w_up, rw_a0, rw_a_up, rw_g_up, rw_k_k, rw_k_a, rw_r_k, rw_ln_w, rw_ln_b, w_out, norm_mlp_w, mlp_up, mlp_down, norm_f_w):
    depth = w_in.shape[0]
    assert depth == 1, "the final RMSNorm is fused into the layer's output kernel"
    weights = (norm_mix_w, w_in, mlstm_conv_w, mlstm_conv_b, mlstm_i_b, mlstm_f_b, mlstm_norm_w,
               rw_mu, rw_w0, rw_w_up, rw_a0, rw_a_up, rw_g_up, rw_k_k, rw_k_a, rw_r_k, rw_ln_w, rw_ln_b,
               w_out, norm_mlp_w, mlp_up, mlp_down)
    nfw = norm_f_w[None, :]
    p = _layer_params(0, *weights)
    y_p, st_p = _prompt_layer(x_prompt, p, nfw)
    y_s, st_s = _sample_layer(x_sample, state_mlstm_C[0], state_mlstm_n[0], state_mlstm_m[0],
                              state_mlstm_conv[0], state_rwkv_S[0], state_rwkv_shift[0], p, nfw)
    lead = lambda t: tuple(a[None] for a in t)
    return (y_p, y_s) + lead(st_p) + lead(st_s)
```

```python
import itertools
import math

import jax
import jax.numpy as jnp
from jax import lax
from jax.experimental import pallas as pl
from jax.experimental.pallas import tpu as pltpu

F32 = jnp.float32
BF16 = jnp.bfloat16

D_MODEL = 1024
HEAD_DIM = 64
N_HEADS = 8
PAIRS = N_HEADS // 2
GROUP_W = N_HEADS * HEAD_DIM
CONV_W = 4
D_LORA = 64
D_GATE = 128
D_FF = 4 * D_MODEL
EPS = 1e-6
GN_EPS = 64e-5
LANES = 128
SUBLANES = 8

Q0, K0, V0, O0 = 0, GROUP_W, 2 * GROUP_W, 3 * GROUP_W
GI0 = 4 * GROUP_W
GF0 = GI0 + LANES
RW0 = GF0 + LANES
RW_IN_W = 3 * GROUP_W + 2 * D_LORA + D_GATE
IN_WP = RW0 + RW_IN_W
RS_R, RS_K, RS_V = 0, GROUP_W, 2 * GROUP_W
RS_LORA = 3 * GROUP_W
RS_G = RS_LORA + 2 * D_LORA

TB = 256
LM = 128
LR = 64
ML_EVERY = 2
_DONE = object()
_STATE = object()
RW_GROUP = 2
TM_IN = 512
TM_OUT = 512
FF_CHUNK = 1024
SAMPLE_UNROLL = 8
NEG = -1e30
W_SCALE = math.exp(-0.5)
VMEM_LIMIT = 56 * 1024 * 1024


def _dot(a, b):
    return jnp.dot(a.astype(BF16), b.astype(BF16), preferred_element_type=F32)


def _dot_nt(a, b):
    return lax.dot_general(a.astype(BF16), b.astype(BF16), (((1,), (1,)), ((), ())),
                           preferred_element_type=F32)


def _split3(x):
    h1 = x.astype(BF16)
    r1 = x - h1.astype(F32)
    h2 = r1.astype(BF16)
    h3 = (r1 - h2.astype(F32)).astype(BF16)
    return h1, h2, h3


def _dot01_l(m01, x):
    h1, h2, h3 = _split3(x)
    d = lambda h: jnp.dot(m01, h, preferred_element_type=F32)
    return d(h1) + d(h2) + d(h3)


def _log_sigmoid(x):
    return jnp.minimum(x, 0.0) - jnp.log(1.0 + jnp.exp(-jnp.abs(x)))


def _tanh(x):
    e = jnp.exp(-2.0 * jnp.abs(x))
    t = (1.0 - e) / (1.0 + e)
    return jnp.where(x < 0.0, -t, t)


def _iota(shape, dim):
    return lax.broadcasted_iota(jnp.int32, shape, dim)


def _rowsum(x):
    return jnp.sum(x, axis=1, keepdims=True)


def _pair_rowsum(x, lo):
    s_lo = _rowsum(jnp.where(lo, x, 0.0))
    s_hi = _rowsum(jnp.where(lo, 0.0, x))
    return jnp.where(lo, s_lo, s_hi)


def _stack2(x, lo):
    zero = jnp.zeros_like(x)
    return jnp.concatenate([jnp.where(lo, x, zero), jnp.where(lo, zero, x)], axis=0)


def _rms(x, w):
    ms = jnp.mean(x * x, axis=-1, keepdims=True)
    return x * lax.rsqrt(ms + EPS) * w


def _project(x_ref, nw_ref, wml_ref, wgate_ref, wrw_ref):
    xn = _rms(x_ref[...], nw_ref[...]).astype(BF16)
    return [jnp.dot(xn, w[...], preferred_element_type=F32) for w in (wml_ref, wgate_ref, wrw_ref)]


def _inproj_kernel(x_ref, nw_ref, wml_ref, wgate_ref, wrw_ref, o_ref):
    o_ml, o_gate, o_rw = _project(x_ref, nw_ref, wml_ref, wgate_ref, wrw_ref)
    o_ref[:, 0:GI0] = o_ml
    o_ref[:, GI0:RW0] = o_gate
    o_ref[:, RW0:IN_WP] = o_rw


def _inproj(x2d, norm_w, w_in_parts):
    rows = x2d.shape[0]
    tm = min(TM_IN, rows)
    return pl.pallas_call(
        _inproj_kernel,
        out_shape=jax.ShapeDtypeStruct((rows, IN_WP), F32),
        grid=(rows // tm,),
        in_specs=[
            pl.BlockSpec((tm, D_MODEL), lambda i: (i, 0)),
            pl.BlockSpec((1, D_MODEL), lambda i: (0, 0)),
        ] + [pl.BlockSpec(w.shape, lambda i: (0, 0)) for w in w_in_parts],
        out_specs=pl.BlockSpec((tm, IN_WP), lambda i: (i, 0)),
        compiler_params=pltpu.CompilerParams(
            dimension_semantics=("arbitrary",), vmem_limit_bytes=VMEM_LIMIT),
        name="inproj",
    )(x2d, norm_w, *w_in_parts)


def _outmlp_kernel(x_ref, mix_ref, wout_ref, nmw_ref, up_ref, down_ref, nfw_ref, y_ref):
    x1 = x_ref[...] + jnp.dot(mix_ref[...].astype(BF16), wout_ref[...], preferred_element_type=F32)
    xn = _rms(x1, nmw_ref[...]).astype(BF16)
    acc = x1
    for c in range(D_FF // FF_CHUNK):
        sl = slice(c * FF_CHUNK, (c + 1) * FF_CHUNK)
        hid = jnp.maximum(jnp.dot(xn, up_ref[:, sl], preferred_element_type=F32), 0.0)
        acc = acc + jnp.dot((hid * hid).astype(BF16), down_ref[sl, :], preferred_element_type=F32)
    y_ref[...] = _rms(acc, nfw_ref[...])


def _outmlp(x2d, mix2d, w_out, norm_mlp_w, mlp_up, mlp_down, norm_f_w):
    rows = x2d.shape[0]
    tm = min(TM_OUT, rows)
    const = lambda shape: pl.BlockSpec(shape, lambda i: (0, 0), pipeline_mode=pl.Buffered(1))
    return pl.pallas_call(
        _outmlp_kernel,
        out_shape=jax.ShapeDtypeStruct((rows, D_MODEL), F32),
        grid=(rows // tm,),
        in_specs=[
            pl.BlockSpec((tm, D_MODEL), lambda i: (i, 0)),
            pl.BlockSpec((tm, D_MODEL), lambda i: (i, 0)),
            const((D_MODEL, D_MODEL)),
            const((1, D_MODEL)),
            const((D_MODEL, D_FF)),
            const((D_FF, D_MODEL)),
            const((1, D_MODEL)),
        ],
        out_specs=pl.BlockSpec((tm, D_MODEL), lambda i: (i, 0)),
        compiler_params=pltpu.CompilerParams(
            dimension_semantics=("arbitrary",), vmem_limit_bytes=VMEM_LIMIT),
        name="outmlp",
    )(x2d, mix2d, w_out, norm_mlp_w, mlp_up, mlp_down, norm_f_w)


def _rwkv_token_prep(lora_t, xg, w0, wup, a0, aup, gup):
    w_raw = w0 + jnp.dot(_tanh(lora_t).astype(BF16), wup, preferred_element_type=F32)
    wlog = -W_SCALE * jax.nn.sigmoid(w_raw)
    a = jax.nn.sigmoid(a0 + jnp.dot(lora_t.astype(BF16), aup, preferred_element_type=F32))
    g = jnp.dot(jax.nn.sigmoid(xg).astype(BF16), gup, preferred_element_type=F32)
    return wlog, a, g


def _mlstm_chunk(c, proj_ref, q_ref, k_ref, ib_ref, fb_ref, mlnw_ref, mix_ref, C_ref, n_ref, m_ref):
    rows = pl.ds(pl.multiple_of(c * LM, LM), LM)
    row = _iota((LM, LM), 0)
    col = _iota((LM, LM), 1)
    causal = col <= row
    tril = jnp.where(causal, 1.0, 0.0).astype(BF16)
    blockdiag = (row < HEAD_DIM) == (col < HEAD_DIM)
    lane = _iota((1, LANES), 1)
    lo = lane < HEAD_DIM

    i_tile = proj_ref[0, rows, GI0:GI0 + LANES] + ib_ref[...]
    f_tile = _log_sigmoid(proj_ref[0, rows, GF0:GF0 + LANES] + fb_ref[...])
    bc = _dot01_l(tril, f_tile)
    u = i_tile - bc
    ut = u.T
    yield

    nt = (((1,), (1,)), ((), ()))
    pairs = range(PAIRS)
    heads = range(N_HEADS)
    tile = lambda base, j: slice(base + j * LANES, base + (j + 1) * LANES)
    hmask = lambda h: lo if h % 2 == 0 else jnp.logical_not(lo)
    q_t = [q_ref[rows, tile(0, j)] for j in pairs]
    k_t = [k_ref[rows, tile(0, j)] for j in pairs]
    v_t = [proj_ref[0, rows, tile(V0, j)] for j in pairs]
    c_pair = [C_ref[0, j] for j in pairs]
    n_row = [n_ref[0, j:j + 1, :] for j in pairs]
    k_bf = [k.astype(BF16) for k in k_t]
    v_bf = [v.astype(BF16) for v in v_t]
    s = [lax.dot_general(jnp.where(hmask(h), q_t[h // 2], 0.0).astype(BF16), k_bf[h // 2], nt,
                         preferred_element_type=F32) for h in heads]
    q_c = [_dot(q_t[j], c_pair[j]) for j in pairs]
    k_tr = [k.T for k in k_t]
    yield

    m_prev_t = m_ref[0]
    ut8 = ut[0:N_HEADS, :]
    m_last_t = jnp.maximum(jnp.max(ut8, axis=1, keepdims=True), m_prev_t)
    ws_rows = jnp.exp(ut8 - m_last_t)
    dec_t = jnp.exp(m_prev_t - m_last_t)
    m_ref[0] = bc.T[0:N_HEADS, LM - 1:LM] + m_last_t

    m_prev = [m_prev_t[h:h + 1, 0:1] for h in heads]
    u_row = [ut[h:h + 1, :] for h in heads]
    bc_col = [_rowsum(jnp.where(lane == h, bc, 0.0)) for h in heads]
    m_col = [jnp.maximum(jnp.max(jnp.where(causal, u_row[h], NEG), axis=1, keepdims=True), m_prev[h])
             for h in heads]
    p = [s[h] * jnp.exp(jnp.where(causal, u_row[h] - m_col[h], NEG)) for h in heads]
    sc = [jnp.exp(m_prev[h] - m_col[h]) for h in heads]
    yield
    pv = [jnp.dot(p[h].astype(BF16), v_bf[h // 2], preferred_element_type=F32) for h in heads]
    yield
    h_parts = []
    for h in heads:
        q_n = _rowsum(jnp.where(hmask(h), q_t[h // 2] * n_row[h // 2], 0.0))
        den = _rowsum(p[h]) + sc[h] * q_n
        num = pv[h] + sc[h] * q_c[h // 2]
        h_parts.append(num / jnp.maximum(jnp.abs(den), jnp.exp(-(bc_col[h] + m_col[h]))))

    top = _iota((LM, 1), 0) < HEAD_DIM
    kw = [k_tr[j] * jnp.where(top, ws_rows[2 * j:2 * j + 1, :], ws_rows[2 * j + 1:2 * j + 2, :])
          for j in pairs]
    upd = [jnp.dot(kw[j].astype(BF16), v_bf[j], preferred_element_type=F32) for j in pairs]
    n_upd = [jnp.dot(ws_rows.astype(BF16), k_bf[j], preferred_element_type=F32) for j in pairs]
    yield
    for j in pairs:
        dec_j = jnp.where(lo, dec_t[2 * j:2 * j + 1, :], dec_t[2 * j + 1:2 * j + 2, :])
        C_ref[0, j] = c_pair[j] * dec_j + jnp.where(blockdiag, upd[j], 0.0)
        n_ref[0, j:j + 1, :] = n_row[j] * dec_j + jnp.where(lo, n_upd[j][2 * j:2 * j + 1, :],
                                                            n_upd[j][2 * j + 1:2 * j + 2, :])

    for j in pairs:
        h_til = jnp.where(lo, h_parts[2 * j], h_parts[2 * j + 1])
        hml = jax.nn.sigmoid(proj_ref[0, rows, tile(O0, j)]) * h_til
        ms = _pair_rowsum(hml * hml, lo) * (1.0 / HEAD_DIM)
        mix_ref[0, rows, tile(0, j)] = hml * lax.rsqrt(ms + EPS) * mlnw_ref[:, tile(0, j)]


def _neumann_inverse(ns):
    shape = ns[0].shape
    eye = jnp.where(_iota(shape, 0) == _iota(shape, 1), 1.0, 0.0)
    n_sq = int(math.log2(LR)) - 1
    qs = [n.astype(BF16) for n in ns]
    ts = [eye + n for n in ns]
    sq = [jnp.dot(q, q, preferred_element_type=F32) for q in qs]
    yield
    for _ in range(n_sq - 1):
        qs = [s.astype(BF16) for s in sq]
        both = [jnp.dot(q, jnp.concatenate([q, t.astype(BF16)], axis=1), preferred_element_type=F32)
                for q, t in zip(qs, ts)]
        sq = [b[:, 0:LANES] for b in both]
        ts = [t + b[:, LANES:2 * LANES] for t, b in zip(ts, both)]
        yield
    return [(t + jnp.dot(s.astype(BF16), t.astype(BF16), preferred_element_type=F32)).astype(BF16)
            for s, t in zip(sq, ts)]


def _rwkv_chunks(c, rs_ref, w0_ref, wup_ref, a0_ref, aup_ref, gup_ref, kk_ref, ka_ref, rk_ref,
                 lnw_ref, lnb_ref, mix_ref, S_ref):
    gl = RW_GROUP * LR
    rows = pl.ds(pl.multiple_of(c * gl, gl), gl)
    lane = _iota((1, LANES), 1)
    lo = lane < HEAD_DIM
    row = _iota((LANES, LANES), 0)
    col = _iota((LANES, LANES), 1)
    same = (row < LR) == (col < LR)
    mask_sl = jnp.logical_and(same, col < row)
    mask_li = jnp.logical_and(same, col <= row)
    mask_li2 = jnp.concatenate([mask_li, mask_li], axis=1)
    grow = _iota((gl, gl), 0)
    gcol = _iota((gl, gl), 1)
    tril = jnp.where(jnp.logical_and(gcol <= grow, gcol >= (grow // LR) * LR), 1.0, 0.0).astype(BF16)
    nt = (((1,), (1,)), ((), ()))
    pairs = range(PAIRS)
    items = [(ci, j) for ci in range(RW_GROUP) for j in pairs]
    tile = lambda base, j: slice(base + j * LANES, base + (j + 1) * LANES)
    crow = lambda ci: slice(ci * LR, (ci + 1) * LR)

    wlog, a, g = _rwkv_token_prep(rs_ref[rows, RS_LORA:RS_LORA + LANES], rs_ref[rows, RS_G:RS_G + D_GATE],
                                  w0_ref[...], wup_ref[...], a0_ref[...], aup_ref[...], gup_ref[...])
    lw = _dot01_l(tril, wlog)
    e_in = jnp.exp(lw)
    e_ex = jnp.exp(lw - wlog)
    e_ng = jnp.exp(-lw)

    r_g = [rs_ref[rows, tile(RS_R, j)] for j in pairs]
    kr_g = [rs_ref[rows, tile(RS_K, j)] for j in pairs]
    v_g = [rs_ref[rows, tile(RS_V, j)] for j in pairs]
    a_g = [a[:, tile(0, j)] for j in pairs]
    kk = [kr_g[j] * kk_ref[:, tile(0, j)] for j in pairs]
    kkn = [kk[j] * lax.rsqrt(jnp.maximum(_pair_rowsum(kk[j] * kk[j], lo), 1e-24)) for j in pairs]
    keff_g = [kr_g[j] * (1.0 + (a_g[j] - 1.0) * ka_ref[:, tile(0, j)]) for j in pairs]
    alpha_g = [-kkn[j] * e_ex[:, tile(0, j)] for j in pairs]
    rb_g = [r_g[j] * e_in[:, tile(0, j)] for j in pairs]
    beta_g = [kkn[j] * a_g[j] * e_ng[:, tile(0, j)] for j in pairs]
    ktl_g = [keff_g[j] * e_ng[:, tile(0, j)] for j in pairs]
    bonus_g = [_pair_rowsum(r_g[j] * keff_g[j] * rk_ref[:, tile(0, j)], lo) * v_g[j] for j in pairs]

    lhs, rhs, v2, bk, gamma, ar = {}, {}, {}, {}, {}, {}
    for it in items:
        ci, j = it
        cr = crow(ci)
        ar[it] = jnp.concatenate([alpha_g[j][cr], rb_g[j][cr]], axis=0).astype(BF16)
        lhs[it] = jnp.concatenate([_stack2(ar[it][0:LR], lo), _stack2(ar[it][LR:2 * LR], lo)], axis=0)
        beta_c, ktl_c = beta_g[j][cr].astype(BF16), ktl_g[j][cr].astype(BF16)
        rhs[it] = jnp.concatenate([beta_c, beta_c, ktl_c, ktl_c], axis=0)
        v2[it] = _stack2(v_g[j][cr].astype(BF16), lo)
        bk[it] = jnp.concatenate([_stack2(beta_c, lo), _stack2(ktl_c, lo)], axis=0)
        gamma[it] = e_in[(ci + 1) * LR - 1:(ci + 1) * LR, tile(0, j)]
    aa = {it: lax.dot_general(lhs[it], rhs[it], nt, preferred_element_type=F32) for it in items}
    yield
    a_ab = [jnp.where(mask_sl, aa[it][0:LANES, 0:LANES], 0.0) for it in items]
    a_ak = {it: jnp.where(mask_sl, aa[it][0:LANES, LANES:2 * LANES], 0.0).astype(BF16) for it in items}
    a_r = {it: jnp.where(mask_li2, aa[it][LANES:2 * LANES, :], 0.0).astype(BF16) for it in items}
    akv = {it: jnp.dot(a_ak[it], v2[it], preferred_element_type=F32) for it in items}
    yield
    t_list = yield from _neumann_inverse(a_ab)
    t_inv = dict(zip(items, t_list))
    yield _STATE

    s_pair = [S_ref[0, j] for j in pairs]
    for ci in range(RW_GROUP):
        its = [(ci, j) for j in pairs]
        xs = [lax.dot_general(ar[it], s_pair[it[1]].astype(BF16), nt, preferred_element_type=F32)
              for it in its]
        yield
        p2 = [_dot(t_inv[it], _stack2(xs[j][0:LR], lo) + akv[it]) for j, it in enumerate(its)]
        yield
        pv = [jnp.concatenate([p2[j].astype(BF16), v2[it]], axis=0) for j, it in enumerate(its)]
        y2 = [_dot(a_r[it], pv[j]) for j, it in enumerate(its)]
        upd = [lax.dot_general(pv[j], bk[it], (((0,), (0,)), ((), ())), preferred_element_type=F32)
               for j, it in enumerate(its)]
        s_pair = [(s_pair[j] + upd[j]) * gamma[it] for j, it in enumerate(its)]
        yield
        cr = crow(ci)
        for j in pairs:
            yb = xs[j][LR:2 * LR] + y2[j][0:LR] + y2[j][LR:2 * LR] + bonus_g[j][cr]
            mu = _pair_rowsum(yb, lo) * (1.0 / HEAD_DIM)
            dlt = yb - mu
            var = _pair_rowsum(dlt * dlt, lo) * (1.0 / HEAD_DIM)
            yn = dlt * lax.rsqrt(var + GN_EPS) * lnw_ref[:, tile(0, j)] + lnb_ref[:, tile(0, j)]
            out_rows = pl.ds(pl.multiple_of(c * gl + ci * LR, LR), LR)
            mix_ref[0, out_rows, tile(GROUP_W, j)] = yn * g[cr, tile(0, j)]
    for j in pairs:
        S_ref[0, j] = s_pair[j]


def _shift_rows(x, tail, j):
    rolled = pltpu.roll(x, j, axis=0)
    head = jnp.where(_iota((SUBLANES, 1), 0) < j, pltpu.roll(tail, j, axis=0), rolled[0:SUBLANES])
    return jnp.concatenate([head, rolled[SUBLANES:]], axis=0)


def _until_state(gen):
    for tok in gen:
        if tok is _STATE:
            return
        yield


def _drain(gen):
    for _ in gen:
        yield


def _interleave(primary, secondary, every):
    done_p = done_s = False
    k = 0
    while not (done_p and done_s):
        if not done_p:
            done_p = next(primary, _DONE) is _DONE
        k += 1
        if not done_s and (done_p or k % every == 0):
            done_s = next(secondary, _DONE) is _DONE


def _mixer_kernel(proj_ref, convw_ref, convb_ref, ib_ref, fb_ref, mlnw_ref, mu_ref, w0_ref, wup_ref,
                  a0_ref, aup_ref, gup_ref, kk_ref, ka_ref, rk_ref, lnw_ref, lnb_ref,
                  mix_ref, C_ref, n_ref, m_ref, S_ref,
                  qk_tail_ref, rw_tail_ref, q_ref, k_ref, rs_ref):
    t = pl.program_id(1)

    @pl.when(t == 0)
    def _():
        C_ref[...] = jnp.zeros_like(C_ref)
        n_ref[...] = jnp.zeros_like(n_ref)
        m_ref[...] = jnp.zeros_like(m_ref)
        S_ref[...] = jnp.zeros_like(S_ref)
        qk_tail_ref[...] = jnp.zeros_like(qk_tail_ref)
        rw_tail_ref[...] = jnp.zeros_like(rw_tail_ref)

    u = proj_ref[0, :, Q0:Q0 + 2 * GROUP_W]
    qk_tail = qk_tail_ref[...]
    acc = convb_ref[...] + u * convw_ref[CONV_W - 1:CONV_W, :]
    for jw in range(CONV_W - 1):
        acc = acc + _shift_rows(u, qk_tail, CONV_W - 1 - jw) * convw_ref[jw:jw + 1, :]
    qk_tail_ref[...] = u[TB - SUBLANES:TB]
    qk = acc * jax.nn.sigmoid(acc)
    q_ref[...] = qk[:, 0:GROUP_W]
    k_ref[...] = qk[:, GROUP_W:2 * GROUP_W] * (HEAD_DIM ** -0.5)

    rw = proj_ref[0, :, RW0:RW0 + RW_IN_W]
    rs_ref[...] = rw + mu_ref[...] * (_shift_rows(rw, rw_tail_ref[...], 1) - rw)
    rw_tail_ref[...] = rw[TB - SUBLANES:TB]

    assert RW_GROUP * LR == LM
    rwkv = lambda c: _rwkv_chunks(c, rs_ref, w0_ref, wup_ref, a0_ref, aup_ref, gup_ref, kk_ref, ka_ref, rk_ref,
                                  lnw_ref, lnb_ref, mix_ref, S_ref)
    mlstm = lambda c: _mlstm_chunk(c, proj_ref, q_ref, k_ref, ib_ref, fb_ref, mlnw_ref, mix_ref, C_ref, n_ref, m_ref)
    n_groups = TB // LM
    gens = [rwkv(c) for c in range(n_groups)]
    _interleave(_until_state(gens[0]), mlstm(0), every=ML_EVERY)
    for c in range(1, n_groups):
        _interleave(_until_state(gens[c]), _drain(gens[c - 1]), every=1)
    _interleave(_drain(gens[-1]), itertools.chain(*[mlstm(c) for c in range(1, n_groups)]), every=1)


def _mixer(proj, p):
    bsz, seq, _ = proj.shape
    row = lambda w: pl.BlockSpec((1, w), lambda b, t: (0, 0))
    mat = lambda r, w: pl.BlockSpec((r, w), lambda b, t: (0, 0))
    state4 = pl.BlockSpec((1, PAIRS, LANES, LANES), lambda b, t: (b, 0, 0, 0))
    state3 = pl.BlockSpec((1, SUBLANES, LANES), lambda b, t: (b, 0, 0))
    return pl.pallas_call(
        _mixer_kernel,
        out_shape=(
            jax.ShapeDtypeStruct((bsz, seq, 2 * GROUP_W), F32),
            jax.ShapeDtypeStruct((bsz, PAIRS, LANES, LANES), F32),
            jax.ShapeDtypeStruct((bsz, SUBLANES, LANES), F32),
            jax.ShapeDtypeStruct((bsz, SUBLANES, LANES), F32),
            jax.ShapeDtypeStruct((bsz, PAIRS, LANES, LANES), F32),
        ),
        grid=(bsz, seq // TB),
        in_specs=[
            pl.BlockSpec((1, TB, IN_WP), lambda b, t: (b, t, 0)),
            mat(CONV_W, 2 * GROUP_W), row(2 * GROUP_W), row(LANES), row(LANES), row(GROUP_W),
            row(RW_IN_W), row(GROUP_W), mat(LANES, GROUP_W), row(GROUP_W), mat(LANES, GROUP_W),
            mat(D_GATE, GROUP_W), row(GROUP_W), row(GROUP_W), row(GROUP_W), row(GROUP_W), row(GROUP_W),
        ],
        out_specs=(
            pl.BlockSpec((1, TB, 2 * GROUP_W), lambda b, t: (b, t, 0)),
            state4, state3, state3, state4,
        ),
        scratch_shapes=[
            pltpu.VMEM((SUBLANES, 2 * GROUP_W), F32),
            pltpu.VMEM((SUBLANES, RW_IN_W), F32),
            pltpu.VMEM((TB, GROUP_W), F32),
            pltpu.VMEM((TB, GROUP_W), F32),
            pltpu.VMEM((TB, RW_IN_W), F32),
        ],
        compiler_params=pltpu.CompilerParams(
            dimension_semantics=("arbitrary", "arbitrary"), vmem_limit_bytes=VMEM_LIMIT),
        name="mixer",
    )(proj, p["conv_w"], p["conv_b"], p["ib_row"], p["fb_row"], p["mlnorm_w"], p["mu"], p["w0"],
      p["wup_p"], p["a0"], p["aup_p"], p["gup"], p["k_k"], p["k_a"], p["r_k"], p["ln_w"], p["ln_b"])


def _inproj_t_kernel(x_ref, nw_ref, wml_ref, wgate_ref, wrw_ref, o_ref, ot_ref):
    o_ml, o_gate, o_rw = _project(x_ref, nw_ref, wml_ref, wgate_ref, wrw_ref)
    for lo_col, hi_col, o in ((0, GI0, o_ml), (GI0, RW0, o_gate), (RW0, IN_WP, o_rw)):
        o_ref[:, lo_col:hi_col] = o
        ot_ref[lo_col:hi_col, :] = o.T


def _inproj_t(x2d, norm_w, w_in_parts):
    rows = x2d.shape[0]
    return pl.pallas_call(
        _inproj_t_kernel,
        out_shape=(jax.ShapeDtypeStruct((rows, IN_WP), F32), jax.ShapeDtypeStruct((IN_WP, rows), F32)),
        compiler_params=pltpu.CompilerParams(vmem_limit_bytes=VMEM_LIMIT),
        name="inproj_t",
    )(x2d, norm_w, *w_in_parts)


(PC_W0, PC_A0, PC_MLNW, PC_KK, PC_KA, PC_RK, PC_LNW, PC_LNB) = range(8)
PC_CONVB = CONV_W


def _sample_kernel(pt_ref, conv0_ref, shift0_ref, m0_ref, n0_ref, C0_ref, S0_ref,
                   p1024_ref, p512_ref, mu_ref, gb_ref, wup_ref, aup_ref, gup_ref,
                   hml_ref, yrw_ref, C_ref, n_ref, m_ref, S_ref,
                   wk_s, q_s, vr_s, y_s):
    h = pl.program_id(0)
    hrow = lambda base, n=HEAD_DIM: pl.ds(pl.multiple_of(base + h * n, n), n)
    pcol = lambda j: p512_ref[hrow(0), j:j + 1]
    colsum = lambda t: jnp.sum(t, axis=0, keepdims=True)

    def conv_silu(base):
        rows = hrow(base)
        acc = p1024_ref[rows, PC_CONVB:PC_CONVB + 1] + pt_ref[rows, :] * p1024_ref[rows, CONV_W - 1:CONV_W]
        for jw in range(CONV_W - 1):
            acc = acc + conv0_ref[jw, rows, :] * p1024_ref[rows, jw:jw + 1]
        return acc * jax.nn.sigmoid(acc)

    q = conv_silu(Q0)
    k = conv_silu(K0) * (HEAD_DIM ** -0.5)
    v = pt_ref[hrow(V0), :]
    i_g = pt_ref[pl.ds(GI0 + h, 1), :] + gb_ref[pl.ds(h, 1), 0:1]
    logf = _log_sigmoid(pt_ref[pl.ds(GF0 + h, 1), :] + gb_ref[pl.ds(h, 1), 1:2])
    m0 = m0_ref[pl.ds(h, 1), :]
    m_new = jnp.maximum(logf + m0, i_g)
    ws = jnp.exp(i_g - m_new)
    dec = jnp.exp(logf + m0 - m_new)
    wk = ws * k
    n_new = dec * n0_ref[0] + wk
    n_ref[0] = n_new
    m_ref[pl.ds(h, 1), :] = m_new
    wk_s[...] = wk
    q_s[...] = q

    def c_body(kk_, num):
        ck = dec * C0_ref[0, kk_] + wk_s[pl.ds(kk_, 1), :] * v
        C_ref[0, kk_] = ck
        return num + q_s[pl.ds(kk_, 1), :] * ck

    num = lax.fori_loop(0, HEAD_DIM, c_body, jnp.zeros((HEAD_DIM, pt_ref.shape[1]), F32), unroll=SAMPLE_UNROLL)
    den = colsum(q * n_new)
    hml = jax.nn.sigmoid(pt_ref[hrow(O0), :]) * (num / jnp.maximum(jnp.abs(den), jnp.exp(-m_new)))
    hml_ref[...] = hml * lax.rsqrt(colsum(hml * hml) * (1.0 / HEAD_DIM) + EPS) * pcol(PC_MLNW)

    def shifted(base, n=HEAD_DIM):
        rw = pt_ref[hrow(RW0 + base, n) if n == HEAD_DIM else pl.ds(RW0 + base, n), :]
        rows = hrow(base, n) if n == HEAD_DIM else pl.ds(base, n)
        return rw + mu_ref[rows, :] * (shift0_ref[rows, :] - rw)

    r, kr, vr = shifted(RS_R), shifted(RS_K), shifted(RS_V)
    lora = shifted(RS_LORA, LANES)
    xg = shifted(RS_G, D_GATE)
    hw = hrow(0)
    w_raw = pcol(PC_W0) + jnp.dot(wup_ref[hw, :], jnp.tanh(lora).astype(BF16), preferred_element_type=F32)
    w = jnp.exp(-W_SCALE * jax.nn.sigmoid(w_raw))
    a = jax.nn.sigmoid(pcol(PC_A0) + jnp.dot(aup_ref[hw, :], lora.astype(BF16), preferred_element_type=F32))
    g = jnp.dot(gup_ref[hw, :], jax.nn.sigmoid(xg).astype(BF16), preferred_element_type=F32)
    kk = kr * pcol(PC_KK)
    kkn = kk / jnp.maximum(jnp.sqrt(colsum(kk * kk)), 1e-12)
    keff = kr * (1.0 + (a - 1.0) * pcol(PC_KA))
    nkk = -kkn
    ka = kkn * a
    vr_s[...] = vr

    def s_body(v_, carry):
        sv = S0_ref[0, v_]
        sn = sv * w + colsum(sv * nkk) * ka + vr_s[pl.ds(v_, 1), :] * keff
        S_ref[0, v_] = sn
        y_s[pl.ds(v_, 1), :] = colsum(sn * r)
        return carry

    lax.fori_loop(0, HEAD_DIM, s_body, 0, unroll=SAMPLE_UNROLL)
    yb = y_s[...] + colsum(r * keff * pcol(PC_RK)) * vr
    dlt = yb - colsum(yb) * (1.0 / HEAD_DIM)
    var = colsum(dlt * dlt) * (1.0 / HEAD_DIM)
    yrw_ref[...] = (dlt * lax.rsqrt(var + GN_EPS) * pcol(PC_LNW) + pcol(PC_LNB)) * g


def _sample(proj_t, conv0_t, shift0_t, m0_t, n0_t, c0_t, s0_t, p):
    bsz = proj_t.shape[1]
    whole = lambda a: pl.BlockSpec(a.shape, lambda h: (0,) * a.ndim)
    state4 = pl.BlockSpec((1, HEAD_DIM, HEAD_DIM, bsz), lambda h: (h, 0, 0, 0))
    head2 = pl.BlockSpec((HEAD_DIM, bsz), lambda h: (h, 0))
    head3 = pl.BlockSpec((1, HEAD_DIM, bsz), lambda h: (h, 0, 0))
    consts = [p["p1024"], p["p512"], p["mu_col"], p["gate_b"], p["wup_t"], p["aup_t"], p["gup_t"]]
    return pl.pallas_call(
        _sample_kernel,
        out_shape=(
            jax.ShapeDtypeStruct((GROUP_W, bsz), F32),
            jax.ShapeDtypeStruct((GROUP_W, bsz), F32),
            jax.ShapeDtypeStruct((N_HEADS, HEAD_DIM, HEAD_DIM, bsz), F32),
            jax.ShapeDtypeStruct((N_HEADS, HEAD_DIM, bsz), F32),
            jax.ShapeDtypeStruct((N_HEADS, bsz), F32),
            jax.ShapeDtypeStruct((N_HEADS, HEAD_DIM, HEAD_DIM, bsz), F32),
        ),
        grid=(N_HEADS,),
        in_specs=[whole(proj_t), whole(conv0_t), whole(shift0_t), whole(m0_t), head3, state4, state4]
                 + [whole(c) for c in consts],
        out_specs=(head2, head2, state4, head3, pl.BlockSpec((N_HEADS, bsz), lambda h: (0, 0)), state4),
        scratch_shapes=[pltpu.VMEM((HEAD_DIM, bsz), F32) for _ in range(4)],
        compiler_params=pltpu.CompilerParams(
            dimension_semantics=("arbitrary",), vmem_limit_bytes=VMEM_LIMIT),
        name="sample",
    )(proj_t, conv0_t, shift0_t, m0_t, n0_t, c0_t, s0_t, *consts)


def _layer_params(l, norm_mix_w, w_in, mlstm_conv_w, mlstm_conv_b, mlstm_i_b, mlstm_f_b, mlstm_norm_w,
                  rw_mu, rw_w0, rw_w_up, rw_a0, rw_a_up, rw_g_up, rw_k_k, rw_k_a, rw_r_k, rw_ln_w, rw_ln_b,
                  w_out, norm_mlp_w, mlp_up, mlp_down):
    ml_in = 4 * GROUP_W
    wi = w_in[l]
    zpad = jnp.zeros((D_MODEL, LANES - N_HEADS), BF16)
    w_in_parts = (
        wi[:, :ml_in].astype(BF16),
        jnp.concatenate([wi[:, ml_in:ml_in + N_HEADS].astype(BF16), zpad,
                         wi[:, ml_in + N_HEADS:ml_in + 2 * N_HEADS].astype(BF16), zpad], axis=1),
        wi[:, ml_in + 2 * N_HEADS:].astype(BF16),
    )
    gate_row = lambda b: jnp.concatenate([b, jnp.zeros((LANES - N_HEADS,), F32)])[None, :]
    zl = jnp.zeros((D_LORA, GROUP_W), F32)
    return {
        "norm_mix_w": norm_mix_w[l][None, :],
        "w_in_parts": w_in_parts,
        "conv_w": mlstm_conv_w[l], "conv_b": mlstm_conv_b[l][None, :],
        "ib_row": gate_row(mlstm_i_b[l]), "fb_row": gate_row(mlstm_f_b[l]),
        "gate_b": jnp.stack([mlstm_i_b[l], mlstm_f_b[l]], axis=1),
        "p1024": jnp.concatenate([mlstm_conv_w[l].T, mlstm_conv_b[l][:, None],
                                  jnp.zeros((2 * GROUP_W, SUBLANES - CONV_W - 1), F32)], axis=1),
        "p512": jnp.stack([rw_w0[l], rw_a0[l], mlstm_norm_w[l], rw_k_k[l], rw_k_a[l], rw_r_k[l].reshape(GROUP_W),
                           rw_ln_w[l], rw_ln_b[l]], axis=1),
        "mu_col": rw_mu[l][:, None],
        "mlnorm_w": mlstm_norm_w[l][None, :],
        "mu": rw_mu[l][None, :], "w0": rw_w0[l][None, :], "a0": rw_a0[l][None, :],
        "wup_p": jnp.concatenate([rw_w_up[l], zl], axis=0).astype(BF16),
        "aup_p": jnp.concatenate([zl, rw_a_up[l]], axis=0).astype(BF16),
        "gup": rw_g_up[l].astype(BF16),
        "wup_t": jnp.concatenate([rw_w_up[l], zl], axis=0).T.astype(BF16),
        "aup_t": jnp.concatenate([zl, rw_a_up[l]], axis=0).T.astype(BF16),
        "gup_t": rw_g_up[l].T.astype(BF16),
        "k_k": rw_k_k[l][None, :], "k_a": rw_k_a[l][None, :], "r_k": rw_r_k[l].reshape(1, GROUP_W),
        "ln_w": rw_ln_w[l][None, :], "ln_b": rw_ln_b[l][None, :],
        "w_out": w_out[l].astype(BF16), "norm_mlp_w": norm_mlp_w[l][None, :],
        "mlp_up": mlp_up[l].astype(BF16), "mlp_down": mlp_down[l].astype(BF16),
    }


def _unpair(st):
    d = HEAD_DIM
    return jnp.stack([st[:, h // 2, (h % 2) * d:(h % 2 + 1) * d, (h % 2) * d:(h % 2 + 1) * d]
                      for h in range(N_HEADS)], axis=1)


def _prompt_layer(x, p, norm_f_w):
    bsz, seq, _ = x.shape
    x2d = x.reshape(bsz * seq, D_MODEL)
    proj = _inproj(x2d, p["norm_mix_w"], p["w_in_parts"]).reshape(bsz, seq, IN_WP)
    mix, c_st, n_st, m_st, s_st = _mixer(proj, p)
    y = _outmlp(x2d, mix.reshape(bsz * seq, D_MODEL), p["w_out"], p["norm_mlp_w"], p["mlp_up"],
                p["mlp_down"], norm_f_w)
    c_new = _unpair(c_st)
    n_new = n_st[:, :PAIRS, :].reshape(bsz, N_HEADS, HEAD_DIM)
    m_new = m_st[:, :, 0]
    conv_new = proj[:, seq - (CONV_W - 1):, Q0:Q0 + 2 * GROUP_W]
    s_new = _unpair(s_st)
    shift_new = proj[:, seq - 1:, RW0:RW0 + RW_IN_W]
    return y.reshape(bsz, seq, D_MODEL), (c_new, n_new, m_new, conv_new, s_new, shift_new)


def _sample_layer(x, c0, n0, m0, conv0, s0, shift0, p, norm_f_w):
    bsz = x.shape[0]
    x2d = x.reshape(bsz, D_MODEL)
    proj, proj_t = _inproj_t(x2d, p["norm_mix_w"], p["w_in_parts"])
    hml_t, yrw_t, c_t, n_t, m_t, s_t = _sample(
        proj_t, jnp.transpose(conv0, (1, 2, 0)), shift0[:, 0, :].T, m0.T, jnp.transpose(n0, (1, 2, 0)),
        jnp.transpose(c0, (1, 2, 3, 0)), jnp.transpose(s0, (1, 2, 3, 0)), p)
    mix = jnp.concatenate([hml_t, yrw_t], axis=0).T
    y = _outmlp(x2d, mix, p["w_out"], p["norm_mlp_w"], p["mlp_up"], p["mlp_down"], norm_f_w)
    conv_new = jnp.concatenate([conv0[:, 1:], proj[:, None, Q0:Q0 + 2 * GROUP_W]], axis=1)
    shift_new = proj[:, None, RW0:RW0 + RW_IN_W]
    return y.reshape(bsz, 1, D_MODEL), (
        jnp.transpose(c_t, (3, 0, 1, 2)), jnp.transpose(n_t, (2, 0, 1)), m_t.T, conv_new,
        jnp.transpose(s_t, (3, 0, 1, 2)), shift_new)


def kernel(x_prompt, x_sample, state_mlstm_C, state_mlstm_n, state_mlstm_m, state_mlstm_conv, state_rwkv_S, state_rwkv_shift, norm_mix_w, w_in, mlstm_conv_w, mlstm_conv_b, mlstm_i_b, mlstm_f_b, mlstm_norm_w, rw_mu, rw_w0, rw_w_up, rw_a0, rw_a_up, rw_g_up, rw_k_k, rw_k_a, rw_r_k, rw_ln_w, rw_ln_b, w_out, norm_mlp_w, mlp_up, mlp_down, norm_f_w):
    depth = w_in.shape[0]
    assert depth == 1, "the final RMSNorm is fused into the layer's output kernel"
    weights = (norm_mix_w, w_in, mlstm_conv_w, mlstm_conv_b, mlstm_i_b, mlstm_f_b, mlstm_norm_w,
               rw_mu, rw_w0, rw_w_up, rw_a0, rw_a_up, rw_g_up, rw_k_k, rw_k_a, rw_r_k, rw_ln_w, rw_ln_b,
               w_out, norm_mlp_w, mlp_up, mlp_down)
    nfw = norm_f_w[None, :]
    p = _layer_params(0, *weights)
    y_p, st_p = _prompt_layer(x_prompt, p, nfw)
    y_s, st_s = _sample_layer(x_sample, state_mlstm_C[0], state_mlstm_n[0], state_mlstm_m[0],
                              state_mlstm_conv[0], state_rwkv_S[0], state_rwkv_shift[0], p, nfw)
    lead = lambda t: tuple(a[None] for a in t)
    return (y_p, y_s) + lead(st_p) + lead(st_s)
```

```python
import itertools
import math

import jax
import jax.numpy as jnp
from jax import lax
from jax.experimental import pallas as pl
from jax.experimental.pallas import tpu as pltpu

F32 = jnp.float32
BF16 = jnp.bfloat16

D_MODEL = 1024
HEAD_DIM = 64
N_HEADS = 8
PAIRS = N_HEADS // 2
GROUP_W = N_HEADS * HEAD_DIM
CONV_W = 4
D_LORA = 64
D_GATE = 128
D_FF = 4 * D_MODEL
EPS = 1e-6
GN_EPS = 64e-5
LANES = 128
SUBLANES = 8

Q0, K0, V0, O0 = 0, GROUP_W, 2 * GROUP_W, 3 * GROUP_W
GI0 = 4 * GROUP_W
GF0 = GI0 + LANES
RW0 = GF0 + LANES
RW_IN_W = 3 * GROUP_W + 2 * D_LORA + D_GATE
IN_WP = RW0 + RW_IN_W
RS_R, RS_K, RS_V = 0, GROUP_W, 2 * GROUP_W
RS_LORA = 3 * GROUP_W
RS_G = RS_LORA + 2 * D_LORA

TB = 256
LM = 128
LR = 64
ML_EVERY = 2
_DONE = object()
_STATE = object()
RW_GROUP = 2
TM_IN = 512
TM_OUT = 1024
FF_CHUNK = 1024
SAMPLE_UNROLL = 8
NEG = -1e30
W_SCALE = math.exp(-0.5)
VMEM_LIMIT = 56 * 1024 * 1024


def _dot(a, b):
    return jnp.dot(a.astype(BF16), b.astype(BF16), preferred_element_type=F32)


def _split3(x):
    h1 = x.astype(BF16)
    r1 = x - h1.astype(F32)
    h2 = r1.astype(BF16)
    h3 = (r1 - h2.astype(F32)).astype(BF16)
    return h1, h2, h3


def _dot01_l(m01, x):
    h1, h2, h3 = _split3(x)
    d = lambda h: jnp.dot(m01, h, preferred_element_type=F32)
    return d(h1) + d(h2) + d(h3)


def _log_sigmoid(x):
    return jnp.minimum(x, 0.0) - jnp.log(1.0 + jnp.exp(-jnp.abs(x)))


def _tanh(x):
    e = jnp.exp(-2.0 * jnp.abs(x))
    t = (1.0 - e) / (1.0 + e)
    return jnp.where(x < 0.0, -t, t)


def _iota(shape, dim):
    return lax.broadcasted_iota(jnp.int32, shape, dim)


def _rowsum(x):
    return jnp.sum(x, axis=1, keepdims=True)


def _pair_rowsum(x, lo):
    s_lo = _rowsum(jnp.where(lo, x, 0.0))
    s_hi = _rowsum(jnp.where(lo, 0.0, x))
    return jnp.where(lo, s_lo, s_hi)


def _stack2(x, lo):
    zero = jnp.zeros_like(x)
    return jnp.concatenate([jnp.where(lo, x, zero), jnp.where(lo, zero, x)], axis=0)


def _rms(x, w):
    ms = jnp.mean(x * x, axis=-1, keepdims=True)
    return x * lax.rsqrt(ms + EPS) * w


def _project(x_ref, nw_ref, wml_ref, wgate_ref, wrw_ref):
    xn = _rms(x_ref[...], nw_ref[...]).astype(BF16)
    return [jnp.dot(xn, w[...], preferred_element_type=F32) for w in (wml_ref, wgate_ref, wrw_ref)]


def _inproj_kernel(x_ref, nw_ref, wml_ref, wgate_ref, wrw_ref, o_ref):
    o_ml, o_gate, o_rw = _project(x_ref, nw_ref, wml_ref, wgate_ref, wrw_ref)
    o_ref[:, 0:GI0] = o_ml
    o_ref[:, GI0:RW0] = o_gate
    o_ref[:, RW0:IN_WP] = o_rw


def _inproj(x2d, norm_w, w_in_parts):
    rows = x2d.shape[0]
    tm = min(TM_IN, rows)
    return pl.pallas_call(
        _inproj_kernel,
        out_shape=jax.ShapeDtypeStruct((rows, IN_WP), F32),
        grid=(rows // tm,),
        in_specs=[
            pl.BlockSpec((tm, D_MODEL), lambda i: (i, 0)),
            pl.BlockSpec((1, D_MODEL), lambda i: (0, 0)),
        ] + [pl.BlockSpec(w.shape, lambda i: (0, 0)) for w in w_in_parts],
        out_specs=pl.BlockSpec((tm, IN_WP), lambda i: (i, 0)),
        compiler_params=pltpu.CompilerParams(
            dimension_semantics=("arbitrary",), vmem_limit_bytes=VMEM_LIMIT),
        name="inproj",
    )(x2d, norm_w, *w_in_parts)


def _outmlp_kernel(x_ref, mix_ref, wout_ref, nmw_ref, up_ref, down_ref, nfw_ref, y_ref):
    x1 = x_ref[...] + jnp.dot(mix_ref[...].astype(BF16), wout_ref[...], preferred_element_type=F32)
    xn = _rms(x1, nmw_ref[...]).astype(BF16)
    acc = x1
    for c in range(D_FF // FF_CHUNK):
        sl = slice(c * FF_CHUNK, (c + 1) * FF_CHUNK)
        hid = jnp.maximum(jnp.dot(xn, up_ref[:, sl], preferred_element_type=F32), 0.0)
        acc = acc + jnp.dot((hid * hid).astype(BF16), down_ref[sl, :], preferred_element_type=F32)
    y_ref[...] = _rms(acc, nfw_ref[...])


def _outmlp(x2d, mix2d, w_out, norm_mlp_w, mlp_up, mlp_down, norm_f_w):
    rows = x2d.shape[0]
    tm = min(TM_OUT, rows)
    const = lambda shape: pl.BlockSpec(shape, lambda i: (0, 0), pipeline_mode=pl.Buffered(1))
    return pl.pallas_call(
        _outmlp_kernel,
        out_shape=jax.ShapeDtypeStruct((rows, D_MODEL), F32),
        grid=(rows // tm,),
        in_specs=[
            pl.BlockSpec((tm, D_MODEL), lambda i: (i, 0)),
            pl.BlockSpec((tm, D_MODEL), lambda i: (i, 0)),
            const((D_MODEL, D_MODEL)),
            const((1, D_MODEL)),
            const((D_MODEL, D_FF)),
            const((D_FF, D_MODEL)),
            const((1, D_MODEL)),
        ],
        out_specs=pl.BlockSpec((tm, D_MODEL), lambda i: (i, 0)),
        compiler_params=pltpu.CompilerParams(
            dimension_semantics=("arbitrary",), vmem_limit_bytes=VMEM_LIMIT),
        name="outmlp",
    )(x2d, mix2d, w_out, norm_mlp_w, mlp_up, mlp_down, norm_f_w)


def _rwkv_token_prep(lora_t, xg, w0, wup, a0, aup, gup):
    w_raw = w0 + jnp.dot(_tanh(lora_t).astype(BF16), wup, preferred_element_type=F32)
    wlog = -W_SCALE * jax.nn.sigmoid(w_raw)
    a = jax.nn.sigmoid(a0 + jnp.dot(lora_t.astype(BF16), aup, preferred_element_type=F32))
    g = jnp.dot(jax.nn.sigmoid(xg).astype(BF16), gup, preferred_element_type=F32)
    return wlog, a, g


def _mlstm_chunk(c, proj_ref, q_ref, k_ref, ib_ref, fb_ref, mlnw_ref, mix_ref, C_ref, n_ref, m_ref):
    rows = pl.ds(pl.multiple_of(c * LM, LM), LM)
    row = _iota((LM, LM), 0)
    col = _iota((LM, LM), 1)
    causal = col <= row
    tril = jnp.where(causal, 1.0, 0.0).astype(BF16)
    blockdiag = (row < HEAD_DIM) == (col < HEAD_DIM)
    lane = _iota((1, LANES), 1)
    lo = lane < HEAD_DIM

    i_tile = proj_ref[0, rows, GI0:GI0 + LANES] + ib_ref[...]
    f_tile = _log_sigmoid(proj_ref[0, rows, GF0:GF0 + LANES] + fb_ref[...])
    bc = _dot01_l(tril, f_tile)
    u = i_tile - bc
    ut = u.T
    yield

    nt = (((1,), (1,)), ((), ()))
    pairs = range(PAIRS)
    heads = range(N_HEADS)
    tile = lambda base, j: slice(base + j * LANES, base + (j + 1) * LANES)
    hmask = lambda h: lo if h % 2 == 0 else jnp.logical_not(lo)
    q_t = [q_ref[rows, tile(0, j)] for j in pairs]
    k_t = [k_ref[rows, tile(0, j)] for j in pairs]
    v_t = [proj_ref[0, rows, tile(V0, j)] for j in pairs]
    c_pair = [C_ref[0, j] for j in pairs]
    n_row = [n_ref[0, j:j + 1, :] for j in pairs]
    k_bf = [k.astype(BF16) for k in k_t]
    v_bf = [v.astype(BF16) for v in v_t]
    s = [lax.dot_general(jnp.where(hmask(h), q_t[h // 2], 0.0).astype(BF16), k_bf[h // 2], nt,
                         preferred_element_type=F32) for h in heads]
    q_c = [_dot(q_t[j], c_pair[j]) for j in pairs]
    k_tr = [k.T for k in k_t]
    yield

    m_prev_t = m_ref[0]
    ut8 = ut[0:N_HEADS, :]
    m_last_t = jnp.maximum(jnp.max(ut8, axis=1, keepdims=True), m_prev_t)
    ws_rows = jnp.exp(ut8 - m_last_t)
    dec_t = jnp.exp(m_prev_t - m_last_t)
    m_ref[0] = bc.T[0:N_HEADS, LM - 1:LM] + m_last_t

    m_prev = [m_prev_t[h:h + 1, 0:1] for h in heads]
    u_row = [ut[h:h + 1, :] for h in heads]
    bc_col = [_rowsum(jnp.where(lane == h, bc, 0.0)) for h in heads]
    m_col = [jnp.maximum(jnp.max(jnp.where(causal, u_row[h], NEG), axis=1, keepdims=True), m_prev[h])
             for h in heads]
    p = [s[h] * jnp.exp(jnp.where(causal, u_row[h] - m_col[h], NEG)) for h in heads]
    sc = [jnp.exp(m_prev[h] - m_col[h]) for h in heads]
    yield
    pv = [jnp.dot(p[h].astype(BF16), v_bf[h // 2], preferred_element_type=F32) for h in heads]
    yield
    h_parts = []
    for h in heads:
        q_n = _rowsum(jnp.where(hmask(h), q_t[h // 2] * n_row[h // 2], 0.0))
        den = _rowsum(p[h]) + sc[h] * q_n
        num = pv[h] + sc[h] * q_c[h // 2]
        h_parts.append(num / jnp.maximum(jnp.abs(den), jnp.exp(-(bc_col[h] + m_col[h]))))

    top = _iota((LM, 1), 0) < HEAD_DIM
    kw = [k_tr[j] * jnp.where(top, ws_rows[2 * j:2 * j + 1, :], ws_rows[2 * j + 1:2 * j + 2, :])
          for j in pairs]
    upd = [jnp.dot(kw[j].astype(BF16), v_bf[j], preferred_element_type=F32) for j in pairs]
    n_upd = [jnp.dot(ws_rows.astype(BF16), k_bf[j], preferred_element_type=F32) for j in pairs]
    yield
    for j in pairs:
        dec_j = jnp.where(lo, dec_t[2 * j:2 * j + 1, :], dec_t[2 * j + 1:2 * j + 2, :])
        C_ref[0, j] = c_pair[j] * dec_j + jnp.where(blockdiag, upd[j], 0.0)
        n_ref[0, j:j + 1, :] = n_row[j] * dec_j + jnp.where(lo, n_upd[j][2 * j:2 * j + 1, :],
                                                            n_upd[j][2 * j + 1:2 * j + 2, :])

    for j in pairs:
        h_til = jnp.where(lo, h_parts[2 * j], h_parts[2 * j + 1])
        hml = jax.nn.sigmoid(proj_ref[0, rows, tile(O0, j)]) * h_til
        ms = _pair_rowsum(hml * hml, lo) * (1.0 / HEAD_DIM)
        mix_ref[0, rows, tile(0, j)] = hml * lax.rsqrt(ms + EPS) * mlnw_ref[:, tile(0, j)]


def _neumann_inverse(ns):
    shape = ns[0].shape
    eye = jnp.where(_iota(shape, 0) == _iota(shape, 1), 1.0, 0.0)
    n_sq = int(math.log2(LR)) - 1
    qs = [n.astype(BF16) for n in ns]
    ts = [eye + n for n in ns]
    sq = [jnp.dot(q, q, preferred_element_type=F32) for q in qs]
    yield
    for _ in range(n_sq - 1):
        qs = [s.astype(BF16) for s in sq]
        both = [jnp.dot(q, jnp.concatenate([q, t.astype(BF16)], axis=1), preferred_element_type=F32)
                for q, t in zip(qs, ts)]
        sq = [b[:, 0:LANES] for b in both]
        ts = [t + b[:, LANES:2 * LANES] for t, b in zip(ts, both)]
        yield
    return [(t + jnp.dot(s.astype(BF16), t.astype(BF16), preferred_element_type=F32)).astype(BF16)
            for s, t in zip(sq, ts)]


def _rwkv_chunks(c, rs_ref, w0_ref, wup_ref, a0_ref, aup_ref, gup_ref, kk_ref, ka_ref, rk_ref,
                 lnw_ref, lnb_ref, mix_ref, S_ref):
    gl = RW_GROUP * LR
    rows = pl.ds(pl.multiple_of(c * gl, gl), gl)
    lane = _iota((1, LANES), 1)
    lo = lane < HEAD_DIM
    row = _iota((LANES, LANES), 0)
    col = _iota((LANES, LANES), 1)
    same = (row < LR) == (col < LR)
    mask_sl = jnp.logical_and(same, col < row)
    mask_li = jnp.logical_and(same, col <= row)
    mask_li2 = jnp.concatenate([mask_li, mask_li], axis=1)
    grow = _iota((gl, gl), 0)
    gcol = _iota((gl, gl), 1)
    tril = jnp.where(jnp.logical_and(gcol <= grow, gcol >= (grow // LR) * LR), 1.0, 0.0).astype(BF16)
    nt = (((1,), (1,)), ((), ()))
    pairs = range(PAIRS)
    items = [(ci, j) for ci in range(RW_GROUP) for j in pairs]
    tile = lambda base, j: slice(base + j * LANES, base + (j + 1) * LANES)
    crow = lambda ci: slice(ci * LR, (ci + 1) * LR)

    wlog, a, g = _rwkv_token_prep(rs_ref[rows, RS_LORA:RS_LORA + LANES], rs_ref[rows, RS_G:RS_G + D_GATE],
                                  w0_ref[...], wup_ref[...], a0_ref[...], aup_ref[...], gup_ref[...])
    lw = _dot01_l(tril, wlog)
    e_in = jnp.exp(lw)
    e_ex = jnp.exp(lw - wlog)
    e_ng = jnp.exp(-lw)

    r_g = [rs_ref[rows, tile(RS_R, j)] for j in pairs]
    kr_g = [rs_ref[rows, tile(RS_K, j)] for j in pairs]
    v_g = [rs_ref[rows, tile(RS_V, j)] for j in pairs]
    a_g = [a[:, tile(0, j)] for j in pairs]
    kk = [kr_g[j] * kk_ref[:, tile(0, j)] for j in pairs]
    kkn = [kk[j] * lax.rsqrt(jnp.maximum(_pair_rowsum(kk[j] * kk[j], lo), 1e-24)) for j in pairs]
    keff_g = [kr_g[j] * (1.0 + (a_g[j] - 1.0) * ka_ref[:, tile(0, j)]) for j in pairs]
    alpha_g = [-kkn[j] * e_ex[:, tile(0, j)] for j in pairs]
    rb_g = [r_g[j] * e_in[:, tile(0, j)] for j in pairs]
    beta_g = [kkn[j] * a_g[j] * e_ng[:, tile(0, j)] for j in pairs]
    ktl_g = [keff_g[j] * e_ng[:, tile(0, j)] for j in pairs]
    bonus_g = [_pair_rowsum(r_g[j] * keff_g[j] * rk_ref[:, tile(0, j)], lo) * v_g[j] for j in pairs]

    lhs, rhs, v2, bk, gamma, ar = {}, {}, {}, {}, {}, {}
    for it in items:
        ci, j = it
        cr = crow(ci)
        ar[it] = jnp.concatenate([alpha_g[j][cr], rb_g[j][cr]], axis=0).astype(BF16)
        lhs[it] = jnp.concatenate([_stack2(ar[it][0:LR], lo), _stack2(ar[it][LR:2 * LR], lo)], axis=0)
        beta_c, ktl_c = beta_g[j][cr].astype(BF16), ktl_g[j][cr].astype(BF16)
        rhs[it] = jnp.concatenate([beta_c, beta_c, ktl_c, ktl_c], axis=0)
        v2[it] = _stack2(v_g[j][cr].astype(BF16), lo)
        bk[it] = jnp.concatenate([_stack2(beta_c, lo), _stack2(ktl_c, lo)], axis=0)
        gamma[it] = e_in[(ci + 1) * LR - 1:(ci + 1) * LR, tile(0, j)]
    aa = {it: lax.dot_general(lhs[it], rhs[it], nt, preferred_element_type=F32) for it in items}
    yield
    a_ab = [jnp.where(mask_sl, aa[it][0:LANES, 0:LANES], 0.0) for it in items]
    a_ak = {it: jnp.where(mask_sl, aa[it][0:LANES, LANES:2 * LANES], 0.0).astype(BF16) for it in items}
    a_r = {it: jnp.where(mask_li2, aa[it][LANES:2 * LANES, :], 0.0).astype(BF16) for it in items}
    akv = {it: jnp.dot(a_ak[it], v2[it], preferred_element_type=F32) for it in items}
    yield
    t_list = yield from _neumann_inverse(a_ab)
    t_inv = dict(zip(items, t_list))
    yield _STATE

    s_pair = [S_ref[0, j] for j in pairs]
    for ci in range(RW_GROUP):
        its = [(ci, j) for j in pairs]
        xs = [lax.dot_general(ar[it], s_pair[it[1]].astype(BF16), nt, preferred_element_type=F32)
              for it in its]
        yield
        p2 = [_dot(t_inv[it], _stack2(xs[j][0:LR], lo) + akv[it]) for j, it in enumerate(its)]
        yield
        pv = [jnp.concatenate([p2[j].astype(BF16), v2[it]], axis=0) for j, it in enumerate(its)]
        y2 = [_dot(a_r[it], pv[j]) for j, it in enumerate(its)]
        upd = [lax.dot_general(pv[j], bk[it], (((0,), (0,)), ((), ())), preferred_element_type=F32)
               for j, it in enumerate(its)]
        s_pair = [(s_pair[j] + upd[j]) * gamma[it] for j, it in enumerate(its)]
        yield
        cr = crow(ci)
        for j in pairs:
            yb = xs[j][LR:2 * LR] + y2[j][0:LR] + y2[j][LR:2 * LR] + bonus_g[j][cr]
            mu = _pair_rowsum(yb, lo) * (1.0 / HEAD_DIM)
            dlt = yb - mu
            var = _pair_rowsum(dlt * dlt, lo) * (1.0 / HEAD_DIM)
            yn = dlt * lax.rsqrt(var + GN_EPS) * lnw_ref[:, tile(0, j)] + lnb_ref[:, tile(0, j)]
            out_rows = pl.ds(pl.multiple_of(c * gl + ci * LR, LR), LR)
            mix_ref[0, out_rows, tile(GROUP_W, j)] = yn * g[cr, tile(0, j)]
    for j in pairs:
        S_ref[0, j] = s_pair[j]


def _shift_rows(x, tail, j):
    rolled = pltpu.roll(x, j, axis=0)
    head = jnp.where(_iota((SUBLANES, 1), 0) < j, pltpu.roll(tail, j, axis=0), rolled[0:SUBLANES])
    return jnp.concatenate([head, rolled[SUBLANES:]], axis=0)


def _until_state(gen):
    for tok in gen:
        if tok is _STATE:
            return
        yield


def _drain(gen):
    for _ in gen:
        yield


def _interleave(primary, secondary, every):
    done_p = done_s = False
    k = 0
    while not (done_p and done_s):
        if not done_p:
            done_p = next(primary, _DONE) is _DONE
        k += 1
        if not done_s and (done_p or k % every == 0):
            done_s = next(secondary, _DONE) is _DONE


def _mixer_kernel(proj_ref, convw_ref, convb_ref, ib_ref, fb_ref, mlnw_ref, mu_ref, w0_ref, wup_ref,
                  a0_ref, aup_ref, gup_ref, kk_ref, ka_ref, rk_ref, lnw_ref, lnb_ref,
                  mix_ref, C_ref, n_ref, m_ref, S_ref,
                  qk_tail_ref, rw_tail_ref, q_ref, k_ref, rs_ref):
    t = pl.program_id(1)

    @pl.when(t == 0)
    def _():
        C_ref[...] = jnp.zeros_like(C_ref)
        n_ref[...] = jnp.zeros_like(n_ref)
        m_ref[...] = jnp.zeros_like(m_ref)
        S_ref[...] = jnp.zeros_like(S_ref)
        qk_tail_ref[...] = jnp.zeros_like(qk_tail_ref)
        rw_tail_ref[...] = jnp.zeros_like(rw_tail_ref)

    u = proj_ref[0, :, Q0:Q0 + 2 * GROUP_W]
    qk_tail = qk_tail_ref[...]
    acc = convb_ref[...] + u * convw_ref[CONV_W - 1:CONV_W, :]
    for jw in range(CONV_W - 1):
        acc = acc + _shift_rows(u, qk_tail, CONV_W - 1 - jw) * convw_ref[jw:jw + 1, :]
    qk_tail_ref[...] = u[TB - SUBLANES:TB]
    qk = acc * jax.nn.sigmoid(acc)
    q_ref[...] = qk[:, 0:GROUP_W]
    k_ref[...] = qk[:, GROUP_W:2 * GROUP_W] * (HEAD_DIM ** -0.5)

    rw = proj_ref[0, :, RW0:RW0 + RW_IN_W]
    rs_ref[...] = rw + mu_ref[...] * (_shift_rows(rw, rw_tail_ref[...], 1) - rw)
    rw_tail_ref[...] = rw[TB - SUBLANES:TB]

    assert RW_GROUP * LR == LM
    rwkv = lambda c: _rwkv_chunks(c, rs_ref, w0_ref, wup_ref, a0_ref, aup_ref, gup_ref, kk_ref, ka_ref, rk_ref,
                                  lnw_ref, lnb_ref, mix_ref, S_ref)
    mlstm = lambda c: _mlstm_chunk(c, proj_ref, q_ref, k_ref, ib_ref, fb_ref, mlnw_ref, mix_ref, C_ref, n_ref, m_ref)
    n_groups = TB // LM
    gens = [rwkv(c) for c in range(n_groups)]
    _interleave(_until_state(gens[0]), mlstm(0), every=ML_EVERY)
    for c in range(1, n_groups):
        _interleave(_until_state(gens[c]), _drain(gens[c - 1]), every=1)
    _interleave(_drain(gens[-1]), itertools.chain(*[mlstm(c) for c in range(1, n_groups)]), every=1)


def _mixer(proj, p):
    bsz, seq, _ = proj.shape
    row = lambda w: pl.BlockSpec((1, w), lambda b, t: (0, 0))
    mat = lambda r, w: pl.BlockSpec((r, w), lambda b, t: (0, 0))
    state4 = pl.BlockSpec((1, PAIRS, LANES, LANES), lambda b, t: (b, 0, 0, 0))
    state3 = pl.BlockSpec((1, SUBLANES, LANES), lambda b, t: (b, 0, 0))
    return pl.pallas_call(
        _mixer_kernel,
        out_shape=(
            jax.ShapeDtypeStruct((bsz, seq, 2 * GROUP_W), F32),
            jax.ShapeDtypeStruct((bsz, PAIRS, LANES, LANES), F32),
            jax.ShapeDtypeStruct((bsz, SUBLANES, LANES), F32),
            jax.ShapeDtypeStruct((bsz, SUBLANES, LANES), F32),
            jax.ShapeDtypeStruct((bsz, PAIRS, LANES, LANES), F32),
        ),
        grid=(bsz, seq // TB),
        in_specs=[
            pl.BlockSpec((1, TB, IN_WP), lambda b, t: (b, t, 0)),
            mat(CONV_W, 2 * GROUP_W), row(2 * GROUP_W), row(LANES), row(LANES), row(GROUP_W),
            row(RW_IN_W), row(GROUP_W), mat(LANES, GROUP_W), row(GROUP_W), mat(LANES, GROUP_W),
            mat(D_GATE, GROUP_W), row(GROUP_W), row(GROUP_W), row(GROUP_W), row(GROUP_W), row(GROUP_W),
        ],
        out_specs=(
            pl.BlockSpec((1, TB, 2 * GROUP_W), lambda b, t: (b, t, 0)),
            state4, state3, state3, state4,
        ),
        scratch_shapes=[
            pltpu.VMEM((SUBLANES, 2 * GROUP_W), F32),
            pltpu.VMEM((SUBLANES, RW_IN_W), F32),
            pltpu.VMEM((TB, GROUP_W), F32),
            pltpu.VMEM((TB, GROUP_W), F32),
            pltpu.VMEM((TB, RW_IN_W), F32),
        ],
        compiler_params=pltpu.CompilerParams(
            dimension_semantics=("arbitrary", "arbitrary"), vmem_limit_bytes=VMEM_LIMIT),
        name="mixer",
    )(proj, p["conv_w"], p["conv_b"], p["ib_row"], p["fb_row"], p["mlnorm_w"], p["mu"], p["w0"],
      p["wup_p"], p["a0"], p["aup_p"], p["gup"], p["k_k"], p["k_a"], p["r_k"], p["ln_w"], p["ln_b"])


def _inproj_t_kernel(x_ref, nw_ref, wml_ref, wgate_ref, wrw_ref, o_ref, ot_ref):
    o_ml, o_gate, o_rw = _project(x_ref, nw_ref, wml_ref, wgate_ref, wrw_ref)
    for lo_col, hi_col, o in ((0, GI0, o_ml), (GI0, RW0, o_gate), (RW0, IN_WP, o_rw)):
        o_ref[:, lo_col:hi_col] = o
        ot_ref[lo_col:hi_col, :] = o.T


def _inproj_t(x2d, norm_w, w_in_parts):
    rows = x2d.shape[0]
    return pl.pallas_call(
        _inproj_t_kernel,
        out_shape=(jax.ShapeDtypeStruct((rows, IN_WP), F32), jax.ShapeDtypeStruct((IN_WP, rows), F32)),
        compiler_params=pltpu.CompilerParams(vmem_limit_bytes=VMEM_LIMIT),
        name="inproj_t",
    )(x2d, norm_w, *w_in_parts)


(PC_W0, PC_A0, PC_MLNW, PC_KK, PC_KA, PC_RK, PC_LNW, PC_LNB) = range(8)
PC_CONVB = CONV_W


def _sample_kernel(pt_ref, conv0_ref, shift0_ref, m0_ref, n0_ref, C0_ref, S0_ref,
                   p1024_ref, p512_ref, mu_ref, gb_ref, wup_ref, aup_ref, gup_ref,
                   hml_ref, yrw_ref, C_ref, n_ref, m_ref, S_ref,
                   wk_s, q_s, vr_s, y_s):
    h = pl.program_id(0)
    hrow = lambda base, n=HEAD_DIM: pl.ds(pl.multiple_of(base + h * n, n), n)
    pcol = lambda j: p512_ref[hrow(0), j:j + 1]
    colsum = lambda t: jnp.sum(t, axis=0, keepdims=True)

    def conv_silu(base):
        rows = hrow(base)
        acc = p1024_ref[rows, PC_CONVB:PC_CONVB + 1] + pt_ref[rows, :] * p1024_ref[rows, CONV_W - 1:CONV_W]
        for jw in range(CONV_W - 1):
            acc = acc + conv0_ref[jw, rows, :] * p1024_ref[rows, jw:jw + 1]
        return acc * jax.nn.sigmoid(acc)

    q = conv_silu(Q0)
    k = conv_silu(K0) * (HEAD_DIM ** -0.5)
    v = pt_ref[hrow(V0), :]
    i_g = pt_ref[pl.ds(GI0 + h, 1), :] + gb_ref[pl.ds(h, 1), 0:1]
    logf = _log_sigmoid(pt_ref[pl.ds(GF0 + h, 1), :] + gb_ref[pl.ds(h, 1), 1:2])
    m0 = m0_ref[pl.ds(h, 1), :]
    m_new = jnp.maximum(logf + m0, i_g)
    ws = jnp.exp(i_g - m_new)
    dec = jnp.exp(logf + m0 - m_new)
    wk = ws * k
    n_new = dec * n0_ref[0] + wk
    n_ref[0] = n_new
    m_ref[pl.ds(h, 1), :] = m_new
    wk_s[...] = wk
    q_s[...] = q

    def c_body(kk_, num):
        ck = dec * C0_ref[0, kk_] + wk_s[pl.ds(kk_, 1), :] * v
        C_ref[0, kk_] = ck
        return num + q_s[pl.ds(kk_, 1), :] * ck

    num = lax.fori_loop(0, HEAD_DIM, c_body, jnp.zeros((HEAD_DIM, pt_ref.shape[1]), F32), unroll=SAMPLE_UNROLL)
    den = colsum(q * n_new)
    hml = jax.nn.sigmoid(pt_ref[hrow(O0), :]) * (num / jnp.maximum(jnp.abs(den), jnp.exp(-m_new)))
    hml_ref[...] = hml * lax.rsqrt(colsum(hml * hml) * (1.0 / HEAD_DIM) + EPS) * pcol(PC_MLNW)

    def shifted(base, n=HEAD_DIM):
        rw = pt_ref[hrow(RW0 + base, n) if n == HEAD_DIM else pl.ds(RW0 + base, n), :]
        rows = hrow(base, n) if n == HEAD_DIM else pl.ds(base, n)
        return rw + mu_ref[rows, :] * (shift0_ref[rows, :] - rw)

    r, kr, vr = shifted(RS_R), shifted(RS_K), shifted(RS_V)
    lora = shifted(RS_LORA, LANES)
    xg = shifted(RS_G, D_GATE)
    hw = hrow(0)
    w_raw = pcol(PC_W0) + jnp.dot(wup_ref[hw, :], jnp.tanh(lora).astype(BF16), preferred_element_type=F32)
    w = jnp.exp(-W_SCALE * jax.nn.sigmoid(w_raw))
    a = jax.nn.sigmoid(pcol(PC_A0) + jnp.dot(aup_ref[hw, :], lora.astype(BF16), preferred_element_type=F32))
    g = jnp.dot(gup_ref[hw, :], jax.nn.sigmoid(xg).astype(BF16), preferred_element_type=F32)
    kk = kr * pcol(PC_KK)
    kkn = kk / jnp.maximum(jnp.sqrt(colsum(kk * kk)), 1e-12)
    keff = kr * (1.0 + (a - 1.0) * pcol(PC_KA))
    nkk = -kkn
    ka = kkn * a
    vr_s[...] = vr

    def s_body(v_, carry):
        sv = S0_ref[0, v_]
        sn = sv * w + colsum(sv * nkk) * ka + vr_s[pl.ds(v_, 1), :] * keff
        S_ref[0, v_] = sn
        y_s[pl.ds(v_, 1), :] = colsum(sn * r)
        return carry

    lax.fori_loop(0, HEAD_DIM, s_body, 0, unroll=SAMPLE_UNROLL)
    yb = y_s[...] + colsum(r * keff * pcol(PC_RK)) * vr
    dlt = yb - colsum(yb) * (1.0 / HEAD_DIM)
    var = colsum(dlt * dlt) * (1.0 / HEAD_DIM)
    yrw_ref[...] = (dlt * lax.rsqrt(var + GN_EPS) * pcol(PC_LNW) + pcol(PC_LNB)) * g


def _sample(proj_t, conv0_t, shift0_t, m0_t, n0_t, c0_t, s0_t, p):
    bsz = proj_t.shape[1]
    whole = lambda a: pl.BlockSpec(a.shape, lambda h: (0,) * a.ndim)
    state4 = pl.BlockSpec((1, HEAD_DIM, HEAD_DIM, bsz), lambda h: (h, 0, 0, 0))
    head2 = pl.BlockSpec((HEAD_DIM, bsz), lambda h: (h, 0))
    head3 = pl.BlockSpec((1, HEAD_DIM, bsz), lambda h: (h, 0, 0))
    consts = [p["p1024"], p["p512"], p["mu_col"], p["gate_b"], p["wup_t"], p["aup_t"], p["gup_t"]]
    return pl.pallas_call(
        _sample_kernel,
        out_shape=(
            jax.ShapeDtypeStruct((GROUP_W, bsz), F32),
            jax.ShapeDtypeStruct((GROUP_W, bsz), F32),
            jax.ShapeDtypeStruct((N_HEADS, HEAD_DIM, HEAD_DIM, bsz), F32),
            jax.ShapeDtypeStruct((N_HEADS, HEAD_DIM, bsz), F32),
            jax.ShapeDtypeStruct((N_HEADS, bsz), F32),
            jax.ShapeDtypeStruct((N_HEADS, HEAD_DIM, HEAD_DIM, bsz), F32),
        ),
        grid=(N_HEADS,),
        in_specs=[whole(proj_t), whole(conv0_t), whole(shift0_t), whole(m0_t), head3, state4, state4]
                 + [whole(c) for c in consts],
        out_specs=(head2, head2, state4, head3, pl.BlockSpec((N_HEADS, bsz), lambda h: (0, 0)), state4),
        scratch_shapes=[pltpu.VMEM((HEAD_DIM, bsz), F32) for _ in range(4)],
        compiler_params=pltpu.CompilerParams(
            dimension_semantics=("arbitrary",), vmem_limit_bytes=VMEM_LIMIT),
        name="sample",
    )(proj_t, conv0_t, shift0_t, m0_t, n0_t, c0_t, s0_t, *consts)


def _layer_params(l, norm_mix_w, w_in, mlstm_conv_w, mlstm_conv_b, mlstm_i_b, mlstm_f_b, mlstm_norm_w,
                  rw_mu, rw_w0, rw_w_up, rw_a0, rw_a_up, rw_g_up, rw_k_k, rw_k_a, rw_r_k, rw_ln_w, rw_ln_b,
                  w_out, norm_mlp_w, mlp_up, mlp_down):
    ml_in = 4 * GROUP_W
    wi = w_in[l]
    zpad = jnp.zeros((D_MODEL, LANES - N_HEADS), BF16)
    w_in_parts = (
        wi[:, :ml_in].astype(BF16),
        jnp.concatenate([wi[:, ml_in:ml_in + N_HEADS].astype(BF16), zpad,
                         wi[:, ml_in + N_HEADS:ml_in + 2 * N_HEADS].astype(BF16), zpad], axis=1),
        wi[:, ml_in + 2 * N_HEADS:].astype(BF16),
    )
    gate_row = lambda b: jnp.concatenate([b, jnp.zeros((LANES - N_HEADS,), F32)])[None, :]
    zl = jnp.zeros((D_LORA, GROUP_W), F32)
    return {
        "norm_mix_w": norm_mix_w[l][None, :],
        "w_in_parts": w_in_parts,
        "conv_w": mlstm_conv_w[l], "conv_b": mlstm_conv_b[l][None, :],
        "ib_row": gate_row(mlstm_i_b[l]), "fb_row": gate_row(mlstm_f_b[l]),
        "gate_b": jnp.stack([mlstm_i_b[l], mlstm_f_b[l]], axis=1),
        "p1024": jnp.concatenate([mlstm_conv_w[l].T, mlstm_conv_b[l][:, None],
                                  jnp.zeros((2 * GROUP_W, SUBLANES - CONV_W - 1), F32)], axis=1),
        "p512": jnp.stack([rw_w0[l], rw_a0[l], mlstm_norm_w[l], rw_k_k[l], rw_k_a[l], rw_r_k[l].reshape(GROUP_W),
                           rw_ln_w[l], rw_ln_b[l]], axis=1),
        "mu_col": rw_mu[l][:, None],
        "mlnorm_w": mlstm_norm_w[l][None, :],
        "mu": rw_mu[l][None, :], "w0": rw_w0[l][None, :], "a0": rw_a0[l][None, :],
        "wup_p": jnp.concatenate([rw_w_up[l], zl], axis=0).astype(BF16),
        "aup_p": jnp.concatenate([zl, rw_a_up[l]], axis=0).astype(BF16),
        "gup": rw_g_up[l].astype(BF16),
        "wup_t": jnp.concatenate([rw_w_up[l], zl], axis=0).T.astype(BF16),
        "aup_t": jnp.concatenate([zl, rw_a_up[l]], axis=0).T.astype(BF16),
        "gup_t": rw_g_up[l].T.astype(BF16),
        "k_k": rw_k_k[l][None, :], "k_a": rw_k_a[l][None, :], "r_k": rw_r_k[l].reshape(1, GROUP_W),
        "ln_w": rw_ln_w[l][None, :], "ln_b": rw_ln_b[l][None, :],
        "w_out": w_out[l].astype(BF16), "norm_mlp_w": norm_mlp_w[l][None, :],
        "mlp_up": mlp_up[l].astype(BF16), "mlp_down": mlp_down[l].astype(BF16),
    }


def _unpair(st):
    d = HEAD_DIM
    blocks = st.reshape(st.shape[0], PAIRS, 2, d, 2, d)
    diag = jnp.stack([blocks[:, :, e, :, e, :] for e in range(2)], axis=2)
    return diag.reshape(st.shape[0], N_HEADS, d, d)


def _prompt_layer(x, p, norm_f_w):
    bsz, seq, _ = x.shape
    x2d = x.reshape(bsz * seq, D_MODEL)
    proj = _inproj(x2d, p["norm_mix_w"], p["w_in_parts"]).reshape(bsz, seq, IN_WP)
    mix, c_st, n_st, m_st, s_st = _mixer(proj, p)
    y = _outmlp(x2d, mix.reshape(bsz * seq, D_MODEL), p["w_out"], p["norm_mlp_w"], p["mlp_up"],
                p["mlp_down"], norm_f_w)
    c_new = _unpair(c_st)
    n_new = n_st[:, :PAIRS, :].reshape(bsz, N_HEADS, HEAD_DIM)
    m_new = m_st[:, :, 0]
    conv_new = proj[:, seq - (CONV_W - 1):, Q0:Q0 + 2 * GROUP_W]
    s_new = _unpair(s_st)
    shift_new = proj[:, seq - 1:, RW0:RW0 + RW_IN_W]
    return y.reshape(bsz, seq, D_MODEL), (c_new, n_new, m_new, conv_new, s_new, shift_new)


def _sample_layer(x, c0, n0, m0, conv0, s0, shift0, p, norm_f_w):
    bsz = x.shape[0]
    x2d = x.reshape(bsz, D_MODEL)
    proj, proj_t = _inproj_t(x2d, p["norm_mix_w"], p["w_in_parts"])
    hml_t, yrw_t, c_t, n_t, m_t, s_t = _sample(
        proj_t, jnp.transpose(conv0, (1, 2, 0)), shift0[:, 0, :].T, m0.T, jnp.transpose(n0, (1, 2, 0)),
        jnp.transpose(c0, (1, 2, 3, 0)), jnp.transpose(s0, (1, 2, 3, 0)), p)
    mix = jnp.concatenate([hml_t, yrw_t], axis=0).T
    y = _outmlp(x2d, mix, p["w_out"], p["norm_mlp_w"], p["mlp_up"], p["mlp_down"], norm_f_w)
    conv_new = jnp.concatenate([conv0[:, 1:], proj[:, None, Q0:Q0 + 2 * GROUP_W]], axis=1)
    shift_new = proj[:, None, RW0:RW0 + RW_IN_W]
    return y.reshape(bsz, 1, D_MODEL), (
        jnp.transpose(c_t, (3, 0, 1, 2)), jnp.transpose(n_t, (2, 0, 1)), m_t.T, conv_new,
        jnp.transpose(s_t, (3, 0, 1, 2)), shift_new)


def kernel(x_prompt, x_sample, state_mlstm_C, state_mlstm_n, state_mlstm_m, state_mlstm_conv, state_rwkv_S, state_rwkv_shift, norm_mix_w, w_in, mlstm_conv_w, mlstm_conv_b, mlstm_i_b, mlstm_f_b, mlstm_norm_w, rw_mu, rw_w0, rw_w_up, rw_a0, rw_a_up, rw_g_up, rw_k_k, rw_k_a, rw_r_k, rw_ln_w, rw_ln_b, w_out, norm_mlp_w, mlp_up, mlp_down, norm_f_w):
    depth = w_in.shape[0]
    assert depth == 1, "the final RMSNorm is fused into the layer's output kernel"
    weights = (norm_mix_w, w_in, mlstm_conv_w, mlstm_conv_b, mlstm_i_b, mlstm_f_b, mlstm_norm_w,
               rw_mu, rw_w0, rw_w_up, rw_a0, rw_a_up, rw_g_up, rw_k_k, rw_k_a, rw_r_k, rw_ln_w, rw_ln_b,
               w_out, norm_mlp_w, mlp_up, mlp_down)
    nfw = norm_f_w[None, :]
    p = _layer_params(0, *weights)
    y_p, st_p = _prompt_layer(x_prompt, p, nfw)
    y_s, st_s = _sample_layer(x_sample, state_mlstm_C[0], state_mlstm_n[0], state_mlstm_m[0],
                              state_mlstm_conv[0], state_rwkv_S[0], state_rwkv_shift[0], p, nfw)
    lead = lambda t: tuple(a[None] for a in t)
    return (y_p, y_s) + lead(st_p) + lead(st_s)
```

```python
import itertools
import math

import jax
import jax.numpy as jnp
from jax import lax
from jax.experimental import pallas as pl
from jax.experimental.pallas import tpu as pltpu

F32 = jnp.float32
BF16 = jnp.bfloat16

D_MODEL = 1024
HEAD_DIM = 64
N_HEADS = 8
PAIRS = N_HEADS // 2
GROUP_W = N_HEADS * HEAD_DIM
CONV_W = 4
D_LORA = 64
D_GATE = 128
D_FF = 4 * D_MODEL
EPS = 1e-6
GN_EPS = 64e-5
LANES = 128
SUBLANES = 8

Q0, K0, V0, O0 = 0, GROUP_W, 2 * GROUP_W, 3 * GROUP_W
GI0 = 4 * GROUP_W
GF0 = GI0 + LANES
RW0 = GF0 + LANES
RW_IN_W = 3 * GROUP_W + 2 * D_LORA + D_GATE
IN_WP = RW0 + RW_IN_W
RS_R, RS_K, RS_V = 0, GROUP_W, 2 * GROUP_W
RS_LORA = 3 * GROUP_W
RS_G = RS_LORA + 2 * D_LORA

TB = 256
LM = 128
LR = 64
ML_EVERY = 2
_DONE = object()
_STATE = object()
RW_GROUP = 2
TM_IN = 512
TM_OUT = 1024
FF_CHUNK = 1024
SAMPLE_UNROLL = 8
NEG = -1e30
W_SCALE = math.exp(-0.5)
VMEM_LIMIT = 56 * 1024 * 1024


def _dot(a, b):
    return jnp.dot(a.astype(BF16), b.astype(BF16), preferred_element_type=F32)


def _split3(x):
    h1 = x.astype(BF16)
    r1 = x - h1.astype(F32)
    h2 = r1.astype(BF16)
    h3 = (r1 - h2.astype(F32)).astype(BF16)
    return h1, h2, h3


def _dot01_l(m01, x):
    h1, h2, h3 = _split3(x)
    d = lambda h: jnp.dot(m01, h, preferred_element_type=F32)
    return d(h1) + d(h2) + d(h3)


def _log_sigmoid(x):
    return jnp.minimum(x, 0.0) - jnp.log(1.0 + jnp.exp(-jnp.abs(x)))


def _tanh(x):
    e = jnp.exp(-2.0 * jnp.abs(x))
    t = (1.0 - e) / (1.0 + e)
    return jnp.where(x < 0.0, -t, t)


def _iota(shape, dim):
    return lax.broadcasted_iota(jnp.int32, shape, dim)


def _rowsum(x):
    return jnp.sum(x, axis=1, keepdims=True)


def _pair_rowsum(x, lo):
    s_lo = _rowsum(jnp.where(lo, x, 0.0))
    s_hi = _rowsum(jnp.where(lo, 0.0, x))
    return jnp.where(lo, s_lo, s_hi)


def _stack2(x, lo):
    zero = jnp.zeros_like(x)
    return jnp.concatenate([jnp.where(lo, x, zero), jnp.where(lo, zero, x)], axis=0)


def _rms(x, w):
    ms = jnp.mean(x * x, axis=-1, keepdims=True)
    return x * lax.rsqrt(ms + EPS) * w


def _project(x_ref, nw_ref, wml_ref, wgate_ref, wrw_ref):
    xn = _rms(x_ref[...], nw_ref[...]).astype(BF16)
    return [jnp.dot(xn, w[...], preferred_element_type=F32) for w in (wml_ref, wgate_ref, wrw_ref)]


def _inproj_kernel(x_ref, nw_ref, wml_ref, wgate_ref, wrw_ref, o_ref):
    o_ml, o_gate, o_rw = _project(x_ref, nw_ref, wml_ref, wgate_ref, wrw_ref)
    o_ref[:, 0:GI0] = o_ml
    o_ref[:, GI0:RW0] = o_gate
    o_ref[:, RW0:IN_WP] = o_rw


def _inproj(x2d, norm_w, w_in_parts):
    rows = x2d.shape[0]
    tm = min(TM_IN, rows)
    return pl.pallas_call(
        _inproj_kernel,
        out_shape=jax.ShapeDtypeStruct((rows, IN_WP), F32),
        grid=(rows // tm,),
        in_specs=[
            pl.BlockSpec((tm, D_MODEL), lambda i: (i, 0)),
            pl.BlockSpec((1, D_MODEL), lambda i: (0, 0)),
        ] + [pl.BlockSpec(w.shape, lambda i: (0, 0)) for w in w_in_parts],
        out_specs=pl.BlockSpec((tm, IN_WP), lambda i: (i, 0)),
        compiler_params=pltpu.CompilerParams(
            dimension_semantics=("arbitrary",), vmem_limit_bytes=VMEM_LIMIT),
        name="inproj",
    )(x2d, norm_w, *w_in_parts)


def _outmlp_kernel(x_ref, mix_ref, wout_ref, nmw_ref, up_ref, down_ref, nfw_ref, y_ref):
    x1 = x_ref[...] + jnp.dot(mix_ref[...].astype(BF16), wout_ref[...], preferred_element_type=F32)
    xn = _rms(x1, nmw_ref[...]).astype(BF16)
    acc = x1
    for c in range(D_FF // FF_CHUNK):
        sl = slice(c * FF_CHUNK, (c + 1) * FF_CHUNK)
        hid = jnp.maximum(jnp.dot(xn, up_ref[:, sl], preferred_element_type=F32), 0.0)
        acc = acc + jnp.dot((hid * hid).astype(BF16), down_ref[sl, :], preferred_element_type=F32)
    y_ref[...] = _rms(acc, nfw_ref[...])


def _outmlp(x2d, mix2d, w_out, norm_mlp_w, mlp_up, mlp_down, norm_f_w):
    rows = x2d.shape[0]
    tm = min(TM_OUT, rows)
    const = lambda shape: pl.BlockSpec(shape, lambda i: (0, 0), pipeline_mode=pl.Buffered(1))
    return pl.pallas_call(
        _outmlp_kernel,
        out_shape=jax.ShapeDtypeStruct((rows, D_MODEL), F32),
        grid=(rows // tm,),
        in_specs=[
            pl.BlockSpec((tm, D_MODEL), lambda i: (i, 0)),
            pl.BlockSpec((tm, D_MODEL), lambda i: (i, 0)),
            const((D_MODEL, D_MODEL)),
            const((1, D_MODEL)),
            const((D_MODEL, D_FF)),
            const((D_FF, D_MODEL)),
            const((1, D_MODEL)),
        ],
        out_specs=pl.BlockSpec((tm, D_MODEL), lambda i: (i, 0)),
        compiler_params=pltpu.CompilerParams(
            dimension_semantics=("arbitrary",), vmem_limit_bytes=VMEM_LIMIT),
        name="outmlp",
    )(x2d, mix2d, w_out, norm_mlp_w, mlp_up, mlp_down, norm_f_w)


def _rwkv_token_prep(lora_t, xg, w0, wup, a0, aup, gup):
    w_raw = w0 + jnp.dot(_tanh(lora_t).astype(BF16), wup, preferred_element_type=F32)
    wlog = -W_SCALE * jax.nn.sigmoid(w_raw)
    a = jax.nn.sigmoid(a0 + jnp.dot(lora_t.astype(BF16), aup, preferred_element_type=F32))
    g = jnp.dot(jax.nn.sigmoid(xg).astype(BF16), gup, preferred_element_type=F32)
    return wlog, a, g


def _mlstm_chunk(c, proj_ref, q_ref, k_ref, ib_ref, fb_ref, mlnw_ref, mix_ref, C_ref, n_ref, m_ref):
    rows = pl.ds(pl.multiple_of(c * LM, LM), LM)
    row = _iota((LM, LM), 0)
    col = _iota((LM, LM), 1)
    causal = col <= row
    tril = jnp.where(causal, 1.0, 0.0).astype(BF16)
    blockdiag = (row < HEAD_DIM) == (col < HEAD_DIM)
    lane = _iota((1, LANES), 1)
    lo = lane < HEAD_DIM

    i_tile = proj_ref[0, rows, GI0:GI0 + LANES] + ib_ref[...]
    f_tile = _log_sigmoid(proj_ref[0, rows, GF0:GF0 + LANES] + fb_ref[...])
    bc = _dot01_l(tril, f_tile)
    u = i_tile - bc
    ut = u.T
    yield

    nt = (((1,), (1,)), ((), ()))
    pairs = range(PAIRS)
    heads = range(N_HEADS)
    tile = lambda base, j: slice(base + j * LANES, base + (j + 1) * LANES)
    hmask = lambda h: lo if h % 2 == 0 else jnp.logical_not(lo)
    q_t = [q_ref[rows, tile(0, j)] for j in pairs]
    k_t = [k_ref[rows, tile(0, j)] for j in pairs]
    v_t = [proj_ref[0, rows, tile(V0, j)] for j in pairs]
    c_pair = [C_ref[0, j] for j in pairs]
    n_row = [n_ref[0, j:j + 1, :] for j in pairs]
    k_bf = [k.astype(BF16) for k in k_t]
    v_bf = [v.astype(BF16) for v in v_t]
    s = [lax.dot_general(jnp.where(hmask(h), q_t[h // 2], 0.0).astype(BF16), k_bf[h // 2], nt,
                         preferred_element_type=F32) for h in heads]
    q_c = [_dot(q_t[j], c_pair[j]) for j in pairs]
    k_tr = [k.T for k in k_t]
    yield

    m_prev_t = m_ref[0]
    ut8 = ut[0:N_HEADS, :]
    m_last_t = jnp.maximum(jnp.max(ut8, axis=1, keepdims=True), m_prev_t)
    ws_rows = jnp.exp(ut8 - m_last_t)
    dec_t = jnp.exp(m_prev_t - m_last_t)
    m_ref[0] = bc.T[0:N_HEADS, LM - 1:LM] + m_last_t

    m_prev = [m_prev_t[h:h + 1, 0:1] for h in heads]
    u_row = [ut[h:h + 1, :] for h in heads]
    bc_col = [_rowsum(jnp.where(lane == h, bc, 0.0)) for h in heads]
    m_col = [jnp.maximum(jnp.max(jnp.where(causal, u_row[h], NEG), axis=1, keepdims=True), m_prev[h])
             for h in heads]
    p = [s[h] * jnp.exp(jnp.where(causal, u_row[h] - m_col[h], NEG)) for h in heads]
    sc = [jnp.exp(m_prev[h] - m_col[h]) for h in heads]
    yield
    pv = [jnp.dot(p[h].astype(BF16), v_bf[h // 2], preferred_element_type=F32) for h in heads]
    yield
    h_parts = []
    for h in heads:
        q_n = _rowsum(jnp.where(hmask(h), q_t[h // 2] * n_row[h // 2], 0.0))
        den = _rowsum(p[h]) + sc[h] * q_n
        num = pv[h] + sc[h] * q_c[h // 2]
        h_parts.append(num / jnp.maximum(jnp.abs(den), jnp.exp(-(bc_col[h] + m_col[h]))))

    top = _iota((LM, 1), 0) < HEAD_DIM
    kw = [k_tr[j] * jnp.where(top, ws_rows[2 * j:2 * j + 1, :], ws_rows[2 * j + 1:2 * j + 2, :])
          for j in pairs]
    upd = [jnp.dot(kw[j].astype(BF16), v_bf[j], preferred_element_type=F32) for j in pairs]
    n_upd = [jnp.dot(ws_rows.astype(BF16), k_bf[j], preferred_element_type=F32) for j in pairs]
    yield
    for j in pairs:
        dec_j = jnp.where(lo, dec_t[2 * j:2 * j + 1, :], dec_t[2 * j + 1:2 * j + 2, :])
        C_ref[0, j] = c_pair[j] * dec_j + jnp.where(blockdiag, upd[j], 0.0)
        n_ref[0, j:j + 1, :] = n_row[j] * dec_j + jnp.where(lo, n_upd[j][2 * j:2 * j + 1, :],
                                                            n_upd[j][2 * j + 1:2 * j + 2, :])

    for j in pairs:
        h_til = jnp.where(lo, h_parts[2 * j], h_parts[2 * j + 1])
        hml = jax.nn.sigmoid(proj_ref[0, rows, tile(O0, j)]) * h_til
        ms = _pair_rowsum(hml * hml, lo) * (1.0 / HEAD_DIM)
        mix_ref[0, rows, tile(0, j)] = hml * lax.rsqrt(ms + EPS) * mlnw_ref[:, tile(0, j)]


def _neumann_inverse(ns):
    shape = ns[0].shape
    eye = jnp.where(_iota(shape, 0) == _iota(shape, 1), 1.0, 0.0)
    n_sq = int(math.log2(LR)) - 1
    qs = [n.astype(BF16) for n in ns]
    ts = [eye + n for n in ns]
    sq = [jnp.dot(q, q, preferred_element_type=F32) for q in qs]
    yield
    for _ in range(n_sq - 1):
        qs = [s.astype(BF16) for s in sq]
        both = [jnp.dot(q, jnp.concatenate([q, t.astype(BF16)], axis=1), preferred_element_type=F32)
                for q, t in zip(qs, ts)]
        sq = [b[:, 0:LANES] for b in both]
        ts = [t + b[:, LANES:2 * LANES] for t, b in zip(ts, both)]
        yield
    return [(t + jnp.dot(s.astype(BF16), t.astype(BF16), preferred_element_type=F32)).astype(BF16)
            for s, t in zip(sq, ts)]


def _rwkv_chunks(c, rs_ref, w0_ref, wup_ref, a0_ref, aup_ref, gup_ref, kk_ref, ka_ref, rk_ref,
                 lnw_ref, lnb_ref, mix_ref, S_ref):
    gl = RW_GROUP * LR
    rows = pl.ds(pl.multiple_of(c * gl, gl), gl)
    lane = _iota((1, LANES), 1)
    lo = lane < HEAD_DIM
    row = _iota((LANES, LANES), 0)
    col = _iota((LANES, LANES), 1)
    same = (row < LR) == (col < LR)
    mask_sl = jnp.logical_and(same, col < row)
    mask_li = jnp.logical_and(same, col <= row)
    mask_li2 = jnp.concatenate([mask_li, mask_li], axis=1)
    grow = _iota((gl, gl), 0)
    gcol = _iota((gl, gl), 1)
    tril = jnp.where(jnp.logical_and(gcol <= grow, gcol >= (grow // LR) * LR), 1.0, 0.0).astype(BF16)
    nt = (((1,), (1,)), ((), ()))
    pairs = range(PAIRS)
    items = [(ci, j) for ci in range(RW_GROUP) for j in pairs]
    tile = lambda base, j: slice(base + j * LANES, base + (j + 1) * LANES)
    crow = lambda ci: slice(ci * LR, (ci + 1) * LR)

    wlog, a, g = _rwkv_token_prep(rs_ref[rows, RS_LORA:RS_LORA + LANES], rs_ref[rows, RS_G:RS_G + D_GATE],
                                  w0_ref[...], wup_ref[...], a0_ref[...], aup_ref[...], gup_ref[...])
    lw = _dot01_l(tril, wlog)
    e_in = jnp.exp(lw)
    e_ex = jnp.exp(lw - wlog)
    e_ng = jnp.exp(-lw)

    r_g = [rs_ref[rows, tile(RS_R, j)] for j in pairs]
    kr_g = [rs_ref[rows, tile(RS_K, j)] for j in pairs]
    v_g = [rs_ref[rows, tile(RS_V, j)] for j in pairs]
    a_g = [a[:, tile(0, j)] for j in pairs]
    kk = [kr_g[j] * kk_ref[:, tile(0, j)] for j in pairs]
    kkn = [kk[j] * lax.rsqrt(jnp.maximum(_pair_rowsum(kk[j] * kk[j], lo), 1e-24)) for j in pairs]
    keff_g = [kr_g[j] * (1.0 + (a_g[j] - 1.0) * ka_ref[:, tile(0, j)]) for j in pairs]
    alpha_g = [-kkn[j] * e_ex[:, tile(0, j)] for j in pairs]
    rb_g = [r_g[j] * e_in[:, tile(0, j)] for j in pairs]
    beta_g = [kkn[j] * a_g[j] * e_ng[:, tile(0, j)] for j in pairs]
    ktl_g = [keff_g[j] * e_ng[:, tile(0, j)] for j in pairs]
    bonus_g = [_pair_rowsum(r_g[j] * keff_g[j] * rk_ref[:, tile(0, j)], lo) * v_g[j] for j in pairs]

    lhs, rhs, v2, bk, gamma, ar = {}, {}, {}, {}, {}, {}
    for it in items:
        ci, j = it
        cr = crow(ci)
        ar[it] = jnp.concatenate([alpha_g[j][cr], rb_g[j][cr]], axis=0).astype(BF16)
        lhs[it] = jnp.concatenate([_stack2(ar[it][0:LR], lo), _stack2(ar[it][LR:2 * LR], lo)], axis=0)
        beta_c, ktl_c = beta_g[j][cr].astype(BF16), ktl_g[j][cr].astype(BF16)
        rhs[it] = jnp.concatenate([beta_c, beta_c, ktl_c, ktl_c], axis=0)
        v2[it] = _stack2(v_g[j][cr].astype(BF16), lo)
        bk[it] = jnp.concatenate([_stack2(beta_c, lo), _stack2(ktl_c, lo)], axis=0)
        gamma[it] = e_in[(ci + 1) * LR - 1:(ci + 1) * LR, tile(0, j)]
    aa = {it: lax.dot_general(lhs[it], rhs[it], nt, preferred_element_type=F32) for it in items}
    yield
    a_ab = [jnp.where(mask_sl, aa[it][0:LANES, 0:LANES], 0.0) for it in items]
    a_ak = {it: jnp.where(mask_sl, aa[it][0:LANES, LANES:2 * LANES], 0.0).astype(BF16) for it in items}
    a_r = {it: jnp.where(mask_li2, aa[it][LANES:2 * LANES, :], 0.0).astype(BF16) for it in items}
    akv = {it: jnp.dot(a_ak[it], v2[it], preferred_element_type=F32) for it in items}
    yield
    t_list = yield from _neumann_inverse(a_ab)
    t_inv = dict(zip(items, t_list))
    yield _STATE

    s_pair = [S_ref[0, j] for j in pairs]
    for ci in range(RW_GROUP):
        its = [(ci, j) for j in pairs]
        xs = [lax.dot_general(ar[it], s_pair[it[1]].astype(BF16), nt, preferred_element_type=F32)
              for it in its]
        yield
        p2 = [_dot(t_inv[it], _stack2(xs[j][0:LR], lo) + akv[it]) for j, it in enumerate(its)]
        yield
        pv = [jnp.concatenate([p2[j].astype(BF16), v2[it]], axis=0) for j, it in enumerate(its)]
        y2 = [_dot(a_r[it], pv[j]) for j, it in enumerate(its)]
        upd = [lax.dot_general(pv[j], bk[it], (((0,), (0,)), ((), ())), preferred_element_type=F32)
               for j, it in enumerate(its)]
        s_pair = [(s_pair[j] + upd[j]) * gamma[it] for j, it in enumerate(its)]
        yield
        cr = crow(ci)
        for j in pairs:
            yb = xs[j][LR:2 * LR] + y2[j][0:LR] + y2[j][LR:2 * LR] + bonus_g[j][cr]
            mu = _pair_rowsum(yb, lo) * (1.0 / HEAD_DIM)
            dlt = yb - mu
            var = _pair_rowsum(dlt * dlt, lo) * (1.0 / HEAD_DIM)
            yn = dlt * lax.rsqrt(var + GN_EPS) * lnw_ref[:, tile(0, j)] + lnb_ref[:, tile(0, j)]
            out_rows = pl.ds(pl.multiple_of(c * gl + ci * LR, LR), LR)
            mix_ref[0, out_rows, tile(GROUP_W, j)] = yn * g[cr, tile(0, j)]
    for j in pairs:
        S_ref[0, j] = s_pair[j]


def _shift_rows(x, tail, j):
    rolled = pltpu.roll(x, j, axis=0)
    head = jnp.where(_iota((SUBLANES, 1), 0) < j, pltpu.roll(tail, j, axis=0), rolled[0:SUBLANES])
    return jnp.concatenate([head, rolled[SUBLANES:]], axis=0)


def _until_state(gen):
    for tok in gen:
        if tok is _STATE:
            return
        yield


def _drain(gen):
    for _ in gen:
        yield


def _interleave(primary, secondary, every):
    done_p = done_s = False
    k = 0
    while not (done_p and done_s):
        if not done_p:
            done_p = next(primary, _DONE) is _DONE
        k += 1
        if not done_s and (done_p or k % every == 0):
            done_s = next(secondary, _DONE) is _DONE


def _mixer_kernel(proj_ref, convw_ref, convb_ref, ib_ref, fb_ref, mlnw_ref, mu_ref, w0_ref, wup_ref,
                  a0_ref, aup_ref, gup_ref, kk_ref, ka_ref, rk_ref, lnw_ref, lnb_ref,
                  mix_ref, Cout_ref, n_ref, m_ref, Sout_ref,
                  C_ref, S_ref, qk_tail_ref, rw_tail_ref, q_ref, k_ref, rs_ref):
    t = pl.program_id(1)

    @pl.when(t == 0)
    def _():
        C_ref[...] = jnp.zeros_like(C_ref)
        n_ref[...] = jnp.zeros_like(n_ref)
        m_ref[...] = jnp.zeros_like(m_ref)
        S_ref[...] = jnp.zeros_like(S_ref)
        qk_tail_ref[...] = jnp.zeros_like(qk_tail_ref)
        rw_tail_ref[...] = jnp.zeros_like(rw_tail_ref)

    u = proj_ref[0, :, Q0:Q0 + 2 * GROUP_W]
    qk_tail = qk_tail_ref[...]
    acc = convb_ref[...] + u * convw_ref[CONV_W - 1:CONV_W, :]
    for jw in range(CONV_W - 1):
        acc = acc + _shift_rows(u, qk_tail, CONV_W - 1 - jw) * convw_ref[jw:jw + 1, :]
    qk_tail_ref[...] = u[TB - SUBLANES:TB]
    qk = acc * jax.nn.sigmoid(acc)
    q_ref[...] = qk[:, 0:GROUP_W]
    k_ref[...] = qk[:, GROUP_W:2 * GROUP_W] * (HEAD_DIM ** -0.5)

    rw = proj_ref[0, :, RW0:RW0 + RW_IN_W]
    rs_ref[...] = rw + mu_ref[...] * (_shift_rows(rw, rw_tail_ref[...], 1) - rw)
    rw_tail_ref[...] = rw[TB - SUBLANES:TB]

    assert RW_GROUP * LR == LM
    rwkv = lambda c: _rwkv_chunks(c, rs_ref, w0_ref, wup_ref, a0_ref, aup_ref, gup_ref, kk_ref, ka_ref, rk_ref,
                                  lnw_ref, lnb_ref, mix_ref, S_ref)
    mlstm = lambda c: _mlstm_chunk(c, proj_ref, q_ref, k_ref, ib_ref, fb_ref, mlnw_ref, mix_ref, C_ref, n_ref, m_ref)
    n_groups = TB // LM
    gens = [rwkv(c) for c in range(n_groups)]
    _interleave(_until_state(gens[0]), mlstm(0), every=ML_EVERY)
    for c in range(1, n_groups):
        _interleave(_until_state(gens[c]), _drain(gens[c - 1]), every=1)
    _interleave(_drain(gens[-1]), itertools.chain(*[mlstm(c) for c in range(1, n_groups)]), every=1)

    @pl.when(t == pl.num_programs(1) - 1)
    def _():
        for h in range(N_HEADS):
            blk = slice((h % 2) * HEAD_DIM, (h % 2 + 1) * HEAD_DIM)
            Cout_ref[0, h] = C_ref[0, h // 2, blk, blk]
            Sout_ref[0, h] = S_ref[0, h // 2, blk, blk]


def _mixer(proj, p):
    bsz, seq, _ = proj.shape
    row = lambda w: pl.BlockSpec((1, w), lambda b, t: (0, 0))
    mat = lambda r, w: pl.BlockSpec((r, w), lambda b, t: (0, 0))
    state4 = pl.BlockSpec((1, N_HEADS, HEAD_DIM, HEAD_DIM), lambda b, t: (b, 0, 0, 0))
    state3 = pl.BlockSpec((1, SUBLANES, LANES), lambda b, t: (b, 0, 0))
    return pl.pallas_call(
        _mixer_kernel,
        out_shape=(
            jax.ShapeDtypeStruct((bsz, seq, 2 * GROUP_W), F32),
            jax.ShapeDtypeStruct((bsz, N_HEADS, HEAD_DIM, HEAD_DIM), F32),
            jax.ShapeDtypeStruct((bsz, SUBLANES, LANES), F32),
            jax.ShapeDtypeStruct((bsz, SUBLANES, LANES), F32),
            jax.ShapeDtypeStruct((bsz, N_HEADS, HEAD_DIM, HEAD_DIM), F32),
        ),
        grid=(bsz, seq // TB),
        in_specs=[
            pl.BlockSpec((1, TB, IN_WP), lambda b, t: (b, t, 0)),
            mat(CONV_W, 2 * GROUP_W), row(2 * GROUP_W), row(LANES), row(LANES), row(GROUP_W),
            row(RW_IN_W), row(GROUP_W), mat(LANES, GROUP_W), row(GROUP_W), mat(LANES, GROUP_W),
            mat(D_GATE, GROUP_W), row(GROUP_W), row(GROUP_W), row(GROUP_W), row(GROUP_W), row(GROUP_W),
        ],
        out_specs=(
            pl.BlockSpec((1, TB, 2 * GROUP_W), lambda b, t: (b, t, 0)),
            state4, state3, state3, state4,
        ),
        scratch_shapes=[
            pltpu.VMEM((1, PAIRS, LANES, LANES), F32),
            pltpu.VMEM((1, PAIRS, LANES, LANES), F32),
            pltpu.VMEM((SUBLANES, 2 * GROUP_W), F32),
            pltpu.VMEM((SUBLANES, RW_IN_W), F32),
            pltpu.VMEM((TB, GROUP_W), F32),
            pltpu.VMEM((TB, GROUP_W), F32),
            pltpu.VMEM((TB, RW_IN_W), F32),
        ],
        compiler_params=pltpu.CompilerParams(
            dimension_semantics=("arbitrary", "arbitrary"), vmem_limit_bytes=VMEM_LIMIT),
        name="mixer",
    )(proj, p["conv_w"], p["conv_b"], p["ib_row"], p["fb_row"], p["mlnorm_w"], p["mu"], p["w0"],
      p["wup_p"], p["a0"], p["aup_p"], p["gup"], p["k_k"], p["k_a"], p["r_k"], p["ln_w"], p["ln_b"])


def _inproj_t_kernel(x_ref, nw_ref, wml_ref, wgate_ref, wrw_ref, o_ref, ot_ref):
    o_ml, o_gate, o_rw = _project(x_ref, nw_ref, wml_ref, wgate_ref, wrw_ref)
    for lo_col, hi_col, o in ((0, GI0, o_ml), (GI0, RW0, o_gate), (RW0, IN_WP, o_rw)):
        o_ref[:, lo_col:hi_col] = o
        ot_ref[lo_col:hi_col, :] = o.T


def _inproj_t(x2d, norm_w, w_in_parts):
    rows = x2d.shape[0]
    return pl.pallas_call(
        _inproj_t_kernel,
        out_shape=(jax.ShapeDtypeStruct((rows, IN_WP), F32), jax.ShapeDtypeStruct((IN_WP, rows), F32)),
        compiler_params=pltpu.CompilerParams(vmem_limit_bytes=VMEM_LIMIT),
        name="inproj_t",
    )(x2d, norm_w, *w_in_parts)


(PC_W0, PC_A0, PC_MLNW, PC_KK, PC_KA, PC_RK, PC_LNW, PC_LNB) = range(8)
PC_CONVB = CONV_W


def _sample_kernel(pt_ref, conv0_ref, shift0_ref, m0_ref, n0_ref, C0_ref, S0_ref,
                   p1024_ref, p512_ref, mu_ref, gb_ref, wup_ref, aup_ref, gup_ref,
                   hml_ref, yrw_ref, C_ref, n_ref, m_ref, S_ref,
                   wk_s, q_s, vr_s, y_s):
    h = pl.program_id(0)
    hrow = lambda base, n=HEAD_DIM: pl.ds(pl.multiple_of(base + h * n, n), n)
    pcol = lambda j: p512_ref[hrow(0), j:j + 1]
    colsum = lambda t: jnp.sum(t, axis=0, keepdims=True)

    def conv_silu(base):
        rows = hrow(base)
        acc = p1024_ref[rows, PC_CONVB:PC_CONVB + 1] + pt_ref[rows, :] * p1024_ref[rows, CONV_W - 1:CONV_W]
        for jw in range(CONV_W - 1):
            acc = acc + conv0_ref[jw, rows, :] * p1024_ref[rows, jw:jw + 1]
        return acc * jax.nn.sigmoid(acc)

    q = conv_silu(Q0)
    k = conv_silu(K0) * (HEAD_DIM ** -0.5)
    v = pt_ref[hrow(V0), :]
    i_g = pt_ref[pl.ds(GI0 + h, 1), :] + gb_ref[pl.ds(h, 1), 0:1]
    logf = _log_sigmoid(pt_ref[pl.ds(GF0 + h, 1), :] + gb_ref[pl.ds(h, 1), 1:2])
    m0 = m0_ref[pl.ds(h, 1), :]
    m_new = jnp.maximum(logf + m0, i_g)
    ws = jnp.exp(i_g - m_new)
    dec = jnp.exp(logf + m0 - m_new)
    wk = ws * k
    n_new = dec * n0_ref[0] + wk
    n_ref[0] = n_new
    m_ref[pl.ds(h, 1), :] = m_new
    wk_s[...] = wk
    q_s[...] = q

    def c_body(kk_, num):
        ck = dec * C0_ref[0, kk_] + wk_s[pl.ds(kk_, 1), :] * v
        C_ref[0, kk_] = ck
        return num + q_s[pl.ds(kk_, 1), :] * ck

    num = lax.fori_loop(0, HEAD_DIM, c_body, jnp.zeros((HEAD_DIM, pt_ref.shape[1]), F32), unroll=SAMPLE_UNROLL)
    den = colsum(q * n_new)
    hml = jax.nn.sigmoid(pt_ref[hrow(O0), :]) * (num / jnp.maximum(jnp.abs(den), jnp.exp(-m_new)))
    hml_ref[...] = hml * lax.rsqrt(colsum(hml * hml) * (1.0 / HEAD_DIM) + EPS) * pcol(PC_MLNW)

    def shifted(base, n=HEAD_DIM):
        rw = pt_ref[hrow(RW0 + base, n) if n == HEAD_DIM else pl.ds(RW0 + base, n), :]
        rows = hrow(base, n) if n == HEAD_DIM else pl.ds(base, n)
        return rw + mu_ref[rows, :] * (shift0_ref[rows, :] - rw)

    r, kr, vr = shifted(RS_R), shifted(RS_K), shifted(RS_V)
    lora = shifted(RS_LORA, LANES)
    xg = shifted(RS_G, D_GATE)
    hw = hrow(0)
    w_raw = pcol(PC_W0) + jnp.dot(wup_ref[hw, :], jnp.tanh(lora).astype(BF16), preferred_element_type=F32)
    w = jnp.exp(-W_SCALE * jax.nn.sigmoid(w_raw))
    a = jax.nn.sigmoid(pcol(PC_A0) + jnp.dot(aup_ref[hw, :], lora.astype(BF16), preferred_element_type=F32))
    g = jnp.dot(gup_ref[hw, :], jax.nn.sigmoid(xg).astype(BF16), preferred_element_type=F32)
    kk = kr * pcol(PC_KK)
    kkn = kk / jnp.maximum(jnp.sqrt(colsum(kk * kk)), 1e-12)
    keff = kr * (1.0 + (a - 1.0) * pcol(PC_KA))
    nkk = -kkn
    ka = kkn * a
    vr_s[...] = vr

    def s_body(v_, carry):
        sv = S0_ref[0, v_]
        sn = sv * w + colsum(sv * nkk) * ka + vr_s[pl.ds(v_, 1), :] * keff
        S_ref[0, v_] = sn
        y_s[pl.ds(v_, 1), :] = colsum(sn * r)
        return carry

    lax.fori_loop(0, HEAD_DIM, s_body, 0, unroll=SAMPLE_UNROLL)
    yb = y_s[...] + colsum(r * keff * pcol(PC_RK)) * vr
    dlt = yb - colsum(yb) * (1.0 / HEAD_DIM)
    var = colsum(dlt * dlt) * (1.0 / HEAD_DIM)
    yrw_ref[...] = (dlt * lax.rsqrt(var + GN_EPS) * pcol(PC_LNW) + pcol(PC_LNB)) * g


def _sample(proj_t, conv0_t, shift0_t, m0_t, n0_t, c0_t, s0_t, p):
    bsz = proj_t.shape[1]
    whole = lambda a: pl.BlockSpec(a.shape, lambda h: (0,) * a.ndim)
    state4 = pl.BlockSpec((1, HEAD_DIM, HEAD_DIM, bsz), lambda h: (h, 0, 0, 0))
    head2 = pl.BlockSpec((HEAD_DIM, bsz), lambda h: (h, 0))
    head3 = pl.BlockSpec((1, HEAD_DIM, bsz), lambda h: (h, 0, 0))
    consts = [p["p1024"], p["p512"], p["mu_col"], p["gate_b"], p["wup_t"], p["aup_t"], p["gup_t"]]
    return pl.pallas_call(
        _sample_kernel,
        out_shape=(
            jax.ShapeDtypeStruct((GROUP_W, bsz), F32),
            jax.ShapeDtypeStruct((GROUP_W, bsz), F32),
            jax.ShapeDtypeStruct((N_HEADS, HEAD_DIM, HEAD_DIM, bsz), F32),
            jax.ShapeDtypeStruct((N_HEADS, HEAD_DIM, bsz), F32),
            jax.ShapeDtypeStruct((N_HEADS, bsz), F32),
            jax.ShapeDtypeStruct((N_HEADS, HEAD_DIM, HEAD_DIM, bsz), F32),
        ),
        grid=(N_HEADS,),
        in_specs=[whole(proj_t), whole(conv0_t), whole(shift0_t), whole(m0_t), head3, state4, state4]
                 + [whole(c) for c in consts],
        out_specs=(head2, head2, state4, head3, pl.BlockSpec((N_HEADS, bsz), lambda h: (0, 0)), state4),
        scratch_shapes=[pltpu.VMEM((HEAD_DIM, bsz), F32) for _ in range(4)],
        compiler_params=pltpu.CompilerParams(
            dimension_semantics=("arbitrary",), vmem_limit_bytes=VMEM_LIMIT),
        name="sample",
    )(proj_t, conv0_t, shift0_t, m0_t, n0_t, c0_t, s0_t, *consts)


def _layer_params(l, norm_mix_w, w_in, mlstm_conv_w, mlstm_conv_b, mlstm_i_b, mlstm_f_b, mlstm_norm_w,
                  rw_mu, rw_w0, rw_w_up, rw_a0, rw_a_up, rw_g_up, rw_k_k, rw_k_a, rw_r_k, rw_ln_w, rw_ln_b,
                  w_out, norm_mlp_w, mlp_up, mlp_down):
    ml_in = 4 * GROUP_W
    wi = w_in[l]
    zpad = jnp.zeros((D_MODEL, LANES - N_HEADS), BF16)
    w_in_parts = (
        wi[:, :ml_in].astype(BF16),
        jnp.concatenate([wi[:, ml_in:ml_in + N_HEADS].astype(BF16), zpad,
                         wi[:, ml_in + N_HEADS:ml_in + 2 * N_HEADS].astype(BF16), zpad], axis=1),
        wi[:, ml_in + 2 * N_HEADS:].astype(BF16),
    )
    gate_row = lambda b: jnp.concatenate([b, jnp.zeros((LANES - N_HEADS,), F32)])[None, :]
    zl = jnp.zeros((D_LORA, GROUP_W), F32)
    return {
        "norm_mix_w": norm_mix_w[l][None, :],
        "w_in_parts": w_in_parts,
        "conv_w": mlstm_conv_w[l], "conv_b": mlstm_conv_b[l][None, :],
        "ib_row": gate_row(mlstm_i_b[l]), "fb_row": gate_row(mlstm_f_b[l]),
        "gate_b": jnp.stack([mlstm_i_b[l], mlstm_f_b[l]], axis=1),
        "p1024": jnp.concatenate([mlstm_conv_w[l].T, mlstm_conv_b[l][:, None],
                                  jnp.zeros((2 * GROUP_W, SUBLANES - CONV_W - 1), F32)], axis=1),
        "p512": jnp.stack([rw_w0[l], rw_a0[l], mlstm_norm_w[l], rw_k_k[l], rw_k_a[l], rw_r_k[l].reshape(GROUP_W),
                           rw_ln_w[l], rw_ln_b[l]], axis=1),
        "mu_col": rw_mu[l][:, None],
        "mlnorm_w": mlstm_norm_w[l][None, :],
        "mu": rw_mu[l][None, :], "w0": rw_w0[l][None, :], "a0": rw_a0[l][None, :],
        "wup_p": jnp.concatenate([rw_w_up[l], zl], axis=0).astype(BF16),
        "aup_p": jnp.concatenate([zl, rw_a_up[l]], axis=0).astype(BF16),
        "gup": rw_g_up[l].astype(BF16),
        "wup_t": jnp.concatenate([rw_w_up[l], zl], axis=0).T.astype(BF16),
        "aup_t": jnp.concatenate([zl, rw_a_up[l]], axis=0).T.astype(BF16),
        "gup_t": rw_g_up[l].T.astype(BF16),
        "k_k": rw_k_k[l][None, :], "k_a": rw_k_a[l][None, :], "r_k": rw_r_k[l].reshape(1, GROUP_W),
        "ln_w": rw_ln_w[l][None, :], "ln_b": rw_ln_b[l][None, :],
        "w_out": w_out[l].astype(BF16), "norm_mlp_w": norm_mlp_w[l][None, :],
        "mlp_up": mlp_up[l].astype(BF16), "mlp_down": mlp_down[l].astype(BF16),
    }


def _prompt_layer(x, p, norm_f_w):
    bsz, seq, _ = x.shape
    x2d = x.reshape(bsz * seq, D_MODEL)
    proj = _inproj(x2d, p["norm_mix_w"], p["w_in_parts"]).reshape(bsz, seq, IN_WP)
    mix, c_st, n_st, m_st, s_st = _mixer(proj, p)
    y = _outmlp(x2d, mix.reshape(bsz * seq, D_MODEL), p["w_out"], p["norm_mlp_w"], p["mlp_up"],
                p["mlp_down"], norm_f_w)
    c_new = c_st
    n_new = n_st[:, :PAIRS, :].reshape(bsz, N_HEADS, HEAD_DIM)
    m_new = m_st[:, :, 0]
    conv_new = proj[:, seq - (CONV_W - 1):, Q0:Q0 + 2 * GROUP_W]
    s_new = s_st
    shift_new = proj[:, seq - 1:, RW0:RW0 + RW_IN_W]
    return y.reshape(bsz, seq, D_MODEL), (c_new, n_new, m_new, conv_new, s_new, shift_new)


def _sample_layer(x, c0, n0, m0, conv0, s0, shift0, p, norm_f_w):
    bsz = x.shape[0]
    x2d = x.reshape(bsz, D_MODEL)
    proj, proj_t = _inproj_t(x2d, p["norm_mix_w"], p["w_in_parts"])
    hml_t, yrw_t, c_t, n_t, m_t, s_t = _sample(
        proj_t, jnp.transpose(conv0, (1, 2, 0)), shift0[:, 0, :].T, m0.T, jnp.transpose(n0, (1, 2, 0)),
        jnp.transpose(c0, (1, 2, 3, 0)), jnp.transpose(s0, (1, 2, 3, 0)), p)
    mix = jnp.concatenate([hml_t, yrw_t], axis=0).T
    y = _outmlp(x2d, mix, p["w_out"], p["norm_mlp_w"], p["mlp_up"], p["mlp_down"], norm_f_w)
    conv_new = jnp.concatenate([conv0[:, 1:], proj[:, None, Q0:Q0 + 2 * GROUP_W]], axis=1)
    shift_new = proj[:, None, RW0:RW0 + RW_IN_W]
    return y.reshape(bsz, 1, D_MODEL), (
        jnp.transpose(c_t, (3, 0, 1, 2)), jnp.transpose(n_t, (2, 0, 1)), m_t.T, conv_new,
        jnp.transpose(s_t, (3, 0, 1, 2)), shift_new)


def kernel(x_prompt, x_sample, state_mlstm_C, state_mlstm_n, state_mlstm_m, state_mlstm_conv, state_rwkv_S, state_rwkv_shift, norm_mix_w, w_in, mlstm_conv_w, mlstm_conv_b, mlstm_i_b, mlstm_f_b, mlstm_norm_w, rw_mu, rw_w0, rw_w_up, rw_a0, rw_a_up, rw_g_up, rw_k_k, rw_k_a, rw_r_k, rw_ln_w, rw_ln_b, w_out, norm_mlp_w, mlp_up, mlp_down, norm_f_w):
    depth = w_in.shape[0]
    assert depth == 1, "the final RMSNorm is fused into the layer's output kernel"
    weights = (norm_mix_w, w_in, mlstm_conv_w, mlstm_conv_b, mlstm_i_b, mlstm_f_b, mlstm_norm_w,
               rw_mu, rw_w0, rw_w_up, rw_a0, rw_a_up, rw_g_up, rw_k_k, rw_k_a, rw_r_k, rw_ln_w, rw_ln_b,
               w_out, norm_mlp_w, mlp_up, mlp_down)
    nfw = norm_f_w[None, :]
    p = _layer_params(0, *weights)
    y_p, st_p = _prompt_layer(x_prompt, p, nfw)
    y_s, st_s = _sample_layer(x_sample, state_mlstm_C[0], state_mlstm_n[0], state_mlstm_m[0],
                              state_mlstm_conv[0], state_rwkv_S[0], state_rwkv_shift[0], p, nfw)
    lead = lambda t: tuple(a[None] for a in t)
    return (y_p, y_s) + lead(st_p) + lead(st_s)
```

```python
import itertools
import math

import jax
import jax.numpy as jnp
from jax import lax
from jax.experimental import pallas as pl
from jax.experimental.pallas import tpu as pltpu

F32 = jnp.float32
BF16 = jnp.bfloat16

D_MODEL = 1024
HEAD_DIM = 64
N_HEADS = 8
PAIRS = N_HEADS // 2
GROUP_W = N_HEADS * HEAD_DIM
CONV_W = 4
D_LORA = 64
D_GATE = 128
D_FF = 4 * D_MODEL
EPS = 1e-6
GN_EPS = 64e-5
LANES = 128
SUBLANES = 8

Q0, K0, V0, O0 = 0, GROUP_W, 2 * GROUP_W, 3 * GROUP_W
GI0 = 4 * GROUP_W
GF0 = GI0 + LANES
RW0 = GF0 + LANES
RW_IN_W = 3 * GROUP_W + 2 * D_LORA + D_GATE
IN_WP = RW0 + RW_IN_W
RS_R, RS_K, RS_V = 0, GROUP_W, 2 * GROUP_W
RS_LORA = 3 * GROUP_W
RS_G = RS_LORA + 2 * D_LORA

TB = 512
LM = 128
LR = 64
RW_GROUP = 2
ML_EVERY = 2
_DONE = object()
_STATE = object()
TM_IN = 512
TM_OUT = 1024
FF_CHUNK = 1024
SAMPLE_UNROLL = 8
NEG = -1e30
W_SCALE = math.exp(-0.5)
VMEM_LIMIT = 56 * 1024 * 1024


def _dot(a, b):
    return jnp.dot(a.astype(BF16), b.astype(BF16), preferred_element_type=F32)


def _split3(x):
    h1 = x.astype(BF16)
    r1 = x - h1.astype(F32)
    h2 = r1.astype(BF16)
    h3 = (r1 - h2.astype(F32)).astype(BF16)
    return h1, h2, h3


def _dot01_l(m01, x):
    h1, h2, h3 = _split3(x)
    d = lambda h: jnp.dot(m01, h, preferred_element_type=F32)
    return d(h1) + d(h2) + d(h3)


def _log_sigmoid(x):
    return jnp.minimum(x, 0.0) - jnp.log(1.0 + jnp.exp(-jnp.abs(x)))


def _tanh(x):
    e = jnp.exp(-2.0 * jnp.abs(x))
    t = (1.0 - e) / (1.0 + e)
    return jnp.where(x < 0.0, -t, t)


def _iota(shape, dim):
    return lax.broadcasted_iota(jnp.int32, shape, dim)


def _rowsum(x):
    return jnp.sum(x, axis=1, keepdims=True)


def _pair_rowsum(x, lo):
    s_lo = _rowsum(jnp.where(lo, x, 0.0))
    s_hi = _rowsum(jnp.where(lo, 0.0, x))
    return jnp.where(lo, s_lo, s_hi)


def _stack2(x, lo):
    zero = jnp.zeros_like(x)
    return jnp.concatenate([jnp.where(lo, x, zero), jnp.where(lo, zero, x)], axis=0)


def _rms(x, w):
    ms = jnp.mean(x * x, axis=-1, keepdims=True)
    return x * lax.rsqrt(ms + EPS) * w


def _project(x_ref, nw_ref, wml_ref, wgate_ref, wrw_ref):
    xn = _rms(x_ref[...], nw_ref[...]).astype(BF16)
    return [jnp.dot(xn, w[...], preferred_element_type=F32) for w in (wml_ref, wgate_ref, wrw_ref)]


def _inproj_kernel(x_ref, nw_ref, wml_ref, wgate_ref, wrw_ref, o_ref):
    o_ml, o_gate, o_rw = _project(x_ref, nw_ref, wml_ref, wgate_ref, wrw_ref)
    o_ref[:, 0:GI0] = o_ml
    o_ref[:, GI0:RW0] = o_gate
    o_ref[:, RW0:IN_WP] = o_rw


def _inproj(x2d, norm_w, w_in_parts):
    rows = x2d.shape[0]
    tm = min(TM_IN, rows)
    return pl.pallas_call(
        _inproj_kernel,
        out_shape=jax.ShapeDtypeStruct((rows, IN_WP), F32),
        grid=(rows // tm,),
        in_specs=[
            pl.BlockSpec((tm, D_MODEL), lambda i: (i, 0)),
            pl.BlockSpec((1, D_MODEL), lambda i: (0, 0)),
        ] + [pl.BlockSpec(w.shape, lambda i: (0, 0)) for w in w_in_parts],
        out_specs=pl.BlockSpec((tm, IN_WP), lambda i: (i, 0)),
        compiler_params=pltpu.CompilerParams(
            dimension_semantics=("arbitrary",), vmem_limit_bytes=VMEM_LIMIT),
        name="inproj",
    )(x2d, norm_w, *w_in_parts)


def _outmlp_kernel(x_ref, mix_ref, wout_ref, nmw_ref, up_ref, down_ref, nfw_ref, y_ref):
    x1 = x_ref[...] + jnp.dot(mix_ref[...].astype(BF16), wout_ref[...], preferred_element_type=F32)
    xn = _rms(x1, nmw_ref[...]).astype(BF16)
    acc = x1
    for c in range(D_FF // FF_CHUNK):
        sl = slice(c * FF_CHUNK, (c + 1) * FF_CHUNK)
        hid = jnp.maximum(jnp.dot(xn, up_ref[:, sl], preferred_element_type=F32), 0.0)
        acc = acc + jnp.dot((hid * hid).astype(BF16), down_ref[sl, :], preferred_element_type=F32)
    y_ref[...] = _rms(acc, nfw_ref[...])


def _outmlp(x2d, mix2d, w_out, norm_mlp_w, mlp_up, mlp_down, norm_f_w):
    rows = x2d.shape[0]
    tm = min(TM_OUT, rows)
    const = lambda shape: pl.BlockSpec(shape, lambda i: (0, 0), pipeline_mode=pl.Buffered(1))
    return pl.pallas_call(
        _outmlp_kernel,
        out_shape=jax.ShapeDtypeStruct((rows, D_MODEL), F32),
        grid=(rows // tm,),
        in_specs=[
            pl.BlockSpec((tm, D_MODEL), lambda i: (i, 0)),
            pl.BlockSpec((tm, D_MODEL), lambda i: (i, 0)),
            const((D_MODEL, D_MODEL)),
            const((1, D_MODEL)),
            const((D_MODEL, D_FF)),
            const((D_FF, D_MODEL)),
            const((1, D_MODEL)),
        ],
        out_specs=pl.BlockSpec((tm, D_MODEL), lambda i: (i, 0)),
        compiler_params=pltpu.CompilerParams(
            dimension_semantics=("arbitrary",), vmem_limit_bytes=VMEM_LIMIT),
        name="outmlp",
    )(x2d, mix2d, w_out, norm_mlp_w, mlp_up, mlp_down, norm_f_w)


def _rwkv_token_prep(lora_t, xg, w0, wup, a0, aup, gup):
    w_raw = w0 + jnp.dot(_tanh(lora_t).astype(BF16), wup, preferred_element_type=F32)
    wlog = -W_SCALE * jax.nn.sigmoid(w_raw)
    a = jax.nn.sigmoid(a0 + jnp.dot(lora_t.astype(BF16), aup, preferred_element_type=F32))
    g = jnp.dot(jax.nn.sigmoid(xg).astype(BF16), gup, preferred_element_type=F32)
    return wlog, a, g


def _mlstm_chunk(c, proj_ref, q_ref, k_ref, ib_ref, fb_ref, mlnw_ref, mix_ref, C_ref, n_ref, m_ref):
    rows = pl.ds(pl.multiple_of(c * LM, LM), LM)
    row = _iota((LM, LM), 0)
    col = _iota((LM, LM), 1)
    causal = col <= row
    tril = jnp.where(causal, 1.0, 0.0).astype(BF16)
    blockdiag = (row < HEAD_DIM) == (col < HEAD_DIM)
    lane = _iota((1, LANES), 1)
    lo = lane < HEAD_DIM

    i_tile = proj_ref[0, rows, GI0:GI0 + LANES] + ib_ref[...]
    f_tile = _log_sigmoid(proj_ref[0, rows, GF0:GF0 + LANES] + fb_ref[...])
    bc = _dot01_l(tril, f_tile)
    u = i_tile - bc
    ut = u.T
    yield

    nt = (((1,), (1,)), ((), ()))
    pairs = range(PAIRS)
    heads = range(N_HEADS)
    tile = lambda base, j: slice(base + j * LANES, base + (j + 1) * LANES)
    hmask = lambda h: lo if h % 2 == 0 else jnp.logical_not(lo)
    q_t = [q_ref[rows, tile(0, j)] for j in pairs]
    k_t = [k_ref[rows, tile(0, j)] for j in pairs]
    v_t = [proj_ref[0, rows, tile(V0, j)] for j in pairs]
    c_pair = [C_ref[0, j] for j in pairs]
    n_row = [n_ref[0, j:j + 1, :] for j in pairs]
    k_bf = [k.astype(BF16) for k in k_t]
    v_bf = [v.astype(BF16) for v in v_t]
    s = [lax.dot_general(jnp.where(hmask(h), q_t[h // 2], 0.0).astype(BF16), k_bf[h // 2], nt,
                         preferred_element_type=F32) for h in heads]
    q_c = [_dot(q_t[j], c_pair[j]) for j in pairs]
    k_tr = [k.T for k in k_t]
    yield

    m_prev_t = m_ref[0]
    ut8 = ut[0:N_HEADS, :]
    m_last_t = jnp.maximum(jnp.max(ut8, axis=1, keepdims=True), m_prev_t)
    ws_rows = jnp.exp(ut8 - m_last_t)
    dec_t = jnp.exp(m_prev_t - m_last_t)
    m_ref[0] = bc.T[0:N_HEADS, LM - 1:LM] + m_last_t

    m_prev = [m_prev_t[h:h + 1, 0:1] for h in heads]
    u_row = [ut[h:h + 1, :] for h in heads]
    bc_col = [_rowsum(jnp.where(lane == h, bc, 0.0)) for h in heads]
    m_col = [jnp.maximum(jnp.max(jnp.where(causal, u_row[h], NEG), axis=1, keepdims=True), m_prev[h])
             for h in heads]
    p = [s[h] * jnp.exp(jnp.where(causal, u_row[h] - m_col[h], NEG)) for h in heads]
    sc = [jnp.exp(m_prev[h] - m_col[h]) for h in heads]
    yield
    pv = [jnp.dot(p[h].astype(BF16), v_bf[h // 2], preferred_element_type=F32) for h in heads]
    yield
    h_parts = []
    for h in heads:
        q_n = _rowsum(jnp.where(hmask(h), q_t[h // 2] * n_row[h // 2], 0.0))
        den = _rowsum(p[h]) + sc[h] * q_n
        num = pv[h] + sc[h] * q_c[h // 2]
        h_parts.append(num / jnp.maximum(jnp.abs(den), jnp.exp(-(bc_col[h] + m_col[h]))))

    top = _iota((LM, 1), 0) < HEAD_DIM
    kw = [k_tr[j] * jnp.where(top, ws_rows[2 * j:2 * j + 1, :], ws_rows[2 * j + 1:2 * j + 2, :])
          for j in pairs]
    upd = [jnp.dot(kw[j].astype(BF16), v_bf[j], preferred_element_type=F32) for j in pairs]
    n_upd = [jnp.dot(ws_rows.astype(BF16), k_bf[j], preferred_element_type=F32) for j in pairs]
    yield
    for j in pairs:
        dec_j = jnp.where(lo, dec_t[2 * j:2 * j + 1, :], dec_t[2 * j + 1:2 * j + 2, :])
        C_ref[0, j] = c_pair[j] * dec_j + jnp.where(blockdiag, upd[j], 0.0)
        n_ref[0, j:j + 1, :] = n_row[j] * dec_j + jnp.where(lo, n_upd[j][2 * j:2 * j + 1, :],
                                                            n_upd[j][2 * j + 1:2 * j + 2, :])

    for j in pairs:
        h_til = jnp.where(lo, h_parts[2 * j], h_parts[2 * j + 1])
        hml = jax.nn.sigmoid(proj_ref[0, rows, tile(O0, j)]) * h_til
        ms = _pair_rowsum(hml * hml, lo) * (1.0 / HEAD_DIM)
        mix_ref[0, rows, tile(0, j)] = hml * lax.rsqrt(ms + EPS) * mlnw_ref[:, tile(0, j)]


def _neumann_inverse(ns):
    shape = ns[0].shape
    eye = jnp.where(_iota(shape, 0) == _iota(shape, 1), 1.0, 0.0)
    n_sq = int(math.log2(LR)) - 1
    qs = [n.astype(BF16) for n in ns]
    ts = [eye + n for n in ns]
    sq = [jnp.dot(q, q, preferred_element_type=F32) for q in qs]
    yield
    for _ in range(n_sq - 1):
        qs = [s.astype(BF16) for s in sq]
        both = [jnp.dot(q, jnp.concatenate([q, t.astype(BF16)], axis=1), preferred_element_type=F32)
                for q, t in zip(qs, ts)]
        sq = [b[:, 0:LANES] for b in both]
        ts = [t + b[:, LANES:2 * LANES] for t, b in zip(ts, both)]
        yield
    return [(t + jnp.dot(s.astype(BF16), t.astype(BF16), preferred_element_type=F32)).astype(BF16)
            for s, t in zip(sq, ts)]


def _rwkv_chunks(c, rs_ref, w0_ref, wup_ref, a0_ref, aup_ref, gup_ref, kk_ref, ka_ref, rk_ref,
                 lnw_ref, lnb_ref, mix_ref, S_ref):
    gl = RW_GROUP * LR
    rows = pl.ds(pl.multiple_of(c * gl, gl), gl)
    lane = _iota((1, LANES), 1)
    lo = lane < HEAD_DIM
    row = _iota((LANES, LANES), 0)
    col = _iota((LANES, LANES), 1)
    same = (row < LR) == (col < LR)
    mask_sl = jnp.logical_and(same, col < row)
    mask_li = jnp.logical_and(same, col <= row)
    mask_li2 = jnp.concatenate([mask_li, mask_li], axis=1)
    grow = _iota((gl, gl), 0)
    gcol = _iota((gl, gl), 1)
    tril = jnp.where(jnp.logical_and(gcol <= grow, gcol >= (grow // LR) * LR), 1.0, 0.0).astype(BF16)
    nt = (((1,), (1,)), ((), ()))
    pairs = range(PAIRS)
    items = [(ci, j) for ci in range(RW_GROUP) for j in pairs]
    tile = lambda base, j: slice(base + j * LANES, base + (j + 1) * LANES)
    crow = lambda ci: slice(ci * LR, (ci + 1) * LR)

    wlog, a, g = _rwkv_token_prep(rs_ref[rows, RS_LORA:RS_LORA + LANES], rs_ref[rows, RS_G:RS_G + D_GATE],
                                  w0_ref[...], wup_ref[...], a0_ref[...], aup_ref[...], gup_ref[...])
    lw = _dot01_l(tril, wlog)
    e_in = jnp.exp(lw)
    e_ex = jnp.exp(lw - wlog)
    e_ng = jnp.exp(-lw)

    r_g = [rs_ref[rows, tile(RS_R, j)] for j in pairs]
    kr_g = [rs_ref[rows, tile(RS_K, j)] for j in pairs]
    v_g = [rs_ref[rows, tile(RS_V, j)] for j in pairs]
    a_g = [a[:, tile(0, j)] for j in pairs]
    kk = [kr_g[j] * kk_ref[:, tile(0, j)] for j in pairs]
    kkn = [kk[j] * lax.rsqrt(jnp.maximum(_pair_rowsum(kk[j] * kk[j], lo), 1e-24)) for j in pairs]
    keff_g = [kr_g[j] * (1.0 + (a_g[j] - 1.0) * ka_ref[:, tile(0, j)]) for j in pairs]
    alpha_g = [-kkn[j] * e_ex[:, tile(0, j)] for j in pairs]
    rb_g = [r_g[j] * e_in[:, tile(0, j)] for j in pairs]
    beta_g = [kkn[j] * a_g[j] * e_ng[:, tile(0, j)] for j in pairs]
    ktl_g = [keff_g[j] * e_ng[:, tile(0, j)] for j in pairs]
    bonus_g = [_pair_rowsum(r_g[j] * keff_g[j] * rk_ref[:, tile(0, j)], lo) * v_g[j] for j in pairs]

    lhs, rhs, v2, bk, gamma, ar = {}, {}, {}, {}, {}, {}
    for it in items:
        ci, j = it
        cr = crow(ci)
        ar[it] = jnp.concatenate([alpha_g[j][cr], rb_g[j][cr]], axis=0).astype(BF16)
        lhs[it] = jnp.concatenate([_stack2(ar[it][0:LR], lo), _stack2(ar[it][LR:2 * LR], lo)], axis=0)
        beta_c, ktl_c = beta_g[j][cr].astype(BF16), ktl_g[j][cr].astype(BF16)
        rhs[it] = jnp.concatenate([beta_c, beta_c, ktl_c, ktl_c], axis=0)
        v2[it] = _stack2(v_g[j][cr].astype(BF16), lo)
        bk[it] = jnp.concatenate([_stack2(beta_c, lo), _stack2(ktl_c, lo)], axis=0)
        gamma[it] = e_in[(ci + 1) * LR - 1:(ci + 1) * LR, tile(0, j)]
    aa = {it: lax.dot_general(lhs[it], rhs[it], nt, preferred_element_type=F32) for it in items}
    yield
    a_ab = [jnp.where(mask_sl, aa[it][0:LANES, 0:LANES], 0.0) for it in items]
    a_ak = {it: jnp.where(mask_sl, aa[it][0:LANES, LANES:2 * LANES], 0.0).astype(BF16) for it in items}
    a_r = {it: jnp.where(mask_li2, aa[it][LANES:2 * LANES, :], 0.0).astype(BF16) for it in items}
    akv = {it: jnp.dot(a_ak[it], v2[it], preferred_element_type=F32) for it in items}
    yield
    t_list = yield from _neumann_inverse(a_ab)
    t_inv = dict(zip(items, t_list))
    yield _STATE

    s_pair = [S_ref[0, j] for j in pairs]
    for ci in range(RW_GROUP):
        its = [(ci, j) for j in pairs]
        xs = [lax.dot_general(ar[it], s_pair[it[1]].astype(BF16), nt, preferred_element_type=F32)
              for it in its]
        yield
        p2 = [_dot(t_inv[it], _stack2(xs[j][0:LR], lo) + akv[it]) for j, it in enumerate(its)]
        yield
        pv = [jnp.concatenate([p2[j].astype(BF16), v2[it]], axis=0) for j, it in enumerate(its)]
        y2 = [_dot(a_r[it], pv[j]) for j, it in enumerate(its)]
        upd = [lax.dot_general(pv[j], bk[it], (((0,), (0,)), ((), ())), preferred_element_type=F32)
               for j, it in enumerate(its)]
        s_pair = [(s_pair[j] + upd[j]) * gamma[it] for j, it in enumerate(its)]
        yield
        cr = crow(ci)
        for j in pairs:
            yb = xs[j][LR:2 * LR] + y2[j][0:LR] + y2[j][LR:2 * LR] + bonus_g[j][cr]
            mu = _pair_rowsum(yb, lo) * (1.0 / HEAD_DIM)
            dlt = yb - mu
            var = _pair_rowsum(dlt * dlt, lo) * (1.0 / HEAD_DIM)
            yn = dlt * lax.rsqrt(var + GN_EPS) * lnw_ref[:, tile(0, j)] + lnb_ref[:, tile(0, j)]
            out_rows = pl.ds(pl.multiple_of(c * gl + ci * LR, LR), LR)
            mix_ref[0, out_rows, tile(GROUP_W, j)] = yn * g[cr, tile(0, j)]
    for j in pairs:
        S_ref[0, j] = s_pair[j]


def _shift_rows(x, tail, j):
    rolled = pltpu.roll(x, j, axis=0)
    head = jnp.where(_iota((SUBLANES, 1), 0) < j, pltpu.roll(tail, j, axis=0), rolled[0:SUBLANES])
    return jnp.concatenate([head, rolled[SUBLANES:]], axis=0)


def _until_state(gen):
    for tok in gen:
        if tok is _STATE:
            return
        yield


def _drain(gen):
    for _ in gen:
        yield


def _interleave(primary, secondary, every):
    done_p = done_s = False
    k = 0
    while not (done_p and done_s):
        if not done_p:
            done_p = next(primary, _DONE) is _DONE
        k += 1
        if not done_s and (done_p or k % every == 0):
            done_s = next(secondary, _DONE) is _DONE


def _mixer_kernel(proj_ref, convw_ref, convb_ref, ib_ref, fb_ref, mlnw_ref, mu_ref, w0_ref, wup_ref,
                  a0_ref, aup_ref, gup_ref, kk_ref, ka_ref, rk_ref, lnw_ref, lnb_ref,
                  mix_ref, Cout_ref, n_ref, m_ref, Sout_ref,
                  C_ref, S_ref, qk_tail_ref, rw_tail_ref, q_ref, k_ref, rs_ref):
    t = pl.program_id(1)

    @pl.when(t == 0)
    def _():
        C_ref[...] = jnp.zeros_like(C_ref)
        n_ref[...] = jnp.zeros_like(n_ref)
        m_ref[...] = jnp.zeros_like(m_ref)
        S_ref[...] = jnp.zeros_like(S_ref)
        qk_tail_ref[...] = jnp.zeros_like(qk_tail_ref)
        rw_tail_ref[...] = jnp.zeros_like(rw_tail_ref)

    u = proj_ref[0, :, Q0:Q0 + 2 * GROUP_W]
    qk_tail = qk_tail_ref[...]
    acc = convb_ref[...] + u * convw_ref[CONV_W - 1:CONV_W, :]
    for jw in range(CONV_W - 1):
        acc = acc + _shift_rows(u, qk_tail, CONV_W - 1 - jw) * convw_ref[jw:jw + 1, :]
    qk_tail_ref[...] = u[TB - SUBLANES:TB]
    qk = acc * jax.nn.sigmoid(acc)
    q_ref[...] = qk[:, 0:GROUP_W]
    k_ref[...] = qk[:, GROUP_W:2 * GROUP_W] * (HEAD_DIM ** -0.5)

    rw = proj_ref[0, :, RW0:RW0 + RW_IN_W]
    rs_ref[...] = rw + mu_ref[...] * (_shift_rows(rw, rw_tail_ref[...], 1) - rw)
    rw_tail_ref[...] = rw[TB - SUBLANES:TB]

    assert RW_GROUP * LR == LM
    rwkv = lambda c: _rwkv_chunks(c, rs_ref, w0_ref, wup_ref, a0_ref, aup_ref, gup_ref, kk_ref, ka_ref, rk_ref,
                                  lnw_ref, lnb_ref, mix_ref, S_ref)
    mlstm = lambda c: _mlstm_chunk(c, proj_ref, q_ref, k_ref, ib_ref, fb_ref, mlnw_ref, mix_ref, C_ref, n_ref, m_ref)
    n_groups = TB // LM
    gens = [rwkv(c) for c in range(n_groups)]
    _interleave(_until_state(gens[0]), mlstm(0), every=ML_EVERY)
    for c in range(1, n_groups):
        _interleave(_until_state(gens[c]), _drain(gens[c - 1]), every=1)
    _interleave(_drain(gens[-1]), itertools.chain(*[mlstm(c) for c in range(1, n_groups)]), every=1)

    @pl.when(t == pl.num_programs(1) - 1)
    def _():
        for h in range(N_HEADS):
            blk = slice((h % 2) * HEAD_DIM, (h % 2 + 1) * HEAD_DIM)
            Cout_ref[0, h] = C_ref[0, h // 2, blk, blk]
            Sout_ref[0, h] = S_ref[0, h // 2, blk, blk]


def _mixer(proj, p):
    bsz, seq, _ = proj.shape
    row = lambda w: pl.BlockSpec((1, w), lambda b, t: (0, 0))
    mat = lambda r, w: pl.BlockSpec((r, w), lambda b, t: (0, 0))
    state4 = pl.BlockSpec((1, N_HEADS, HEAD_DIM, HEAD_DIM), lambda b, t: (b, 0, 0, 0))
    state3 = pl.BlockSpec((1, SUBLANES, LANES), lambda b, t: (b, 0, 0))
    return pl.pallas_call(
        _mixer_kernel,
        out_shape=(
            jax.ShapeDtypeStruct((bsz, seq, 2 * GROUP_W), F32),
            jax.ShapeDtypeStruct((bsz, N_HEADS, HEAD_DIM, HEAD_DIM), F32),
            jax.ShapeDtypeStruct((bsz, SUBLANES, LANES), F32),
            jax.ShapeDtypeStruct((bsz, SUBLANES, LANES), F32),
            jax.ShapeDtypeStruct((bsz, N_HEADS, HEAD_DIM, HEAD_DIM), F32),
        ),
        grid=(bsz, seq // TB),
        in_specs=[
            pl.BlockSpec((1, TB, IN_WP), lambda b, t: (b, t, 0)),
            mat(CONV_W, 2 * GROUP_W), row(2 * GROUP_W), row(LANES), row(LANES), row(GROUP_W),
            row(RW_IN_W), row(GROUP_W), mat(LANES, GROUP_W), row(GROUP_W), mat(LANES, GROUP_W),
            mat(D_GATE, GROUP_W), row(GROUP_W), row(GROUP_W), row(GROUP_W), row(GROUP_W), row(GROUP_W),
        ],
        out_specs=(
            pl.BlockSpec((1, TB, 2 * GROUP_W), lambda b, t: (b, t, 0)),
            state4, state3, state3, state4,
        ),
        scratch_shapes=[
            pltpu.VMEM((1, PAIRS, LANES, LANES), F32),
            pltpu.VMEM((1, PAIRS, LANES, LANES), F32),
            pltpu.VMEM((SUBLANES, 2 * GROUP_W), F32),
            pltpu.VMEM((SUBLANES, RW_IN_W), F32),
            pltpu.VMEM((TB, GROUP_W), F32),
            pltpu.VMEM((TB, GROUP_W), F32),
            pltpu.VMEM((TB, RW_IN_W), F32),
        ],
        compiler_params=pltpu.CompilerParams(
            dimension_semantics=("arbitrary", "arbitrary"), vmem_limit_bytes=VMEM_LIMIT),
        name="mixer",
    )(proj, p["conv_w"], p["conv_b"], p["ib_row"], p["fb_row"], p["mlnorm_w"], p["mu"], p["w0"],
      p["wup_p"], p["a0"], p["aup_p"], p["gup"], p["k_k"], p["k_a"], p["r_k"], p["ln_w"], p["ln_b"])


def _inproj_t_kernel(x_ref, nw_ref, wml_ref, wgate_ref, wrw_ref, o_ref, ot_ref):
    o_ml, o_gate, o_rw = _project(x_ref, nw_ref, wml_ref, wgate_ref, wrw_ref)
    for lo_col, hi_col, o in ((0, GI0, o_ml), (GI0, RW0, o_gate), (RW0, IN_WP, o_rw)):
        o_ref[:, lo_col:hi_col] = o
        ot_ref[lo_col:hi_col, :] = o.T


def _inproj_t(x2d, norm_w, w_in_parts):
    rows = x2d.shape[0]
    return pl.pallas_call(
        _inproj_t_kernel,
        out_shape=(jax.ShapeDtypeStruct((rows, IN_WP), F32), jax.ShapeDtypeStruct((IN_WP, rows), F32)),
        compiler_params=pltpu.CompilerParams(vmem_limit_bytes=VMEM_LIMIT),
        name="inproj_t",
    )(x2d, norm_w, *w_in_parts)


(PC_W0, PC_A0, PC_MLNW, PC_KK, PC_KA, PC_RK, PC_LNW, PC_LNB) = range(8)
PC_CONVB = CONV_W


def _sample_kernel(pt_ref, conv0_ref, shift0_ref, m0_ref, n0_ref, C0_ref, S0_ref,
                   p1024_ref, p512_ref, mu_ref, gb_ref, wup_ref, aup_ref, gup_ref,
                   hml_ref, yrw_ref, C_ref, n_ref, m_ref, S_ref,
                   wk_s, q_s, vr_s, y_s):
    h = pl.program_id(0)
    hrow = lambda base, n=HEAD_DIM: pl.ds(pl.multiple_of(base + h * n, n), n)
    pcol = lambda j: p512_ref[hrow(0), j:j + 1]
    colsum = lambda t: jnp.sum(t, axis=0, keepdims=True)

    def conv_silu(base):
        rows = hrow(base)
        acc = p1024_ref[rows, PC_CONVB:PC_CONVB + 1] + pt_ref[rows, :] * p1024_ref[rows, CONV_W - 1:CONV_W]
        for jw in range(CONV_W - 1):
            acc = acc + conv0_ref[jw, rows, :] * p1024_ref[rows, jw:jw + 1]
        return acc * jax.nn.sigmoid(acc)

    q = conv_silu(Q0)
    k = conv_silu(K0) * (HEAD_DIM ** -0.5)
    v = pt_ref[hrow(V0), :]
    i_g = pt_ref[pl.ds(GI0 + h, 1), :] + gb_ref[pl.ds(h, 1), 0:1]
    logf = _log_sigmoid(pt_ref[pl.ds(GF0 + h, 1), :] + gb_ref[pl.ds(h, 1), 1:2])
    m0 = m0_ref[pl.ds(h, 1), :]
    m_new = jnp.maximum(logf + m0, i_g)
    ws = jnp.exp(i_g - m_new)
    dec = jnp.exp(logf + m0 - m_new)
    wk = ws * k
    n_new = dec * n0_ref[0] + wk
    n_ref[0] = n_new
    m_ref[pl.ds(h, 1), :] = m_new
    wk_s[...] = wk
    q_s[...] = q

    def c_body(kk_, num):
        ck = dec * C0_ref[0, kk_] + wk_s[pl.ds(kk_, 1), :] * v
        C_ref[0, kk_] = ck
        return num + q_s[pl.ds(kk_, 1), :] * ck

    num = lax.fori_loop(0, HEAD_DIM, c_body, jnp.zeros((HEAD_DIM, pt_ref.shape[1]), F32), unroll=SAMPLE_UNROLL)
    den = colsum(q * n_new)
    hml = jax.nn.sigmoid(pt_ref[hrow(O0), :]) * (num / jnp.maximum(jnp.abs(den), jnp.exp(-m_new)))
    hml_ref[...] = hml * lax.rsqrt(colsum(hml * hml) * (1.0 / HEAD_DIM) + EPS) * pcol(PC_MLNW)

    def shifted(base, n=HEAD_DIM):
        rw = pt_ref[hrow(RW0 + base, n) if n == HEAD_DIM else pl.ds(RW0 + base, n), :]
        rows = hrow(base, n) if n == HEAD_DIM else pl.ds(base, n)
        return rw + mu_ref[rows, :] * (shift0_ref[rows, :] - rw)

    r, kr, vr = shifted(RS_R), shifted(RS_K), shifted(RS_V)
    lora = shifted(RS_LORA, LANES)
    xg = shifted(RS_G, D_GATE)
    hw = hrow(0)
    w_raw = pcol(PC_W0) + jnp.dot(wup_ref[hw, :], jnp.tanh(lora).astype(BF16), preferred_element_type=F32)
    w = jnp.exp(-W_SCALE * jax.nn.sigmoid(w_raw))
    a = jax.nn.sigmoid(pcol(PC_A0) + jnp.dot(aup_ref[hw, :], lora.astype(BF16), preferred_element_type=F32))
    g = jnp.dot(gup_ref[hw, :], jax.nn.sigmoid(xg).astype(BF16), preferred_element_type=F32)
    kk = kr * pcol(PC_KK)
    kkn = kk / jnp.maximum(jnp.sqrt(colsum(kk * kk)), 1e-12)
    keff = kr * (1.0 + (a - 1.0) * pcol(PC_KA))
    nkk = -kkn
    ka = kkn * a
    vr_s[...] = vr

    def s_body(v_, carry):
        sv = S0_ref[0, v_]
        sn = sv * w + colsum(sv * nkk) * ka + vr_s[pl.ds(v_, 1), :] * keff
        S_ref[0, v_] = sn
        y_s[pl.ds(v_, 1), :] = colsum(sn * r)
        return carry

    lax.fori_loop(0, HEAD_DIM, s_body, 0, unroll=SAMPLE_UNROLL)
    yb = y_s[...] + colsum(r * keff * pcol(PC_RK)) * vr
    dlt = yb - colsum(yb) * (1.0 / HEAD_DIM)
    var = colsum(dlt * dlt) * (1.0 / HEAD_DIM)
    yrw_ref[...] = (dlt * lax.rsqrt(var + GN_EPS) * pcol(PC_LNW) + pcol(PC_LNB)) * g


def _sample(proj_t, conv0_t, shift0_t, m0_t, n0_t, c0_t, s0_t, p):
    bsz = proj_t.shape[1]
    whole = lambda a: pl.BlockSpec(a.shape, lambda h: (0,) * a.ndim)
    state4 = pl.BlockSpec((1, HEAD_DIM, HEAD_DIM, bsz), lambda h: (h, 0, 0, 0))
    head2 = pl.BlockSpec((HEAD_DIM, bsz), lambda h: (h, 0))
    head3 = pl.BlockSpec((1, HEAD_DIM, bsz), lambda h: (h, 0, 0))
    consts = [p["p1024"], p["p512"], p["mu_col"], p["gate_b"], p["wup_t"], p["aup_t"], p["gup_t"]]
    return pl.pallas_call(
        _sample_kernel,
        out_shape=(
            jax.ShapeDtypeStruct((GROUP_W, bsz), F32),
            jax.ShapeDtypeStruct((GROUP_W, bsz), F32),
            jax.ShapeDtypeStruct((N_HEADS, HEAD_DIM, HEAD_DIM, bsz), F32),
            jax.ShapeDtypeStruct((N_HEADS, HEAD_DIM, bsz), F32),
            jax.ShapeDtypeStruct((N_HEADS, bsz), F32),
            jax.ShapeDtypeStruct((N_HEADS, HEAD_DIM, HEAD_DIM, bsz), F32),
        ),
        grid=(N_HEADS,),
        in_specs=[whole(proj_t), whole(conv0_t), whole(shift0_t), whole(m0_t), head3, state4, state4]
                 + [whole(c) for c in consts],
        out_specs=(head2, head2, state4, head3, pl.BlockSpec((N_HEADS, bsz), lambda h: (0, 0)), state4),
        scratch_shapes=[pltpu.VMEM((HEAD_DIM, bsz), F32) for _ in range(4)],
        compiler_params=pltpu.CompilerParams(
            dimension_semantics=("arbitrary",), vmem_limit_bytes=VMEM_LIMIT),
        name="sample",
    )(proj_t, conv0_t, shift0_t, m0_t, n0_t, c0_t, s0_t, *consts)


def _layer_params(l, norm_mix_w, w_in, mlstm_conv_w, mlstm_conv_b, mlstm_i_b, mlstm_f_b, mlstm_norm_w,
                  rw_mu, rw_w0, rw_w_up, rw_a0, rw_a_up, rw_g_up, rw_k_k, rw_k_a, rw_r_k, rw_ln_w, rw_ln_b,
                  w_out, norm_mlp_w, mlp_up, mlp_down):
    ml_in = 4 * GROUP_W
    wi = w_in[l]
    zpad = jnp.zeros((D_MODEL, LANES - N_HEADS), BF16)
    w_in_parts = (
        wi[:, :ml_in].astype(BF16),
        jnp.concatenate([wi[:, ml_in:ml_in + N_HEADS].astype(BF16), zpad,
                         wi[:, ml_in + N_HEADS:ml_in + 2 * N_HEADS].astype(BF16), zpad], axis=1),
        wi[:, ml_in + 2 * N_HEADS:].astype(BF16),
    )
    gate_row = lambda b: jnp.concatenate([b, jnp.zeros((LANES - N_HEADS,), F32)])[None, :]
    zl = jnp.zeros((D_LORA, GROUP_W), F32)
    return {
        "norm_mix_w": norm_mix_w[l][None, :],
        "w_in_parts": w_in_parts,
        "conv_w": mlstm_conv_w[l], "conv_b": mlstm_conv_b[l][None, :],
        "ib_row": gate_row(mlstm_i_b[l]), "fb_row": gate_row(mlstm_f_b[l]),
        "gate_b": jnp.stack([mlstm_i_b[l], mlstm_f_b[l]], axis=1),
        "p1024": jnp.concatenate([mlstm_conv_w[l].T, mlstm_conv_b[l][:, None],
                                  jnp.zeros((2 * GROUP_W, SUBLANES - CONV_W - 1), F32)], axis=1),
        "p512": jnp.stack([rw_w0[l], rw_a0[l], mlstm_norm_w[l], rw_k_k[l], rw_k_a[l], rw_r_k[l].reshape(GROUP_W),
                           rw_ln_w[l], rw_ln_b[l]], axis=1),
        "mu_col": rw_mu[l][:, None],
        "mlnorm_w": mlstm_norm_w[l][None, :],
        "mu": rw_mu[l][None, :], "w0": rw_w0[l][None, :], "a0": rw_a0[l][None, :],
        "wup_p": jnp.concatenate([rw_w_up[l], zl], axis=0).astype(BF16),
        "aup_p": jnp.concatenate([zl, rw_a_up[l]], axis=0).astype(BF16),
        "gup": rw_g_up[l].astype(BF16),
        "wup_t": jnp.concatenate([rw_w_up[l], zl], axis=0).T.astype(BF16),
        "aup_t": jnp.concatenate([zl, rw_a_up[l]], axis=0).T.astype(BF16),
        "gup_t": rw_g_up[l].T.astype(BF16),
        "k_k": rw_k_k[l][None, :], "k_a": rw_k_a[l][None, :], "r_k": rw_r_k[l].reshape(1, GROUP_W),
        "ln_w": rw_ln_w[l][None, :], "ln_b": rw_ln_b[l][None, :],
        "w_out": w_out[l].astype(BF16), "norm_mlp_w": norm_mlp_w[l][None, :],
        "mlp_up": mlp_up[l].astype(BF16), "mlp_down": mlp_down[l].astype(BF16),
    }


def _prompt_layer(x, p, norm_f_w):
    bsz, seq, _ = x.shape
    x2d = x.reshape(bsz * seq, D_MODEL)
    proj = _inproj(x2d, p["norm_mix_w"], p["w_in_parts"]).reshape(bsz, seq, IN_WP)
    mix, c_st, n_st, m_st, s_st = _mixer(proj, p)
    y = _outmlp(x2d, mix.reshape(bsz * seq, D_MODEL), p["w_out"], p["norm_mlp_w"], p["mlp_up"],
                p["mlp_down"], norm_f_w)
    c_new = c_st
    n_new = n_st[:, :PAIRS, :].reshape(bsz, N_HEADS, HEAD_DIM)
    m_new = m_st[:, :, 0]
    conv_new = proj[:, seq - (CONV_W - 1):, Q0:Q0 + 2 * GROUP_W]
    s_new = s_st
    shift_new = proj[:, seq - 1:, RW0:RW0 + RW_IN_W]
    return y.reshape(bsz, seq, D_MODEL), (c_new, n_new, m_new, conv_new, s_new, shift_new)


def _sample_layer(x, c0, n0, m0, conv0, s0, shift0, p, norm_f_w):
    bsz = x.shape[0]
    x2d = x.reshape(bsz, D_MODEL)
    proj, proj_t = _inproj_t(x2d, p["norm_mix_w"], p["w_in_parts"])
    hml_t, yrw_t, c_t, n_t, m_t, s_t = _sample(
        proj_t, jnp.transpose(conv0, (1, 2, 0)), shift0[:, 0, :].T, m0.T, jnp.transpose(n0, (1, 2, 0)),
        jnp.transpose(c0, (1, 2, 3, 0)), jnp.transpose(s0, (1, 2, 3, 0)), p)
    mix = jnp.concatenate([hml_t, yrw_t], axis=0).T
    y = _outmlp(x2d, mix, p["w_out"], p["norm_mlp_w"], p["mlp_up"], p["mlp_down"], norm_f_w)
    conv_new = jnp.concatenate([conv0[:, 1:], proj[:, None, Q0:Q0 + 2 * GROUP_W]], axis=1)
    shift_new = proj[:, None, RW0:RW0 + RW_IN_W]
    return y.reshape(bsz, 1, D_MODEL), (
        jnp.transpose(c_t, (3, 0, 1, 2)), jnp.transpose(n_t, (2, 0, 1)), m_t.T, conv_new,
        jnp.transpose(s_t, (3, 0, 1, 2)), shift_new)


def kernel(x_prompt, x_sample, state_mlstm_C, state_mlstm_n, state_mlstm_m, state_mlstm_conv, state_rwkv_S, state_rwkv_shift, norm_mix_w, w_in, mlstm_conv_w, mlstm_conv_b, mlstm_i_b, mlstm_f_b, mlstm_norm_w, rw_mu, rw_w0, rw_w_up, rw_a0, rw_a_up, rw_g_up, rw_k_k, rw_k_a, rw_r_k, rw_ln_w, rw_ln_b, w_out, norm_mlp_w, mlp_up, mlp_down, norm_f_w):
    depth = w_in.shape[0]
    assert depth == 1, "the final RMSNorm is fused into the layer's output kernel"
    weights = (norm_mix_w, w_in, mlstm_conv_w, mlstm_conv_b, mlstm_i_b, mlstm_f_b, mlstm_norm_w,
               rw_mu, rw_w0, rw_w_up, rw_a0, rw_a_up, rw_g_up, rw_k_k, rw_k_a, rw_r_k, rw_ln_w, rw_ln_b,
               w_out, norm_mlp_w, mlp_up, mlp_down)
    nfw = norm_f_w[None, :]
    p = _layer_params(0, *weights)
    y_p, st_p = _prompt_layer(x_prompt, p, nfw)
    y_s, st_s = _sample_layer(x_sample, state_mlstm_C[0], state_mlstm_n[0], state_mlstm_m[0],
                              state_mlstm_conv[0], state_rwkv_S[0], state_rwkv_shift[0], p, nfw)
    lead = lambda t: tuple(a[None] for a in t)
    return (y_p, y_s) + lead(st_p) + lead(st_s)
```

```python
import itertools
import math

import jax
import jax.numpy as jnp
from jax import lax
from jax.experimental import pallas as pl
from jax.experimental.pallas import tpu as pltpu

F32 = jnp.float32
BF16 = jnp.bfloat16

D_MODEL = 1024
HEAD_DIM = 64
N_HEADS = 8
PAIRS = N_HEADS // 2
GROUP_W = N_HEADS * HEAD_DIM
CONV_W = 4
D_LORA = 64
D_GATE = 128
D_FF = 4 * D_MODEL
EPS = 1e-6
GN_EPS = 64e-5
LANES = 128
SUBLANES = 8

Q0, K0, V0, O0 = 0, GROUP_W, 2 * GROUP_W, 3 * GROUP_W
GI0 = 4 * GROUP_W
GF0 = GI0 + LANES
RW0 = GF0 + LANES
RW_IN_W = 3 * GROUP_W + 2 * D_LORA + D_GATE
IN_WP = RW0 + RW_IN_W
RS_R, RS_K, RS_V = 0, GROUP_W, 2 * GROUP_W
RS_LORA = 3 * GROUP_W
RS_G = RS_LORA + 2 * D_LORA

TB = 256
LM = 128
LR = 64
RW_GROUP = 2
ML_EVERY = 2
_DONE = object()
_STATE = object()
TM_IN = 512
TM_OUT = 1024
FF_CHUNK = 1024
SAMPLE_UNROLL = 8
NEG = -1e30
W_SCALE = math.exp(-0.5)
VMEM_LIMIT = 56 * 1024 * 1024


def _dot(a, b):
    return jnp.dot(a.astype(BF16), b.astype(BF16), preferred_element_type=F32)


def _split3(x):
    h1 = x.astype(BF16)
    r1 = x - h1.astype(F32)
    h2 = r1.astype(BF16)
    h3 = (r1 - h2.astype(F32)).astype(BF16)
    return h1, h2, h3


def _dot01_l(m01, x):
    h1, h2, h3 = _split3(x)
    d = lambda h: jnp.dot(m01, h, preferred_element_type=F32)
    return d(h1) + d(h2) + d(h3)


def _log_sigmoid(x):
    return jnp.minimum(x, 0.0) - jnp.log(1.0 + jnp.exp(-jnp.abs(x)))


def _tanh(x):
    e = jnp.exp(-2.0 * jnp.abs(x))
    t = (1.0 - e) / (1.0 + e)
    return jnp.where(x < 0.0, -t, t)


def _iota(shape, dim):
    return lax.broadcasted_iota(jnp.int32, shape, dim)


def _rowsum(x):
    return jnp.sum(x, axis=1, keepdims=True)


def _pair_rowsum(x, lo):
    s_lo = _rowsum(jnp.where(lo, x, 0.0))
    s_hi = _rowsum(jnp.where(lo, 0.0, x))
    return jnp.where(lo, s_lo, s_hi)


def _stack2(x, lo):
    zero = jnp.zeros_like(x)
    return jnp.concatenate([jnp.where(lo, x, zero), jnp.where(lo, zero, x)], axis=0)


def _rms(x, w):
    ms = jnp.mean(x * x, axis=-1, keepdims=True)
    return x * lax.rsqrt(ms + EPS) * w


def _project(x_ref, nw_ref, wml_ref, wgate_ref, wrw_ref):
    xn = _rms(x_ref[...], nw_ref[...]).astype(BF16)
    return [jnp.dot(xn, w[...], preferred_element_type=F32) for w in (wml_ref, wgate_ref, wrw_ref)]


def _inproj_kernel(x_ref, nw_ref, wml_ref, wgate_ref, wrw_ref, o_ref):
    o_ml, o_gate, o_rw = _project(x_ref, nw_ref, wml_ref, wgate_ref, wrw_ref)
    o_ref[:, 0:GI0] = o_ml
    o_ref[:, GI0:RW0] = o_gate
    o_ref[:, RW0:IN_WP] = o_rw


def _inproj(x2d, norm_w, w_in_parts):
    rows = x2d.shape[0]
    tm = min(TM_IN, rows)
    return pl.pallas_call(
        _inproj_kernel,
        out_shape=jax.ShapeDtypeStruct((rows, IN_WP), F32),
        grid=(rows // tm,),
        in_specs=[
            pl.BlockSpec((tm, D_MODEL), lambda i: (i, 0)),
            pl.BlockSpec((1, D_MODEL), lambda i: (0, 0)),
        ] + [pl.BlockSpec(w.shape, lambda i: (0, 0)) for w in w_in_parts],
        out_specs=pl.BlockSpec((tm, IN_WP), lambda i: (i, 0)),
        compiler_params=pltpu.CompilerParams(
            dimension_semantics=("arbitrary",), vmem_limit_bytes=VMEM_LIMIT),
        name="inproj",
    )(x2d, norm_w, *w_in_parts)


def _outmlp_kernel(x_ref, mix_ref, wout_ref, nmw_ref, up_ref, down_ref, nfw_ref, y_ref):
    x1 = x_ref[...] + jnp.dot(mix_ref[...].astype(BF16), wout_ref[...], preferred_element_type=F32)
    xn = _rms(x1, nmw_ref[...]).astype(BF16)
    acc = x1
    for c in range(D_FF // FF_CHUNK):
        sl = slice(c * FF_CHUNK, (c + 1) * FF_CHUNK)
        hid = jnp.maximum(jnp.dot(xn, up_ref[:, sl], preferred_element_type=F32), 0.0)
        acc = acc + jnp.dot((hid * hid).astype(BF16), down_ref[sl, :], preferred_element_type=F32)
    y_ref[...] = _rms(acc, nfw_ref[...])


def _outmlp(x2d, mix2d, w_out, norm_mlp_w, mlp_up, mlp_down, norm_f_w):
    rows = x2d.shape[0]
    tm = min(TM_OUT, rows)
    const = lambda shape: pl.BlockSpec(shape, lambda i: (0, 0), pipeline_mode=pl.Buffered(1))
    return pl.pallas_call(
        _outmlp_kernel,
        out_shape=jax.ShapeDtypeStruct((rows, D_MODEL), F32),
        grid=(rows // tm,),
        in_specs=[
            pl.BlockSpec((tm, D_MODEL), lambda i: (i, 0)),
            pl.BlockSpec((tm, D_MODEL), lambda i: (i, 0)),
            const((D_MODEL, D_MODEL)),
            const((1, D_MODEL)),
            const((D_MODEL, D_FF)),
            const((D_FF, D_MODEL)),
            const((1, D_MODEL)),
        ],
        out_specs=pl.BlockSpec((tm, D_MODEL), lambda i: (i, 0)),
        compiler_params=pltpu.CompilerParams(
            dimension_semantics=("arbitrary",), vmem_limit_bytes=VMEM_LIMIT),
        name="outmlp",
    )(x2d, mix2d, w_out, norm_mlp_w, mlp_up, mlp_down, norm_f_w)


def _rwkv_token_prep(lora_t, xg, w0, wup, a0, aup, gup):
    w_raw = w0 + jnp.dot(_tanh(lora_t).astype(BF16), wup, preferred_element_type=F32)
    wlog = -W_SCALE * jax.nn.sigmoid(w_raw)
    a = jax.nn.sigmoid(a0 + jnp.dot(lora_t.astype(BF16), aup, preferred_element_type=F32))
    g = jnp.dot(jax.nn.sigmoid(xg).astype(BF16), gup, preferred_element_type=F32)
    return wlog, a, g


def _mlstm_chunk(c, proj_ref, q_ref, k_ref, ib_ref, fb_ref, mlnw_ref, mix_ref, C_ref, n_ref, m_ref):
    rows = pl.ds(pl.multiple_of(c * LM, LM), LM)
    row = _iota((LM, LM), 0)
    col = _iota((LM, LM), 1)
    causal = col <= row
    tril = jnp.where(causal, 1.0, 0.0).astype(BF16)
    blockdiag = (row < HEAD_DIM) == (col < HEAD_DIM)
    lane = _iota((1, LANES), 1)
    lo = lane < HEAD_DIM

    i_tile = proj_ref[0, rows, GI0:GI0 + LANES] + ib_ref[...]
    f_tile = _log_sigmoid(proj_ref[0, rows, GF0:GF0 + LANES] + fb_ref[...])
    bc = _dot01_l(tril, f_tile)
    u = i_tile - bc
    ut = u.T
    yield

    nt = (((1,), (1,)), ((), ()))
    pairs = range(PAIRS)
    heads = range(N_HEADS)
    tile = lambda base, j: slice(base + j * LANES, base + (j + 1) * LANES)
    hmask = lambda h: lo if h % 2 == 0 else jnp.logical_not(lo)
    q_t = [q_ref[rows, tile(0, j)] for j in pairs]
    k_t = [k_ref[rows, tile(0, j)] for j in pairs]
    v_t = [proj_ref[0, rows, tile(V0, j)] for j in pairs]
    c_pair = [C_ref[0, j] for j in pairs]
    n_row = [n_ref[0, j:j + 1, :] for j in pairs]
    k_bf = [k.astype(BF16) for k in k_t]
    v_bf = [v.astype(BF16) for v in v_t]
    s = [lax.dot_general(jnp.where(hmask(h), q_t[h // 2], 0.0).astype(BF16), k_bf[h // 2], nt,
                         preferred_element_type=F32) for h in heads]
    q_c = [_dot(q_t[j], c_pair[j]) for j in pairs]
    k_tr = [k.T for k in k_t]
    yield

    m_prev_t = m_ref[0]
    ut8 = ut[0:N_HEADS, :]
    m_last_t = jnp.maximum(jnp.max(ut8, axis=1, keepdims=True), m_prev_t)
    ws_rows = jnp.exp(ut8 - m_last_t)
    dec_t = jnp.exp(m_prev_t - m_last_t)
    m_ref[0] = bc.T[0:N_HEADS, LM - 1:LM] + m_last_t

    m_prev = [m_prev_t[h:h + 1, 0:1] for h in heads]
    u_row = [ut[h:h + 1, :] for h in heads]
    bc_col = [_rowsum(jnp.where(lane == h, bc, 0.0)) for h in heads]
    m_col = [jnp.maximum(jnp.max(jnp.where(causal, u_row[h], NEG), axis=1, keepdims=True), m_prev[h])
             for h in heads]
    p = [s[h] * jnp.exp(jnp.where(causal, u_row[h] - m_col[h], NEG)) for h in heads]
    sc = [jnp.exp(m_prev[h] - m_col[h]) for h in heads]
    yield
    pv = [jnp.dot(p[h].astype(BF16), v_bf[h // 2], preferred_element_type=F32) for h in heads]
    yield
    h_parts = []
    for h in heads:
        q_n = _rowsum(jnp.where(hmask(h), q_t[h // 2] * n_row[h // 2], 0.0))
        den = _rowsum(p[h]) + sc[h] * q_n
        num = pv[h] + sc[h] * q_c[h // 2]
        h_parts.append(num / jnp.maximum(jnp.abs(den), jnp.exp(-(bc_col[h] + m_col[h]))))

    top = _iota((LM, 1), 0) < HEAD_DIM
    kw = [k_tr[j] * jnp.where(top, ws_rows[2 * j:2 * j + 1, :], ws_rows[2 * j + 1:2 * j + 2, :])
          for j in pairs]
    upd = [jnp.dot(kw[j].astype(BF16), v_bf[j], preferred_element_type=F32) for j in pairs]
    n_upd = [jnp.dot(ws_rows.astype(BF16), k_bf[j], preferred_element_type=F32) for j in pairs]
    yield
    for j in pairs:
        dec_j = jnp.where(lo, dec_t[2 * j:2 * j + 1, :], dec_t[2 * j + 1:2 * j + 2, :])
        C_ref[0, j] = c_pair[j] * dec_j + jnp.where(blockdiag, upd[j], 0.0)
        n_ref[0, j:j + 1, :] = n_row[j] * dec_j + jnp.where(lo, n_upd[j][2 * j:2 * j + 1, :],
                                                            n_upd[j][2 * j + 1:2 * j + 2, :])

    for j in pairs:
        h_til = jnp.where(lo, h_parts[2 * j], h_parts[2 * j + 1])
        hml = jax.nn.sigmoid(proj_ref[0, rows, tile(O0, j)]) * h_til
        ms = _pair_rowsum(hml * hml, lo) * (1.0 / HEAD_DIM)
        mix_ref[0, rows, tile(0, j)] = hml * lax.rsqrt(ms + EPS) * mlnw_ref[:, tile(0, j)]


def _neumann_inverse(ns):
    shape = ns[0].shape
    eye = jnp.where(_iota(shape, 0) == _iota(shape, 1), 1.0, 0.0)
    n_sq = int(math.log2(LR)) - 1
    qs = [n.astype(BF16) for n in ns]
    ts = [eye + n for n in ns]
    sq = [jnp.dot(q, q, preferred_element_type=F32) for q in qs]
    yield
    for _ in range(n_sq - 1):
        qs = [s.astype(BF16) for s in sq]
        both = [jnp.dot(q, jnp.concatenate([q, t.astype(BF16)], axis=1), preferred_element_type=F32)
                for q, t in zip(qs, ts)]
        sq = [b[:, 0:LANES] for b in both]
        ts = [t + b[:, LANES:2 * LANES] for t, b in zip(ts, both)]
        yield
    return [(t + jnp.dot(s.astype(BF16), t.astype(BF16), preferred_element_type=F32)).astype(BF16)
            for s, t in zip(sq, ts)]


def _rwkv_chunks(c, rs_ref, w0_ref, wup_ref, a0_ref, aup_ref, gup_ref, kk_ref, ka_ref, rk_ref,
                 lnw_ref, lnb_ref, mix_ref, S_ref):
    gl = RW_GROUP * LR
    rows = pl.ds(pl.multiple_of(c * gl, gl), gl)
    lane = _iota((1, LANES), 1)
    lo = lane < HEAD_DIM
    row = _iota((LANES, LANES), 0)
    col = _iota((LANES, LANES), 1)
    same = (row < LR) == (col < LR)
    mask_sl = jnp.logical_and(same, col < row)
    mask_li = jnp.logical_and(same, col <= row)
    mask_li2 = jnp.concatenate([mask_li, mask_li], axis=1)
    grow = _iota((gl, gl), 0)
    gcol = _iota((gl, gl), 1)
    tril = jnp.where(jnp.logical_and(gcol <= grow, gcol >= (grow // LR) * LR), 1.0, 0.0).astype(BF16)
    nt = (((1,), (1,)), ((), ()))
    pairs = range(PAIRS)
    items = [(ci, j) for ci in range(RW_GROUP) for j in pairs]
    tile = lambda base, j: slice(base + j * LANES, base + (j + 1) * LANES)
    crow = lambda ci: slice(ci * LR, (ci + 1) * LR)

    wlog, a, g = _rwkv_token_prep(rs_ref[rows, RS_LORA:RS_LORA + LANES], rs_ref[rows, RS_G:RS_G + D_GATE],
                                  w0_ref[...], wup_ref[...], a0_ref[...], aup_ref[...], gup_ref[...])
    lw = _dot01_l(tril, wlog)
    e_in = jnp.exp(lw)
    e_ex = jnp.exp(lw - wlog)
    e_ng = jnp.exp(-lw)

    r_g = [rs_ref[rows, tile(RS_R, j)] for j in pairs]
    kr_g = [rs_ref[rows, tile(RS_K, j)] for j in pairs]
    v_g = [rs_ref[rows, tile(RS_V, j)] for j in pairs]
    a_g = [a[:, tile(0, j)] for j in pairs]
    kk = [kr_g[j] * kk_ref[:, tile(0, j)] for j in pairs]
    kkn = [kk[j] * lax.rsqrt(jnp.maximum(_pair_rowsum(kk[j] * kk[j], lo), 1e-24)) for j in pairs]
    keff_g = [kr_g[j] * (1.0 + (a_g[j] - 1.0) * ka_ref[:, tile(0, j)]) for j in pairs]
    alpha_g = [-kkn[j] * e_ex[:, tile(0, j)] for j in pairs]
    rb_g = [r_g[j] * e_in[:, tile(0, j)] for j in pairs]
    beta_g = [kkn[j] * a_g[j] * e_ng[:, tile(0, j)] for j in pairs]
    ktl_g = [keff_g[j] * e_ng[:, tile(0, j)] for j in pairs]
    bonus_g = [_pair_rowsum(r_g[j] * keff_g[j] * rk_ref[:, tile(0, j)], lo) * v_g[j] for j in pairs]

    lhs, rhs, v2, bk, gamma, ar = {}, {}, {}, {}, {}, {}
    for it in items:
        ci, j = it
        cr = crow(ci)
        ar[it] = jnp.concatenate([alpha_g[j][cr], rb_g[j][cr]], axis=0).astype(BF16)
        lhs[it] = jnp.concatenate([_stack2(ar[it][0:LR], lo), _stack2(ar[it][LR:2 * LR], lo)], axis=0)
        beta_c, ktl_c = beta_g[j][cr].astype(BF16), ktl_g[j][cr].astype(BF16)
        rhs[it] = jnp.concatenate([beta_c, beta_c, ktl_c, ktl_c], axis=0)
        v2[it] = _stack2(v_g[j][cr].astype(BF16), lo)
        bk[it] = jnp.concatenate([_stack2(beta_c, lo), _stack2(ktl_c, lo)], axis=0)
        gamma[it] = e_in[(ci + 1) * LR - 1:(ci + 1) * LR, tile(0, j)]
    aa = {it: lax.dot_general(lhs[it], rhs[it], nt, preferred_element_type=F32) for it in items}
    yield
    a_ab = [jnp.where(mask_sl, aa[it][0:LANES, 0:LANES], 0.0) for it in items]
    a_ak = {it: jnp.where(mask_sl, aa[it][0:LANES, LANES:2 * LANES], 0.0).astype(BF16) for it in items}
    a_r = {it: jnp.where(mask_li2, aa[it][LANES:2 * LANES, :], 0.0).astype(BF16) for it in items}
    akv = {it: jnp.dot(a_ak[it], v2[it], preferred_element_type=F32) for it in items}
    yield
    t_list = yield from _neumann_inverse(a_ab)
    t_inv = dict(zip(items, t_list))
    yield _STATE

    s_pair = [S_ref[0, j] for j in pairs]
    for ci in range(RW_GROUP):
        its = [(ci, j) for j in pairs]
        xs = [lax.dot_general(ar[it], s_pair[it[1]].astype(BF16), nt, preferred_element_type=F32)
              for it in its]
        yield
        p2 = [_dot(t_inv[it], _stack2(xs[j][0:LR], lo) + akv[it]) for j, it in enumerate(its)]
        yield
        pv = [jnp.concatenate([p2[j].astype(BF16), v2[it]], axis=0) for j, it in enumerate(its)]
        y2 = [_dot(a_r[it], pv[j]) for j, it in enumerate(its)]
        upd = [lax.dot_general(pv[j], bk[it], (((0,), (0,)), ((), ())), preferred_element_type=F32)
               for j, it in enumerate(its)]
        s_pair = [(s_pair[j] + upd[j]) * gamma[it] for j, it in enumerate(its)]
        yield
        cr = crow(ci)
        for j in pairs:
            yb = xs[j][LR:2 * LR] + y2[j][0:LR] + y2[j][LR:2 * LR] + bonus_g[j][cr]
            mu = _pair_rowsum(yb, lo) * (1.0 / HEAD_DIM)
            dlt = yb - mu
            var = _pair_rowsum(dlt * dlt, lo) * (1.0 / HEAD_DIM)
            yn = dlt * lax.rsqrt(var + GN_EPS) * lnw_ref[:, tile(0, j)] + lnb_ref[:, tile(0, j)]
            out_rows = pl.ds(pl.multiple_of(c * gl + ci * LR, LR), LR)
            mix_ref[0, out_rows, tile(GROUP_W, j)] = yn * g[cr, tile(0, j)]
    for j in pairs:
        S_ref[0, j] = s_pair[j]


def _shift_rows(x, tail, j):
    rolled = pltpu.roll(x, j, axis=0)
    head = jnp.where(_iota((SUBLANES, 1), 0) < j, pltpu.roll(tail, j, axis=0), rolled[0:SUBLANES])
    return jnp.concatenate([head, rolled[SUBLANES:]], axis=0)


def _until_state(gen):
    for tok in gen:
        if tok is _STATE:
            return
        yield


def _drain(gen):
    for _ in gen:
        yield


def _interleave(primary, secondary, every):
    done_p = done_s = False
    k = 0
    while not (done_p and done_s):
        if not done_p:
            done_p = next(primary, _DONE) is _DONE
        k += 1
        if not done_s and (done_p or k % every == 0):
            done_s = next(secondary, _DONE) is _DONE


def _mixer_kernel(proj_ref, convw_ref, convb_ref, ib_ref, fb_ref, mlnw_ref, mu_ref, w0_ref, wup_ref,
                  a0_ref, aup_ref, gup_ref, kk_ref, ka_ref, rk_ref, lnw_ref, lnb_ref,
                  mix_ref, Cout_ref, n_ref, m_ref, Sout_ref,
                  C_ref, S_ref, qk_tail_ref, rw_tail_ref, q_ref, k_ref, rs_ref):
    t = pl.program_id(1)

    @pl.when(t == 0)
    def _():
        C_ref[...] = jnp.zeros_like(C_ref)
        n_ref[...] = jnp.zeros_like(n_ref)
        m_ref[...] = jnp.zeros_like(m_ref)
        S_ref[...] = jnp.zeros_like(S_ref)
        qk_tail_ref[...] = jnp.zeros_like(qk_tail_ref)
        rw_tail_ref[...] = jnp.zeros_like(rw_tail_ref)

    u = proj_ref[0, :, Q0:Q0 + 2 * GROUP_W]
    qk_tail = qk_tail_ref[...]
    acc = convb_ref[...] + u * convw_ref[CONV_W - 1:CONV_W, :]
    for jw in range(CONV_W - 1):
        acc = acc + _shift_rows(u, qk_tail, CONV_W - 1 - jw) * convw_ref[jw:jw + 1, :]
    qk_tail_ref[...] = u[TB - SUBLANES:TB]
    qk = acc * jax.nn.sigmoid(acc)
    q_ref[...] = qk[:, 0:GROUP_W]
    k_ref[...] = qk[:, GROUP_W:2 * GROUP_W] * (HEAD_DIM ** -0.5)

    rw = proj_ref[0, :, RW0:RW0 + RW_IN_W]
    rs_ref[...] = rw + mu_ref[...] * (_shift_rows(rw, rw_tail_ref[...], 1) - rw)
    rw_tail_ref[...] = rw[TB - SUBLANES:TB]

    assert RW_GROUP * LR == LM
    rwkv = lambda c: _rwkv_chunks(c, rs_ref, w0_ref, wup_ref, a0_ref, aup_ref, gup_ref, kk_ref, ka_ref, rk_ref,
                                  lnw_ref, lnb_ref, mix_ref, S_ref)
    mlstm = lambda c: _mlstm_chunk(c, proj_ref, q_ref, k_ref, ib_ref, fb_ref, mlnw_ref, mix_ref, C_ref, n_ref, m_ref)
    n_groups = TB // LM
    gens = [rwkv(c) for c in range(n_groups)]
    _interleave(_until_state(gens[0]), mlstm(0), every=ML_EVERY)
    for c in range(1, n_groups):
        _interleave(_until_state(gens[c]), _drain(gens[c - 1]), every=1)
    _interleave(_drain(gens[-1]), itertools.chain(*[mlstm(c) for c in range(1, n_groups)]), every=1)

    @pl.when(t == pl.num_programs(1) - 1)
    def _():
        for h in range(N_HEADS):
            blk = slice((h % 2) * HEAD_DIM, (h % 2 + 1) * HEAD_DIM)
            Cout_ref[0, h] = C_ref[0, h // 2, blk, blk]
            Sout_ref[0, h] = S_ref[0, h // 2, blk, blk]


def _mixer(proj, p):
    bsz, seq, _ = proj.shape
    row = lambda w: pl.BlockSpec((1, w), lambda b, t: (0, 0))
    mat = lambda r, w: pl.BlockSpec((r, w), lambda b, t: (0, 0))
    state4 = pl.BlockSpec((1, N_HEADS, HEAD_DIM, HEAD_DIM), lambda b, t: (b, 0, 0, 0))
    state3 = pl.BlockSpec((1, SUBLANES, LANES), lambda b, t: (b, 0, 0))
    return pl.pallas_call(
        _mixer_kernel,
        out_shape=(
            jax.ShapeDtypeStruct((bsz, seq, 2 * GROUP_W), F32),
            jax.ShapeDtypeStruct((bsz, N_HEADS, HEAD_DIM, HEAD_DIM), F32),
            jax.ShapeDtypeStruct((bsz, SUBLANES, LANES), F32),
            jax.ShapeDtypeStruct((bsz, SUBLANES, LANES), F32),
            jax.ShapeDtypeStruct((bsz, N_HEADS, HEAD_DIM, HEAD_DIM), F32),
        ),
        grid=(bsz, seq // TB),
        in_specs=[
            pl.BlockSpec((1, TB, IN_WP), lambda b, t: (b, t, 0)),
            mat(CONV_W, 2 * GROUP_W), row(2 * GROUP_W), row(LANES), row(LANES), row(GROUP_W),
            row(RW_IN_W), row(GROUP_W), mat(LANES, GROUP_W), row(GROUP_W), mat(LANES, GROUP_W),
            mat(D_GATE, GROUP_W), row(GROUP_W), row(GROUP_W), row(GROUP_W), row(GROUP_W), row(GROUP_W),
        ],
        out_specs=(
            pl.BlockSpec((1, TB, 2 * GROUP_W), lambda b, t: (b, t, 0)),
            state4, state3, state3, state4,
        ),
        scratch_shapes=[
            pltpu.VMEM((1, PAIRS, LANES, LANES), F32),
            pltpu.VMEM((1, PAIRS, LANES, LANES), F32),
            pltpu.VMEM((SUBLANES, 2 * GROUP_W), F32),
            pltpu.VMEM((SUBLANES, RW_IN_W), F32),
            pltpu.VMEM((TB, GROUP_W), F32),
            pltpu.VMEM((TB, GROUP_W), F32),
            pltpu.VMEM((TB, RW_IN_W), F32),
        ],
        compiler_params=pltpu.CompilerParams(
            dimension_semantics=("arbitrary", "arbitrary"), vmem_limit_bytes=VMEM_LIMIT),
        name="mixer",
    )(proj, p["conv_w"], p["conv_b"], p["ib_row"], p["fb_row"], p["mlnorm_w"], p["mu"], p["w0"],
      p["wup_p"], p["a0"], p["aup_p"], p["gup"], p["k_k"], p["k_a"], p["r_k"], p["ln_w"], p["ln_b"])


def _inproj_t_kernel(x_ref, nw_ref, wml_ref, wgate_ref, wrw_ref, o_ref, ot_ref):
    o_ml, o_gate, o_rw = _project(x_ref, nw_ref, wml_ref, wgate_ref, wrw_ref)
    for lo_col, hi_col, o in ((0, GI0, o_ml), (GI0, RW0, o_gate), (RW0, IN_WP, o_rw)):
        o_ref[:, lo_col:hi_col] = o
        ot_ref[lo_col:hi_col, :] = o.T


def _inproj_t(x2d, norm_w, w_in_parts):
    rows = x2d.shape[0]
    return pl.pallas_call(
        _inproj_t_kernel,
        out_shape=(jax.ShapeDtypeStruct((rows, IN_WP), F32), jax.ShapeDtypeStruct((IN_WP, rows), F32)),
        compiler_params=pltpu.CompilerParams(vmem_limit_bytes=VMEM_LIMIT),
        name="inproj_t",
    )(x2d, norm_w, *w_in_parts)


(PC_W0, PC_A0, PC_MLNW, PC_KK, PC_KA, PC_RK, PC_LNW, PC_LNB) = range(8)
PC_CONVB = CONV_W


def _sample_kernel(pt_ref, conv0_ref, shift0_ref, m0_ref, n0_ref, C0_ref, S0_ref,
                   p1024_ref, p512_ref, mu_ref, gb_ref, wup_ref, aup_ref, gup_ref,
                   hml_ref, yrw_ref, C_ref, n_ref, m_ref, S_ref,
                   wk_s, q_s, vr_s, y_s):
    h = pl.program_id(0)
    hrow = lambda base, n=HEAD_DIM: pl.ds(pl.multiple_of(base + h * n, n), n)
    pcol = lambda j: p512_ref[hrow(0), j:j + 1]
    colsum = lambda t: jnp.sum(t, axis=0, keepdims=True)

    def conv_silu(base):
        rows = hrow(base)
        acc = p1024_ref[rows, PC_CONVB:PC_CONVB + 1] + pt_ref[rows, :] * p1024_ref[rows, CONV_W - 1:CONV_W]
        for jw in range(CONV_W - 1):
            acc = acc + conv0_ref[jw, rows, :] * p1024_ref[rows, jw:jw + 1]
        return acc * jax.nn.sigmoid(acc)

    q = conv_silu(Q0)
    k = conv_silu(K0) * (HEAD_DIM ** -0.5)
    v = pt_ref[hrow(V0), :]
    i_g = pt_ref[pl.ds(GI0 + h, 1), :] + gb_ref[pl.ds(h, 1), 0:1]
    logf = _log_sigmoid(pt_ref[pl.ds(GF0 + h, 1), :] + gb_ref[pl.ds(h, 1), 1:2])
    m0 = m0_ref[pl.ds(h, 1), :]
    m_new = jnp.maximum(logf + m0, i_g)
    ws = jnp.exp(i_g - m_new)
    dec = jnp.exp(logf + m0 - m_new)
    wk = ws * k
    n_new = dec * n0_ref[0] + wk
    n_ref[0] = n_new
    m_ref[pl.ds(h, 1), :] = m_new
    wk_s[...] = wk
    q_s[...] = q

    def c_body(kk_, num):
        ck = dec * C0_ref[0, kk_] + wk_s[pl.ds(kk_, 1), :] * v
        C_ref[0, kk_] = ck
        return num + q_s[pl.ds(kk_, 1), :] * ck

    num = lax.fori_loop(0, HEAD_DIM, c_body, jnp.zeros((HEAD_DIM, pt_ref.shape[1]), F32), unroll=SAMPLE_UNROLL)
    den = colsum(q * n_new)
    hml = jax.nn.sigmoid(pt_ref[hrow(O0), :]) * (num / jnp.maximum(jnp.abs(den), jnp.exp(-m_new)))
    hml_ref[...] = hml * lax.rsqrt(colsum(hml * hml) * (1.0 / HEAD_DIM) + EPS) * pcol(PC_MLNW)

    def shifted(base, n=HEAD_DIM):
        rw = pt_ref[hrow(RW0 + base, n) if n == HEAD_DIM else pl.ds(RW0 + base, n), :]
        rows = hrow(base, n) if n == HEAD_DIM else pl.ds(base, n)
        return rw + mu_ref[rows, :] * (shift0_ref[rows, :] - rw)

    r, kr, vr = shifted(RS_R), shifted(RS_K), shifted(RS_V)
    lora = shifted(RS_LORA, LANES)
    xg = shifted(RS_G, D_GATE)
    hw = hrow(0)
    w_raw = pcol(PC_W0) + jnp.dot(wup_ref[hw, :], jnp.tanh(lora).astype(BF16), preferred_element_type=F32)
    w = jnp.exp(-W_SCALE * jax.nn.sigmoid(w_raw))
    a = jax.nn.sigmoid(pcol(PC_A0) + jnp.dot(aup_ref[hw, :], lora.astype(BF16), preferred_element_type=F32))
    g = jnp.dot(gup_ref[hw, :], jax.nn.sigmoid(xg).astype(BF16), preferred_element_type=F32)
    kk = kr * pcol(PC_KK)
    kkn = kk / jnp.maximum(jnp.sqrt(colsum(kk * kk)), 1e-12)
    keff = kr * (1.0 + (a - 1.0) * pcol(PC_KA))
    nkk = -kkn
    ka = kkn * a
    vr_s[...] = vr

    def s_body(v_, carry):
        sv = S0_ref[0, v_]
        sn = sv * w + colsum(sv * nkk) * ka + vr_s[pl.ds(v_, 1), :] * keff
        S_ref[0, v_] = sn
        y_s[pl.ds(v_, 1), :] = colsum(sn * r)
        return carry

    lax.fori_loop(0, HEAD_DIM, s_body, 0, unroll=SAMPLE_UNROLL)
    yb = y_s[...] + colsum(r * keff * pcol(PC_RK)) * vr
    dlt = yb - colsum(yb) * (1.0 / HEAD_DIM)
    var = colsum(dlt * dlt) * (1.0 / HEAD_DIM)
    yrw_ref[...] = (dlt * lax.rsqrt(var + GN_EPS) * pcol(PC_LNW) + pcol(PC_LNB)) * g


def _sample(proj_t, conv0_t, shift0_t, m0_t, n0_t, c0_t, s0_t, p):
    bsz = proj_t.shape[1]
    whole = lambda a: pl.BlockSpec(a.shape, lambda h: (0,) * a.ndim)
    state4 = pl.BlockSpec((1, HEAD_DIM, HEAD_DIM, bsz), lambda h: (h, 0, 0, 0))
    head2 = pl.BlockSpec((HEAD_DIM, bsz), lambda h: (h, 0))
    head3 = pl.BlockSpec((1, HEAD_DIM, bsz), lambda h: (h, 0, 0))
    consts = [p["p1024"], p["p512"], p["mu_col"], p["gate_b"], p["wup_t"], p["aup_t"], p["gup_t"]]
    return pl.pallas_call(
        _sample_kernel,
        out_shape=(
            jax.ShapeDtypeStruct((GROUP_W, bsz), F32),
            jax.ShapeDtypeStruct((GROUP_W, bsz), F32),
            jax.ShapeDtypeStruct((N_HEADS, HEAD_DIM, HEAD_DIM, bsz), F32),
            jax.ShapeDtypeStruct((N_HEADS, HEAD_DIM, bsz), F32),
            jax.ShapeDtypeStruct((N_HEADS, bsz), F32),
            jax.ShapeDtypeStruct((N_HEADS, HEAD_DIM, HEAD_DIM, bsz), F32),
        ),
        grid=(N_HEADS,),
        in_specs=[whole(proj_t), whole(conv0_t), whole(shift0_t), whole(m0_t), head3, state4, state4]
                 + [whole(c) for c in consts],
        out_specs=(head2, head2, state4, head3, pl.BlockSpec((N_HEADS, bsz), lambda h: (0, 0)), state4),
        scratch_shapes=[pltpu.VMEM((HEAD_DIM, bsz), F32) for _ in range(4)],
        compiler_params=pltpu.CompilerParams(
            dimension_semantics=("arbitrary",), vmem_limit_bytes=VMEM_LIMIT),
        name="sample",
    )(proj_t, conv0_t, shift0_t, m0_t, n0_t, c0_t, s0_t, *consts)


def _layer_params(l, norm_mix_w, w_in, mlstm_conv_w, mlstm_conv_b, mlstm_i_b, mlstm_f_b, mlstm_norm_w,
                  rw_mu, rw_w0, rw_w_up, rw_a0, rw_a_up, rw_g_up, rw_k_k, rw_k_a, rw_r_k, rw_ln_w, rw_ln_b,
                  w_out, norm_mlp_w, mlp_up, mlp_down):
    ml_in = 4 * GROUP_W
    wi = w_in[l]
    zpad = jnp.zeros((D_MODEL, LANES - N_HEADS), BF16)
    w_in_parts = (
        wi[:, :ml_in].astype(BF16),
        jnp.concatenate([wi[:, ml_in:ml_in + N_HEADS].astype(BF16), zpad,
                         wi[:, ml_in + N_HEADS:ml_in + 2 * N_HEADS].astype(BF16), zpad], axis=1),
        wi[:, ml_in + 2 * N_HEADS:].astype(BF16),
    )
    gate_row = lambda b: jnp.concatenate([b, jnp.zeros((LANES - N_HEADS,), F32)])[None, :]
    zl = jnp.zeros((D_LORA, GROUP_W), F32)
    return {
        "norm_mix_w": norm_mix_w[l][None, :],
        "w_in_parts": w_in_parts,
        "conv_w": mlstm_conv_w[l], "conv_b": mlstm_conv_b[l][None, :],
        "ib_row": gate_row(mlstm_i_b[l]), "fb_row": gate_row(mlstm_f_b[l]),
        "gate_b": jnp.stack([mlstm_i_b[l], mlstm_f_b[l]], axis=1),
        "p1024": jnp.concatenate([mlstm_conv_w[l].T, mlstm_conv_b[l][:, None],
                                  jnp.zeros((2 * GROUP_W, SUBLANES - CONV_W - 1), F32)], axis=1),
        "p512": jnp.stack([rw_w0[l], rw_a0[l], mlstm_norm_w[l], rw_k_k[l], rw_k_a[l], rw_r_k[l].reshape(GROUP_W),
                           rw_ln_w[l], rw_ln_b[l]], axis=1),
        "mu_col": rw_mu[l][:, None],
        "mlnorm_w": mlstm_norm_w[l][None, :],
        "mu": rw_mu[l][None, :], "w0": rw_w0[l][None, :], "a0": rw_a0[l][None, :],
        "wup_p": jnp.concatenate([rw_w_up[l], zl], axis=0).astype(BF16),
        "aup_p": jnp.concatenate([zl, rw_a_up[l]], axis=0).astype(BF16),
        "gup": rw_g_up[l].astype(BF16),
        "wup_t": jnp.concatenate([rw_w_up[l], zl], axis=0).T.astype(BF16),
        "aup_t": jnp.concatenate([zl, rw_a_up[l]], axis=0).T.astype(BF16),
        "gup_t": rw_g_up[l].T.astype(BF16),
        "k_k": rw_k_k[l][None, :], "k_a": rw_k_a[l][None, :], "r_k": rw_r_k[l].reshape(1, GROUP_W),
        "ln_w": rw_ln_w[l][None, :], "ln_b": rw_ln_b[l][None, :],
        "w_out": w_out[l].astype(BF16), "norm_mlp_w": norm_mlp_w[l][None, :],
        "mlp_up": mlp_up[l].astype(BF16), "mlp_down": mlp_down[l].astype(BF16),
    }


def _prompt_layer(x, p, norm_f_w):
    bsz, seq, _ = x.shape
    x2d = x.reshape(bsz * seq, D_MODEL)
    proj = _inproj(x2d, p["norm_mix_w"], p["w_in_parts"]).reshape(bsz, seq, IN_WP)
    mix, c_st, n_st, m_st, s_st = _mixer(proj, p)
    y = _outmlp(x2d, mix.reshape(bsz * seq, D_MODEL), p["w_out"], p["norm_mlp_w"], p["mlp_up"],
                p["mlp_down"], norm_f_w)
    c_new = c_st
    n_new = n_st[:, :PAIRS, :].reshape(bsz, N_HEADS, HEAD_DIM)
    m_new = m_st[:, :, 0]
    conv_new = proj[:, seq - (CONV_W - 1):, Q0:Q0 + 2 * GROUP_W]
    s_new = s_st
    shift_new = proj[:, seq - 1:, RW0:RW0 + RW_IN_W]
    return y.reshape(bsz, seq, D_MODEL), (c_new, n_new, m_new, conv_new, s_new, shift_new)


def _sample_layer(x, c0, n0, m0, conv0, s0, shift0, p, norm_f_w):
    bsz = x.shape[0]
    x2d = x.reshape(bsz, D_MODEL)
    proj, proj_t = _inproj_t(x2d, p["norm_mix_w"], p["w_in_parts"])
    hml_t, yrw_t, c_t, n_t, m_t, s_t = _sample(
        proj_t, jnp.transpose(conv0, (1, 2, 0)), shift0[:, 0, :].T, m0.T, jnp.transpose(n0, (1, 2, 0)),
        jnp.transpose(c0, (1, 2, 3, 0)), jnp.transpose(s0, (1, 2, 3, 0)), p)
    mix = jnp.concatenate([hml_t, yrw_t], axis=0).T
    y = _outmlp(x2d, mix, p["w_out"], p["norm_mlp_w"], p["mlp_up"], p["mlp_down"], norm_f_w)
    conv_new = jnp.concatenate([conv0[:, 1:], proj[:, None, Q0:Q0 + 2 * GROUP_W]], axis=1)
    shift_new = proj[:, None, RW0:RW0 + RW_IN_W]
    return y.reshape(bsz, 1, D_MODEL), (
        jnp.transpose(c_t, (3, 0, 1, 2)), jnp.transpose(n_t, (2, 0, 1)), m_t.T, conv_new,
        jnp.transpose(s_t, (3, 0, 1, 2)), shift_new)


def kernel(x_prompt, x_sample, state_mlstm_C, state_mlstm_n, state_mlstm_m, state_mlstm_conv, state_rwkv_S, state_rwkv_shift, norm_mix_w, w_in, mlstm_conv_w, mlstm_conv_b, mlstm_i_b, mlstm_f_b, mlstm_norm_w, rw_mu, rw_w0, rw_w_up, rw_a0, rw_a_up, rw_g_up, rw_k_k, rw_k_a, rw_r_k, rw_ln_w, rw_ln_b, w_out, norm_mlp_w, mlp_up, mlp_down, norm_f_w):
    depth = w_in.shape[0]
    assert depth == 1, "the final RMSNorm is fused into the layer's output kernel"
    weights = (norm_mix_w, w_in, mlstm_conv_w, mlstm_conv_b, mlstm_i_b, mlstm_f_b, mlstm_norm_w,
               rw_mu, rw_w0, rw_w_up, rw_a0, rw_a_up, rw_g_up, rw_k_k, rw_k_a, rw_r_k, rw_ln_w, rw_ln_b,
               w_out, norm_mlp_w, mlp_up, mlp_down)
    nfw = norm_f_w[None, :]
    p = _layer_params(0, *weights)
    y_p, st_p = _prompt_layer(x_prompt, p, nfw)
    y_s, st_s = _sample_layer(x_sample, state_mlstm_C[0], state_mlstm_n[0], state_mlstm_m[0],
                              state_mlstm_conv[0], state_rwkv_S[0], state_rwkv_shift[0], p, nfw)
    lead = lambda t: tuple(a[None] for a in t)
    return (y_p, y_s) + lead(st_p) + lead(st_s)
```

```python
import itertools
import math

import jax
import jax.numpy as jnp
from jax import lax
from jax.experimental import pallas as pl
from jax.experimental.pallas import tpu as pltpu

F32 = jnp.float32
BF16 = jnp.bfloat16

D_MODEL = 1024
HEAD_DIM = 64
N_HEADS = 8
PAIRS = N_HEADS // 2
GROUP_W = N_HEADS * HEAD_DIM
CONV_W = 4
D_LORA = 64
D_GATE = 128
D_FF = 4 * D_MODEL
EPS = 1e-6
GN_EPS = 64e-5
LANES = 128
SUBLANES = 8

Q0, K0, V0, O0 = 0, GROUP_W, 2 * GROUP_W, 3 * GROUP_W
GI0 = 4 * GROUP_W
GF0 = GI0 + LANES
RW0 = GF0 + LANES
RW_IN_W = 3 * GROUP_W + 2 * D_LORA + D_GATE
IN_WP = RW0 + RW_IN_W
RS_R, RS_K, RS_V = 0, GROUP_W, 2 * GROUP_W
RS_LORA = 3 * GROUP_W
RS_G = RS_LORA + 2 * D_LORA

TB = 256
LM = 128
LR = 64
RW_GROUP = 2
ML_EVERY = 2
_DONE = object()
_STATE = object()
TM_IN = 1024
TM_OUT = 1024
FF_CHUNK = 1024
SAMPLE_UNROLL = 8
NEG = -1e30
W_SCALE = math.exp(-0.5)
VMEM_LIMIT = 56 * 1024 * 1024


def _dot(a, b):
    return jnp.dot(a.astype(BF16), b.astype(BF16), preferred_element_type=F32)


def _split3(x):
    h1 = x.astype(BF16)
    r1 = x - h1.astype(F32)
    h2 = r1.astype(BF16)
    h3 = (r1 - h2.astype(F32)).astype(BF16)
    return h1, h2, h3


def _dot01_l(m01, x):
    h1, h2, h3 = _split3(x)
    d = lambda h: jnp.dot(m01, h, preferred_element_type=F32)
    return d(h1) + d(h2) + d(h3)


def _log_sigmoid(x):
    return jnp.minimum(x, 0.0) - jnp.log(1.0 + jnp.exp(-jnp.abs(x)))


def _tanh(x):
    e = jnp.exp(-2.0 * jnp.abs(x))
    t = (1.0 - e) / (1.0 + e)
    return jnp.where(x < 0.0, -t, t)


def _iota(shape, dim):
    return lax.broadcasted_iota(jnp.int32, shape, dim)


def _rowsum(x):
    return jnp.sum(x, axis=1, keepdims=True)


def _pair_rowsum(x, lo):
    s_lo = _rowsum(jnp.where(lo, x, 0.0))
    s_hi = _rowsum(jnp.where(lo, 0.0, x))
    return jnp.where(lo, s_lo, s_hi)


def _stack2(x, lo):
    zero = jnp.zeros_like(x)
    return jnp.concatenate([jnp.where(lo, x, zero), jnp.where(lo, zero, x)], axis=0)


def _rms(x, w):
    ms = jnp.mean(x * x, axis=-1, keepdims=True)
    return x * lax.rsqrt(ms + EPS) * w


def _project(x_ref, nw_ref, wml_ref, wgate_ref, wrw_ref):
    xn = _rms(x_ref[...], nw_ref[...]).astype(BF16)
    return [jnp.dot(xn, w[...], preferred_element_type=F32) for w in (wml_ref, wgate_ref, wrw_ref)]


def _inproj_kernel(x_ref, nw_ref, wml_ref, wgate_ref, wrw_ref, o_ref):
    o_ml, o_gate, o_rw = _project(x_ref, nw_ref, wml_ref, wgate_ref, wrw_ref)
    o_ref[:, 0:GI0] = o_ml
    o_ref[:, GI0:RW0] = o_gate
    o_ref[:, RW0:IN_WP] = o_rw


def _inproj(x2d, norm_w, w_in_parts):
    rows = x2d.shape[0]
    tm = min(TM_IN, rows)
    return pl.pallas_call(
        _inproj_kernel,
        out_shape=jax.ShapeDtypeStruct((rows, IN_WP), F32),
        grid=(rows // tm,),
        in_specs=[
            pl.BlockSpec((tm, D_MODEL), lambda i: (i, 0)),
            pl.BlockSpec((1, D_MODEL), lambda i: (0, 0)),
        ] + [pl.BlockSpec(w.shape, lambda i: (0, 0), pipeline_mode=pl.Buffered(1)) for w in w_in_parts],
        out_specs=pl.BlockSpec((tm, IN_WP), lambda i: (i, 0)),
        compiler_params=pltpu.CompilerParams(
            dimension_semantics=("arbitrary",), vmem_limit_bytes=VMEM_LIMIT),
        name="inproj",
    )(x2d, norm_w, *w_in_parts)


def _outmlp_kernel(x_ref, mix_ref, wout_ref, nmw_ref, up_ref, down_ref, nfw_ref, y_ref):
    x1 = x_ref[...] + jnp.dot(mix_ref[...].astype(BF16), wout_ref[...], preferred_element_type=F32)
    xn = _rms(x1, nmw_ref[...]).astype(BF16)
    acc = x1
    for c in range(D_FF // FF_CHUNK):
        sl = slice(c * FF_CHUNK, (c + 1) * FF_CHUNK)
        hid = jnp.maximum(jnp.dot(xn, up_ref[:, sl], preferred_element_type=F32), 0.0)
        acc = acc + jnp.dot((hid * hid).astype(BF16), down_ref[sl, :], preferred_element_type=F32)
    y_ref[...] = _rms(acc, nfw_ref[...])


def _outmlp(x2d, mix2d, w_out, norm_mlp_w, mlp_up, mlp_down, norm_f_w):
    rows = x2d.shape[0]
    tm = min(TM_OUT, rows)
    const = lambda shape: pl.BlockSpec(shape, lambda i: (0, 0), pipeline_mode=pl.Buffered(1))
    return pl.pallas_call(
        _outmlp_kernel,
        out_shape=jax.ShapeDtypeStruct((rows, D_MODEL), F32),
        grid=(rows // tm,),
        in_specs=[
            pl.BlockSpec((tm, D_MODEL), lambda i: (i, 0)),
            pl.BlockSpec((tm, D_MODEL), lambda i: (i, 0)),
            const((D_MODEL, D_MODEL)),
            const((1, D_MODEL)),
            const((D_MODEL, D_FF)),
            const((D_FF, D_MODEL)),
            const((1, D_MODEL)),
        ],
        out_specs=pl.BlockSpec((tm, D_MODEL), lambda i: (i, 0)),
        compiler_params=pltpu.CompilerParams(
            dimension_semantics=("arbitrary",), vmem_limit_bytes=VMEM_LIMIT),
        name="outmlp",
    )(x2d, mix2d, w_out, norm_mlp_w, mlp_up, mlp_down, norm_f_w)


def _rwkv_token_prep(lora_t, xg, w0, wup, a0, aup, gup):
    w_raw = w0 + jnp.dot(_tanh(lora_t).astype(BF16), wup, preferred_element_type=F32)
    wlog = -W_SCALE * jax.nn.sigmoid(w_raw)
    a = jax.nn.sigmoid(a0 + jnp.dot(lora_t.astype(BF16), aup, preferred_element_type=F32))
    g = jnp.dot(jax.nn.sigmoid(xg).astype(BF16), gup, preferred_element_type=F32)
    return wlog, a, g


def _mlstm_chunk(c, proj_ref, q_ref, k_ref, ib_ref, fb_ref, mlnw_ref, mix_ref, C_ref, n_ref, m_ref):
    rows = pl.ds(pl.multiple_of(c * LM, LM), LM)
    row = _iota((LM, LM), 0)
    col = _iota((LM, LM), 1)
    causal = col <= row
    tril = jnp.where(causal, 1.0, 0.0).astype(BF16)
    blockdiag = (row < HEAD_DIM) == (col < HEAD_DIM)
    lane = _iota((1, LANES), 1)
    lo = lane < HEAD_DIM

    i_tile = proj_ref[0, rows, GI0:GI0 + LANES] + ib_ref[...]
    f_tile = _log_sigmoid(proj_ref[0, rows, GF0:GF0 + LANES] + fb_ref[...])
    bc = _dot01_l(tril, f_tile)
    u = i_tile - bc
    ut = u.T
    yield

    nt = (((1,), (1,)), ((), ()))
    pairs = range(PAIRS)
    heads = range(N_HEADS)
    tile = lambda base, j: slice(base + j * LANES, base + (j + 1) * LANES)
    hmask = lambda h: lo if h % 2 == 0 else jnp.logical_not(lo)
    q_t = [q_ref[rows, tile(0, j)] for j in pairs]
    k_t = [k_ref[rows, tile(0, j)] for j in pairs]
    v_t = [proj_ref[0, rows, tile(V0, j)] for j in pairs]
    c_pair = [C_ref[0, j] for j in pairs]
    n_row = [n_ref[0, j:j + 1, :] for j in pairs]
    k_bf = [k.astype(BF16) for k in k_t]
    v_bf = [v.astype(BF16) for v in v_t]
    s = [lax.dot_general(jnp.where(hmask(h), q_t[h // 2], 0.0).astype(BF16), k_bf[h // 2], nt,
                         preferred_element_type=F32) for h in heads]
    q_c = [_dot(q_t[j], c_pair[j]) for j in pairs]
    k_tr = [k.T for k in k_t]
    yield

    m_prev_t = m_ref[0]
    ut8 = ut[0:N_HEADS, :]
    m_last_t = jnp.maximum(jnp.max(ut8, axis=1, keepdims=True), m_prev_t)
    ws_rows = jnp.exp(ut8 - m_last_t)
    dec_t = jnp.exp(m_prev_t - m_last_t)
    m_ref[0] = bc.T[0:N_HEADS, LM - 1:LM] + m_last_t

    m_prev = [m_prev_t[h:h + 1, 0:1] for h in heads]
    u_row = [ut[h:h + 1, :] for h in heads]
    bc_col = [_rowsum(jnp.where(lane == h, bc, 0.0)) for h in heads]
    m_col = [jnp.maximum(jnp.max(jnp.where(causal, u_row[h], NEG), axis=1, keepdims=True), m_prev[h])
             for h in heads]
    p = [s[h] * jnp.exp(jnp.where(causal, u_row[h] - m_col[h], NEG)) for h in heads]
    sc = [jnp.exp(m_prev[h] - m_col[h]) for h in heads]
    yield
    pv = [jnp.dot(p[h].astype(BF16), v_bf[h // 2], preferred_element_type=F32) for h in heads]
    yield
    h_parts = []
    for h in heads:
        q_n = _rowsum(jnp.where(hmask(h), q_t[h // 2] * n_row[h // 2], 0.0))
        den = _rowsum(p[h]) + sc[h] * q_n
        num = pv[h] + sc[h] * q_c[h // 2]
        h_parts.append(num / jnp.maximum(jnp.abs(den), jnp.exp(-(bc_col[h] + m_col[h]))))

    top = _iota((LM, 1), 0) < HEAD_DIM
    kw = [k_tr[j] * jnp.where(top, ws_rows[2 * j:2 * j + 1, :], ws_rows[2 * j + 1:2 * j + 2, :])
          for j in pairs]
    upd = [jnp.dot(kw[j].astype(BF16), v_bf[j], preferred_element_type=F32) for j in pairs]
    n_upd = [jnp.dot(ws_rows.astype(BF16), k_bf[j], preferred_element_type=F32) for j in pairs]
    yield
    for j in pairs:
        dec_j = jnp.where(lo, dec_t[2 * j:2 * j + 1, :], dec_t[2 * j + 1:2 * j + 2, :])
        C_ref[0, j] = c_pair[j] * dec_j + jnp.where(blockdiag, upd[j], 0.0)
        n_ref[0, j:j + 1, :] = n_row[j] * dec_j + jnp.where(lo, n_upd[j][2 * j:2 * j + 1, :],
                                                            n_upd[j][2 * j + 1:2 * j + 2, :])

    for j in pairs:
        h_til = jnp.where(lo, h_parts[2 * j], h_parts[2 * j + 1])
        hml = jax.nn.sigmoid(proj_ref[0, rows, tile(O0, j)]) * h_til
        ms = _pair_rowsum(hml * hml, lo) * (1.0 / HEAD_DIM)
        mix_ref[0, rows, tile(0, j)] = hml * lax.rsqrt(ms + EPS) * mlnw_ref[:, tile(0, j)]


def _neumann_inverse(ns):
    shape = ns[0].shape
    eye = jnp.where(_iota(shape, 0) == _iota(shape, 1), 1.0, 0.0)
    n_sq = int(math.log2(LR)) - 1
    qs = [n.astype(BF16) for n in ns]
    ts = [eye + n for n in ns]
    sq = [jnp.dot(q, q, preferred_element_type=F32) for q in qs]
    yield
    for _ in range(n_sq - 1):
        qs = [s.astype(BF16) for s in sq]
        both = [jnp.dot(q, jnp.concatenate([q, t.astype(BF16)], axis=1), preferred_element_type=F32)
                for q, t in zip(qs, ts)]
        sq = [b[:, 0:LANES] for b in both]
        ts = [t + b[:, LANES:2 * LANES] for t, b in zip(ts, both)]
        yield
    return [(t + jnp.dot(s.astype(BF16), t.astype(BF16), preferred_element_type=F32)).astype(BF16)
            for s, t in zip(sq, ts)]


def _rwkv_chunks(c, rs_ref, w0_ref, wup_ref, a0_ref, aup_ref, gup_ref, kk_ref, ka_ref, rk_ref,
                 lnw_ref, lnb_ref, mix_ref, S_ref):
    gl = RW_GROUP * LR
    rows = pl.ds(pl.multiple_of(c * gl, gl), gl)
    lane = _iota((1, LANES), 1)
    lo = lane < HEAD_DIM
    row = _iota((LANES, LANES), 0)
    col = _iota((LANES, LANES), 1)
    same = (row < LR) == (col < LR)
    mask_sl = jnp.logical_and(same, col < row)
    mask_li = jnp.logical_and(same, col <= row)
    mask_li2 = jnp.concatenate([mask_li, mask_li], axis=1)
    grow = _iota((gl, gl), 0)
    gcol = _iota((gl, gl), 1)
    tril = jnp.where(jnp.logical_and(gcol <= grow, gcol >= (grow // LR) * LR), 1.0, 0.0).astype(BF16)
    nt = (((1,), (1,)), ((), ()))
    pairs = range(PAIRS)
    items = [(ci, j) for ci in range(RW_GROUP) for j in pairs]
    tile = lambda base, j: slice(base + j * LANES, base + (j + 1) * LANES)
    crow = lambda ci: slice(ci * LR, (ci + 1) * LR)

    wlog, a, g = _rwkv_token_prep(rs_ref[rows, RS_LORA:RS_LORA + LANES], rs_ref[rows, RS_G:RS_G + D_GATE],
                                  w0_ref[...], wup_ref[...], a0_ref[...], aup_ref[...], gup_ref[...])
    lw = _dot01_l(tril, wlog)
    e_in = jnp.exp(lw)
    e_ex = jnp.exp(lw - wlog)
    e_ng = jnp.exp(-lw)

    r_g = [rs_ref[rows, tile(RS_R, j)] for j in pairs]
    kr_g = [rs_ref[rows, tile(RS_K, j)] for j in pairs]
    v_g = [rs_ref[rows, tile(RS_V, j)] for j in pairs]
    a_g = [a[:, tile(0, j)] for j in pairs]
    kk = [kr_g[j] * kk_ref[:, tile(0, j)] for j in pairs]
    kkn = [kk[j] * lax.rsqrt(jnp.maximum(_pair_rowsum(kk[j] * kk[j], lo), 1e-24)) for j in pairs]
    keff_g = [kr_g[j] * (1.0 + (a_g[j] - 1.0) * ka_ref[:, tile(0, j)]) for j in pairs]
    alpha_g = [-kkn[j] * e_ex[:, tile(0, j)] for j in pairs]
    rb_g = [r_g[j] * e_in[:, tile(0, j)] for j in pairs]
    beta_g = [kkn[j] * a_g[j] * e_ng[:, tile(0, j)] for j in pairs]
    ktl_g = [keff_g[j] * e_ng[:, tile(0, j)] for j in pairs]
    bonus_g = [_pair_rowsum(r_g[j] * keff_g[j] * rk_ref[:, tile(0, j)], lo) * v_g[j] for j in pairs]

    lhs, rhs, v2, bk, gamma, ar = {}, {}, {}, {}, {}, {}
    for it in items:
        ci, j = it
        cr = crow(ci)
        ar[it] = jnp.concatenate([alpha_g[j][cr], rb_g[j][cr]], axis=0).astype(BF16)
        lhs[it] = jnp.concatenate([_stack2(ar[it][0:LR], lo), _stack2(ar[it][LR:2 * LR], lo)], axis=0)
        beta_c, ktl_c = beta_g[j][cr].astype(BF16), ktl_g[j][cr].astype(BF16)
        rhs[it] = jnp.concatenate([beta_c, beta_c, ktl_c, ktl_c], axis=0)
        v2[it] = _stack2(v_g[j][cr].astype(BF16), lo)
        bk[it] = jnp.concatenate([_stack2(beta_c, lo), _stack2(ktl_c, lo)], axis=0)
        gamma[it] = e_in[(ci + 1) * LR - 1:(ci + 1) * LR, tile(0, j)]
    aa = {it: lax.dot_general(lhs[it], rhs[it], nt, preferred_element_type=F32) for it in items}
    yield
    a_ab = [jnp.where(mask_sl, aa[it][0:LANES, 0:LANES], 0.0) for it in items]
    a_ak = {it: jnp.where(mask_sl, aa[it][0:LANES, LANES:2 * LANES], 0.0).astype(BF16) for it in items}
    a_r = {it: jnp.where(mask_li2, aa[it][LANES:2 * LANES, :], 0.0).astype(BF16) for it in items}
    akv = {it: jnp.dot(a_ak[it], v2[it], preferred_element_type=F32) for it in items}
    yield
    t_list = yield from _neumann_inverse(a_ab)
    t_inv = dict(zip(items, t_list))
    yield _STATE

    s_pair = [S_ref[0, j] for j in pairs]
    for ci in range(RW_GROUP):
        its = [(ci, j) for j in pairs]
        xs = [lax.dot_general(ar[it], s_pair[it[1]].astype(BF16), nt, preferred_element_type=F32)
              for it in its]
        yield
        p2 = [_dot(t_inv[it], _stack2(xs[j][0:LR], lo) + akv[it]) for j, it in enumerate(its)]
        yield
        pv = [jnp.concatenate([p2[j].astype(BF16), v2[it]], axis=0) for j, it in enumerate(its)]
        y2 = [_dot(a_r[it], pv[j]) for j, it in enumerate(its)]
        upd = [lax.dot_general(pv[j], bk[it], (((0,), (0,)), ((), ())), preferred_element_type=F32)
               for j, it in enumerate(its)]
        s_pair = [(s_pair[j] + upd[j]) * gamma[it] for j, it in enumerate(its)]
        yield
        cr = crow(ci)
        for j in pairs:
            yb = xs[j][LR:2 * LR] + y2[j][0:LR] + y2[j][LR:2 * LR] + bonus_g[j][cr]
            mu = _pair_rowsum(yb, lo) * (1.0 / HEAD_DIM)
            dlt = yb - mu
            var = _pair_rowsum(dlt * dlt, lo) * (1.0 / HEAD_DIM)
            yn = dlt * lax.rsqrt(var + GN_EPS) * lnw_ref[:, tile(0, j)] + lnb_ref[:, tile(0, j)]
            out_rows = pl.ds(pl.multiple_of(c * gl + ci * LR, LR), LR)
            mix_ref[0, out_rows, tile(GROUP_W, j)] = yn * g[cr, tile(0, j)]
    for j in pairs:
        S_ref[0, j] = s_pair[j]


def _shift_rows(x, tail, j):
    rolled = pltpu.roll(x, j, axis=0)
    head = jnp.where(_iota((SUBLANES, 1), 0) < j, pltpu.roll(tail, j, axis=0), rolled[0:SUBLANES])
    return jnp.concatenate([head, rolled[SUBLANES:]], axis=0)


def _until_state(gen):
    for tok in gen:
        if tok is _STATE:
            return
        yield


def _drain(gen):
    for _ in gen:
        yield


def _interleave(primary, secondary, every):
    done_p = done_s = False
    k = 0
    while not (done_p and done_s):
        if not done_p:
            done_p = next(primary, _DONE) is _DONE
        k += 1
        if not done_s and (done_p or k % every == 0):
            done_s = next(secondary, _DONE) is _DONE


def _mixer_kernel(proj_ref, convw_ref, convb_ref, ib_ref, fb_ref, mlnw_ref, mu_ref, w0_ref, wup_ref,
                  a0_ref, aup_ref, gup_ref, kk_ref, ka_ref, rk_ref, lnw_ref, lnb_ref,
                  mix_ref, Cout_ref, n_ref, m_ref, Sout_ref,
                  C_ref, S_ref, qk_tail_ref, rw_tail_ref, q_ref, k_ref, rs_ref):
    t = pl.program_id(1)

    @pl.when(t == 0)
    def _():
        C_ref[...] = jnp.zeros_like(C_ref)
        n_ref[...] = jnp.zeros_like(n_ref)
        m_ref[...] = jnp.zeros_like(m_ref)
        S_ref[...] = jnp.zeros_like(S_ref)
        qk_tail_ref[...] = jnp.zeros_like(qk_tail_ref)
        rw_tail_ref[...] = jnp.zeros_like(rw_tail_ref)

    u = proj_ref[0, :, Q0:Q0 + 2 * GROUP_W]
    qk_tail = qk_tail_ref[...]
    acc = convb_ref[...] + u * convw_ref[CONV_W - 1:CONV_W, :]
    for jw in range(CONV_W - 1):
        acc = acc + _shift_rows(u, qk_tail, CONV_W - 1 - jw) * convw_ref[jw:jw + 1, :]
    qk_tail_ref[...] = u[TB - SUBLANES:TB]
    qk = acc * jax.nn.sigmoid(acc)
    q_ref[...] = qk[:, 0:GROUP_W]
    k_ref[...] = qk[:, GROUP_W:2 * GROUP_W] * (HEAD_DIM ** -0.5)

    rw = proj_ref[0, :, RW0:RW0 + RW_IN_W]
    rs_ref[...] = rw + mu_ref[...] * (_shift_rows(rw, rw_tail_ref[...], 1) - rw)
    rw_tail_ref[...] = rw[TB - SUBLANES:TB]

    assert RW_GROUP * LR == LM
    rwkv = lambda c: _rwkv_chunks(c, rs_ref, w0_ref, wup_ref, a0_ref, aup_ref, gup_ref, kk_ref, ka_ref, rk_ref,
                                  lnw_ref, lnb_ref, mix_ref, S_ref)
    mlstm = lambda c: _mlstm_chunk(c, proj_ref, q_ref, k_ref, ib_ref, fb_ref, mlnw_ref, mix_ref, C_ref, n_ref, m_ref)
    n_groups = TB // LM
    gens = [rwkv(c) for c in range(n_groups)]
    _interleave(_until_state(gens[0]), mlstm(0), every=ML_EVERY)
    for c in range(1, n_groups):
        _interleave(_until_state(gens[c]), _drain(gens[c - 1]), every=1)
    _interleave(_drain(gens[-1]), itertools.chain(*[mlstm(c) for c in range(1, n_groups)]), every=1)

    @pl.when(t == pl.num_programs(1) - 1)
    def _():
        for h in range(N_HEADS):
            blk = slice((h % 2) * HEAD_DIM, (h % 2 + 1) * HEAD_DIM)
            Cout_ref[0, h] = C_ref[0, h // 2, blk, blk]
            Sout_ref[0, h] = S_ref[0, h // 2, blk, blk]


def _mixer(proj, p):
    bsz, seq, _ = proj.shape
    row = lambda w: pl.BlockSpec((1, w), lambda b, t: (0, 0))
    mat = lambda r, w: pl.BlockSpec((r, w), lambda b, t: (0, 0))
    state4 = pl.BlockSpec((1, N_HEADS, HEAD_DIM, HEAD_DIM), lambda b, t: (b, 0, 0, 0))
    state3 = pl.BlockSpec((1, SUBLANES, LANES), lambda b, t: (b, 0, 0))
    return pl.pallas_call(
        _mixer_kernel,
        out_shape=(
            jax.ShapeDtypeStruct((bsz, seq, 2 * GROUP_W), F32),
            jax.ShapeDtypeStruct((bsz, N_HEADS, HEAD_DIM, HEAD_DIM), F32),
            jax.ShapeDtypeStruct((bsz, SUBLANES, LANES), F32),
            jax.ShapeDtypeStruct((bsz, SUBLANES, LANES), F32),
            jax.ShapeDtypeStruct((bsz, N_HEADS, HEAD_DIM, HEAD_DIM), F32),
        ),
        grid=(bsz, seq // TB),
        in_specs=[
            pl.BlockSpec((1, TB, IN_WP), lambda b, t: (b, t, 0)),
            mat(CONV_W, 2 * GROUP_W), row(2 * GROUP_W), row(LANES), row(LANES), row(GROUP_W),
            row(RW_IN_W), row(GROUP_W), mat(LANES, GROUP_W), row(GROUP_W), mat(LANES, GROUP_W),
            mat(D_GATE, GROUP_W), row(GROUP_W), row(GROUP_W), row(GROUP_W), row(GROUP_W), row(GROUP_W),
        ],
        out_specs=(
            pl.BlockSpec((1, TB, 2 * GROUP_W), lambda b, t: (b, t, 0)),
            state4, state3, state3, state4,
        ),
        scratch_shapes=[
            pltpu.VMEM((1, PAIRS, LANES, LANES), F32),
            pltpu.VMEM((1, PAIRS, LANES, LANES), F32),
            pltpu.VMEM((SUBLANES, 2 * GROUP_W), F32),
            pltpu.VMEM((SUBLANES, RW_IN_W), F32),
            pltpu.VMEM((TB, GROUP_W), F32),
            pltpu.VMEM((TB, GROUP_W), F32),
            pltpu.VMEM((TB, RW_IN_W), F32),
        ],
        compiler_params=pltpu.CompilerParams(
            dimension_semantics=("arbitrary", "arbitrary"), vmem_limit_bytes=VMEM_LIMIT),
        name="mixer",
    )(proj, p["conv_w"], p["conv_b"], p["ib_row"], p["fb_row"], p["mlnorm_w"], p["mu"], p["w0"],
      p["wup_p"], p["a0"], p["aup_p"], p["gup"], p["k_k"], p["k_a"], p["r_k"], p["ln_w"], p["ln_b"])


def _inproj_t_kernel(x_ref, nw_ref, wml_ref, wgate_ref, wrw_ref, o_ref, ot_ref):
    o_ml, o_gate, o_rw = _project(x_ref, nw_ref, wml_ref, wgate_ref, wrw_ref)
    for lo_col, hi_col, o in ((0, GI0, o_ml), (GI0, RW0, o_gate), (RW0, IN_WP, o_rw)):
        o_ref[:, lo_col:hi_col] = o
        ot_ref[lo_col:hi_col, :] = o.T


def _inproj_t(x2d, norm_w, w_in_parts):
    rows = x2d.shape[0]
    return pl.pallas_call(
        _inproj_t_kernel,
        out_shape=(jax.ShapeDtypeStruct((rows, IN_WP), F32), jax.ShapeDtypeStruct((IN_WP, rows), F32)),
        compiler_params=pltpu.CompilerParams(vmem_limit_bytes=VMEM_LIMIT),
        name="inproj_t",
    )(x2d, norm_w, *w_in_parts)


(PC_W0, PC_A0, PC_MLNW, PC_KK, PC_KA, PC_RK, PC_LNW, PC_LNB) = range(8)
PC_CONVB = CONV_W


def _sample_kernel(pt_ref, conv0_ref, shift0_ref, m0_ref, n0_ref, C0_ref, S0_ref,
                   p1024_ref, p512_ref, mu_ref, gb_ref, wup_ref, aup_ref, gup_ref,
                   hml_ref, yrw_ref, C_ref, n_ref, m_ref, S_ref,
                   wk_s, q_s, vr_s, y_s):
    h = pl.program_id(0)
    hrow = lambda base, n=HEAD_DIM: pl.ds(pl.multiple_of(base + h * n, n), n)
    pcol = lambda j: p512_ref[hrow(0), j:j + 1]
    colsum = lambda t: jnp.sum(t, axis=0, keepdims=True)

    def conv_silu(base):
        rows = hrow(base)
        acc = p1024_ref[rows, PC_CONVB:PC_CONVB + 1] + pt_ref[rows, :] * p1024_ref[rows, CONV_W - 1:CONV_W]
        for jw in range(CONV_W - 1):
            acc = acc + conv0_ref[jw, rows, :] * p1024_ref[rows, jw:jw + 1]
        return acc * jax.nn.sigmoid(acc)

    q = conv_silu(Q0)
    k = conv_silu(K0) * (HEAD_DIM ** -0.5)
    v = pt_ref[hrow(V0), :]
    i_g = pt_ref[pl.ds(GI0 + h, 1), :] + gb_ref[pl.ds(h, 1), 0:1]
    logf = _log_sigmoid(pt_ref[pl.ds(GF0 + h, 1), :] + gb_ref[pl.ds(h, 1), 1:2])
    m0 = m0_ref[pl.ds(h, 1), :]
    m_new = jnp.maximum(logf + m0, i_g)
    ws = jnp.exp(i_g - m_new)
    dec = jnp.exp(logf + m0 - m_new)
    wk = ws * k
    n_new = dec * n0_ref[0] + wk
    n_ref[0] = n_new
    m_ref[pl.ds(h, 1), :] = m_new
    wk_s[...] = wk
    q_s[...] = q

    def c_body(kk_, num):
        ck = dec * C0_ref[0, kk_] + wk_s[pl.ds(kk_, 1), :] * v
        C_ref[0, kk_] = ck
        return num + q_s[pl.ds(kk_, 1), :] * ck

    num = lax.fori_loop(0, HEAD_DIM, c_body, jnp.zeros((HEAD_DIM, pt_ref.shape[1]), F32), unroll=SAMPLE_UNROLL)
    den = colsum(q * n_new)
    hml = jax.nn.sigmoid(pt_ref[hrow(O0), :]) * (num / jnp.maximum(jnp.abs(den), jnp.exp(-m_new)))
    hml_ref[...] = hml * lax.rsqrt(colsum(hml * hml) * (1.0 / HEAD_DIM) + EPS) * pcol(PC_MLNW)

    def shifted(base, n=HEAD_DIM):
        rw = pt_ref[hrow(RW0 + base, n) if n == HEAD_DIM else pl.ds(RW0 + base, n), :]
        rows = hrow(base, n) if n == HEAD_DIM else pl.ds(base, n)
        return rw + mu_ref[rows, :] * (shift0_ref[rows, :] - rw)

    r, kr, vr = shifted(RS_R), shifted(RS_K), shifted(RS_V)
    lora = shifted(RS_LORA, LANES)
    xg = shifted(RS_G, D_GATE)
    hw = hrow(0)
    w_raw = pcol(PC_W0) + jnp.dot(wup_ref[hw, :], jnp.tanh(lora).astype(BF16), preferred_element_type=F32)
    w = jnp.exp(-W_SCALE * jax.nn.sigmoid(w_raw))
    a = jax.nn.sigmoid(pcol(PC_A0) + jnp.dot(aup_ref[hw, :], lora.astype(BF16), preferred_element_type=F32))
    g = jnp.dot(gup_ref[hw, :], jax.nn.sigmoid(xg).astype(BF16), preferred_element_type=F32)
    kk = kr * pcol(PC_KK)
    kkn = kk / jnp.maximum(jnp.sqrt(colsum(kk * kk)), 1e-12)
    keff = kr * (1.0 + (a - 1.0) * pcol(PC_KA))
    nkk = -kkn
    ka = kkn * a
    vr_s[...] = vr

    def s_body(v_, carry):
        sv = S0_ref[0, v_]
        sn = sv * w + colsum(sv * nkk) * ka + vr_s[pl.ds(v_, 1), :] * keff
        S_ref[0, v_] = sn
        y_s[pl.ds(v_, 1), :] = colsum(sn * r)
        return carry

    lax.fori_loop(0, HEAD_DIM, s_body, 0, unroll=SAMPLE_UNROLL)
    yb = y_s[...] + colsum(r * keff * pcol(PC_RK)) * vr
    dlt = yb - colsum(yb) * (1.0 / HEAD_DIM)
    var = colsum(dlt * dlt) * (1.0 / HEAD_DIM)
    yrw_ref[...] = (dlt * lax.rsqrt(var + GN_EPS) * pcol(PC_LNW) + pcol(PC_LNB)) * g


def _sample(proj_t, conv0_t, shift0_t, m0_t, n0_t, c0_t, s0_t, p):
    bsz = proj_t.shape[1]
    whole = lambda a: pl.BlockSpec(a.shape, lambda h: (0,) * a.ndim)
    state4 = pl.BlockSpec((1, HEAD_DIM, HEAD_DIM, bsz), lambda h: (h, 0, 0, 0))
    head2 = pl.BlockSpec((HEAD_DIM, bsz), lambda h: (h, 0))
    head3 = pl.BlockSpec((1, HEAD_DIM, bsz), lambda h: (h, 0, 0))
    consts = [p["p1024"], p["p512"], p["mu_col"], p["gate_b"], p["wup_t"], p["aup_t"], p["gup_t"]]
    return pl.pallas_call(
        _sample_kernel,
        out_shape=(
            jax.ShapeDtypeStruct((GROUP_W, bsz), F32),
            jax.ShapeDtypeStruct((GROUP_W, bsz), F32),
            jax.ShapeDtypeStruct((N_HEADS, HEAD_DIM, HEAD_DIM, bsz), F32),
            jax.ShapeDtypeStruct((N_HEADS, HEAD_DIM, bsz), F32),
            jax.ShapeDtypeStruct((N_HEADS, bsz), F32),
            jax.ShapeDtypeStruct((N_HEADS, HEAD_DIM, HEAD_DIM, bsz), F32),
        ),
        grid=(N_HEADS,),
        in_specs=[whole(proj_t), whole(conv0_t), whole(shift0_t), whole(m0_t), head3, state4, state4]
                 + [whole(c) for c in consts],
        out_specs=(head2, head2, state4, head3, pl.BlockSpec((N_HEADS, bsz), lambda h: (0, 0)), state4),
        scratch_shapes=[pltpu.VMEM((HEAD_DIM, bsz), F32) for _ in range(4)],
        compiler_params=pltpu.CompilerParams(
            dimension_semantics=("arbitrary",), vmem_limit_bytes=VMEM_LIMIT),
        name="sample",
    )(proj_t, conv0_t, shift0_t, m0_t, n0_t, c0_t, s0_t, *consts)


def _layer_params(l, norm_mix_w, w_in, mlstm_conv_w, mlstm_conv_b, mlstm_i_b, mlstm_f_b, mlstm_norm_w,
                  rw_mu, rw_w0, rw_w_up, rw_a0, rw_a_up, rw_g_up, rw_k_k, rw_k_a, rw_r_k, rw_ln_w, rw_ln_b,
                  w_out, norm_mlp_w, mlp_up, mlp_down):
    ml_in = 4 * GROUP_W
    wi = w_in[l]
    zpad = jnp.zeros((D_MODEL, LANES - N_HEADS), BF16)
    w_in_parts = (
        wi[:, :ml_in].astype(BF16),
        jnp.concatenate([wi[:, ml_in:ml_in + N_HEADS].astype(BF16), zpad,
                         wi[:, ml_in + N_HEADS:ml_in + 2 * N_HEADS].astype(BF16), zpad], axis=1),
        wi[:, ml_in + 2 * N_HEADS:].astype(BF16),
    )
    gate_row = lambda b: jnp.concatenate([b, jnp.zeros((LANES - N_HEADS,), F32)])[None, :]
    zl = jnp.zeros((D_LORA, GROUP_W), F32)
    return {
        "norm_mix_w": norm_mix_w[l][None, :],
        "w_in_parts": w_in_parts,
        "conv_w": mlstm_conv_w[l], "conv_b": mlstm_conv_b[l][None, :],
        "ib_row": gate_row(mlstm_i_b[l]), "fb_row": gate_row(mlstm_f_b[l]),
        "gate_b": jnp.stack([mlstm_i_b[l], mlstm_f_b[l]], axis=1),
        "p1024": jnp.concatenate([mlstm_conv_w[l].T, mlstm_conv_b[l][:, None],
                                  jnp.zeros((2 * GROUP_W, SUBLANES - CONV_W - 1), F32)], axis=1),
        "p512": jnp.stack([rw_w0[l], rw_a0[l], mlstm_norm_w[l], rw_k_k[l], rw_k_a[l], rw_r_k[l].reshape(GROUP_W),
                           rw_ln_w[l], rw_ln_b[l]], axis=1),
        "mu_col": rw_mu[l][:, None],
        "mlnorm_w": mlstm_norm_w[l][None, :],
        "mu": rw_mu[l][None, :], "w0": rw_w0[l][None, :], "a0": rw_a0[l][None, :],
        "wup_p": jnp.concatenate([rw_w_up[l], zl], axis=0).astype(BF16),
        "aup_p": jnp.concatenate([zl, rw_a_up[l]], axis=0).astype(BF16),
        "gup": rw_g_up[l].astype(BF16),
        "wup_t": jnp.concatenate([rw_w_up[l], zl], axis=0).T.astype(BF16),
        "aup_t": jnp.concatenate([zl, rw_a_up[l]], axis=0).T.astype(BF16),
        "gup_t": rw_g_up[l].T.astype(BF16),
        "k_k": rw_k_k[l][None, :], "k_a": rw_k_a[l][None, :], "r_k": rw_r_k[l].reshape(1, GROUP_W),
        "ln_w": rw_ln_w[l][None, :], "ln_b": rw_ln_b[l][None, :],
        "w_out": w_out[l].astype(BF16), "norm_mlp_w": norm_mlp_w[l][None, :],
        "mlp_up": mlp_up[l].astype(BF16), "mlp_down": mlp_down[l].astype(BF16),
    }


def _prompt_layer(x, p, norm_f_w):
    bsz, seq, _ = x.shape
    x2d = x.reshape(bsz * seq, D_MODEL)
    proj = _inproj(x2d, p["norm_mix_w"], p["w_in_parts"]).reshape(bsz, seq, IN_WP)
    mix, c_st, n_st, m_st, s_st = _mixer(proj, p)
    y = _outmlp(x2d, mix.reshape(bsz * seq, D_MODEL), p["w_out"], p["norm_mlp_w"], p["mlp_up"],
                p["mlp_down"], norm_f_w)
    c_new = c_st
    n_new = n_st[:, :PAIRS, :].reshape(bsz, N_HEADS, HEAD_DIM)
    m_new = m_st[:, :, 0]
    conv_new = proj[:, seq - (CONV_W - 1):, Q0:Q0 + 2 * GROUP_W]
    s_new = s_st
    shift_new = proj[:, seq - 1:, RW0:RW0 + RW_IN_W]
    return y.reshape(bsz, seq, D_MODEL), (c_new, n_new, m_new, conv_new, s_new, shift_new)


def _sample_layer(x, c0, n0, m0, conv0, s0, shift0, p, norm_f_w):
    bsz = x.shape[0]
    x2d = x.reshape(bsz, D_MODEL)
    proj, proj_t = _inproj_t(x2d, p["norm_mix_w"], p["w_in_parts"])
    hml_t, yrw_t, c_t, n_t, m_t, s_t = _sample(
        proj_t, jnp.transpose(conv0, (1, 2, 0)), shift0[:, 0, :].T, m0.T, jnp.transpose(n0, (1, 2, 0)),
        jnp.transpose(c0, (1, 2, 3, 0)), jnp.transpose(s0, (1, 2, 3, 0)), p)
    mix = jnp.concatenate([hml_t, yrw_t], axis=0).T
    y = _outmlp(x2d, mix, p["w_out"], p["norm_mlp_w"], p["mlp_up"], p["mlp_down"], norm_f_w)
    conv_new = jnp.concatenate([conv0[:, 1:], proj[:, None, Q0:Q0 + 2 * GROUP_W]], axis=1)
    shift_new = proj[:, None, RW0:RW0 + RW_IN_W]
    return y.reshape(bsz, 1, D_MODEL), (
        jnp.transpose(c_t, (3, 0, 1, 2)), jnp.transpose(n_t, (2, 0, 1)), m_t.T, conv_new,
        jnp.transpose(s_t, (3, 0, 1, 2)), shift_new)


def kernel(x_prompt, x_sample, state_mlstm_C, state_mlstm_n, state_mlstm_m, state_mlstm_conv, state_rwkv_S, state_rwkv_shift, norm_mix_w, w_in, mlstm_conv_w, mlstm_conv_b, mlstm_i_b, mlstm_f_b, mlstm_norm_w, rw_mu, rw_w0, rw_w_up, rw_a0, rw_a_up, rw_g_up, rw_k_k, rw_k_a, rw_r_k, rw_ln_w, rw_ln_b, w_out, norm_mlp_w, mlp_up, mlp_down, norm_f_w):
    depth = w_in.shape[0]
    assert depth == 1, "the final RMSNorm is fused into the layer's output kernel"
    weights = (norm_mix_w, w_in, mlstm_conv_w, mlstm_conv_b, mlstm_i_b, mlstm_f_b, mlstm_norm_w,
               rw_mu, rw_w0, rw_w_up, rw_a0, rw_a_up, rw_g_up, rw_k_k, rw_k_a, rw_r_k, rw_ln_w, rw_ln_b,
               w_out, norm_mlp_w, mlp_up, mlp_down)
    nfw = norm_f_w[None, :]
    p = _layer_params(0, *weights)
    y_p, st_p = _prompt_layer(x_prompt, p, nfw)
    y_s, st_s = _sample_layer(x_sample, state_mlstm_C[0], state_mlstm_n[0], state_mlstm_m[0],
                              state_mlstm_conv[0], state_rwkv_S[0], state_rwkv_shift[0], p, nfw)
    lead = lambda t: tuple(a[None] for a in t)
    return (y_p, y_s) + lead(st_p) + lead(st_s)
```

```python
import functools
import itertools
import math

import jax
import jax.numpy as jnp
from jax import lax
from jax.experimental import pallas as pl
from jax.experimental.pallas import tpu as pltpu

F32 = jnp.float32
BF16 = jnp.bfloat16

D_MODEL = 1024
HEAD_DIM = 64
N_HEADS = 8
PAIRS = N_HEADS // 2
GROUP_W = N_HEADS * HEAD_DIM
CONV_W = 4
D_LORA = 64
D_GATE = 128
D_FF = 4 * D_MODEL
EPS = 1e-6
GN_EPS = 64e-5
LANES = 128
SUBLANES = 8

Q0, K0, V0, O0 = 0, GROUP_W, 2 * GROUP_W, 3 * GROUP_W
GI0 = 4 * GROUP_W
GF0 = GI0 + LANES
RW0 = GF0 + LANES
RW_IN_W = 3 * GROUP_W + 2 * D_LORA + D_GATE
IN_WP = RW0 + RW_IN_W
RS_R, RS_K, RS_V = 0, GROUP_W, 2 * GROUP_W
RS_LORA = 3 * GROUP_W
RS_G = RS_LORA + 2 * D_LORA

TB = 256
LM = 128
LR = 64
RW_GROUP = 2
PROJ_TILE = 256
_DONE = object()
_STATE = object()
TM_OUT = 1024
FF_CHUNK = 1024
SAMPLE_UNROLL = 8
NEG = -1e30
W_SCALE = math.exp(-0.5)
VMEM_LIMIT = 56 * 1024 * 1024


def _dot(a, b):
    return jnp.dot(a.astype(BF16), b.astype(BF16), preferred_element_type=F32)


def _split3(x):
    h1 = x.astype(BF16)
    r1 = x - h1.astype(F32)
    h2 = r1.astype(BF16)
    h3 = (r1 - h2.astype(F32)).astype(BF16)
    return h1, h2, h3


def _dot01_l(m01, x):
    h1, h2, h3 = _split3(x)
    d = lambda h: jnp.dot(m01, h, preferred_element_type=F32)
    return d(h1) + d(h2) + d(h3)


def _log_sigmoid(x):
    return jnp.minimum(x, 0.0) - jnp.log(1.0 + jnp.exp(-jnp.abs(x)))


def _tanh(x):
    e = jnp.exp(-2.0 * jnp.abs(x))
    t = (1.0 - e) / (1.0 + e)
    return jnp.where(x < 0.0, -t, t)


def _iota(shape, dim):
    return lax.broadcasted_iota(jnp.int32, shape, dim)


def _rowsum(x):
    return jnp.sum(x, axis=1, keepdims=True)


def _pair_rowsum(x, lo):
    s_lo = _rowsum(jnp.where(lo, x, 0.0))
    s_hi = _rowsum(jnp.where(lo, 0.0, x))
    return jnp.where(lo, s_lo, s_hi)


def _stack2(x, lo):
    zero = jnp.zeros_like(x)
    return jnp.concatenate([jnp.where(lo, x, zero), jnp.where(lo, zero, x)], axis=0)


def _rms(x, w):
    ms = jnp.mean(x * x, axis=-1, keepdims=True)
    return x * lax.rsqrt(ms + EPS) * w


def _project(x_ref, nw_ref, wml_ref, wgate_ref, wrw_ref):
    xn = _rms(x_ref[...], nw_ref[...]).astype(BF16)
    return [jnp.dot(xn, w[...], preferred_element_type=F32) for w in (wml_ref, wgate_ref, wrw_ref)]


def _outmlp_kernel(x_ref, mix_ref, wout_ref, nmw_ref, up_ref, down_ref, nfw_ref, y_ref):
    x1 = x_ref[...] + jnp.dot(mix_ref[...].astype(BF16), wout_ref[...], preferred_element_type=F32)
    xn = _rms(x1, nmw_ref[...]).astype(BF16)
    acc = x1
    for c in range(D_FF // FF_CHUNK):
        sl = slice(c * FF_CHUNK, (c + 1) * FF_CHUNK)
        hid = jnp.maximum(jnp.dot(xn, up_ref[:, sl], preferred_element_type=F32), 0.0)
        acc = acc + jnp.dot((hid * hid).astype(BF16), down_ref[sl, :], preferred_element_type=F32)
    y_ref[...] = _rms(acc, nfw_ref[...])


def _outmlp(x2d, mix2d, w_out, norm_mlp_w, mlp_up, mlp_down, norm_f_w):
    rows = x2d.shape[0]
    tm = min(TM_OUT, rows)
    const = lambda shape: pl.BlockSpec(shape, lambda i: (0, 0), pipeline_mode=pl.Buffered(1))
    return pl.pallas_call(
        _outmlp_kernel,
        out_shape=jax.ShapeDtypeStruct((rows, D_MODEL), F32),
        grid=(rows // tm,),
        in_specs=[
            pl.BlockSpec((tm, D_MODEL), lambda i: (i, 0)),
            pl.BlockSpec((tm, D_MODEL), lambda i: (i, 0)),
            const((D_MODEL, D_MODEL)),
            const((1, D_MODEL)),
            const((D_MODEL, D_FF)),
            const((D_FF, D_MODEL)),
            const((1, D_MODEL)),
        ],
        out_specs=pl.BlockSpec((tm, D_MODEL), lambda i: (i, 0)),
        compiler_params=pltpu.CompilerParams(
            dimension_semantics=("arbitrary",), vmem_limit_bytes=VMEM_LIMIT),
        name="outmlp",
    )(x2d, mix2d, w_out, norm_mlp_w, mlp_up, mlp_down, norm_f_w)


def _rwkv_token_prep(lora_t, xg, w0, wup, a0, aup, gup):
    w_raw = w0 + jnp.dot(_tanh(lora_t).astype(BF16), wup, preferred_element_type=F32)
    wlog = -W_SCALE * jax.nn.sigmoid(w_raw)
    a = jax.nn.sigmoid(a0 + jnp.dot(lora_t.astype(BF16), aup, preferred_element_type=F32))
    g = jnp.dot(jax.nn.sigmoid(xg).astype(BF16), gup, preferred_element_type=F32)
    return wlog, a, g


def _mlstm_chunk(c, proj_ref, q_ref, k_ref, ib_ref, fb_ref, mlnw_ref, mix_ref, C_ref, n_ref, m_ref):
    rows = pl.ds(pl.multiple_of(c * LM, LM), LM)
    row = _iota((LM, LM), 0)
    col = _iota((LM, LM), 1)
    causal = col <= row
    tril = jnp.where(causal, 1.0, 0.0).astype(BF16)
    blockdiag = (row < HEAD_DIM) == (col < HEAD_DIM)
    lane = _iota((1, LANES), 1)
    lo = lane < HEAD_DIM

    i_tile = proj_ref[0, rows, GI0:GI0 + LANES] + ib_ref[...]
    f_tile = _log_sigmoid(proj_ref[0, rows, GF0:GF0 + LANES] + fb_ref[...])
    bc = _dot01_l(tril, f_tile)
    u = i_tile - bc
    ut = u.T
    yield

    nt = (((1,), (1,)), ((), ()))
    pairs = range(PAIRS)
    heads = range(N_HEADS)
    tile = lambda base, j: slice(base + j * LANES, base + (j + 1) * LANES)
    hmask = lambda h: lo if h % 2 == 0 else jnp.logical_not(lo)
    q_t = [q_ref[rows, tile(0, j)] for j in pairs]
    k_t = [k_ref[rows, tile(0, j)] for j in pairs]
    v_t = [proj_ref[0, rows, tile(V0, j)] for j in pairs]
    c_pair = [C_ref[0, j] for j in pairs]
    n_row = [n_ref[0, j:j + 1, :] for j in pairs]
    k_bf = [k.astype(BF16) for k in k_t]
    v_bf = [v.astype(BF16) for v in v_t]
    s = [lax.dot_general(jnp.where(hmask(h), q_t[h // 2], 0.0).astype(BF16), k_bf[h // 2], nt,
                         preferred_element_type=F32) for h in heads]
    q_c = [_dot(q_t[j], c_pair[j]) for j in pairs]
    k_tr = [k.T for k in k_t]
    yield

    m_prev_t = m_ref[0]
    ut8 = ut[0:N_HEADS, :]
    m_last_t = jnp.maximum(jnp.max(ut8, axis=1, keepdims=True), m_prev_t)
    ws_rows = jnp.exp(ut8 - m_last_t)
    dec_t = jnp.exp(m_prev_t - m_last_t)
    m_ref[0] = bc.T[0:N_HEADS, LM - 1:LM] + m_last_t

    m_prev = [m_prev_t[h:h + 1, 0:1] for h in heads]
    u_row = [ut[h:h + 1, :] for h in heads]
    bc_col = [_rowsum(jnp.where(lane == h, bc, 0.0)) for h in heads]
    m_col = [jnp.maximum(jnp.max(jnp.where(causal, u_row[h], NEG), axis=1, keepdims=True), m_prev[h])
             for h in heads]
    p = [s[h] * jnp.exp(jnp.where(causal, u_row[h] - m_col[h], NEG)) for h in heads]
    sc = [jnp.exp(m_prev[h] - m_col[h]) for h in heads]
    yield
    pv = [jnp.dot(p[h].astype(BF16), v_bf[h // 2], preferred_element_type=F32) for h in heads]
    yield
    h_parts = []
    for h in heads:
        q_n = _rowsum(jnp.where(hmask(h), q_t[h // 2] * n_row[h // 2], 0.0))
        den = _rowsum(p[h]) + sc[h] * q_n
        num = pv[h] + sc[h] * q_c[h // 2]
        h_parts.append(num / jnp.maximum(jnp.abs(den), jnp.exp(-(bc_col[h] + m_col[h]))))

    top = _iota((LM, 1), 0) < HEAD_DIM
    kw = [k_tr[j] * jnp.where(top, ws_rows[2 * j:2 * j + 1, :], ws_rows[2 * j + 1:2 * j + 2, :])
          for j in pairs]
    upd = [jnp.dot(kw[j].astype(BF16), v_bf[j], preferred_element_type=F32) for j in pairs]
    n_upd = [jnp.dot(ws_rows.astype(BF16), k_bf[j], preferred_element_type=F32) for j in pairs]
    yield
    for j in pairs:
        dec_j = jnp.where(lo, dec_t[2 * j:2 * j + 1, :], dec_t[2 * j + 1:2 * j + 2, :])
        C_ref[0, j] = c_pair[j] * dec_j + jnp.where(blockdiag, upd[j], 0.0)
        n_ref[0, j:j + 1, :] = n_row[j] * dec_j + jnp.where(lo, n_upd[j][2 * j:2 * j + 1, :],
                                                            n_upd[j][2 * j + 1:2 * j + 2, :])

    for j in pairs:
        h_til = jnp.where(lo, h_parts[2 * j], h_parts[2 * j + 1])
        hml = jax.nn.sigmoid(proj_ref[0, rows, tile(O0, j)]) * h_til
        ms = _pair_rowsum(hml * hml, lo) * (1.0 / HEAD_DIM)
        mix_ref[0, rows, tile(0, j)] = hml * lax.rsqrt(ms + EPS) * mlnw_ref[:, tile(0, j)]


def _neumann_inverse(ns):
    shape = ns[0].shape
    eye = jnp.where(_iota(shape, 0) == _iota(shape, 1), 1.0, 0.0)
    n_sq = int(math.log2(LR)) - 1
    qs = [n.astype(BF16) for n in ns]
    ts = [eye + n for n in ns]
    sq = [jnp.dot(q, q, preferred_element_type=F32) for q in qs]
    yield
    for _ in range(n_sq - 1):
        qs = [s.astype(BF16) for s in sq]
        both = [jnp.dot(q, jnp.concatenate([q, t.astype(BF16)], axis=1), preferred_element_type=F32)
                for q, t in zip(qs, ts)]
        sq = [b[:, 0:LANES] for b in both]
        ts = [t + b[:, LANES:2 * LANES] for t, b in zip(ts, both)]
        yield
    return [(t + jnp.dot(s.astype(BF16), t.astype(BF16), preferred_element_type=F32)).astype(BF16)
            for s, t in zip(sq, ts)]


def _rwkv_chunks(c, rs_ref, w0_ref, wup_ref, a0_ref, aup_ref, gup_ref, kk_ref, ka_ref, rk_ref,
                 lnw_ref, lnb_ref, mix_ref, S_ref):
    gl = RW_GROUP * LR
    rows = pl.ds(pl.multiple_of(c * gl, gl), gl)
    lane = _iota((1, LANES), 1)
    lo = lane < HEAD_DIM
    row = _iota((LANES, LANES), 0)
    col = _iota((LANES, LANES), 1)
    same = (row < LR) == (col < LR)
    mask_sl = jnp.logical_and(same, col < row)
    mask_li = jnp.logical_and(same, col <= row)
    mask_li2 = jnp.concatenate([mask_li, mask_li], axis=1)
    grow = _iota((gl, gl), 0)
    gcol = _iota((gl, gl), 1)
    tril = jnp.where(jnp.logical_and(gcol <= grow, gcol >= (grow // LR) * LR), 1.0, 0.0).astype(BF16)
    nt = (((1,), (1,)), ((), ()))
    pairs = range(PAIRS)
    items = [(ci, j) for ci in range(RW_GROUP) for j in pairs]
    tile = lambda base, j: slice(base + j * LANES, base + (j + 1) * LANES)
    crow = lambda ci: slice(ci * LR, (ci + 1) * LR)

    wlog, a, g = _rwkv_token_prep(rs_ref[rows, RS_LORA:RS_LORA + LANES], rs_ref[rows, RS_G:RS_G + D_GATE],
                                  w0_ref[...], wup_ref[...], a0_ref[...], aup_ref[...], gup_ref[...])
    lw = _dot01_l(tril, wlog)
    e_in = jnp.exp(lw)
    e_ex = jnp.exp(lw - wlog)
    e_ng = jnp.exp(-lw)

    r_g = [rs_ref[rows, tile(RS_R, j)] for j in pairs]
    kr_g = [rs_ref[rows, tile(RS_K, j)] for j in pairs]
    v_g = [rs_ref[rows, tile(RS_V, j)] for j in pairs]
    a_g = [a[:, tile(0, j)] for j in pairs]
    kk = [kr_g[j] * kk_ref[:, tile(0, j)] for j in pairs]
    kkn = [kk[j] * lax.rsqrt(jnp.maximum(_pair_rowsum(kk[j] * kk[j], lo), 1e-24)) for j in pairs]
    keff_g = [kr_g[j] * (1.0 + (a_g[j] - 1.0) * ka_ref[:, tile(0, j)]) for j in pairs]
    alpha_g = [-kkn[j] * e_ex[:, tile(0, j)] for j in pairs]
    rb_g = [r_g[j] * e_in[:, tile(0, j)] for j in pairs]
    beta_g = [kkn[j] * a_g[j] * e_ng[:, tile(0, j)] for j in pairs]
    ktl_g = [keff_g[j] * e_ng[:, tile(0, j)] for j in pairs]
    bonus_g = [_pair_rowsum(r_g[j] * keff_g[j] * rk_ref[:, tile(0, j)], lo) * v_g[j] for j in pairs]

    lhs, rhs, v2, bk, gamma, ar = {}, {}, {}, {}, {}, {}
    for it in items:
        ci, j = it
        cr = crow(ci)
        ar[it] = jnp.concatenate([alpha_g[j][cr], rb_g[j][cr]], axis=0).astype(BF16)
        lhs[it] = jnp.concatenate([_stack2(ar[it][0:LR], lo), _stack2(ar[it][LR:2 * LR], lo)], axis=0)
        beta_c, ktl_c = beta_g[j][cr].astype(BF16), ktl_g[j][cr].astype(BF16)
        rhs[it] = jnp.concatenate([beta_c, beta_c, ktl_c, ktl_c], axis=0)
        v2[it] = _stack2(v_g[j][cr].astype(BF16), lo)
        bk[it] = jnp.concatenate([_stack2(beta_c, lo), _stack2(ktl_c, lo)], axis=0)
        gamma[it] = e_in[(ci + 1) * LR - 1:(ci + 1) * LR, tile(0, j)]
    aa = {it: lax.dot_general(lhs[it], rhs[it], nt, preferred_element_type=F32) for it in items}
    yield
    a_ab = [jnp.where(mask_sl, aa[it][0:LANES, 0:LANES], 0.0) for it in items]
    a_ak = {it: jnp.where(mask_sl, aa[it][0:LANES, LANES:2 * LANES], 0.0).astype(BF16) for it in items}
    a_r = {it: jnp.where(mask_li2, aa[it][LANES:2 * LANES, :], 0.0).astype(BF16) for it in items}
    akv = {it: jnp.dot(a_ak[it], v2[it], preferred_element_type=F32) for it in items}
    yield
    t_list = yield from _neumann_inverse(a_ab)
    t_inv = dict(zip(items, t_list))
    yield _STATE

    s_pair = [S_ref[0, j] for j in pairs]
    for ci in range(RW_GROUP):
        its = [(ci, j) for j in pairs]
        xs = [lax.dot_general(ar[it], s_pair[it[1]].astype(BF16), nt, preferred_element_type=F32)
              for it in its]
        yield
        p2 = [_dot(t_inv[it], _stack2(xs[j][0:LR], lo) + akv[it]) for j, it in enumerate(its)]
        yield
        pv = [jnp.concatenate([p2[j].astype(BF16), v2[it]], axis=0) for j, it in enumerate(its)]
        y2 = [_dot(a_r[it], pv[j]) for j, it in enumerate(its)]
        upd = [lax.dot_general(pv[j], bk[it], (((0,), (0,)), ((), ())), preferred_element_type=F32)
               for j, it in enumerate(its)]
        s_pair = [(s_pair[j] + upd[j]) * gamma[it] for j, it in enumerate(its)]
        yield
        cr = crow(ci)
        for j in pairs:
            yb = xs[j][LR:2 * LR] + y2[j][0:LR] + y2[j][LR:2 * LR] + bonus_g[j][cr]
            mu = _pair_rowsum(yb, lo) * (1.0 / HEAD_DIM)
            dlt = yb - mu
            var = _pair_rowsum(dlt * dlt, lo) * (1.0 / HEAD_DIM)
            yn = dlt * lax.rsqrt(var + GN_EPS) * lnw_ref[:, tile(0, j)] + lnb_ref[:, tile(0, j)]
            out_rows = pl.ds(pl.multiple_of(c * gl + ci * LR, LR), LR)
            mix_ref[0, out_rows, tile(GROUP_W, j)] = yn * g[cr, tile(0, j)]
    for j in pairs:
        S_ref[0, j] = s_pair[j]


def _shift_rows(x, tail, j):
    rolled = pltpu.roll(x, j, axis=0)
    head = jnp.where(_iota((SUBLANES, 1), 0) < j, pltpu.roll(tail, j, axis=0), rolled[0:SUBLANES])
    return jnp.concatenate([head, rolled[SUBLANES:]], axis=0)


def _until_state(gen):
    for tok in gen:
        if tok is _STATE:
            return
        yield


def _drain(gen):
    for _ in gen:
        yield


def _weave(primary, *others):
    while next(primary, _DONE) is not _DONE:
        for g in others:
            next(g, _DONE)


def _proj_tiles(x_ref, nw_ref, w_refs, dst_ref):
    xn = _rms(x_ref[0], nw_ref[...]).astype(BF16)
    yield
    for w_ref, col0 in zip(w_refs, (0, GI0, RW0)):
        for c0 in range(0, w_ref.shape[1], PROJ_TILE):
            cw = min(PROJ_TILE, w_ref.shape[1] - c0)
            dst_ref[0, :, col0 + c0:col0 + c0 + cw] = jnp.dot(xn, w_ref[:, c0:c0 + cw],
                                                              preferred_element_type=F32)
            yield


def _mixer_kernel(nt, x_ref, nw_ref, wml_ref, wgate_ref, wrw_ref,
                  convw_ref, convb_ref, ib_ref, fb_ref, mlnw_ref, mu_ref, w0_ref, wup_ref,
                  a0_ref, aup_ref, gup_ref, kk_ref, ka_ref, rk_ref, lnw_ref, lnb_ref,
                  mix_ref, Cout_ref, n_ref, m_ref, Sout_ref, convt_ref, shiftt_ref,
                  proj_s, C_ref, S_ref, qk_tail_ref, rw_tail_ref, q_ref, k_ref, rs_ref):
    s = pl.program_id(0)
    t = jnp.maximum(s - 1, 0) % nt
    proj_ref = proj_s.at[pl.ds(1 - s % 2, 1)]
    proj_next = proj_s.at[pl.ds(s % 2, 1)]

    @pl.when(s == 0)
    def _():
        proj_s[...] = jnp.zeros_like(proj_s)

    @pl.when(t == 0)
    def _():
        C_ref[...] = jnp.zeros_like(C_ref)
        n_ref[...] = jnp.zeros_like(n_ref)
        m_ref[...] = jnp.zeros_like(m_ref)
        S_ref[...] = jnp.zeros_like(S_ref)
        qk_tail_ref[...] = jnp.zeros_like(qk_tail_ref)
        rw_tail_ref[...] = jnp.zeros_like(rw_tail_ref)

    u = proj_ref[0, :, Q0:Q0 + 2 * GROUP_W]
    qk_tail = qk_tail_ref[...]
    acc = convb_ref[...] + u * convw_ref[CONV_W - 1:CONV_W, :]
    for jw in range(CONV_W - 1):
        acc = acc + _shift_rows(u, qk_tail, CONV_W - 1 - jw) * convw_ref[jw:jw + 1, :]
    qk_tail_ref[...] = u[TB - SUBLANES:TB]
    convt_ref[0] = u[TB - SUBLANES:TB]
    qk = acc * jax.nn.sigmoid(acc)
    q_ref[...] = qk[:, 0:GROUP_W]
    k_ref[...] = qk[:, GROUP_W:2 * GROUP_W] * (HEAD_DIM ** -0.5)

    rw = proj_ref[0, :, RW0:RW0 + RW_IN_W]
    rs_ref[...] = rw + mu_ref[...] * (_shift_rows(rw, rw_tail_ref[...], 1) - rw)
    rw_tail_ref[...] = rw[TB - SUBLANES:TB]
    shiftt_ref[0] = rw[TB - SUBLANES:TB]

    assert RW_GROUP * LR == LM
    rwkv = lambda c: _rwkv_chunks(c, rs_ref, w0_ref, wup_ref, a0_ref, aup_ref, gup_ref, kk_ref, ka_ref, rk_ref,
                                  lnw_ref, lnb_ref, mix_ref, S_ref)
    mlstm = lambda c: _mlstm_chunk(c, proj_ref, q_ref, k_ref, ib_ref, fb_ref, mlnw_ref, mix_ref, C_ref, n_ref, m_ref)
    n_groups = TB // LM
    gens = [rwkv(c) for c in range(n_groups)]
    proj = _proj_tiles(x_ref, nw_ref, (wml_ref, wgate_ref, wrw_ref), proj_next)
    _weave(_until_state(gens[0]), mlstm(0), proj)
    for c in range(1, n_groups):
        _weave(_until_state(gens[c]), _drain(gens[c - 1]), proj)
        _weave(_drain(gens[c - 1]))
    _weave(_drain(gens[-1]), itertools.chain(*[mlstm(c) for c in range(1, n_groups)]), proj)
    _weave(proj)

    @pl.when(t == nt - 1)
    def _():
        for h in range(N_HEADS):
            blk = slice((h % 2) * HEAD_DIM, (h % 2 + 1) * HEAD_DIM)
            Cout_ref[0, h] = C_ref[0, h // 2, blk, blk]
            Sout_ref[0, h] = S_ref[0, h // 2, blk, blk]


def _mixer(x, p):
    bsz, seq, _ = x.shape
    nt = seq // TB
    nblk = bsz * nt
    row = lambda w: pl.BlockSpec((1, w), lambda s: (0, 0))
    mat = lambda r, w: pl.BlockSpec((r, w), lambda s: (0, 0))
    resident = lambda a: pl.BlockSpec(a.shape, lambda s: (0, 0), pipeline_mode=pl.Buffered(1))
    blk = lambda s: jnp.maximum(s - 1, 0)
    seq_of = lambda s: blk(s) // nt
    state4 = pl.BlockSpec((1, N_HEADS, HEAD_DIM, HEAD_DIM), lambda s: (seq_of(s), 0, 0, 0))
    state3 = pl.BlockSpec((1, SUBLANES, LANES), lambda s: (seq_of(s), 0, 0))
    tail = lambda w: pl.BlockSpec((1, SUBLANES, w), lambda s: (seq_of(s), 0, 0))
    return pl.pallas_call(
        functools.partial(_mixer_kernel, nt),
        out_shape=(
            jax.ShapeDtypeStruct((nblk, TB, 2 * GROUP_W), F32),
            jax.ShapeDtypeStruct((bsz, N_HEADS, HEAD_DIM, HEAD_DIM), F32),
            jax.ShapeDtypeStruct((bsz, SUBLANES, LANES), F32),
            jax.ShapeDtypeStruct((bsz, SUBLANES, LANES), F32),
            jax.ShapeDtypeStruct((bsz, N_HEADS, HEAD_DIM, HEAD_DIM), F32),
            jax.ShapeDtypeStruct((bsz, SUBLANES, 2 * GROUP_W), F32),
            jax.ShapeDtypeStruct((bsz, SUBLANES, RW_IN_W), F32),
        ),
        grid=(nblk + 1,),
        in_specs=[
            pl.BlockSpec((1, TB, D_MODEL), lambda s: (jnp.minimum(s, nblk - 1), 0, 0)),
            row(D_MODEL)] + [resident(w) for w in p["w_in_parts"]] + [
            mat(CONV_W, 2 * GROUP_W), row(2 * GROUP_W), row(LANES), row(LANES), row(GROUP_W),
            row(RW_IN_W), row(GROUP_W), mat(LANES, GROUP_W), row(GROUP_W), mat(LANES, GROUP_W),
            mat(D_GATE, GROUP_W), row(GROUP_W), row(GROUP_W), row(GROUP_W), row(GROUP_W), row(GROUP_W),
        ],
        out_specs=(
            pl.BlockSpec((1, TB, 2 * GROUP_W), lambda s: (blk(s), 0, 0)),
            state4, state3, state3, state4, tail(2 * GROUP_W), tail(RW_IN_W),
        ),
        scratch_shapes=[
            pltpu.VMEM((2, TB, IN_WP), F32),
            pltpu.VMEM((1, PAIRS, LANES, LANES), F32),
            pltpu.VMEM((1, PAIRS, LANES, LANES), F32),
            pltpu.VMEM((SUBLANES, 2 * GROUP_W), F32),
            pltpu.VMEM((SUBLANES, RW_IN_W), F32),
            pltpu.VMEM((TB, GROUP_W), F32),
            pltpu.VMEM((TB, GROUP_W), F32),
            pltpu.VMEM((TB, RW_IN_W), F32),
        ],
        compiler_params=pltpu.CompilerParams(
            dimension_semantics=("arbitrary",), vmem_limit_bytes=VMEM_LIMIT),
        name="mixer",
    )(x.reshape(nblk, TB, D_MODEL), p["norm_mix_w"], *p["w_in_parts"],
      p["conv_w"], p["conv_b"], p["ib_row"], p["fb_row"], p["mlnorm_w"], p["mu"], p["w0"],
      p["wup_p"], p["a0"], p["aup_p"], p["gup"], p["k_k"], p["k_a"], p["r_k"], p["ln_w"], p["ln_b"])


def _inproj_t_kernel(x_ref, nw_ref, wml_ref, wgate_ref, wrw_ref, o_ref, ot_ref):
    o_ml, o_gate, o_rw = _project(x_ref, nw_ref, wml_ref, wgate_ref, wrw_ref)
    for lo_col, hi_col, o in ((0, GI0, o_ml), (GI0, RW0, o_gate), (RW0, IN_WP, o_rw)):
        o_ref[:, lo_col:hi_col] = o
        ot_ref[lo_col:hi_col, :] = o.T


def _inproj_t(x2d, norm_w, w_in_parts):
    rows = x2d.shape[0]
    return pl.pallas_call(
        _inproj_t_kernel,
        out_shape=(jax.ShapeDtypeStruct((rows, IN_WP), F32), jax.ShapeDtypeStruct((IN_WP, rows), F32)),
        compiler_params=pltpu.CompilerParams(vmem_limit_bytes=VMEM_LIMIT),
        name="inproj_t",
    )(x2d, norm_w, *w_in_parts)


(PC_W0, PC_A0, PC_MLNW, PC_KK, PC_KA, PC_RK, PC_LNW, PC_LNB) = range(8)
PC_CONVB = CONV_W


def _sample_kernel(pt_ref, conv0_ref, shift0_ref, m0_ref, n0_ref, C0_ref, S0_ref,
                   p1024_ref, p512_ref, mu_ref, gb_ref, wup_ref, aup_ref, gup_ref,
                   hml_ref, yrw_ref, C_ref, n_ref, m_ref, S_ref,
                   wk_s, q_s, vr_s, y_s):
    h = pl.program_id(0)
    hrow = lambda base, n=HEAD_DIM: pl.ds(pl.multiple_of(base + h * n, n), n)
    pcol = lambda j: p512_ref[hrow(0), j:j + 1]
    colsum = lambda t: jnp.sum(t, axis=0, keepdims=True)

    def conv_silu(base):
        rows = hrow(base)
        acc = p1024_ref[rows, PC_CONVB:PC_CONVB + 1] + pt_ref[rows, :] * p1024_ref[rows, CONV_W - 1:CONV_W]
        for jw in range(CONV_W - 1):
            acc = acc + conv0_ref[jw, rows, :] * p1024_ref[rows, jw:jw + 1]
        return acc * jax.nn.sigmoid(acc)

    q = conv_silu(Q0)
    k = conv_silu(K0) * (HEAD_DIM ** -0.5)
    v = pt_ref[hrow(V0), :]
    i_g = pt_ref[pl.ds(GI0 + h, 1), :] + gb_ref[pl.ds(h, 1), 0:1]
    logf = _log_sigmoid(pt_ref[pl.ds(GF0 + h, 1), :] + gb_ref[pl.ds(h, 1), 1:2])
    m0 = m0_ref[pl.ds(h, 1), :]
    m_new = jnp.maximum(logf + m0, i_g)
    ws = jnp.exp(i_g - m_new)
    dec = jnp.exp(logf + m0 - m_new)
    wk = ws * k
    n_new = dec * n0_ref[0] + wk
    n_ref[0] = n_new
    m_ref[pl.ds(h, 1), :] = m_new
    wk_s[...] = wk
    q_s[...] = q

    def c_body(kk_, num):
        ck = dec * C0_ref[0, kk_] + wk_s[pl.ds(kk_, 1), :] * v
        C_ref[0, kk_] = ck
        return num + q_s[pl.ds(kk_, 1), :] * ck

    num = lax.fori_loop(0, HEAD_DIM, c_body, jnp.zeros((HEAD_DIM, pt_ref.shape[1]), F32), unroll=SAMPLE_UNROLL)
    den = colsum(q * n_new)
    hml = jax.nn.sigmoid(pt_ref[hrow(O0), :]) * (num / jnp.maximum(jnp.abs(den), jnp.exp(-m_new)))
    hml_ref[...] = hml * lax.rsqrt(colsum(hml * hml) * (1.0 / HEAD_DIM) + EPS) * pcol(PC_MLNW)

    def shifted(base, n=HEAD_DIM):
        rw = pt_ref[hrow(RW0 + base, n) if n == HEAD_DIM else pl.ds(RW0 + base, n), :]
        rows = hrow(base, n) if n == HEAD_DIM else pl.ds(base, n)
        return rw + mu_ref[rows, :] * (shift0_ref[rows, :] - rw)

    r, kr, vr = shifted(RS_R), shifted(RS_K), shifted(RS_V)
    lora = shifted(RS_LORA, LANES)
    xg = shifted(RS_G, D_GATE)
    hw = hrow(0)
    w_raw = pcol(PC_W0) + jnp.dot(wup_ref[hw, :], jnp.tanh(lora).astype(BF16), preferred_element_type=F32)
    w = jnp.exp(-W_SCALE * jax.nn.sigmoid(w_raw))
    a = jax.nn.sigmoid(pcol(PC_A0) + jnp.dot(aup_ref[hw, :], lora.astype(BF16), preferred_element_type=F32))
    g = jnp.dot(gup_ref[hw, :], jax.nn.sigmoid(xg).astype(BF16), preferred_element_type=F32)
    kk = kr * pcol(PC_KK)
    kkn = kk / jnp.maximum(jnp.sqrt(colsum(kk * kk)), 1e-12)
    keff = kr * (1.0 + (a - 1.0) * pcol(PC_KA))
    nkk = -kkn
    ka = kkn * a
    vr_s[...] = vr

    def s_body(v_, carry):
        sv = S0_ref[0, v_]
        sn = sv * w + colsum(sv * nkk) * ka + vr_s[pl.ds(v_, 1), :] * keff
        S_ref[0, v_] = sn
        y_s[pl.ds(v_, 1), :] = colsum(sn * r)
        return carry

    lax.fori_loop(0, HEAD_DIM, s_body, 0, unroll=SAMPLE_UNROLL)
    yb = y_s[...] + colsum(r * keff * pcol(PC_RK)) * vr
    dlt = yb - colsum(yb) * (1.0 / HEAD_DIM)
    var = colsum(dlt * dlt) * (1.0 / HEAD_DIM)
    yrw_ref[...] = (dlt * lax.rsqrt(var + GN_EPS) * pcol(PC_LNW) + pcol(PC_LNB)) * g


def _sample(proj_t, conv0_t, shift0_t, m0_t, n0_t, c0_t, s0_t, p):
    bsz = proj_t.shape[1]
    whole = lambda a: pl.BlockSpec(a.shape, lambda h: (0,) * a.ndim)
    state4 = pl.BlockSpec((1, HEAD_DIM, HEAD_DIM, bsz), lambda h: (h, 0, 0, 0))
    head2 = pl.BlockSpec((HEAD_DIM, bsz), lambda h: (h, 0))
    head3 = pl.BlockSpec((1, HEAD_DIM, bsz), lambda h: (h, 0, 0))
    consts = [p["p1024"], p["p512"], p["mu_col"], p["gate_b"], p["wup_t"], p["aup_t"], p["gup_t"]]
    return pl.pallas_call(
        _sample_kernel,
        out_shape=(
            jax.ShapeDtypeStruct((GROUP_W, bsz), F32),
            jax.ShapeDtypeStruct((GROUP_W, bsz), F32),
            jax.ShapeDtypeStruct((N_HEADS, HEAD_DIM, HEAD_DIM, bsz), F32),
            jax.ShapeDtypeStruct((N_HEADS, HEAD_DIM, bsz), F32),
            jax.ShapeDtypeStruct((N_HEADS, bsz), F32),
            jax.ShapeDtypeStruct((N_HEADS, HEAD_DIM, HEAD_DIM, bsz), F32),
        ),
        grid=(N_HEADS,),
        in_specs=[whole(proj_t), whole(conv0_t), whole(shift0_t), whole(m0_t), head3, state4, state4]
                 + [whole(c) for c in consts],
        out_specs=(head2, head2, state4, head3, pl.BlockSpec((N_HEADS, bsz), lambda h: (0, 0)), state4),
        scratch_shapes=[pltpu.VMEM((HEAD_DIM, bsz), F32) for _ in range(4)],
        compiler_params=pltpu.CompilerParams(
            dimension_semantics=("arbitrary",), vmem_limit_bytes=VMEM_LIMIT),
        name="sample",
    )(proj_t, conv0_t, shift0_t, m0_t, n0_t, c0_t, s0_t, *consts)


def _layer_params(l, norm_mix_w, w_in, mlstm_conv_w, mlstm_conv_b, mlstm_i_b, mlstm_f_b, mlstm_norm_w,
                  rw_mu, rw_w0, rw_w_up, rw_a0, rw_a_up, rw_g_up, rw_k_k, rw_k_a, rw_r_k, rw_ln_w, rw_ln_b,
                  w_out, norm_mlp_w, mlp_up, mlp_down):
    ml_in = 4 * GROUP_W
    wi = w_in[l]
    zpad = jnp.zeros((D_MODEL, LANES - N_HEADS), BF16)
    w_in_parts = (
        wi[:, :ml_in].astype(BF16),
        jnp.concatenate([wi[:, ml_in:ml_in + N_HEADS].astype(BF16), zpad,
                         wi[:, ml_in + N_HEADS:ml_in + 2 * N_HEADS].astype(BF16), zpad], axis=1),
        wi[:, ml_in + 2 * N_HEADS:].astype(BF16),
    )
    gate_row = lambda b: jnp.concatenate([b, jnp.zeros((LANES - N_HEADS,), F32)])[None, :]
    zl = jnp.zeros((D_LORA, GROUP_W), F32)
    return {
        "norm_mix_w": norm_mix_w[l][None, :],
        "w_in_parts": w_in_parts,
        "conv_w": mlstm_conv_w[l], "conv_b": mlstm_conv_b[l][None, :],
        "ib_row": gate_row(mlstm_i_b[l]), "fb_row": gate_row(mlstm_f_b[l]),
        "gate_b": jnp.stack([mlstm_i_b[l], mlstm_f_b[l]], axis=1),
        "p1024": jnp.concatenate([mlstm_conv_w[l].T, mlstm_conv_b[l][:, None],
                                  jnp.zeros((2 * GROUP_W, SUBLANES - CONV_W - 1), F32)], axis=1),
        "p512": jnp.stack([rw_w0[l], rw_a0[l], mlstm_norm_w[l], rw_k_k[l], rw_k_a[l], rw_r_k[l].reshape(GROUP_W),
                           rw_ln_w[l], rw_ln_b[l]], axis=1),
        "mu_col": rw_mu[l][:, None],
        "mlnorm_w": mlstm_norm_w[l][None, :],
        "mu": rw_mu[l][None, :], "w0": rw_w0[l][None, :], "a0": rw_a0[l][None, :],
        "wup_p": jnp.concatenate([rw_w_up[l], zl], axis=0).astype(BF16),
        "aup_p": jnp.concatenate([zl, rw_a_up[l]], axis=0).astype(BF16),
        "gup": rw_g_up[l].astype(BF16),
        "wup_t": jnp.concatenate([rw_w_up[l], zl], axis=0).T.astype(BF16),
        "aup_t": jnp.concatenate([zl, rw_a_up[l]], axis=0).T.astype(BF16),
        "gup_t": rw_g_up[l].T.astype(BF16),
        "k_k": rw_k_k[l][None, :], "k_a": rw_k_a[l][None, :], "r_k": rw_r_k[l].reshape(1, GROUP_W),
        "ln_w": rw_ln_w[l][None, :], "ln_b": rw_ln_b[l][None, :],
        "w_out": w_out[l].astype(BF16), "norm_mlp_w": norm_mlp_w[l][None, :],
        "mlp_up": mlp_up[l].astype(BF16), "mlp_down": mlp_down[l].astype(BF16),
    }


def _prompt_layer(x, p, norm_f_w):
    bsz, seq, _ = x.shape
    x2d = x.reshape(bsz * seq, D_MODEL)
    mix, c_st, n_st, m_st, s_st, conv_tail, shift_tail = _mixer(x, p)
    y = _outmlp(x2d, mix.reshape(bsz * seq, D_MODEL), p["w_out"], p["norm_mlp_w"], p["mlp_up"],
                p["mlp_down"], norm_f_w)
    c_new = c_st
    n_new = n_st[:, :PAIRS, :].reshape(bsz, N_HEADS, HEAD_DIM)
    m_new = m_st[:, :, 0]
    conv_new = conv_tail[:, SUBLANES - (CONV_W - 1):, :]
    s_new = s_st
    shift_new = shift_tail[:, SUBLANES - 1:, :]
    return y.reshape(bsz, seq, D_MODEL), (c_new, n_new, m_new, conv_new, s_new, shift_new)


def _sample_layer(x, c0, n0, m0, conv0, s0, shift0, p, norm_f_w):
    bsz = x.shape[0]
    x2d = x.reshape(bsz, D_MODEL)
    proj, proj_t = _inproj_t(x2d, p["norm_mix_w"], p["w_in_parts"])
    hml_t, yrw_t, c_t, n_t, m_t, s_t = _sample(
        proj_t, jnp.transpose(conv0, (1, 2, 0)), shift0[:, 0, :].T, m0.T, jnp.transpose(n0, (1, 2, 0)),
        jnp.transpose(c0, (1, 2, 3, 0)), jnp.transpose(s0, (1, 2, 3, 0)), p)
    mix = jnp.concatenate([hml_t, yrw_t], axis=0).T
    y = _outmlp(x2d, mix, p["w_out"], p["norm_mlp_w"], p["mlp_up"], p["mlp_down"], norm_f_w)
    conv_new = jnp.concatenate([conv0[:, 1:], proj[:, None, Q0:Q0 + 2 * GROUP_W]], axis=1)
    shift_new = proj[:, None, RW0:RW0 + RW_IN_W]
    return y.reshape(bsz, 1, D_MODEL), (
        jnp.transpose(c_t, (3, 0, 1, 2)), jnp.transpose(n_t, (2, 0, 1)), m_t.T, conv_new,
        jnp.transpose(s_t, (3, 0, 1, 2)), shift_new)


def kernel(x_prompt, x_sample, state_mlstm_C, state_mlstm_n, state_mlstm_m, state_mlstm_conv, state_rwkv_S, state_rwkv_shift, norm_mix_w, w_in, mlstm_conv_w, mlstm_conv_b, mlstm_i_b, mlstm_f_b, mlstm_norm_w, rw_mu, rw_w0, rw_w_up, rw_a0, rw_a_up, rw_g_up, rw_k_k, rw_k_a, rw_r_k, rw_ln_w, rw_ln_b, w_out, norm_mlp_w, mlp_up, mlp_down, norm_f_w):
    depth = w_in.shape[0]
    assert depth == 1, "the final RMSNorm is fused into the layer's output kernel"
    weights = (norm_mix_w, w_in, mlstm_conv_w, mlstm_conv_b, mlstm_i_b, mlstm_f_b, mlstm_norm_w,
               rw_mu, rw_w0, rw_w_up, rw_a0, rw_a_up, rw_g_up, rw_k_k, rw_k_a, rw_r_k, rw_ln_w, rw_ln_b,
               w_out, norm_mlp_w, mlp_up, mlp_down)
    nfw = norm_f_w[None, :]
    p = _layer_params(0, *weights)
    y_p, st_p = _prompt_layer(x_prompt, p, nfw)
    y_s, st_s = _sample_layer(x_sample, state_mlstm_C[0], state_mlstm_n[0], state_mlstm_m[0],
                              state_mlstm_conv[0], state_rwkv_S[0], state_rwkv_shift[0], p, nfw)
    lead = lambda t: tuple(a[None] for a in t)
    return (y_p, y_s) + lead(st_p) + lead(st_s)
```

```python
import functools
import itertools
import math

import jax
import jax.numpy as jnp
from jax import lax
from jax.experimental import pallas as pl
from jax.experimental.pallas import tpu as pltpu

F32 = jnp.float32
BF16 = jnp.bfloat16

D_MODEL = 1024
HEAD_DIM = 64
N_HEADS = 8
PAIRS = N_HEADS // 2
GROUP_W = N_HEADS * HEAD_DIM
CONV_W = 4
D_LORA = 64
D_GATE = 128
D_FF = 4 * D_MODEL
EPS = 1e-6
GN_EPS = 64e-5
LANES = 128
SUBLANES = 8

Q0, K0, V0, O0 = 0, GROUP_W, 2 * GROUP_W, 3 * GROUP_W
GI0 = 4 * GROUP_W
GF0 = GI0 + LANES
RW0 = GF0 + LANES
RW_IN_W = 3 * GROUP_W + 2 * D_LORA + D_GATE
IN_WP = RW0 + RW_IN_W
RS_R, RS_K, RS_V = 0, GROUP_W, 2 * GROUP_W
RS_LORA = 3 * GROUP_W
RS_G = RS_LORA + 2 * D_LORA

TB = 256
LM = 128
LR = 64
RW_GROUP = 2
PROJ_TILE = 256
_DONE = object()
_STATE = object()
TM_OUT = 1024
FF_CHUNK = 1024
SAMPLE_UNROLL = 8
NEG = -1e30
W_SCALE = math.exp(-0.5)
VMEM_LIMIT = 56 * 1024 * 1024


def _dot(a, b):
    return jnp.dot(a.astype(BF16), b.astype(BF16), preferred_element_type=F32)


def _split3(x):
    h1 = x.astype(BF16)
    r1 = x - h1.astype(F32)
    h2 = r1.astype(BF16)
    h3 = (r1 - h2.astype(F32)).astype(BF16)
    return h1, h2, h3


def _dot01_l(m01, x):
    h1, h2, h3 = _split3(x)
    d = lambda h: jnp.dot(m01, h, preferred_element_type=F32)
    return d(h1) + d(h2) + d(h3)


def _log_sigmoid(x):
    return jnp.minimum(x, 0.0) - jnp.log(1.0 + jnp.exp(-jnp.abs(x)))


def _tanh(x):
    e = jnp.exp(-2.0 * jnp.abs(x))
    t = (1.0 - e) / (1.0 + e)
    return jnp.where(x < 0.0, -t, t)


def _iota(shape, dim):
    return lax.broadcasted_iota(jnp.int32, shape, dim)


def _rowsum(x):
    return jnp.sum(x, axis=1, keepdims=True)


def _pair_rowsum(x, lo):
    s_lo = _rowsum(jnp.where(lo, x, 0.0))
    s_hi = _rowsum(jnp.where(lo, 0.0, x))
    return jnp.where(lo, s_lo, s_hi)


def _stack2(x, lo):
    zero = jnp.zeros_like(x)
    return jnp.concatenate([jnp.where(lo, x, zero), jnp.where(lo, zero, x)], axis=0)


def _rms(x, w):
    ms = jnp.mean(x * x, axis=-1, keepdims=True)
    return x * lax.rsqrt(ms + EPS) * w


def _project(x_ref, nw_ref, wml_ref, wgate_ref, wrw_ref):
    xn = _rms(x_ref[...], nw_ref[...]).astype(BF16)
    return [jnp.dot(xn, w[...], preferred_element_type=F32) for w in (wml_ref, wgate_ref, wrw_ref)]


def _outmlp_kernel(x_ref, mix_ref, wout_ref, nmw_ref, up_ref, down_ref, nfw_ref, y_ref):
    x1 = x_ref[...] + jnp.dot(mix_ref[...].astype(BF16), wout_ref[...], preferred_element_type=F32)
    xn = _rms(x1, nmw_ref[...]).astype(BF16)
    acc = x1
    for c in range(D_FF // FF_CHUNK):
        sl = slice(c * FF_CHUNK, (c + 1) * FF_CHUNK)
        hid = jnp.maximum(jnp.dot(xn, up_ref[:, sl], preferred_element_type=F32), 0.0)
        acc = acc + jnp.dot((hid * hid).astype(BF16), down_ref[sl, :], preferred_element_type=F32)
    y_ref[...] = _rms(acc, nfw_ref[...])


def _outmlp(x2d, mix2d, w_out, norm_mlp_w, mlp_up, mlp_down, norm_f_w):
    rows = x2d.shape[0]
    tm = min(TM_OUT, rows)
    const = lambda shape: pl.BlockSpec(shape, lambda i: (0, 0), pipeline_mode=pl.Buffered(1))
    return pl.pallas_call(
        _outmlp_kernel,
        out_shape=jax.ShapeDtypeStruct((rows, D_MODEL), F32),
        grid=(rows // tm,),
        in_specs=[
            pl.BlockSpec((tm, D_MODEL), lambda i: (i, 0)),
            pl.BlockSpec((tm, D_MODEL), lambda i: (i, 0)),
            const((D_MODEL, D_MODEL)),
            const((1, D_MODEL)),
            const((D_MODEL, D_FF)),
            const((D_FF, D_MODEL)),
            const((1, D_MODEL)),
        ],
        out_specs=pl.BlockSpec((tm, D_MODEL), lambda i: (i, 0)),
        compiler_params=pltpu.CompilerParams(
            dimension_semantics=("arbitrary",), vmem_limit_bytes=VMEM_LIMIT),
        name="outmlp",
    )(x2d, mix2d, w_out, norm_mlp_w, mlp_up, mlp_down, norm_f_w)


def _rwkv_token_prep(lora_t, xg, w0, wup, a0, aup, gup):
    w_raw = w0 + jnp.dot(_tanh(lora_t).astype(BF16), wup, preferred_element_type=F32)
    wlog = -W_SCALE * jax.nn.sigmoid(w_raw)
    a = jax.nn.sigmoid(a0 + jnp.dot(lora_t.astype(BF16), aup, preferred_element_type=F32))
    g = jnp.dot(jax.nn.sigmoid(xg).astype(BF16), gup, preferred_element_type=F32)
    return wlog, a, g


def _mlstm_chunk(c, proj_ref, q_ref, k_ref, ib_ref, fb_ref, mlnw_ref, mix_ref, C_ref, n_ref, m_ref):
    rows = pl.ds(pl.multiple_of(c * LM, LM), LM)
    row = _iota((LM, LM), 0)
    col = _iota((LM, LM), 1)
    causal = col <= row
    tril = jnp.where(causal, 1.0, 0.0).astype(BF16)
    blockdiag = (row < HEAD_DIM) == (col < HEAD_DIM)
    lane = _iota((1, LANES), 1)
    lo = lane < HEAD_DIM

    i_tile = proj_ref[0, rows, GI0:GI0 + LANES] + ib_ref[...]
    f_tile = _log_sigmoid(proj_ref[0, rows, GF0:GF0 + LANES] + fb_ref[...])
    bc = _dot01_l(tril, f_tile)
    u = i_tile - bc
    ut = u.T
    yield

    nt = (((1,), (1,)), ((), ()))
    pairs = range(PAIRS)
    heads = range(N_HEADS)
    tile = lambda base, j: slice(base + j * LANES, base + (j + 1) * LANES)
    hmask = lambda h: lo if h % 2 == 0 else jnp.logical_not(lo)
    q_t = [q_ref[rows, tile(0, j)] for j in pairs]
    k_t = [k_ref[rows, tile(0, j)] for j in pairs]
    v_t = [proj_ref[0, rows, tile(V0, j)] for j in pairs]
    c_pair = [C_ref[0, j] for j in pairs]
    n_row = [n_ref[0, j:j + 1, :] for j in pairs]
    k_bf = [k.astype(BF16) for k in k_t]
    v_bf = [v.astype(BF16) for v in v_t]
    s = [lax.dot_general(jnp.where(hmask(h), q_t[h // 2], 0.0).astype(BF16), k_bf[h // 2], nt,
                         preferred_element_type=F32) for h in heads]
    q_c = [_dot(q_t[j], c_pair[j]) for j in pairs]
    k_tr = [k.T for k in k_t]
    yield

    m_prev_t = m_ref[0]
    ut8 = ut[0:N_HEADS, :]
    m_last_t = jnp.maximum(jnp.max(ut8, axis=1, keepdims=True), m_prev_t)
    ws_rows = jnp.exp(ut8 - m_last_t)
    dec_t = jnp.exp(m_prev_t - m_last_t)
    m_ref[0] = bc.T[0:N_HEADS, LM - 1:LM] + m_last_t

    m_prev = [m_prev_t[h:h + 1, 0:1] for h in heads]
    u_row = [ut[h:h + 1, :] for h in heads]
    bc_col = [_rowsum(jnp.where(lane == h, bc, 0.0)) for h in heads]
    m_col = [jnp.maximum(jnp.max(jnp.where(causal, u_row[h], NEG), axis=1, keepdims=True), m_prev[h])
             for h in heads]
    p = [s[h] * jnp.exp(jnp.where(causal, u_row[h] - m_col[h], NEG)) for h in heads]
    sc = [jnp.exp(m_prev[h] - m_col[h]) for h in heads]
    yield
    pv = [jnp.dot(p[h].astype(BF16), v_bf[h // 2], preferred_element_type=F32) for h in heads]
    yield
    h_parts = []
    for h in heads:
        q_n = _rowsum(jnp.where(hmask(h), q_t[h // 2] * n_row[h // 2], 0.0))
        den = _rowsum(p[h]) + sc[h] * q_n
        num = pv[h] + sc[h] * q_c[h // 2]
        h_parts.append(num / jnp.maximum(jnp.abs(den), jnp.exp(-(bc_col[h] + m_col[h]))))

    top = _iota((LM, 1), 0) < HEAD_DIM
    kw = [k_tr[j] * jnp.where(top, ws_rows[2 * j:2 * j + 1, :], ws_rows[2 * j + 1:2 * j + 2, :])
          for j in pairs]
    upd = [jnp.dot(kw[j].astype(BF16), v_bf[j], preferred_element_type=F32) for j in pairs]
    n_upd = [jnp.dot(ws_rows.astype(BF16), k_bf[j], preferred_element_type=F32) for j in pairs]
    yield
    for j in pairs:
        dec_j = jnp.where(lo, dec_t[2 * j:2 * j + 1, :], dec_t[2 * j + 1:2 * j + 2, :])
        C_ref[0, j] = c_pair[j] * dec_j + jnp.where(blockdiag, upd[j], 0.0)
        n_ref[0, j:j + 1, :] = n_row[j] * dec_j + jnp.where(lo, n_upd[j][2 * j:2 * j + 1, :],
                                                            n_upd[j][2 * j + 1:2 * j + 2, :])

    for j in pairs:
        h_til = jnp.where(lo, h_parts[2 * j], h_parts[2 * j + 1])
        hml = jax.nn.sigmoid(proj_ref[0, rows, tile(O0, j)]) * h_til
        ms = _pair_rowsum(hml * hml, lo) * (1.0 / HEAD_DIM)
        mix_ref[0, rows, tile(0, j)] = hml * lax.rsqrt(ms + EPS) * mlnw_ref[:, tile(0, j)]


def _neumann_inverse(ns):
    shape = ns[0].shape
    eye = jnp.where(_iota(shape, 0) == _iota(shape, 1), 1.0, 0.0)
    n_sq = int(math.log2(LR)) - 1
    qs = [n.astype(BF16) for n in ns]
    ts = [eye + n for n in ns]
    sq = [jnp.dot(q, q, preferred_element_type=F32) for q in qs]
    yield
    for _ in range(n_sq - 1):
        qs = [s.astype(BF16) for s in sq]
        both = [jnp.dot(q, jnp.concatenate([q, t.astype(BF16)], axis=1), preferred_element_type=F32)
                for q, t in zip(qs, ts)]
        sq = [b[:, 0:LANES] for b in both]
        ts = [t + b[:, LANES:2 * LANES] for t, b in zip(ts, both)]
        yield
    return [(t + jnp.dot(s.astype(BF16), t.astype(BF16), preferred_element_type=F32)).astype(BF16)
            for s, t in zip(sq, ts)]


def _rwkv_chunks(c, rs_ref, w0_ref, wup_ref, a0_ref, aup_ref, gup_ref, kk_ref, ka_ref, rk_ref,
                 lnw_ref, lnb_ref, mix_ref, S_ref):
    gl = RW_GROUP * LR
    rows = pl.ds(pl.multiple_of(c * gl, gl), gl)
    lane = _iota((1, LANES), 1)
    lo = lane < HEAD_DIM
    row = _iota((LANES, LANES), 0)
    col = _iota((LANES, LANES), 1)
    same = (row < LR) == (col < LR)
    mask_sl = jnp.logical_and(same, col < row)
    mask_li = jnp.logical_and(same, col <= row)
    mask_li2 = jnp.concatenate([mask_li, mask_li], axis=1)
    grow = _iota((gl, gl), 0)
    gcol = _iota((gl, gl), 1)
    tril = jnp.where(jnp.logical_and(gcol <= grow, gcol >= (grow // LR) * LR), 1.0, 0.0).astype(BF16)
    nt = (((1,), (1,)), ((), ()))
    pairs = range(PAIRS)
    items = [(ci, j) for ci in range(RW_GROUP) for j in pairs]
    tile = lambda base, j: slice(base + j * LANES, base + (j + 1) * LANES)
    crow = lambda ci: slice(ci * LR, (ci + 1) * LR)

    wlog, a, g = _rwkv_token_prep(rs_ref[rows, RS_LORA:RS_LORA + LANES], rs_ref[rows, RS_G:RS_G + D_GATE],
                                  w0_ref[...], wup_ref[...], a0_ref[...], aup_ref[...], gup_ref[...])
    lw = _dot01_l(tril, wlog)
    e_in = jnp.exp(lw)
    e_ex = jnp.exp(lw - wlog)
    e_ng = jnp.exp(-lw)

    r_g = [rs_ref[rows, tile(RS_R, j)] for j in pairs]
    kr_g = [rs_ref[rows, tile(RS_K, j)] for j in pairs]
    v_g = [rs_ref[rows, tile(RS_V, j)] for j in pairs]
    a_g = [a[:, tile(0, j)] for j in pairs]
    kk = [kr_g[j] * kk_ref[:, tile(0, j)] for j in pairs]
    kkn = [kk[j] * lax.rsqrt(jnp.maximum(_pair_rowsum(kk[j] * kk[j], lo), 1e-24)) for j in pairs]
    keff_g = [kr_g[j] * (1.0 + (a_g[j] - 1.0) * ka_ref[:, tile(0, j)]) for j in pairs]
    alpha_g = [-kkn[j] * e_ex[:, tile(0, j)] for j in pairs]
    rb_g = [r_g[j] * e_in[:, tile(0, j)] for j in pairs]
    beta_g = [kkn[j] * a_g[j] * e_ng[:, tile(0, j)] for j in pairs]
    ktl_g = [keff_g[j] * e_ng[:, tile(0, j)] for j in pairs]
    bonus_g = [_pair_rowsum(r_g[j] * keff_g[j] * rk_ref[:, tile(0, j)], lo) * v_g[j] for j in pairs]

    lhs, rhs, v2, bk, gamma, ar = {}, {}, {}, {}, {}, {}
    for it in items:
        ci, j = it
        cr = crow(ci)
        ar[it] = jnp.concatenate([alpha_g[j][cr], rb_g[j][cr]], axis=0).astype(BF16)
        lhs[it] = jnp.concatenate([_stack2(ar[it][0:LR], lo), _stack2(ar[it][LR:2 * LR], lo)], axis=0)
        beta_c, ktl_c = beta_g[j][cr].astype(BF16), ktl_g[j][cr].astype(BF16)
        rhs[it] = jnp.concatenate([beta_c, beta_c, ktl_c, ktl_c], axis=0)
        v2[it] = _stack2(v_g[j][cr].astype(BF16), lo)
        bk[it] = jnp.concatenate([_stack2(beta_c, lo), _stack2(ktl_c, lo)], axis=0)
        gamma[it] = e_in[(ci + 1) * LR - 1:(ci + 1) * LR, tile(0, j)]
    aa = {it: lax.dot_general(lhs[it], rhs[it], nt, preferred_element_type=F32) for it in items}
    yield
    a_ab = [jnp.where(mask_sl, aa[it][0:LANES, 0:LANES], 0.0) for it in items]
    a_ak = {it: jnp.where(mask_sl, aa[it][0:LANES, LANES:2 * LANES], 0.0).astype(BF16) for it in items}
    a_r = {it: jnp.where(mask_li2, aa[it][LANES:2 * LANES, :], 0.0).astype(BF16) for it in items}
    akv = {it: jnp.dot(a_ak[it], v2[it], preferred_element_type=F32) for it in items}
    yield
    t_list = yield from _neumann_inverse(a_ab)
    t_inv = dict(zip(items, t_list))
    yield _STATE

    s_pair = [S_ref[0, j] for j in pairs]
    for ci in range(RW_GROUP):
        its = [(ci, j) for j in pairs]
        xs = [lax.dot_general(ar[it], s_pair[it[1]].astype(BF16), nt, preferred_element_type=F32)
              for it in its]
        yield
        p2 = [_dot(t_inv[it], _stack2(xs[j][0:LR], lo) + akv[it]) for j, it in enumerate(its)]
        yield
        pv = [jnp.concatenate([p2[j].astype(BF16), v2[it]], axis=0) for j, it in enumerate(its)]
        y2 = [_dot(a_r[it], pv[j]) for j, it in enumerate(its)]
        upd = [lax.dot_general(pv[j], bk[it], (((0,), (0,)), ((), ())), preferred_element_type=F32)
               for j, it in enumerate(its)]
        s_pair = [(s_pair[j] + upd[j]) * gamma[it] for j, it in enumerate(its)]
        yield
        cr = crow(ci)
        for j in pairs:
            yb = xs[j][LR:2 * LR] + y2[j][0:LR] + y2[j][LR:2 * LR] + bonus_g[j][cr]
            mu = _pair_rowsum(yb, lo) * (1.0 / HEAD_DIM)
            dlt = yb - mu
            var = _pair_rowsum(dlt * dlt, lo) * (1.0 / HEAD_DIM)
            yn = dlt * lax.rsqrt(var + GN_EPS) * lnw_ref[:, tile(0, j)] + lnb_ref[:, tile(0, j)]
            out_rows = pl.ds(pl.multiple_of(c * gl + ci * LR, LR), LR)
            mix_ref[0, out_rows, tile(GROUP_W, j)] = yn * g[cr, tile(0, j)]
    for j in pairs:
        S_ref[0, j] = s_pair[j]


def _shift_rows(x, tail, j):
    rolled = pltpu.roll(x, j, axis=0)
    head = jnp.where(_iota((SUBLANES, 1), 0) < j, pltpu.roll(tail, j, axis=0), rolled[0:SUBLANES])
    return jnp.concatenate([head, rolled[SUBLANES:]], axis=0)


def _until_state(gen):
    for tok in gen:
        if tok is _STATE:
            return
        yield


def _drain(gen):
    for _ in gen:
        yield


def _weave(primary, *others):
    while next(primary, _DONE) is not _DONE:
        for g in others:
            next(g, _DONE)


def _proj_tiles(x_ref, nw_ref, w_refs, dst_ref):
    xn = _rms(x_ref[0], nw_ref[...]).astype(BF16)
    yield
    for w_ref, col0 in zip(w_refs, (0, GI0, RW0)):
        for c0 in range(0, w_ref.shape[1], PROJ_TILE):
            cw = min(PROJ_TILE, w_ref.shape[1] - c0)
            dst_ref[0, :, col0 + c0:col0 + c0 + cw] = jnp.dot(xn, w_ref[:, c0:c0 + cw],
                                                              preferred_element_type=F32)
            yield


def _mixer_kernel(nt, x_ref, nw_ref, wml_ref, wgate_ref, wrw_ref,
                  convw_ref, convb_ref, ib_ref, fb_ref, mlnw_ref, mu_ref, w0_ref, wup_ref,
                  a0_ref, aup_ref, gup_ref, kk_ref, ka_ref, rk_ref, lnw_ref, lnb_ref,
                  mix_ref, Cout_ref, n_ref, m_ref, Sout_ref, convt_ref, shiftt_ref,
                  proj_s, C_ref, S_ref, qk_tail_ref, rw_tail_ref, q_ref, k_ref, rs_ref):
    s = pl.program_id(0)
    t = jnp.maximum(s - 1, 0) % nt
    proj_ref = proj_s.at[pl.ds(1 - s % 2, 1)]
    proj_next = proj_s.at[pl.ds(s % 2, 1)]

    @pl.when(s == 0)
    def _():
        proj_s[...] = jnp.zeros_like(proj_s)

    @pl.when(t == 0)
    def _():
        C_ref[...] = jnp.zeros_like(C_ref)
        n_ref[...] = jnp.zeros_like(n_ref)
        m_ref[...] = jnp.zeros_like(m_ref)
        S_ref[...] = jnp.zeros_like(S_ref)
        qk_tail_ref[...] = jnp.zeros_like(qk_tail_ref)
        rw_tail_ref[...] = jnp.zeros_like(rw_tail_ref)

    u = proj_ref[0, :, Q0:Q0 + 2 * GROUP_W]
    qk_tail = qk_tail_ref[...]
    acc = convb_ref[...] + u * convw_ref[CONV_W - 1:CONV_W, :]
    for jw in range(CONV_W - 1):
        acc = acc + _shift_rows(u, qk_tail, CONV_W - 1 - jw) * convw_ref[jw:jw + 1, :]
    qk_tail_ref[...] = u[TB - SUBLANES:TB]
    convt_ref[0] = u[TB - SUBLANES:TB]
    qk = acc * jax.nn.sigmoid(acc)
    q_ref[...] = qk[:, 0:GROUP_W]
    k_ref[...] = qk[:, GROUP_W:2 * GROUP_W] * (HEAD_DIM ** -0.5)

    rw = proj_ref[0, :, RW0:RW0 + RW_IN_W]
    rs_ref[...] = rw + mu_ref[...] * (_shift_rows(rw, rw_tail_ref[...], 1) - rw)
    rw_tail_ref[...] = rw[TB - SUBLANES:TB]
    shiftt_ref[0] = rw[TB - SUBLANES:TB]

    assert RW_GROUP * LR == LM
    rwkv = lambda c: _rwkv_chunks(c, rs_ref, w0_ref, wup_ref, a0_ref, aup_ref, gup_ref, kk_ref, ka_ref, rk_ref,
                                  lnw_ref, lnb_ref, mix_ref, S_ref)
    mlstm = lambda c: _mlstm_chunk(c, proj_ref, q_ref, k_ref, ib_ref, fb_ref, mlnw_ref, mix_ref, C_ref, n_ref, m_ref)
    n_groups = TB // LM
    gens = [rwkv(c) for c in range(n_groups)]
    proj = _proj_tiles(x_ref, nw_ref, (wml_ref, wgate_ref, wrw_ref), proj_next)
    _weave(_until_state(gens[0]), proj, mlstm(0))
    for c in range(1, n_groups):
        _weave(_until_state(gens[c]), _drain(gens[c - 1]), proj)
        _weave(_drain(gens[c - 1]))
    _weave(_drain(gens[-1]), itertools.chain(*[mlstm(c) for c in range(1, n_groups)]), proj)
    _weave(proj)

    @pl.when(t == nt - 1)
    def _():
        for h in range(N_HEADS):
            blk = slice((h % 2) * HEAD_DIM, (h % 2 + 1) * HEAD_DIM)
            Cout_ref[0, h] = C_ref[0, h // 2, blk, blk]
            Sout_ref[0, h] = S_ref[0, h // 2, blk, blk]


def _mixer(x, p):
    bsz, seq, _ = x.shape
    nt = seq // TB
    nblk = bsz * nt
    row = lambda w: pl.BlockSpec((1, w), lambda s: (0, 0))
    mat = lambda r, w: pl.BlockSpec((r, w), lambda s: (0, 0))
    resident = lambda a: pl.BlockSpec(a.shape, lambda s: (0, 0), pipeline_mode=pl.Buffered(1))
    blk = lambda s: jnp.maximum(s - 1, 0)
    seq_of = lambda s: blk(s) // nt
    state4 = pl.BlockSpec((1, N_HEADS, HEAD_DIM, HEAD_DIM), lambda s: (seq_of(s), 0, 0, 0))
    state3 = pl.BlockSpec((1, SUBLANES, LANES), lambda s: (seq_of(s), 0, 0))
    tail = lambda w: pl.BlockSpec((1, SUBLANES, w), lambda s: (seq_of(s), 0, 0))
    return pl.pallas_call(
        functools.partial(_mixer_kernel, nt),
        out_shape=(
            jax.ShapeDtypeStruct((nblk, TB, 2 * GROUP_W), F32),
            jax.ShapeDtypeStruct((bsz, N_HEADS, HEAD_DIM, HEAD_DIM), F32),
            jax.ShapeDtypeStruct((bsz, SUBLANES, LANES), F32),
            jax.ShapeDtypeStruct((bsz, SUBLANES, LANES), F32),
            jax.ShapeDtypeStruct((bsz, N_HEADS, HEAD_DIM, HEAD_DIM), F32),
            jax.ShapeDtypeStruct((bsz, SUBLANES, 2 * GROUP_W), F32),
            jax.ShapeDtypeStruct((bsz, SUBLANES, RW_IN_W), F32),
        ),
        grid=(nblk + 1,),
        in_specs=[
            pl.BlockSpec((1, TB, D_MODEL), lambda s: (jnp.minimum(s, nblk - 1), 0, 0)),
            row(D_MODEL)] + [resident(w) for w in p["w_in_parts"]] + [
            mat(CONV_W, 2 * GROUP_W), row(2 * GROUP_W), row(LANES), row(LANES), row(GROUP_W),
            row(RW_IN_W), row(GROUP_W), mat(LANES, GROUP_W), row(GROUP_W), mat(LANES, GROUP_W),
            mat(D_GATE, GROUP_W), row(GROUP_W), row(GROUP_W), row(GROUP_W), row(GROUP_W), row(GROUP_W),
        ],
        out_specs=(
            pl.BlockSpec((1, TB, 2 * GROUP_W), lambda s: (blk(s), 0, 0)),
            state4, state3, state3, state4, tail(2 * GROUP_W), tail(RW_IN_W),
        ),
        scratch_shapes=[
            pltpu.VMEM((2, TB, IN_WP), F32),
            pltpu.VMEM((1, PAIRS, LANES, LANES), F32),
            pltpu.VMEM((1, PAIRS, LANES, LANES), F32),
            pltpu.VMEM((SUBLANES, 2 * GROUP_W), F32),
            pltpu.VMEM((SUBLANES, RW_IN_W), F32),
            pltpu.VMEM((TB, GROUP_W), F32),
            pltpu.VMEM((TB, GROUP_W), F32),
            pltpu.VMEM((TB, RW_IN_W), F32),
        ],
        compiler_params=pltpu.CompilerParams(
            dimension_semantics=("arbitrary",), vmem_limit_bytes=VMEM_LIMIT),
        name="mixer",
    )(x.reshape(nblk, TB, D_MODEL), p["norm_mix_w"], *p["w_in_parts"],
      p["conv_w"], p["conv_b"], p["ib_row"], p["fb_row"], p["mlnorm_w"], p["mu"], p["w0"],
      p["wup_p"], p["a0"], p["aup_p"], p["gup"], p["k_k"], p["k_a"], p["r_k"], p["ln_w"], p["ln_b"])


def _inproj_t_kernel(x_ref, nw_ref, wml_ref, wgate_ref, wrw_ref, o_ref, ot_ref):
    o_ml, o_gate, o_rw = _project(x_ref, nw_ref, wml_ref, wgate_ref, wrw_ref)
    for lo_col, hi_col, o in ((0, GI0, o_ml), (GI0, RW0, o_gate), (RW0, IN_WP, o_rw)):
        o_ref[:, lo_col:hi_col] = o
        ot_ref[lo_col:hi_col, :] = o.T


def _inproj_t(x2d, norm_w, w_in_parts):
    rows = x2d.shape[0]
    return pl.pallas_call(
        _inproj_t_kernel,
        out_shape=(jax.ShapeDtypeStruct((rows, IN_WP), F32), jax.ShapeDtypeStruct((IN_WP, rows), F32)),
        compiler_params=pltpu.CompilerParams(vmem_limit_bytes=VMEM_LIMIT),
        name="inproj_t",
    )(x2d, norm_w, *w_in_parts)


(PC_W0, PC_A0, PC_MLNW, PC_KK, PC_KA, PC_RK, PC_LNW, PC_LNB) = range(8)
PC_CONVB = CONV_W


def _sample_kernel(pt_ref, conv0_ref, shift0_ref, m0_ref, n0_ref, C0_ref, S0_ref,
                   p1024_ref, p512_ref, mu_ref, gb_ref, wup_ref, aup_ref, gup_ref,
                   hml_ref, yrw_ref, C_ref, n_ref, m_ref, S_ref,
                   wk_s, q_s, vr_s, y_s):
    h = pl.program_id(0)
    hrow = lambda base, n=HEAD_DIM: pl.ds(pl.multiple_of(base + h * n, n), n)
    pcol = lambda j: p512_ref[hrow(0), j:j + 1]
    colsum = lambda t: jnp.sum(t, axis=0, keepdims=True)

    def conv_silu(base):
        rows = hrow(base)
        acc = p1024_ref[rows, PC_CONVB:PC_CONVB + 1] + pt_ref[rows, :] * p1024_ref[rows, CONV_W - 1:CONV_W]
        for jw in range(CONV_W - 1):
            acc = acc + conv0_ref[jw, rows, :] * p1024_ref[rows, jw:jw + 1]
        return acc * jax.nn.sigmoid(acc)

    q = conv_silu(Q0)
    k = conv_silu(K0) * (HEAD_DIM ** -0.5)
    v = pt_ref[hrow(V0), :]
    i_g = pt_ref[pl.ds(GI0 + h, 1), :] + gb_ref[pl.ds(h, 1), 0:1]
    logf = _log_sigmoid(pt_ref[pl.ds(GF0 + h, 1), :] + gb_ref[pl.ds(h, 1), 1:2])
    m0 = m0_ref[pl.ds(h, 1), :]
    m_new = jnp.maximum(logf + m0, i_g)
    ws = jnp.exp(i_g - m_new)
    dec = jnp.exp(logf + m0 - m_new)
    wk = ws * k
    n_new = dec * n0_ref[0] + wk
    n_ref[0] = n_new
    m_ref[pl.ds(h, 1), :] = m_new
    wk_s[...] = wk
    q_s[...] = q

    def c_body(kk_, num):
        ck = dec * C0_ref[0, kk_] + wk_s[pl.ds(kk_, 1), :] * v
        C_ref[0, kk_] = ck
        return num + q_s[pl.ds(kk_, 1), :] * ck

    num = lax.fori_loop(0, HEAD_DIM, c_body, jnp.zeros((HEAD_DIM, pt_ref.shape[1]), F32), unroll=SAMPLE_UNROLL)
    den = colsum(q * n_new)
    hml = jax.nn.sigmoid(pt_ref[hrow(O0), :]) * (num / jnp.maximum(jnp.abs(den), jnp.exp(-m_new)))
    hml_ref[...] = hml * lax.rsqrt(colsum(hml * hml) * (1.0 / HEAD_DIM) + EPS) * pcol(PC_MLNW)

    def shifted(base, n=HEAD_DIM):
        rw = pt_ref[hrow(RW0 + base, n) if n == HEAD_DIM else pl.ds(RW0 + base, n), :]
        rows = hrow(base, n) if n == HEAD_DIM else pl.ds(base, n)
        return rw + mu_ref[rows, :] * (shift0_ref[rows, :] - rw)

    r, kr, vr = shifted(RS_R), shifted(RS_K), shifted(RS_V)
    lora = shifted(RS_LORA, LANES)
    xg = shifted(RS_G, D_GATE)
    hw = hrow(0)
    w_raw = pcol(PC_W0) + jnp.dot(wup_ref[hw, :], jnp.tanh(lora).astype(BF16), preferred_element_type=F32)
    w = jnp.exp(-W_SCALE * jax.nn.sigmoid(w_raw))
    a = jax.nn.sigmoid(pcol(PC_A0) + jnp.dot(aup_ref[hw, :], lora.astype(BF16), preferred_element_type=F32))
    g = jnp.dot(gup_ref[hw, :], jax.nn.sigmoid(xg).astype(BF16), preferred_element_type=F32)
    kk = kr * pcol(PC_KK)
    kkn = kk / jnp.maximum(jnp.sqrt(colsum(kk * kk)), 1e-12)
    keff = kr * (1.0 + (a - 1.0) * pcol(PC_KA))
    nkk = -kkn
    ka = kkn * a
    vr_s[...] = vr

    def s_body(v_, carry):
        sv = S0_ref[0, v_]
        sn = sv * w + colsum(sv * nkk) * ka + vr_s[pl.ds(v_, 1), :] * keff
        S_ref[0, v_] = sn
        y_s[pl.ds(v_, 1), :] = colsum(sn * r)
        return carry

    lax.fori_loop(0, HEAD_DIM, s_body, 0, unroll=SAMPLE_UNROLL)
    yb = y_s[...] + colsum(r * keff * pcol(PC_RK)) * vr
    dlt = yb - colsum(yb) * (1.0 / HEAD_DIM)
    var = colsum(dlt * dlt) * (1.0 / HEAD_DIM)
    yrw_ref[...] = (dlt * lax.rsqrt(var + GN_EPS) * pcol(PC_LNW) + pcol(PC_LNB)) * g


def _sample(proj_t, conv0_t, shift0_t, m0_t, n0_t, c0_t, s0_t, p):
    bsz = proj_t.shape[1]
    whole = lambda a: pl.BlockSpec(a.shape, lambda h: (0,) * a.ndim)
    state4 = pl.BlockSpec((1, HEAD_DIM, HEAD_DIM, bsz), lambda h: (h, 0, 0, 0))
    head2 = pl.BlockSpec((HEAD_DIM, bsz), lambda h: (h, 0))
    head3 = pl.BlockSpec((1, HEAD_DIM, bsz), lambda h: (h, 0, 0))
    consts = [p["p1024"], p["p512"], p["mu_col"], p["gate_b"], p["wup_t"], p["aup_t"], p["gup_t"]]
    return pl.pallas_call(
        _sample_kernel,
        out_shape=(
            jax.ShapeDtypeStruct((GROUP_W, bsz), F32),
            jax.ShapeDtypeStruct((GROUP_W, bsz), F32),
            jax.ShapeDtypeStruct((N_HEADS, HEAD_DIM, HEAD_DIM, bsz), F32),
            jax.ShapeDtypeStruct((N_HEADS, HEAD_DIM, bsz), F32),
            jax.ShapeDtypeStruct((N_HEADS, bsz), F32),
            jax.ShapeDtypeStruct((N_HEADS, HEAD_DIM, HEAD_DIM, bsz), F32),
        ),
        grid=(N_HEADS,),
        in_specs=[whole(proj_t), whole(conv0_t), whole(shift0_t), whole(m0_t), head3, state4, state4]
                 + [whole(c) for c in consts],
        out_specs=(head2, head2, state4, head3, pl.BlockSpec((N_HEADS, bsz), lambda h: (0, 0)), state4),
        scratch_shapes=[pltpu.VMEM((HEAD_DIM, bsz), F32) for _ in range(4)],
        compiler_params=pltpu.CompilerParams(
            dimension_semantics=("arbitrary",), vmem_limit_bytes=VMEM_LIMIT),
        name="sample",
    )(proj_t, conv0_t, shift0_t, m0_t, n0_t, c0_t, s0_t, *consts)


def _layer_params(l, norm_mix_w, w_in, mlstm_conv_w, mlstm_conv_b, mlstm_i_b, mlstm_f_b, mlstm_norm_w,
                  rw_mu, rw_w0, rw_w_up, rw_a0, rw_a_up, rw_g_up, rw_k_k, rw_k_a, rw_r_k, rw_ln_w, rw_ln_b,
                  w_out, norm_mlp_w, mlp_up, mlp_down):
    ml_in = 4 * GROUP_W
    wi = w_in[l]
    zpad = jnp.zeros((D_MODEL, LANES - N_HEADS), BF16)
    w_in_parts = (
        wi[:, :ml_in].astype(BF16),
        jnp.concatenate([wi[:, ml_in:ml_in + N_HEADS].astype(BF16), zpad,
                         wi[:, ml_in + N_HEADS:ml_in + 2 * N_HEADS].astype(BF16), zpad], axis=1),
        wi[:, ml_in + 2 * N_HEADS:].astype(BF16),
    )
    gate_row = lambda b: jnp.concatenate([b, jnp.zeros((LANES - N_HEADS,), F32)])[None, :]
    zl = jnp.zeros((D_LORA, GROUP_W), F32)
    return {
        "norm_mix_w": norm_mix_w[l][None, :],
        "w_in_parts": w_in_parts,
        "conv_w": mlstm_conv_w[l], "conv_b": mlstm_conv_b[l][None, :],
        "ib_row": gate_row(mlstm_i_b[l]), "fb_row": gate_row(mlstm_f_b[l]),
        "gate_b": jnp.stack([mlstm_i_b[l], mlstm_f_b[l]], axis=1),
        "p1024": jnp.concatenate([mlstm_conv_w[l].T, mlstm_conv_b[l][:, None],
                                  jnp.zeros((2 * GROUP_W, SUBLANES - CONV_W - 1), F32)], axis=1),
        "p512": jnp.stack([rw_w0[l], rw_a0[l], mlstm_norm_w[l], rw_k_k[l], rw_k_a[l], rw_r_k[l].reshape(GROUP_W),
                           rw_ln_w[l], rw_ln_b[l]], axis=1),
        "mu_col": rw_mu[l][:, None],
        "mlnorm_w": mlstm_norm_w[l][None, :],
        "mu": rw_mu[l][None, :], "w0": rw_w0[l][None, :], "a0": rw_a0[l][None, :],
        "wup_p": jnp.concatenate([rw_w_up[l], zl], axis=0).astype(BF16),
        "aup_p": jnp.concatenate([zl, rw_a_up[l]], axis=0).astype(BF16),
        "gup": rw_g_up[l].astype(BF16),
        "wup_t": jnp.concatenate([rw_w_up[l], zl], axis=0).T.astype(BF16),
        "aup_t": jnp.concatenate([zl, rw_a_up[l]], axis=0).T.astype(BF16),
        "gup_t": rw_g_up[l].T.astype(BF16),
        "k_k": rw_k_k[l][None, :], "k_a": rw_k_a[l][None, :], "r_k": rw_r_k[l].reshape(1, GROUP_W),
        "ln_w": rw_ln_w[l][None, :], "ln_b": rw_ln_b[l][None, :],
        "w_out": w_out[l].astype(BF16), "norm_mlp_w": norm_mlp_w[l][None, :],
        "mlp_up": mlp_up[l].astype(BF16), "mlp_down": mlp_down[l].astype(BF16),
    }


def _prompt_layer(x, p, norm_f_w):
    bsz, seq, _ = x.shape
    x2d = x.reshape(bsz * seq, D_MODEL)
    mix, c_st, n_st, m_st, s_st, conv_tail, shift_tail = _mixer(x, p)
    y = _outmlp(x2d, mix.reshape(bsz * seq, D_MODEL), p["w_out"], p["norm_mlp_w"], p["mlp_up"],
                p["mlp_down"], norm_f_w)
    c_new = c_st
    n_new = n_st[:, :PAIRS, :].reshape(bsz, N_HEADS, HEAD_DIM)
    m_new = m_st[:, :, 0]
    conv_new = conv_tail[:, SUBLANES - (CONV_W - 1):, :]
    s_new = s_st
    shift_new = shift_tail[:, SUBLANES - 1:, :]
    return y.reshape(bsz, seq, D_MODEL), (c_new, n_new, m_new, conv_new, s_new, shift_new)


def _sample_layer(x, c0, n0, m0, conv0, s0, shift0, p, norm_f_w):
    bsz = x.shape[0]
    x2d = x.reshape(bsz, D_MODEL)
    proj, proj_t = _inproj_t(x2d, p["norm_mix_w"], p["w_in_parts"])
    hml_t, yrw_t, c_t, n_t, m_t, s_t = _sample(
        proj_t, jnp.transpose(conv0, (1, 2, 0)), shift0[:, 0, :].T, m0.T, jnp.transpose(n0, (1, 2, 0)),
        jnp.transpose(c0, (1, 2, 3, 0)), jnp.transpose(s0, (1, 2, 3, 0)), p)
    mix = jnp.concatenate([hml_t, yrw_t], axis=0).T
    y = _outmlp(x2d, mix, p["w_out"], p["norm_mlp_w"], p["mlp_up"], p["mlp_down"], norm_f_w)
    conv_new = jnp.concatenate([conv0[:, 1:], proj[:, None, Q0:Q0 + 2 * GROUP_W]], axis=1)
    shift_new = proj[:, None, RW0:RW0 + RW_IN_W]
    return y.reshape(bsz, 1, D_MODEL), (
        jnp.transpose(c_t, (3, 0, 1, 2)), jnp.transpose(n_t, (2, 0, 1)), m_t.T, conv_new,
        jnp.transpose(s_t, (3, 0, 1, 2)), shift_new)


def kernel(x_prompt, x_sample, state_mlstm_C, state_mlstm_n, state_mlstm_m, state_mlstm_conv, state_rwkv_S, state_rwkv_shift, norm_mix_w, w_in, mlstm_conv_w, mlstm_conv_b, mlstm_i_b, mlstm_f_b, mlstm_norm_w, rw_mu, rw_w0, rw_w_up, rw_a0, rw_a_up, rw_g_up, rw_k_k, rw_k_a, rw_r_k, rw_ln_w, rw_ln_b, w_out, norm_mlp_w, mlp_up, mlp_down, norm_f_w):
    depth = w_in.shape[0]
    assert depth == 1, "the final RMSNorm is fused into the layer's output kernel"
    weights = (norm_mix_w, w_in, mlstm_conv_w, mlstm_conv_b, mlstm_i_b, mlstm_f_b, mlstm_norm_w,
               rw_mu, rw_w0, rw_w_up, rw_a0, rw_a_up, rw_g_up, rw_k_k, rw_k_a, rw_r_k, rw_ln_w, rw_ln_b,
               w_out, norm_mlp_w, mlp_up, mlp_down)
    nfw = norm_f_w[None, :]
    p = _layer_params(0, *weights)
    y_p, st_p = _prompt_layer(x_prompt, p, nfw)
    y_s, st_s = _sample_layer(x_sample, state_mlstm_C[0], state_mlstm_n[0], state_mlstm_m[0],
                              state_mlstm_conv[0], state_rwkv_S[0], state_rwkv_shift[0], p, nfw)
    lead = lambda t: tuple(a[None] for a in t)
    return (y_p, y_s) + lead(st_p) + lead(st_s)
```

```python
import functools
import itertools
import math

import jax
import jax.numpy as jnp
from jax import lax
from jax.experimental import pallas as pl
from jax.experimental.pallas import tpu as pltpu

F32 = jnp.float32
BF16 = jnp.bfloat16

D_MODEL = 1024
HEAD_DIM = 64
N_HEADS = 8
PAIRS = N_HEADS // 2
GROUP_W = N_HEADS * HEAD_DIM
CONV_W = 4
D_LORA = 64
D_GATE = 128
D_FF = 4 * D_MODEL
EPS = 1e-6
GN_EPS = 64e-5
LANES = 128
SUBLANES = 8

Q0, K0, V0, O0 = 0, GROUP_W, 2 * GROUP_W, 3 * GROUP_W
GI0 = 4 * GROUP_W
GF0 = GI0 + LANES
RW0 = GF0 + LANES
RW_IN_W = 3 * GROUP_W + 2 * D_LORA + D_GATE
IN_WP = RW0 + RW_IN_W
RS_R, RS_K, RS_V = 0, GROUP_W, 2 * GROUP_W
RS_LORA = 3 * GROUP_W
RS_G = RS_LORA + 2 * D_LORA

TB = 256
LM = 128
LR = 64
RW_GROUP = 2
PROJ_TILE = 256
_DONE = object()
_STATE = object()
TM_OUT = 1024
FF_CHUNK = 1024
SAMPLE_UNROLL = 8
NEG = -1e30
W_SCALE = math.exp(-0.5)
VMEM_LIMIT = 56 * 1024 * 1024


def _dot(a, b):
    return jnp.dot(a.astype(BF16), b.astype(BF16), preferred_element_type=F32)


def _split3(x):
    h1 = x.astype(BF16)
    r1 = x - h1.astype(F32)
    h2 = r1.astype(BF16)
    h3 = (r1 - h2.astype(F32)).astype(BF16)
    return h1, h2, h3


def _dot01_l(m01, x):
    h1, h2, h3 = _split3(x)
    d = lambda h: jnp.dot(m01, h, preferred_element_type=F32)
    return d(h1) + d(h2) + d(h3)


def _log_sigmoid(x):
    return jnp.minimum(x, 0.0) - jnp.log(1.0 + jnp.exp(-jnp.abs(x)))


def _tanh(x):
    e = jnp.exp(-2.0 * jnp.abs(x))
    t = (1.0 - e) / (1.0 + e)
    return jnp.where(x < 0.0, -t, t)


def _iota(shape, dim):
    return lax.broadcasted_iota(jnp.int32, shape, dim)


def _rowsum(x):
    return jnp.sum(x, axis=1, keepdims=True)


def _pair_rowsum(x, lo):
    s_lo = _rowsum(jnp.where(lo, x, 0.0))
    s_hi = _rowsum(jnp.where(lo, 0.0, x))
    return jnp.where(lo, s_lo, s_hi)


def _stack2(x, lo):
    zero = jnp.zeros_like(x)
    return jnp.concatenate([jnp.where(lo, x, zero), jnp.where(lo, zero, x)], axis=0)


def _rms(x, w):
    ms = jnp.mean(x * x, axis=-1, keepdims=True)
    return x * lax.rsqrt(ms + EPS) * w


def _project(x_ref, nw_ref, wml_ref, wgate_ref, wrw_ref):
    xn = _rms(x_ref[...], nw_ref[...]).astype(BF16)
    return [jnp.dot(xn, w[...], preferred_element_type=F32) for w in (wml_ref, wgate_ref, wrw_ref)]


def _outmlp_kernel(x_ref, mix_ref, wout_ref, nmw_ref, up_ref, down_ref, nfw_ref, y_ref):
    x1 = x_ref[...] + jnp.dot(mix_ref[...].astype(BF16), wout_ref[...], preferred_element_type=F32)
    xn = _rms(x1, nmw_ref[...]).astype(BF16)
    acc = x1
    for c in range(D_FF // FF_CHUNK):
        sl = slice(c * FF_CHUNK, (c + 1) * FF_CHUNK)
        hid = jnp.maximum(jnp.dot(xn, up_ref[:, sl], preferred_element_type=F32), 0.0)
        acc = acc + jnp.dot((hid * hid).astype(BF16), down_ref[sl, :], preferred_element_type=F32)
    y_ref[...] = _rms(acc, nfw_ref[...])


def _outmlp(x2d, mix2d, w_out, norm_mlp_w, mlp_up, mlp_down, norm_f_w):
    rows = x2d.shape[0]
    tm = min(TM_OUT, rows)
    const = lambda shape: pl.BlockSpec(shape, lambda i: (0, 0), pipeline_mode=pl.Buffered(1))
    return pl.pallas_call(
        _outmlp_kernel,
        out_shape=jax.ShapeDtypeStruct((rows, D_MODEL), F32),
        grid=(rows // tm,),
        in_specs=[
            pl.BlockSpec((tm, D_MODEL), lambda i: (i, 0)),
            pl.BlockSpec((tm, D_MODEL), lambda i: (i, 0)),
            const((D_MODEL, D_MODEL)),
            const((1, D_MODEL)),
            const((D_MODEL, D_FF)),
            const((D_FF, D_MODEL)),
            const((1, D_MODEL)),
        ],
        out_specs=pl.BlockSpec((tm, D_MODEL), lambda i: (i, 0)),
        compiler_params=pltpu.CompilerParams(
            dimension_semantics=("arbitrary",), vmem_limit_bytes=VMEM_LIMIT),
        name="outmlp",
    )(x2d, mix2d, w_out, norm_mlp_w, mlp_up, mlp_down, norm_f_w)


def _rwkv_token_prep(lora_t, xg, w0, wup, a0, aup, gup):
    w_raw = w0 + jnp.dot(_tanh(lora_t).astype(BF16), wup, preferred_element_type=F32)
    wlog = -W_SCALE * jax.nn.sigmoid(w_raw)
    a = jax.nn.sigmoid(a0 + jnp.dot(lora_t.astype(BF16), aup, preferred_element_type=F32))
    g = jnp.dot(jax.nn.sigmoid(xg).astype(BF16), gup, preferred_element_type=F32)
    return wlog, a, g


def _mlstm_chunk(c, proj_ref, q_ref, k_ref, ib_ref, fb_ref, mlnw_ref, mix_ref, C_ref, n_ref, m_ref):
    rows = pl.ds(pl.multiple_of(c * LM, LM), LM)
    row = _iota((LM, LM), 0)
    col = _iota((LM, LM), 1)
    causal = col <= row
    tril = jnp.where(causal, 1.0, 0.0).astype(BF16)
    blockdiag = (row < HEAD_DIM) == (col < HEAD_DIM)
    lane = _iota((1, LANES), 1)
    lo = lane < HEAD_DIM

    i_tile = proj_ref[0, rows, GI0:GI0 + LANES] + ib_ref[...]
    f_tile = _log_sigmoid(proj_ref[0, rows, GF0:GF0 + LANES] + fb_ref[...])
    bc = _dot01_l(tril, f_tile)
    u = i_tile - bc
    ut = u.T
    yield

    nt = (((1,), (1,)), ((), ()))
    pairs = range(PAIRS)
    heads = range(N_HEADS)
    tile = lambda base, j: slice(base + j * LANES, base + (j + 1) * LANES)
    hmask = lambda h: lo if h % 2 == 0 else jnp.logical_not(lo)
    q_t = [q_ref[rows, tile(0, j)] for j in pairs]
    k_t = [k_ref[rows, tile(0, j)] for j in pairs]
    v_t = [proj_ref[0, rows, tile(V0, j)] for j in pairs]
    c_pair = [C_ref[0, j] for j in pairs]
    n_row = [n_ref[0, j:j + 1, :] for j in pairs]
    k_bf = [k.astype(BF16) for k in k_t]
    v_bf = [v.astype(BF16) for v in v_t]
    s = [lax.dot_general(jnp.where(hmask(h), q_t[h // 2], 0.0).astype(BF16), k_bf[h // 2], nt,
                         preferred_element_type=F32) for h in heads]
    q_c = [_dot(q_t[j], c_pair[j]) for j in pairs]
    k_tr = [k.T for k in k_t]
    yield

    m_prev_t = m_ref[0]
    ut8 = ut[0:N_HEADS, :]
    m_last_t = jnp.maximum(jnp.max(ut8, axis=1, keepdims=True), m_prev_t)
    ws_rows = jnp.exp(ut8 - m_last_t)
    dec_t = jnp.exp(m_prev_t - m_last_t)
    m_ref[0] = bc.T[0:N_HEADS, LM - 1:LM] + m_last_t

    m_prev = [m_prev_t[h:h + 1, 0:1] for h in heads]
    u_row = [ut[h:h + 1, :] for h in heads]
    bc_col = [_rowsum(jnp.where(lane == h, bc, 0.0)) for h in heads]
    m_col = [jnp.maximum(jnp.max(jnp.where(causal, u_row[h], NEG), axis=1, keepdims=True), m_prev[h])
             for h in heads]
    p = [s[h] * jnp.exp(jnp.where(causal, u_row[h] - m_col[h], NEG)) for h in heads]
    sc = [jnp.exp(m_prev[h] - m_col[h]) for h in heads]
    yield
    pv = [jnp.dot(p[h].astype(BF16), v_bf[h // 2], preferred_element_type=F32) for h in heads]
    yield
    h_parts = []
    for h in heads:
        q_n = _rowsum(jnp.where(hmask(h), q_t[h // 2] * n_row[h // 2], 0.0))
        den = _rowsum(p[h]) + sc[h] * q_n
        num = pv[h] + sc[h] * q_c[h // 2]
        h_parts.append(num / jnp.maximum(jnp.abs(den), jnp.exp(-(bc_col[h] + m_col[h]))))

    top = _iota((LM, 1), 0) < HEAD_DIM
    kw = [k_tr[j] * jnp.where(top, ws_rows[2 * j:2 * j + 1, :], ws_rows[2 * j + 1:2 * j + 2, :])
          for j in pairs]
    upd = [jnp.dot(kw[j].astype(BF16), v_bf[j], preferred_element_type=F32) for j in pairs]
    n_upd = [jnp.dot(ws_rows.astype(BF16), k_bf[j], preferred_element_type=F32) for j in pairs]
    yield
    for j in pairs:
        dec_j = jnp.where(lo, dec_t[2 * j:2 * j + 1, :], dec_t[2 * j + 1:2 * j + 2, :])
        C_ref[0, j] = c_pair[j] * dec_j + jnp.where(blockdiag, upd[j], 0.0)
        n_ref[0, j:j + 1, :] = n_row[j] * dec_j + jnp.where(lo, n_upd[j][2 * j:2 * j + 1, :],
                                                            n_upd[j][2 * j + 1:2 * j + 2, :])

    for j in pairs:
        h_til = jnp.where(lo, h_parts[2 * j], h_parts[2 * j + 1])
        hml = jax.nn.sigmoid(proj_ref[0, rows, tile(O0, j)]) * h_til
        ms = _pair_rowsum(hml * hml, lo) * (1.0 / HEAD_DIM)
        mix_ref[0, rows, tile(0, j)] = hml * lax.rsqrt(ms + EPS) * mlnw_ref[:, tile(0, j)]


def _neumann_inverse(ns):
    shape = ns[0].shape
    eye = jnp.where(_iota(shape, 0) == _iota(shape, 1), 1.0, 0.0)
    n_sq = int(math.log2(LR)) - 1
    qs = [n.astype(BF16) for n in ns]
    ts = [eye + n for n in ns]
    sq = [jnp.dot(q, q, preferred_element_type=F32) for q in qs]
    yield
    for _ in range(n_sq - 1):
        qs = [s.astype(BF16) for s in sq]
        both = [jnp.dot(q, jnp.concatenate([q, t.astype(BF16)], axis=1), preferred_element_type=F32)
                for q, t in zip(qs, ts)]
        sq = [b[:, 0:LANES] for b in both]
        ts = [t + b[:, LANES:2 * LANES] for t, b in zip(ts, both)]
        yield
    return [(t + jnp.dot(s.astype(BF16), t.astype(BF16), preferred_element_type=F32)).astype(BF16)
            for s, t in zip(sq, ts)]


def _rwkv_chunks(c, rs_ref, w0_ref, wup_ref, a0_ref, aup_ref, gup_ref, kk_ref, ka_ref, rk_ref,
                 lnw_ref, lnb_ref, mix_ref, S_ref):
    gl = RW_GROUP * LR
    rows = pl.ds(pl.multiple_of(c * gl, gl), gl)
    lane = _iota((1, LANES), 1)
    lo = lane < HEAD_DIM
    row = _iota((LANES, LANES), 0)
    col = _iota((LANES, LANES), 1)
    same = (row < LR) == (col < LR)
    mask_sl = jnp.logical_and(same, col < row)
    mask_li = jnp.logical_and(same, col <= row)
    mask_li2 = jnp.concatenate([mask_li, mask_li], axis=1)
    grow = _iota((gl, gl), 0)
    gcol = _iota((gl, gl), 1)
    tril = jnp.where(jnp.logical_and(gcol <= grow, gcol >= (grow // LR) * LR), 1.0, 0.0).astype(BF16)
    nt = (((1,), (1,)), ((), ()))
    pairs = range(PAIRS)
    items = [(ci, j) for ci in range(RW_GROUP) for j in pairs]
    tile = lambda base, j: slice(base + j * LANES, base + (j + 1) * LANES)
    crow = lambda ci: slice(ci * LR, (ci + 1) * LR)

    wlog, a, g = _rwkv_token_prep(rs_ref[rows, RS_LORA:RS_LORA + LANES], rs_ref[rows, RS_G:RS_G + D_GATE],
                                  w0_ref[...], wup_ref[...], a0_ref[...], aup_ref[...], gup_ref[...])
    lw = _dot01_l(tril, wlog)
    e_in = jnp.exp(lw)
    e_ex = jnp.exp(lw - wlog)
    e_ng = jnp.exp(-lw)

    r_g = [rs_ref[rows, tile(RS_R, j)] for j in pairs]
    kr_g = [rs_ref[rows, tile(RS_K, j)] for j in pairs]
    v_g = [rs_ref[rows, tile(RS_V, j)] for j in pairs]
    a_g = [a[:, tile(0, j)] for j in pairs]
    kk = [kr_g[j] * kk_ref[:, tile(0, j)] for j in pairs]
    kkn = [kk[j] * lax.rsqrt(jnp.maximum(_pair_rowsum(kk[j] * kk[j], lo), 1e-24)) for j in pairs]
    keff_g = [kr_g[j] * (1.0 + (a_g[j] - 1.0) * ka_ref[:, tile(0, j)]) for j in pairs]
    alpha_g = [-kkn[j] * e_ex[:, tile(0, j)] for j in pairs]
    rb_g = [r_g[j] * e_in[:, tile(0, j)] for j in pairs]
    beta_g = [kkn[j] * a_g[j] * e_ng[:, tile(0, j)] for j in pairs]
    ktl_g = [keff_g[j] * e_ng[:, tile(0, j)] for j in pairs]
    bonus_g = [_pair_rowsum(r_g[j] * keff_g[j] * rk_ref[:, tile(0, j)], lo) * v_g[j] for j in pairs]

    lhs, rhs, v2, bk, gamma, ar = {}, {}, {}, {}, {}, {}
    for it in items:
        ci, j = it
        cr = crow(ci)
        ar[it] = jnp.concatenate([alpha_g[j][cr], rb_g[j][cr]], axis=0).astype(BF16)
        lhs[it] = jnp.concatenate([_stack2(ar[it][0:LR], lo), _stack2(ar[it][LR:2 * LR], lo)], axis=0)
        beta_c, ktl_c = beta_g[j][cr].astype(BF16), ktl_g[j][cr].astype(BF16)
        rhs[it] = jnp.concatenate([beta_c, beta_c, ktl_c, ktl_c], axis=0)
        v2[it] = _stack2(v_g[j][cr].astype(BF16), lo)
        bk[it] = jnp.concatenate([_stack2(beta_c, lo), _stack2(ktl_c, lo)], axis=0)
        gamma[it] = e_in[(ci + 1) * LR - 1:(ci + 1) * LR, tile(0, j)]
    aa = {it: lax.dot_general(lhs[it], rhs[it], nt, preferred_element_type=F32) for it in items}
    yield
    a_ab = [jnp.where(mask_sl, aa[it][0:LANES, 0:LANES], 0.0) for it in items]
    a_ak = {it: jnp.where(mask_sl, aa[it][0:LANES, LANES:2 * LANES], 0.0).astype(BF16) for it in items}
    a_r = {it: jnp.where(mask_li2, aa[it][LANES:2 * LANES, :], 0.0).astype(BF16) for it in items}
    akv = {it: jnp.dot(a_ak[it], v2[it], preferred_element_type=F32) for it in items}
    yield
    t_list = yield from _neumann_inverse(a_ab)
    t_inv = dict(zip(items, t_list))
    yield _STATE

    s_pair = [S_ref[0, j] for j in pairs]
    for ci in range(RW_GROUP):
        its = [(ci, j) for j in pairs]
        xs = [lax.dot_general(ar[it], s_pair[it[1]].astype(BF16), nt, preferred_element_type=F32)
              for it in its]
        yield
        p2 = [_dot(t_inv[it], _stack2(xs[j][0:LR], lo) + akv[it]) for j, it in enumerate(its)]
        yield
        pv = [jnp.concatenate([p2[j].astype(BF16), v2[it]], axis=0) for j, it in enumerate(its)]
        y2 = [_dot(a_r[it], pv[j]) for j, it in enumerate(its)]
        upd = [lax.dot_general(pv[j], bk[it], (((0,), (0,)), ((), ())), preferred_element_type=F32)
               for j, it in enumerate(its)]
        s_pair = [(s_pair[j] + upd[j]) * gamma[it] for j, it in enumerate(its)]
        yield
        cr = crow(ci)
        for j in pairs:
            yb = xs[j][LR:2 * LR] + y2[j][0:LR] + y2[j][LR:2 * LR] + bonus_g[j][cr]
            mu = _pair_rowsum(yb, lo) * (1.0 / HEAD_DIM)
            dlt = yb - mu
            var = _pair_rowsum(dlt * dlt, lo) * (1.0 / HEAD_DIM)
            yn = dlt * lax.rsqrt(var + GN_EPS) * lnw_ref[:, tile(0, j)] + lnb_ref[:, tile(0, j)]
            out_rows = pl.ds(pl.multiple_of(c * gl + ci * LR, LR), LR)
            mix_ref[0, out_rows, tile(GROUP_W, j)] = yn * g[cr, tile(0, j)]
    for j in pairs:
        S_ref[0, j] = s_pair[j]


def _shift_rows(x, tail, j):
    rolled = pltpu.roll(x, j, axis=0)
    head = jnp.where(_iota((SUBLANES, 1), 0) < j, pltpu.roll(tail, j, axis=0), rolled[0:SUBLANES])
    return jnp.concatenate([head, rolled[SUBLANES:]], axis=0)


def _until_state(gen):
    for tok in gen:
        if tok is _STATE:
            return
        yield


def _drain(gen):
    for _ in gen:
        yield


def _weave(primary, *others):
    while next(primary, _DONE) is not _DONE:
        for g in others:
            next(g, _DONE)


def _proj_tiles(x_ref, nw_ref, w_refs, dst_ref):
    xn = _rms(x_ref[0], nw_ref[...]).astype(BF16)
    yield
    for w_ref, col0 in zip(w_refs, (0, GI0, RW0)):
        for c0 in range(0, w_ref.shape[1], PROJ_TILE):
            cw = min(PROJ_TILE, w_ref.shape[1] - c0)
            dst_ref[0, :, col0 + c0:col0 + c0 + cw] = jnp.dot(xn, w_ref[:, c0:c0 + cw],
                                                              preferred_element_type=F32)
            yield


def _mixer_kernel(nt, x_ref, nw_ref, wml_ref, wgate_ref, wrw_ref,
                  convw_ref, convb_ref, ib_ref, fb_ref, mlnw_ref, mu_ref, w0_ref, wup_ref,
                  a0_ref, aup_ref, gup_ref, kk_ref, ka_ref, rk_ref, lnw_ref, lnb_ref,
                  mix_ref, Cout_ref, n_ref, m_ref, Sout_ref, convt_ref, shiftt_ref,
                  proj_s, C_ref, S_ref, qk_tail_ref, rw_tail_ref, q_ref, k_ref, rs_ref):
    s = pl.program_id(0)
    t = jnp.maximum(s - 1, 0) % nt
    proj_ref = proj_s.at[pl.ds(1 - s % 2, 1)]
    proj_next = proj_s.at[pl.ds(s % 2, 1)]

    @pl.when(s == 0)
    def _():
        proj_s[...] = jnp.zeros_like(proj_s)

    @pl.when(t == 0)
    def _():
        C_ref[...] = jnp.zeros_like(C_ref)
        n_ref[...] = jnp.zeros_like(n_ref)
        m_ref[...] = jnp.zeros_like(m_ref)
        S_ref[...] = jnp.zeros_like(S_ref)
        qk_tail_ref[...] = jnp.zeros_like(qk_tail_ref)
        rw_tail_ref[...] = jnp.zeros_like(rw_tail_ref)

    u = proj_ref[0, :, Q0:Q0 + 2 * GROUP_W]
    qk_tail = qk_tail_ref[...]
    acc = convb_ref[...] + u * convw_ref[CONV_W - 1:CONV_W, :]
    for jw in range(CONV_W - 1):
        acc = acc + _shift_rows(u, qk_tail, CONV_W - 1 - jw) * convw_ref[jw:jw + 1, :]
    qk_tail_ref[...] = u[TB - SUBLANES:TB]
    convt_ref[0] = u[TB - SUBLANES:TB]
    qk = acc * jax.nn.sigmoid(acc)
    q_ref[...] = qk[:, 0:GROUP_W]
    k_ref[...] = qk[:, GROUP_W:2 * GROUP_W] * (HEAD_DIM ** -0.5)

    rw = proj_ref[0, :, RW0:RW0 + RW_IN_W]
    rs_ref[...] = rw + mu_ref[...] * (_shift_rows(rw, rw_tail_ref[...], 1) - rw)
    rw_tail_ref[...] = rw[TB - SUBLANES:TB]
    shiftt_ref[0] = rw[TB - SUBLANES:TB]

    assert RW_GROUP * LR == LM
    rwkv = lambda c: _rwkv_chunks(c, rs_ref, w0_ref, wup_ref, a0_ref, aup_ref, gup_ref, kk_ref, ka_ref, rk_ref,
                                  lnw_ref, lnb_ref, mix_ref, S_ref)
    mlstm = lambda c: _mlstm_chunk(c, proj_ref, q_ref, k_ref, ib_ref, fb_ref, mlnw_ref, mix_ref, C_ref, n_ref, m_ref)
    n_groups = TB // LM
    gens = [rwkv(c) for c in range(n_groups)]
    proj = _proj_tiles(x_ref, nw_ref, (wml_ref, wgate_ref, wrw_ref), proj_next)
    ml_first = mlstm(0)
    ml_rest = itertools.chain(*[mlstm(c) for c in range(1, n_groups)])
    _weave(_until_state(gens[0]), proj, ml_first)
    _weave(ml_first)
    for c in range(1, n_groups):
        _weave(_until_state(gens[c]), _drain(gens[c - 1]), proj)
        _weave(_drain(gens[c - 1]))
    _weave(_drain(gens[-1]), ml_rest, proj)
    _weave(ml_rest)
    _weave(proj)

    @pl.when(t == nt - 1)
    def _():
        for h in range(N_HEADS):
            blk = slice((h % 2) * HEAD_DIM, (h % 2 + 1) * HEAD_DIM)
            Cout_ref[0, h] = C_ref[0, h // 2, blk, blk]
            Sout_ref[0, h] = S_ref[0, h // 2, blk, blk]


def _mixer(x, p):
    bsz, seq, _ = x.shape
    nt = seq // TB
    nblk = bsz * nt
    row = lambda w: pl.BlockSpec((1, w), lambda s: (0, 0))
    mat = lambda r, w: pl.BlockSpec((r, w), lambda s: (0, 0))
    resident = lambda a: pl.BlockSpec(a.shape, lambda s: (0, 0), pipeline_mode=pl.Buffered(1))
    blk = lambda s: jnp.maximum(s - 1, 0)
    seq_of = lambda s: blk(s) // nt
    state4 = pl.BlockSpec((1, N_HEADS, HEAD_DIM, HEAD_DIM), lambda s: (seq_of(s), 0, 0, 0))
    state3 = pl.BlockSpec((1, SUBLANES, LANES), lambda s: (seq_of(s), 0, 0))
    tail = lambda w: pl.BlockSpec((1, SUBLANES, w), lambda s: (seq_of(s), 0, 0))
    return pl.pallas_call(
        functools.partial(_mixer_kernel, nt),
        out_shape=(
            jax.ShapeDtypeStruct((nblk, TB, 2 * GROUP_W), F32),
            jax.ShapeDtypeStruct((bsz, N_HEADS, HEAD_DIM, HEAD_DIM), F32),
            jax.ShapeDtypeStruct((bsz, SUBLANES, LANES), F32),
            jax.ShapeDtypeStruct((bsz, SUBLANES, LANES), F32),
            jax.ShapeDtypeStruct((bsz, N_HEADS, HEAD_DIM, HEAD_DIM), F32),
            jax.ShapeDtypeStruct((bsz, SUBLANES, 2 * GROUP_W), F32),
            jax.ShapeDtypeStruct((bsz, SUBLANES, RW_IN_W), F32),
        ),
        grid=(nblk + 1,),
        in_specs=[
            pl.BlockSpec((1, TB, D_MODEL), lambda s: (jnp.minimum(s, nblk - 1), 0, 0)),
            row(D_MODEL)] + [resident(w) for w in p["w_in_parts"]] + [
            mat(CONV_W, 2 * GROUP_W), row(2 * GROUP_W), row(LANES), row(LANES), row(GROUP_W),
            row(RW_IN_W), row(GROUP_W), mat(LANES, GROUP_W), row(GROUP_W), mat(LANES, GROUP_W),
            mat(D_GATE, GROUP_W), row(GROUP_W), row(GROUP_W), row(GROUP_W), row(GROUP_W), row(GROUP_W),
        ],
        out_specs=(
            pl.BlockSpec((1, TB, 2 * GROUP_W), lambda s: (blk(s), 0, 0)),
            state4, state3, state3, state4, tail(2 * GROUP_W), tail(RW_IN_W),
        ),
        scratch_shapes=[
            pltpu.VMEM((2, TB, IN_WP), F32),
            pltpu.VMEM((1, PAIRS, LANES, LANES), F32),
            pltpu.VMEM((1, PAIRS, LANES, LANES), F32),
            pltpu.VMEM((SUBLANES, 2 * GROUP_W), F32),
            pltpu.VMEM((SUBLANES, RW_IN_W), F32),
            pltpu.VMEM((TB, GROUP_W), F32),
            pltpu.VMEM((TB, GROUP_W), F32),
            pltpu.VMEM((TB, RW_IN_W), F32),
        ],
        compiler_params=pltpu.CompilerParams(
            dimension_semantics=("arbitrary",), vmem_limit_bytes=VMEM_LIMIT),
        name="mixer",
    )(x.reshape(nblk, TB, D_MODEL), p["norm_mix_w"], *p["w_in_parts"],
      p["conv_w"], p["conv_b"], p["ib_row"], p["fb_row"], p["mlnorm_w"], p["mu"], p["w0"],
      p["wup_p"], p["a0"], p["aup_p"], p["gup"], p["k_k"], p["k_a"], p["r_k"], p["ln_w"], p["ln_b"])


def _inproj_t_kernel(x_ref, nw_ref, wml_ref, wgate_ref, wrw_ref, o_ref, ot_ref):
    o_ml, o_gate, o_rw = _project(x_ref, nw_ref, wml_ref, wgate_ref, wrw_ref)
    for lo_col, hi_col, o in ((0, GI0, o_ml), (GI0, RW0, o_gate), (RW0, IN_WP, o_rw)):
        o_ref[:, lo_col:hi_col] = o
        ot_ref[lo_col:hi_col, :] = o.T


def _inproj_t(x2d, norm_w, w_in_parts):
    rows = x2d.shape[0]
    return pl.pallas_call(
        _inproj_t_kernel,
        out_shape=(jax.ShapeDtypeStruct((rows, IN_WP), F32), jax.ShapeDtypeStruct((IN_WP, rows), F32)),
        compiler_params=pltpu.CompilerParams(vmem_limit_bytes=VMEM_LIMIT),
        name="inproj_t",
    )(x2d, norm_w, *w_in_parts)


(PC_W0, PC_A0, PC_MLNW, PC_KK, PC_KA, PC_RK, PC_LNW, PC_LNB) = range(8)
PC_CONVB = CONV_W


def _sample_kernel(pt_ref, conv0_ref, shift0_ref, m0_ref, n0_ref, C0_ref, S0_ref,
                   p1024_ref, p512_ref, mu_ref, gb_ref, wup_ref, aup_ref, gup_ref,
                   hml_ref, yrw_ref, C_ref, n_ref, m_ref, S_ref,
                   wk_s, q_s, vr_s, y_s):
    h = pl.program_id(0)
    hrow = lambda base, n=HEAD_DIM: pl.ds(pl.multiple_of(base + h * n, n), n)
    pcol = lambda j: p512_ref[hrow(0), j:j + 1]
    colsum = lambda t: jnp.sum(t, axis=0, keepdims=True)

    def conv_silu(base):
        rows = hrow(base)
        acc = p1024_ref[rows, PC_CONVB:PC_CONVB + 1] + pt_ref[rows, :] * p1024_ref[rows, CONV_W - 1:CONV_W]
        for jw in range(CONV_W - 1):
            acc = acc + conv0_ref[jw, rows, :] * p1024_ref[rows, jw:jw + 1]
        return acc * jax.nn.sigmoid(acc)

    q = conv_silu(Q0)
    k = conv_silu(K0) * (HEAD_DIM ** -0.5)
    v = pt_ref[hrow(V0), :]
    i_g = pt_ref[pl.ds(GI0 + h, 1), :] + gb_ref[pl.ds(h, 1), 0:1]
    logf = _log_sigmoid(pt_ref[pl.ds(GF0 + h, 1), :] + gb_ref[pl.ds(h, 1), 1:2])
    m0 = m0_ref[pl.ds(h, 1), :]
    m_new = jnp.maximum(logf + m0, i_g)
    ws = jnp.exp(i_g - m_new)
    dec = jnp.exp(logf + m0 - m_new)
    wk = ws * k
    n_new = dec * n0_ref[0] + wk
    n_ref[0] = n_new
    m_ref[pl.ds(h, 1), :] = m_new
    wk_s[...] = wk
    q_s[...] = q

    def c_body(kk_, num):
        ck = dec * C0_ref[0, kk_] + wk_s[pl.ds(kk_, 1), :] * v
        C_ref[0, kk_] = ck
        return num + q_s[pl.ds(kk_, 1), :] * ck

    num = lax.fori_loop(0, HEAD_DIM, c_body, jnp.zeros((HEAD_DIM, pt_ref.shape[1]), F32), unroll=SAMPLE_UNROLL)
    den = colsum(q * n_new)
    hml = jax.nn.sigmoid(pt_ref[hrow(O0), :]) * (num / jnp.maximum(jnp.abs(den), jnp.exp(-m_new)))
    hml_ref[...] = hml * lax.rsqrt(colsum(hml * hml) * (1.0 / HEAD_DIM) + EPS) * pcol(PC_MLNW)

    def shifted(base, n=HEAD_DIM):
        rw = pt_ref[hrow(RW0 + base, n) if n == HEAD_DIM else pl.ds(RW0 + base, n), :]
        rows = hrow(base, n) if n == HEAD_DIM else pl.ds(base, n)
        return rw + mu_ref[rows, :] * (shift0_ref[rows, :] - rw)

    r, kr, vr = shifted(RS_R), shifted(RS_K), shifted(RS_V)
    lora = shifted(RS_LORA, LANES)
    xg = shifted(RS_G, D_GATE)
    hw = hrow(0)
    w_raw = pcol(PC_W0) + jnp.dot(wup_ref[hw, :], jnp.tanh(lora).astype(BF16), preferred_element_type=F32)
    w = jnp.exp(-W_SCALE * jax.nn.sigmoid(w_raw))
    a = jax.nn.sigmoid(pcol(PC_A0) + jnp.dot(aup_ref[hw, :], lora.astype(BF16), preferred_element_type=F32))
    g = jnp.dot(gup_ref[hw, :], jax.nn.sigmoid(xg).astype(BF16), preferred_element_type=F32)
    kk = kr * pcol(PC_KK)
    kkn = kk / jnp.maximum(jnp.sqrt(colsum(kk * kk)), 1e-12)
    keff = kr * (1.0 + (a - 1.0) * pcol(PC_KA))
    nkk = -kkn
    ka = kkn * a
    vr_s[...] = vr

    def s_body(v_, carry):
        sv = S0_ref[0, v_]
        sn = sv * w + colsum(sv * nkk) * ka + vr_s[pl.ds(v_, 1), :] * keff
        S_ref[0, v_] = sn
        y_s[pl.ds(v_, 1), :] = colsum(sn * r)
        return carry

    lax.fori_loop(0, HEAD_DIM, s_body, 0, unroll=SAMPLE_UNROLL)
    yb = y_s[...] + colsum(r * keff * pcol(PC_RK)) * vr
    dlt = yb - colsum(yb) * (1.0 / HEAD_DIM)
    var = colsum(dlt * dlt) * (1.0 / HEAD_DIM)
    yrw_ref[...] = (dlt * lax.rsqrt(var + GN_EPS) * pcol(PC_LNW) + pcol(PC_LNB)) * g


def _sample(proj_t, conv0_t, shift0_t, m0_t, n0_t, c0_t, s0_t, p):
    bsz = proj_t.shape[1]
    whole = lambda a: pl.BlockSpec(a.shape, lambda h: (0,) * a.ndim)
    state4 = pl.BlockSpec((1, HEAD_DIM, HEAD_DIM, bsz), lambda h: (h, 0, 0, 0))
    head2 = pl.BlockSpec((HEAD_DIM, bsz), lambda h: (h, 0))
    head3 = pl.BlockSpec((1, HEAD_DIM, bsz), lambda h: (h, 0, 0))
    consts = [p["p1024"], p["p512"], p["mu_col"], p["gate_b"], p["wup_t"], p["aup_t"], p["gup_t"]]
    return pl.pallas_call(
        _sample_kernel,
        out_shape=(
            jax.ShapeDtypeStruct((GROUP_W, bsz), F32),
            jax.ShapeDtypeStruct((GROUP_W, bsz), F32),
            jax.ShapeDtypeStruct((N_HEADS, HEAD_DIM, HEAD_DIM, bsz), F32),
            jax.ShapeDtypeStruct((N_HEADS, HEAD_DIM, bsz), F32),
            jax.ShapeDtypeStruct((N_HEADS, bsz), F32),
            jax.ShapeDtypeStruct((N_HEADS, HEAD_DIM, HEAD_DIM, bsz), F32),
        ),
        grid=(N_HEADS,),
        in_specs=[whole(proj_t), whole(conv0_t), whole(shift0_t), whole(m0_t), head3, state4, state4]
                 + [whole(c) for c in consts],
        out_specs=(head2, head2, state4, head3, pl.BlockSpec((N_HEADS, bsz), lambda h: (0, 0)), state4),
        scratch_shapes=[pltpu.VMEM((HEAD_DIM, bsz), F32) for _ in range(4)],
        compiler_params=pltpu.CompilerParams(
            dimension_semantics=("arbitrary",), vmem_limit_bytes=VMEM_LIMIT),
        name="sample",
    )(proj_t, conv0_t, shift0_t, m0_t, n0_t, c0_t, s0_t, *consts)


def _layer_params(l, norm_mix_w, w_in, mlstm_conv_w, mlstm_conv_b, mlstm_i_b, mlstm_f_b, mlstm_norm_w,
                  rw_mu, rw_w0, rw_w_up, rw_a0, rw_a_up, rw_g_up, rw_k_k, rw_k_a, rw_r_k, rw_ln_w, rw_ln_b,
                  w_out, norm_mlp_w, mlp_up, mlp_down):
    ml_in = 4 * GROUP_W
    wi = w_in[l]
    zpad = jnp.zeros((D_MODEL, LANES - N_HEADS), BF16)
    w_in_parts = (
        wi[:, :ml_in].astype(BF16),
        jnp.concatenate([wi[:, ml_in:ml_in + N_HEADS].astype(BF16), zpad,
                         wi[:, ml_in + N_HEADS:ml_in + 2 * N_HEADS].astype(BF16), zpad], axis=1),
        wi[:, ml_in + 2 * N_HEADS:].astype(BF16),
    )
    gate_row = lambda b: jnp.concatenate([b, jnp.zeros((LANES - N_HEADS,), F32)])[None, :]
    zl = jnp.zeros((D_LORA, GROUP_W), F32)
    return {
        "norm_mix_w": norm_mix_w[l][None, :],
        "w_in_parts": w_in_parts,
        "conv_w": mlstm_conv_w[l], "conv_b": mlstm_conv_b[l][None, :],
        "ib_row": gate_row(mlstm_i_b[l]), "fb_row": gate_row(mlstm_f_b[l]),
        "gate_b": jnp.stack([mlstm_i_b[l], mlstm_f_b[l]], axis=1),
        "p1024": jnp.concatenate([mlstm_conv_w[l].T, mlstm_conv_b[l][:, None],
                                  jnp.zeros((2 * GROUP_W, SUBLANES - CONV_W - 1), F32)], axis=1),
        "p512": jnp.stack([rw_w0[l], rw_a0[l], mlstm_norm_w[l], rw_k_k[l], rw_k_a[l], rw_r_k[l].reshape(GROUP_W),
                           rw_ln_w[l], rw_ln_b[l]], axis=1),
        "mu_col": rw_mu[l][:, None],
        "mlnorm_w": mlstm_norm_w[l][None, :],
        "mu": rw_mu[l][None, :], "w0": rw_w0[l][None, :], "a0": rw_a0[l][None, :],
        "wup_p": jnp.concatenate([rw_w_up[l], zl], axis=0).astype(BF16),
        "aup_p": jnp.concatenate([zl, rw_a_up[l]], axis=0).astype(BF16),
        "gup": rw_g_up[l].astype(BF16),
        "wup_t": jnp.concatenate([rw_w_up[l], zl], axis=0).T.astype(BF16),
        "aup_t": jnp.concatenate([zl, rw_a_up[l]], axis=0).T.astype(BF16),
        "gup_t": rw_g_up[l].T.astype(BF16),
        "k_k": rw_k_k[l][None, :], "k_a": rw_k_a[l][None, :], "r_k": rw_r_k[l].reshape(1, GROUP_W),
        "ln_w": rw_ln_w[l][None, :], "ln_b": rw_ln_b[l][None, :],
        "w_out": w_out[l].astype(BF16), "norm_mlp_w": norm_mlp_w[l][None, :],
        "mlp_up": mlp_up[l].astype(BF16), "mlp_down": mlp_down[l].astype(BF16),
    }


def _prompt_layer(x, p, norm_f_w):
    bsz, seq, _ = x.shape
    x2d = x.reshape(bsz * seq, D_MODEL)
    mix, c_st, n_st, m_st, s_st, conv_tail, shift_tail = _mixer(x, p)
    y = _outmlp(x2d, mix.reshape(bsz * seq, D_MODEL), p["w_out"], p["norm_mlp_w"], p["mlp_up"],
                p["mlp_down"], norm_f_w)
    c_new = c_st
    n_new = n_st[:, :PAIRS, :].reshape(bsz, N_HEADS, HEAD_DIM)
    m_new = m_st[:, :, 0]
    conv_new = conv_tail[:, SUBLANES - (CONV_W - 1):, :]
    s_new = s_st
    shift_new = shift_tail[:, SUBLANES - 1:, :]
    return y.reshape(bsz, seq, D_MODEL), (c_new, n_new, m_new, conv_new, s_new, shift_new)


def _sample_layer(x, c0, n0, m0, conv0, s0, shift0, p, norm_f_w):
    bsz = x.shape[0]
    x2d = x.reshape(bsz, D_MODEL)
    proj, proj_t = _inproj_t(x2d, p["norm_mix_w"], p["w_in_parts"])
    hml_t, yrw_t, c_t, n_t, m_t, s_t = _sample(
        proj_t, jnp.transpose(conv0, (1, 2, 0)), shift0[:, 0, :].T, m0.T, jnp.transpose(n0, (1, 2, 0)),
        jnp.transpose(c0, (1, 2, 3, 0)), jnp.transpose(s0, (1, 2, 3, 0)), p)
    mix = jnp.concatenate([hml_t, yrw_t], axis=0).T
    y = _outmlp(x2d, mix, p["w_out"], p["norm_mlp_w"], p["mlp_up"], p["mlp_down"], norm_f_w)
    conv_new = jnp.concatenate([conv0[:, 1:], proj[:, None, Q0:Q0 + 2 * GROUP_W]], axis=1)
    shift_new = proj[:, None, RW0:RW0 + RW_IN_W]
    return y.reshape(bsz, 1, D_MODEL), (
        jnp.transpose(c_t, (3, 0, 1, 2)), jnp.transpose(n_t, (2, 0, 1)), m_t.T, conv_new,
        jnp.transpose(s_t, (3, 0, 1, 2)), shift_new)


def kernel(x_prompt, x_sample, state_mlstm_C, state_mlstm_n, state_mlstm_m, state_mlstm_conv, state_rwkv_S, state_rwkv_shift, norm_mix_w, w_in, mlstm_conv_w, mlstm_conv_b, mlstm_i_b, mlstm_f_b, mlstm_norm_w, rw_mu, rw_w0, rw_w_up, rw_a0, rw_a_up, rw_g_up, rw_k_k, rw_k_a, rw_r_k, rw_ln_w, rw_ln_b, w_out, norm_mlp_w, mlp_up, mlp_down, norm_f_w):
    depth = w_in.shape[0]
    assert depth == 1, "the final RMSNorm is fused into the layer's output kernel"
    weights = (norm_mix_w, w_in, mlstm_conv_w, mlstm_conv_b, mlstm_i_b, mlstm_f_b, mlstm_norm_w,
               rw_mu, rw_w0, rw_w_up, rw_a0, rw_a_up, rw_g_up, rw_k_k, rw_k_a, rw_r_k, rw_ln_w, rw_ln_b,
               w_out, norm_mlp_w, mlp_up, mlp_down)
    nfw = norm_f_w[None, :]
    p = _layer_params(0, *weights)
    y_p, st_p = _prompt_layer(x_prompt, p, nfw)
    y_s, st_s = _sample_layer(x_sample, state_mlstm_C[0], state_mlstm_n[0], state_mlstm_m[0],
                              state_mlstm_conv[0], state_rwkv_S[0], state_rwkv_shift[0], p, nfw)
    lead = lambda t: tuple(a[None] for a in t)
    return (y_p, y_s) + lead(st_p) + lead(st_s)
```

```python
import functools
import itertools
import math

import jax
import jax.numpy as jnp
from jax import lax
from jax.experimental import pallas as pl
from jax.experimental.pallas import tpu as pltpu

F32 = jnp.float32
BF16 = jnp.bfloat16

D_MODEL = 1024
HEAD_DIM = 64
N_HEADS = 8
PAIRS = N_HEADS // 2
GROUP_W = N_HEADS * HEAD_DIM
CONV_W = 4
D_LORA = 64
D_GATE = 128
D_FF = 4 * D_MODEL
EPS = 1e-6
GN_EPS = 64e-5
LANES = 128
SUBLANES = 8

Q0, K0, V0, O0 = 0, GROUP_W, 2 * GROUP_W, 3 * GROUP_W
GI0 = 4 * GROUP_W
GF0 = GI0 + LANES
RW0 = GF0 + LANES
RW_IN_W = 3 * GROUP_W + 2 * D_LORA + D_GATE
IN_WP = RW0 + RW_IN_W
RS_R, RS_K, RS_V = 0, GROUP_W, 2 * GROUP_W
RS_LORA = 3 * GROUP_W
RS_G = RS_LORA + 2 * D_LORA

TB = 256
LM = 128
LR = 64
RW_GROUP = 2
PROJ_TILE = 256
_DONE = object()
_STATE = object()
TM_OUT = 1024
FF_CHUNK = 1024
SAMPLE_UNROLL = 8
NEG = -1e30
W_SCALE = math.exp(-0.5)
VMEM_LIMIT = 56 * 1024 * 1024


def _dot(a, b):
    return jnp.dot(a.astype(BF16), b.astype(BF16), preferred_element_type=F32)


def _split3(x):
    h1 = x.astype(BF16)
    r1 = x - h1.astype(F32)
    h2 = r1.astype(BF16)
    h3 = (r1 - h2.astype(F32)).astype(BF16)
    return h1, h2, h3


def _dot01_l(m01, x):
    h1, h2, h3 = _split3(x)
    d = lambda h: jnp.dot(m01, h, preferred_element_type=F32)
    return d(h1) + d(h2) + d(h3)


def _log_sigmoid(x):
    return jnp.minimum(x, 0.0) - jnp.log(1.0 + jnp.exp(-jnp.abs(x)))


def _tanh(x):
    e = jnp.exp(-2.0 * jnp.abs(x))
    t = (1.0 - e) / (1.0 + e)
    return jnp.where(x < 0.0, -t, t)


def _iota(shape, dim):
    return lax.broadcasted_iota(jnp.int32, shape, dim)


def _rowsum(x):
    return jnp.sum(x, axis=1, keepdims=True)


def _pair_rowsum(x, lo):
    s_lo = _rowsum(jnp.where(lo, x, 0.0))
    s_hi = _rowsum(jnp.where(lo, 0.0, x))
    return jnp.where(lo, s_lo, s_hi)


def _stack2(x, lo):
    zero = jnp.zeros_like(x)
    return jnp.concatenate([jnp.where(lo, x, zero), jnp.where(lo, zero, x)], axis=0)


def _rms(x, w):
    ms = jnp.mean(x * x, axis=-1, keepdims=True)
    return x * lax.rsqrt(ms + EPS) * w


def _project(x_ref, nw_ref, wml_ref, wgate_ref, wrw_ref):
    xn = _rms(x_ref[...], nw_ref[...]).astype(BF16)
    return [jnp.dot(xn, w[...], preferred_element_type=F32) for w in (wml_ref, wgate_ref, wrw_ref)]


def _outmlp_kernel(x_ref, mix_ref, wout_ref, nmw_ref, up_ref, down_ref, nfw_ref, y_ref):
    x1 = x_ref[...] + jnp.dot(mix_ref[...].astype(BF16), wout_ref[...], preferred_element_type=F32)
    xn = _rms(x1, nmw_ref[...]).astype(BF16)
    acc = x1
    for c in range(D_FF // FF_CHUNK):
        sl = slice(c * FF_CHUNK, (c + 1) * FF_CHUNK)
        hid = jnp.maximum(jnp.dot(xn, up_ref[:, sl], preferred_element_type=F32), 0.0)
        acc = acc + jnp.dot((hid * hid).astype(BF16), down_ref[sl, :], preferred_element_type=F32)
    y_ref[...] = _rms(acc, nfw_ref[...])


def _outmlp(x2d, mix2d, w_out, norm_mlp_w, mlp_up, mlp_down, norm_f_w):
    rows = x2d.shape[0]
    tm = min(TM_OUT, rows)
    const = lambda shape: pl.BlockSpec(shape, lambda i: (0, 0), pipeline_mode=pl.Buffered(1))
    return pl.pallas_call(
        _outmlp_kernel,
        out_shape=jax.ShapeDtypeStruct((rows, D_MODEL), F32),
        grid=(rows // tm,),
        in_specs=[
            pl.BlockSpec((tm, D_MODEL), lambda i: (i, 0)),
            pl.BlockSpec((tm, D_MODEL), lambda i: (i, 0)),
            const((D_MODEL, D_MODEL)),
            const((1, D_MODEL)),
            const((D_MODEL, D_FF)),
            const((D_FF, D_MODEL)),
            const((1, D_MODEL)),
        ],
        out_specs=pl.BlockSpec((tm, D_MODEL), lambda i: (i, 0)),
        compiler_params=pltpu.CompilerParams(
            dimension_semantics=("arbitrary",), vmem_limit_bytes=VMEM_LIMIT),
        name="outmlp",
    )(x2d, mix2d, w_out, norm_mlp_w, mlp_up, mlp_down, norm_f_w)


def _rwkv_token_prep(lora_t, xg, w0, wup, a0, aup, gup):
    w_raw = w0 + jnp.dot(_tanh(lora_t).astype(BF16), wup, preferred_element_type=F32)
    wlog = -W_SCALE * jax.nn.sigmoid(w_raw)
    a = jax.nn.sigmoid(a0 + jnp.dot(lora_t.astype(BF16), aup, preferred_element_type=F32))
    g = jnp.dot(jax.nn.sigmoid(xg).astype(BF16), gup, preferred_element_type=F32)
    return wlog, a, g


def _mlstm_chunk(c, proj_ref, q_ref, k_ref, ib_ref, fb_ref, mlnw_ref, mix_ref, C_ref, n_ref, m_ref):
    rows = pl.ds(pl.multiple_of(c * LM, LM), LM)
    row = _iota((LM, LM), 0)
    col = _iota((LM, LM), 1)
    causal = col <= row
    tril = jnp.where(causal, 1.0, 0.0).astype(BF16)
    blockdiag = (row < HEAD_DIM) == (col < HEAD_DIM)
    lane = _iota((1, LANES), 1)
    lo = lane < HEAD_DIM

    i_tile = proj_ref[0, rows, GI0:GI0 + LANES] + ib_ref[...]
    f_tile = _log_sigmoid(proj_ref[0, rows, GF0:GF0 + LANES] + fb_ref[...])
    bc = _dot01_l(tril, f_tile)
    u = i_tile - bc
    ut = u.T
    yield

    nt = (((1,), (1,)), ((), ()))
    pairs = range(PAIRS)
    heads = range(N_HEADS)
    tile = lambda base, j: slice(base + j * LANES, base + (j + 1) * LANES)
    hmask = lambda h: lo if h % 2 == 0 else jnp.logical_not(lo)
    q_t = [q_ref[rows, tile(0, j)] for j in pairs]
    k_t = [k_ref[rows, tile(0, j)] for j in pairs]
    v_t = [proj_ref[0, rows, tile(V0, j)] for j in pairs]
    c_pair = [C_ref[0, j] for j in pairs]
    n_row = [n_ref[0, j:j + 1, :] for j in pairs]
    k_bf = [k.astype(BF16) for k in k_t]
    v_bf = [v.astype(BF16) for v in v_t]
    s = [lax.dot_general(jnp.where(hmask(h), q_t[h // 2], 0.0).astype(BF16), k_bf[h // 2], nt,
                         preferred_element_type=F32) for h in heads]
    q_c = [_dot(q_t[j], c_pair[j]) for j in pairs]
    k_tr = [k.T for k in k_t]
    yield

    m_prev_t = m_ref[0]
    ut8 = ut[0:N_HEADS, :]
    m_last_t = jnp.maximum(jnp.max(ut8, axis=1, keepdims=True), m_prev_t)
    ws_rows = jnp.exp(ut8 - m_last_t)
    dec_t = jnp.exp(m_prev_t - m_last_t)
    m_ref[0] = bc.T[0:N_HEADS, LM - 1:LM] + m_last_t

    m_prev = [m_prev_t[h:h + 1, 0:1] for h in heads]
    u_row = [ut[h:h + 1, :] for h in heads]
    bc_col = [_rowsum(jnp.where(lane == h, bc, 0.0)) for h in heads]
    m_col = [jnp.maximum(jnp.max(jnp.where(causal, u_row[h], NEG), axis=1, keepdims=True), m_prev[h])
             for h in heads]
    p = [s[h] * jnp.exp(jnp.where(causal, u_row[h] - m_col[h], NEG)) for h in heads]
    sc = [jnp.exp(m_prev[h] - m_col[h]) for h in heads]
    yield
    pv = [jnp.dot(p[h].astype(BF16), v_bf[h // 2], preferred_element_type=F32) for h in heads]
    yield
    h_parts = []
    for h in heads:
        q_n = _rowsum(jnp.where(hmask(h), q_t[h // 2] * n_row[h // 2], 0.0))
        den = _rowsum(p[h]) + sc[h] * q_n
        num = pv[h] + sc[h] * q_c[h // 2]
        h_parts.append(num / jnp.maximum(jnp.abs(den), jnp.exp(-(bc_col[h] + m_col[h]))))

    top = _iota((LM, 1), 0) < HEAD_DIM
    kw = [k_tr[j] * jnp.where(top, ws_rows[2 * j:2 * j + 1, :], ws_rows[2 * j + 1:2 * j + 2, :])
          for j in pairs]
    upd = [jnp.dot(kw[j].astype(BF16), v_bf[j], preferred_element_type=F32) for j in pairs]
    n_upd = [jnp.dot(ws_rows.astype(BF16), k_bf[j], preferred_element_type=F32) for j in pairs]
    yield
    for j in pairs:
        dec_j = jnp.where(lo, dec_t[2 * j:2 * j + 1, :], dec_t[2 * j + 1:2 * j + 2, :])
        C_ref[0, j] = c_pair[j] * dec_j + jnp.where(blockdiag, upd[j], 0.0)
        n_ref[0, j:j + 1, :] = n_row[j] * dec_j + jnp.where(lo, n_upd[j][2 * j:2 * j + 1, :],
                                                            n_upd[j][2 * j + 1:2 * j + 2, :])

    for j in pairs:
        h_til = jnp.where(lo, h_parts[2 * j], h_parts[2 * j + 1])
        hml = jax.nn.sigmoid(proj_ref[0, rows, tile(O0, j)]) * h_til
        ms = _pair_rowsum(hml * hml, lo) * (1.0 / HEAD_DIM)
        mix_ref[0, rows, tile(0, j)] = hml * lax.rsqrt(ms + EPS) * mlnw_ref[:, tile(0, j)]


def _neumann_inverse(ns):
    shape = ns[0].shape
    eye = jnp.where(_iota(shape, 0) == _iota(shape, 1), 1.0, 0.0)
    n_sq = int(math.log2(LR)) - 1
    qs = [n.astype(BF16) for n in ns]
    ts = [eye + n for n in ns]
    sq = [jnp.dot(q, q, preferred_element_type=F32) for q in qs]
    yield
    for _ in range(n_sq - 1):
        qs = [s.astype(BF16) for s in sq]
        both = [jnp.dot(q, jnp.concatenate([q, t.astype(BF16)], axis=1), preferred_element_type=F32)
                for q, t in zip(qs, ts)]
        sq = [b[:, 0:LANES] for b in both]
        ts = [t + b[:, LANES:2 * LANES] for t, b in zip(ts, both)]
        yield
    return [(t + jnp.dot(s.astype(BF16), t.astype(BF16), preferred_element_type=F32)).astype(BF16)
            for s, t in zip(sq, ts)]


def _rwkv_chunks(c, rs_ref, w0_ref, wup_ref, a0_ref, aup_ref, gup_ref, kk_ref, ka_ref, rk_ref,
                 lnw_ref, lnb_ref, mix_ref, S_ref):
    gl = RW_GROUP * LR
    rows = pl.ds(pl.multiple_of(c * gl, gl), gl)
    lane = _iota((1, LANES), 1)
    lo = lane < HEAD_DIM
    row = _iota((LANES, LANES), 0)
    col = _iota((LANES, LANES), 1)
    same = (row < LR) == (col < LR)
    mask_sl = jnp.logical_and(same, col < row)
    mask_li = jnp.logical_and(same, col <= row)
    mask_li2 = jnp.concatenate([mask_li, mask_li], axis=1)
    grow = _iota((gl, gl), 0)
    gcol = _iota((gl, gl), 1)
    tril = jnp.where(jnp.logical_and(gcol <= grow, gcol >= (grow // LR) * LR), 1.0, 0.0).astype(BF16)
    nt = (((1,), (1,)), ((), ()))
    pairs = range(PAIRS)
    items = [(ci, j) for ci in range(RW_GROUP) for j in pairs]
    tile = lambda base, j: slice(base + j * LANES, base + (j + 1) * LANES)
    crow = lambda ci: slice(ci * LR, (ci + 1) * LR)

    wlog, a, g = _rwkv_token_prep(rs_ref[rows, RS_LORA:RS_LORA + LANES], rs_ref[rows, RS_G:RS_G + D_GATE],
                                  w0_ref[...], wup_ref[...], a0_ref[...], aup_ref[...], gup_ref[...])
    lw = _dot01_l(tril, wlog)
    e_in = jnp.exp(lw)
    e_ex = jnp.exp(lw - wlog)
    e_ng = jnp.exp(-lw)

    r_g = [rs_ref[rows, tile(RS_R, j)] for j in pairs]
    kr_g = [rs_ref[rows, tile(RS_K, j)] for j in pairs]
    v_g = [rs_ref[rows, tile(RS_V, j)] for j in pairs]
    a_g = [a[:, tile(0, j)] for j in pairs]
    kk = [kr_g[j] * kk_ref[:, tile(0, j)] for j in pairs]
    kkn = [kk[j] * lax.rsqrt(jnp.maximum(_pair_rowsum(kk[j] * kk[j], lo), 1e-24)) for j in pairs]
    keff_g = [kr_g[j] * (1.0 + (a_g[j] - 1.0) * ka_ref[:, tile(0, j)]) for j in pairs]
    alpha_g = [-kkn[j] * e_ex[:, tile(0, j)] for j in pairs]
    rb_g = [r_g[j] * e_in[:, tile(0, j)] for j in pairs]
    beta_g = [kkn[j] * a_g[j] * e_ng[:, tile(0, j)] for j in pairs]
    ktl_g = [keff_g[j] * e_ng[:, tile(0, j)] for j in pairs]
    bonus_g = [_pair_rowsum(r_g[j] * keff_g[j] * rk_ref[:, tile(0, j)], lo) * v_g[j] for j in pairs]

    lhs, rhs, v2, bk, gamma, ar = {}, {}, {}, {}, {}, {}
    for it in items:
        ci, j = it
        cr = crow(ci)
        ar[it] = jnp.concatenate([alpha_g[j][cr], rb_g[j][cr]], axis=0).astype(BF16)
        lhs[it] = jnp.concatenate([_stack2(ar[it][0:LR], lo), _stack2(ar[it][LR:2 * LR], lo)], axis=0)
        beta_c, ktl_c = beta_g[j][cr].astype(BF16), ktl_g[j][cr].astype(BF16)
        rhs[it] = jnp.concatenate([beta_c, beta_c, ktl_c, ktl_c], axis=0)
        v2[it] = _stack2(v_g[j][cr].astype(BF16), lo)
        bk[it] = jnp.concatenate([_stack2(beta_c, lo), _stack2(ktl_c, lo)], axis=0)
        gamma[it] = e_in[(ci + 1) * LR - 1:(ci + 1) * LR, tile(0, j)]
    aa = {it: lax.dot_general(lhs[it], rhs[it], nt, preferred_element_type=F32) for it in items}
    yield
    a_ab = [jnp.where(mask_sl, aa[it][0:LANES, 0:LANES], 0.0) for it in items]
    a_ak = {it: jnp.where(mask_sl, aa[it][0:LANES, LANES:2 * LANES], 0.0).astype(BF16) for it in items}
    a_r = {it: jnp.where(mask_li2, aa[it][LANES:2 * LANES, :], 0.0).astype(BF16) for it in items}
    akv = {it: jnp.dot(a_ak[it], v2[it], preferred_element_type=F32) for it in items}
    yield
    t_list = yield from _neumann_inverse(a_ab)
    t_inv = dict(zip(items, t_list))
    yield _STATE

    s_pair = [S_ref[0, j] for j in pairs]
    for ci in range(RW_GROUP):
        its = [(ci, j) for j in pairs]
        xs = [lax.dot_general(ar[it], s_pair[it[1]].astype(BF16), nt, preferred_element_type=F32)
              for it in its]
        yield
        p2 = [_dot(t_inv[it], _stack2(xs[j][0:LR], lo) + akv[it]) for j, it in enumerate(its)]
        yield
        pv = [jnp.concatenate([p2[j].astype(BF16), v2[it]], axis=0) for j, it in enumerate(its)]
        y2 = [_dot(a_r[it], pv[j]) for j, it in enumerate(its)]
        upd = [lax.dot_general(pv[j], bk[it], (((0,), (0,)), ((), ())), preferred_element_type=F32)
               for j, it in enumerate(its)]
        s_pair = [(s_pair[j] + upd[j]) * gamma[it] for j, it in enumerate(its)]
        yield
        cr = crow(ci)
        for j in pairs:
            yb = xs[j][LR:2 * LR] + y2[j][0:LR] + y2[j][LR:2 * LR] + bonus_g[j][cr]
            mu = _pair_rowsum(yb, lo) * (1.0 / HEAD_DIM)
            dlt = yb - mu
            var = _pair_rowsum(dlt * dlt, lo) * (1.0 / HEAD_DIM)
            yn = dlt * lax.rsqrt(var + GN_EPS) * lnw_ref[:, tile(0, j)] + lnb_ref[:, tile(0, j)]
            out_rows = pl.ds(pl.multiple_of(c * gl + ci * LR, LR), LR)
            mix_ref[0, out_rows, tile(GROUP_W, j)] = yn * g[cr, tile(0, j)]
    for j in pairs:
        S_ref[0, j] = s_pair[j]


def _shift_rows(x, tail, j):
    rolled = pltpu.roll(x, j, axis=0)
    head = jnp.where(_iota((SUBLANES, 1), 0) < j, pltpu.roll(tail, j, axis=0), rolled[0:SUBLANES])
    return jnp.concatenate([head, rolled[SUBLANES:]], axis=0)


def _until_state(gen):
    for tok in gen:
        if tok is _STATE:
            return
        yield


def _drain(gen):
    for _ in gen:
        yield


def _every(gen, n):
    while True:
        for _ in range(n - 1):
            yield
        if next(gen, _DONE) is _DONE:
            return
        yield


def _weave(primary, *others):
    while next(primary, _DONE) is not _DONE:
        for g in others:
            next(g, _DONE)


def _proj_tiles(x_ref, nw_ref, w_refs, dst_ref):
    xn = _rms(x_ref[0], nw_ref[...]).astype(BF16)
    yield
    for w_ref, col0 in zip(w_refs, (0, GI0, RW0)):
        for c0 in range(0, w_ref.shape[1], PROJ_TILE):
            cw = min(PROJ_TILE, w_ref.shape[1] - c0)
            dst_ref[0, :, col0 + c0:col0 + c0 + cw] = jnp.dot(xn, w_ref[:, c0:c0 + cw],
                                                              preferred_element_type=F32)
            yield


def _mixer_kernel(nt, x_ref, nw_ref, wml_ref, wgate_ref, wrw_ref,
                  convw_ref, convb_ref, ib_ref, fb_ref, mlnw_ref, mu_ref, w0_ref, wup_ref,
                  a0_ref, aup_ref, gup_ref, kk_ref, ka_ref, rk_ref, lnw_ref, lnb_ref,
                  mix_ref, Cout_ref, n_ref, m_ref, Sout_ref, convt_ref, shiftt_ref,
                  proj_s, C_ref, S_ref, qk_tail_ref, rw_tail_ref, q_ref, k_ref, rs_ref):
    s = pl.program_id(0)
    t = jnp.maximum(s - 1, 0) % nt
    proj_ref = proj_s.at[pl.ds(1 - s % 2, 1)]
    proj_next = proj_s.at[pl.ds(s % 2, 1)]

    @pl.when(s == 0)
    def _():
        proj_s[...] = jnp.zeros_like(proj_s)

    @pl.when(t == 0)
    def _():
        C_ref[...] = jnp.zeros_like(C_ref)
        n_ref[...] = jnp.zeros_like(n_ref)
        m_ref[...] = jnp.zeros_like(m_ref)
        S_ref[...] = jnp.zeros_like(S_ref)
        qk_tail_ref[...] = jnp.zeros_like(qk_tail_ref)
        rw_tail_ref[...] = jnp.zeros_like(rw_tail_ref)

    u = proj_ref[0, :, Q0:Q0 + 2 * GROUP_W]
    qk_tail = qk_tail_ref[...]
    acc = convb_ref[...] + u * convw_ref[CONV_W - 1:CONV_W, :]
    for jw in range(CONV_W - 1):
        acc = acc + _shift_rows(u, qk_tail, CONV_W - 1 - jw) * convw_ref[jw:jw + 1, :]
    qk_tail_ref[...] = u[TB - SUBLANES:TB]
    convt_ref[0] = u[TB - SUBLANES:TB]
    qk = acc * jax.nn.sigmoid(acc)
    q_ref[...] = qk[:, 0:GROUP_W]
    k_ref[...] = qk[:, GROUP_W:2 * GROUP_W] * (HEAD_DIM ** -0.5)

    rw = proj_ref[0, :, RW0:RW0 + RW_IN_W]
    rs_ref[...] = rw + mu_ref[...] * (_shift_rows(rw, rw_tail_ref[...], 1) - rw)
    rw_tail_ref[...] = rw[TB - SUBLANES:TB]
    shiftt_ref[0] = rw[TB - SUBLANES:TB]

    assert RW_GROUP * LR == LM
    rwkv = lambda c: _rwkv_chunks(c, rs_ref, w0_ref, wup_ref, a0_ref, aup_ref, gup_ref, kk_ref, ka_ref, rk_ref,
                                  lnw_ref, lnb_ref, mix_ref, S_ref)
    mlstm = lambda c: _mlstm_chunk(c, proj_ref, q_ref, k_ref, ib_ref, fb_ref, mlnw_ref, mix_ref, C_ref, n_ref, m_ref)
    n_groups = TB // LM
    gens = [rwkv(c) for c in range(n_groups)]
    proj = _proj_tiles(x_ref, nw_ref, (wml_ref, wgate_ref, wrw_ref), proj_next)
    ml_first = mlstm(0)
    ml_rest = itertools.chain(*[mlstm(c) for c in range(1, n_groups)])
    _weave(_until_state(gens[0]), proj, ml_first)
    _weave(ml_first)
    for c in range(1, n_groups):
        _weave(_until_state(gens[c]), _drain(gens[c - 1]), proj, _every(ml_rest, 2))
        _weave(_drain(gens[c - 1]))
    _weave(_drain(gens[-1]), ml_rest, proj)
    _weave(ml_rest)
    _weave(proj)

    @pl.when(t == nt - 1)
    def _():
        for h in range(N_HEADS):
            blk = slice((h % 2) * HEAD_DIM, (h % 2 + 1) * HEAD_DIM)
            Cout_ref[0, h] = C_ref[0, h // 2, blk, blk]
            Sout_ref[0, h] = S_ref[0, h // 2, blk, blk]


def _mixer(x, p):
    bsz, seq, _ = x.shape
    nt = seq // TB
    nblk = bsz * nt
    row = lambda w: pl.BlockSpec((1, w), lambda s: (0, 0))
    mat = lambda r, w: pl.BlockSpec((r, w), lambda s: (0, 0))
    resident = lambda a: pl.BlockSpec(a.shape, lambda s: (0, 0), pipeline_mode=pl.Buffered(1))
    blk = lambda s: jnp.maximum(s - 1, 0)
    seq_of = lambda s: blk(s) // nt
    state4 = pl.BlockSpec((1, N_HEADS, HEAD_DIM, HEAD_DIM), lambda s: (seq_of(s), 0, 0, 0))
    state3 = pl.BlockSpec((1, SUBLANES, LANES), lambda s: (seq_of(s), 0, 0))
    tail = lambda w: pl.BlockSpec((1, SUBLANES, w), lambda s: (seq_of(s), 0, 0))
    return pl.pallas_call(
        functools.partial(_mixer_kernel, nt),
        out_shape=(
            jax.ShapeDtypeStruct((nblk, TB, 2 * GROUP_W), F32),
            jax.ShapeDtypeStruct((bsz, N_HEADS, HEAD_DIM, HEAD_DIM), F32),
            jax.ShapeDtypeStruct((bsz, SUBLANES, LANES), F32),
            jax.ShapeDtypeStruct((bsz, SUBLANES, LANES), F32),
            jax.ShapeDtypeStruct((bsz, N_HEADS, HEAD_DIM, HEAD_DIM), F32),
            jax.ShapeDtypeStruct((bsz, SUBLANES, 2 * GROUP_W), F32),
            jax.ShapeDtypeStruct((bsz, SUBLANES, RW_IN_W), F32),
        ),
        grid=(nblk + 1,),
        in_specs=[
            pl.BlockSpec((1, TB, D_MODEL), lambda s: (jnp.minimum(s, nblk - 1), 0, 0)),
            row(D_MODEL)] + [resident(w) for w in p["w_in_parts"]] + [
            mat(CONV_W, 2 * GROUP_W), row(2 * GROUP_W), row(LANES), row(LANES), row(GROUP_W),
            row(RW_IN_W), row(GROUP_W), mat(LANES, GROUP_W), row(GROUP_W), mat(LANES, GROUP_W),
            mat(D_GATE, GROUP_W), row(GROUP_W), row(GROUP_W), row(GROUP_W), row(GROUP_W), row(GROUP_W),
        ],
        out_specs=(
            pl.BlockSpec((1, TB, 2 * GROUP_W), lambda s: (blk(s), 0, 0)),
            state4, state3, state3, state4, tail(2 * GROUP_W), tail(RW_IN_W),
        ),
        scratch_shapes=[
            pltpu.VMEM((2, TB, IN_WP), F32),
            pltpu.VMEM((1, PAIRS, LANES, LANES), F32),
            pltpu.VMEM((1, PAIRS, LANES, LANES), F32),
            pltpu.VMEM((SUBLANES, 2 * GROUP_W), F32),
            pltpu.VMEM((SUBLANES, RW_IN_W), F32),
            pltpu.VMEM((TB, GROUP_W), F32),
            pltpu.VMEM((TB, GROUP_W), F32),
            pltpu.VMEM((TB, RW_IN_W), F32),
        ],
        compiler_params=pltpu.CompilerParams(
            dimension_semantics=("arbitrary",), vmem_limit_bytes=VMEM_LIMIT),
        name="mixer",
    )(x.reshape(nblk, TB, D_MODEL), p["norm_mix_w"], *p["w_in_parts"],
      p["conv_w"], p["conv_b"], p["ib_row"], p["fb_row"], p["mlnorm_w"], p["mu"], p["w0"],
      p["wup_p"], p["a0"], p["aup_p"], p["gup"], p["k_k"], p["k_a"], p["r_k"], p["ln_w"], p["ln_b"])


def _inproj_t_kernel(x_ref, nw_ref, wml_ref, wgate_ref, wrw_ref, o_ref, ot_ref):
    o_ml, o_gate, o_rw = _project(x_ref, nw_ref, wml_ref, wgate_ref, wrw_ref)
    for lo_col, hi_col, o in ((0, GI0, o_ml), (GI0, RW0, o_gate), (RW0, IN_WP, o_rw)):
        o_ref[:, lo_col:hi_col] = o
        ot_ref[lo_col:hi_col, :] = o.T


def _inproj_t(x2d, norm_w, w_in_parts):
    rows = x2d.shape[0]
    return pl.pallas_call(
        _inproj_t_kernel,
        out_shape=(jax.ShapeDtypeStruct((rows, IN_WP), F32), jax.ShapeDtypeStruct((IN_WP, rows), F32)),
        compiler_params=pltpu.CompilerParams(vmem_limit_bytes=VMEM_LIMIT),
        name="inproj_t",
    )(x2d, norm_w, *w_in_parts)


(PC_W0, PC_A0, PC_MLNW, PC_KK, PC_KA, PC_RK, PC_LNW, PC_LNB) = range(8)
PC_CONVB = CONV_W


def _sample_kernel(pt_ref, conv0_ref, shift0_ref, m0_ref, n0_ref, C0_ref, S0_ref,
                   p1024_ref, p512_ref, mu_ref, gb_ref, wup_ref, aup_ref, gup_ref,
                   hml_ref, yrw_ref, C_ref, n_ref, m_ref, S_ref,
                   wk_s, q_s, vr_s, y_s):
    h = pl.program_id(0)
    hrow = lambda base, n=HEAD_DIM: pl.ds(pl.multiple_of(base + h * n, n), n)
    pcol = lambda j: p512_ref[hrow(0), j:j + 1]
    colsum = lambda t: jnp.sum(t, axis=0, keepdims=True)

    def conv_silu(base):
        rows = hrow(base)
        acc = p1024_ref[rows, PC_CONVB:PC_CONVB + 1] + pt_ref[rows, :] * p1024_ref[rows, CONV_W - 1:CONV_W]
        for jw in range(CONV_W - 1):
            acc = acc + conv0_ref[jw, rows, :] * p1024_ref[rows, jw:jw + 1]
        return acc * jax.nn.sigmoid(acc)

    q = conv_silu(Q0)
    k = conv_silu(K0) * (HEAD_DIM ** -0.5)
    v = pt_ref[hrow(V0), :]
    i_g = pt_ref[pl.ds(GI0 + h, 1), :] + gb_ref[pl.ds(h, 1), 0:1]
    logf = _log_sigmoid(pt_ref[pl.ds(GF0 + h, 1), :] + gb_ref[pl.ds(h, 1), 1:2])
    m0 = m0_ref[pl.ds(h, 1), :]
    m_new = jnp.maximum(logf + m0, i_g)
    ws = jnp.exp(i_g - m_new)
    dec = jnp.exp(logf + m0 - m_new)
    wk = ws * k
    n_new = dec * n0_ref[0] + wk
    n_ref[0] = n_new
    m_ref[pl.ds(h, 1), :] = m_new
    wk_s[...] = wk
    q_s[...] = q

    def c_body(kk_, num):
        ck = dec * C0_ref[0, kk_] + wk_s[pl.ds(kk_, 1), :] * v
        C_ref[0, kk_] = ck
        return num + q_s[pl.ds(kk_, 1), :] * ck

    num = lax.fori_loop(0, HEAD_DIM, c_body, jnp.zeros((HEAD_DIM, pt_ref.shape[1]), F32), unroll=SAMPLE_UNROLL)
    den = colsum(q * n_new)
    hml = jax.nn.sigmoid(pt_ref[hrow(O0), :]) * (num / jnp.maximum(jnp.abs(den), jnp.exp(-m_new)))
    hml_ref[...] = hml * lax.rsqrt(colsum(hml * hml) * (1.0 / HEAD_DIM) + EPS) * pcol(PC_MLNW)

    def shifted(base, n=HEAD_DIM):
        rw = pt_ref[hrow(RW0 + base, n) if n == HEAD_DIM else pl.ds(RW0 + base, n), :]
        rows = hrow(base, n) if n == HEAD_DIM else pl.ds(base, n)
        return rw + mu_ref[rows, :] * (shift0_ref[rows, :] - rw)

    r, kr, vr = shifted(RS_R), shifted(RS_K), shifted(RS_V)
    lora = shifted(RS_LORA, LANES)
    xg = shifted(RS_G, D_GATE)
    hw = hrow(0)
    w_raw = pcol(PC_W0) + jnp.dot(wup_ref[hw, :], jnp.tanh(lora).astype(BF16), preferred_element_type=F32)
    w = jnp.exp(-W_SCALE * jax.nn.sigmoid(w_raw))
    a = jax.nn.sigmoid(pcol(PC_A0) + jnp.dot(aup_ref[hw, :], lora.astype(BF16), preferred_element_type=F32))
    g = jnp.dot(gup_ref[hw, :], jax.nn.sigmoid(xg).astype(BF16), preferred_element_type=F32)
    kk = kr * pcol(PC_KK)
    kkn = kk / jnp.maximum(jnp.sqrt(colsum(kk * kk)), 1e-12)
    keff = kr * (1.0 + (a - 1.0) * pcol(PC_KA))
    nkk = -kkn
    ka = kkn * a
    vr_s[...] = vr

    def s_body(v_, carry):
        sv = S0_ref[0, v_]
        sn = sv * w + colsum(sv * nkk) * ka + vr_s[pl.ds(v_, 1), :] * keff
        S_ref[0, v_] = sn
        y_s[pl.ds(v_, 1), :] = colsum(sn * r)
        return carry

    lax.fori_loop(0, HEAD_DIM, s_body, 0, unroll=SAMPLE_UNROLL)
    yb = y_s[...] + colsum(r * keff * pcol(PC_RK)) * vr
    dlt = yb - colsum(yb) * (1.0 / HEAD_DIM)
    var = colsum(dlt * dlt) * (1.0 / HEAD_DIM)
    yrw_ref[...] = (dlt * lax.rsqrt(var + GN_EPS) * pcol(PC_LNW) + pcol(PC_LNB)) * g


def _sample(proj_t, conv0_t, shift0_t, m0_t, n0_t, c0_t, s0_t, p):
    bsz = proj_t.shape[1]
    whole = lambda a: pl.BlockSpec(a.shape, lambda h: (0,) * a.ndim)
    state4 = pl.BlockSpec((1, HEAD_DIM, HEAD_DIM, bsz), lambda h: (h, 0, 0, 0))
    head2 = pl.BlockSpec((HEAD_DIM, bsz), lambda h: (h, 0))
    head3 = pl.BlockSpec((1, HEAD_DIM, bsz), lambda h: (h, 0, 0))
    consts = [p["p1024"], p["p512"], p["mu_col"], p["gate_b"], p["wup_t"], p["aup_t"], p["gup_t"]]
    return pl.pallas_call(
        _sample_kernel,
        out_shape=(
            jax.ShapeDtypeStruct((GROUP_W, bsz), F32),
            jax.ShapeDtypeStruct((GROUP_W, bsz), F32),
            jax.ShapeDtypeStruct((N_HEADS, HEAD_DIM, HEAD_DIM, bsz), F32),
            jax.ShapeDtypeStruct((N_HEADS, HEAD_DIM, bsz), F32),
            jax.ShapeDtypeStruct((N_HEADS, bsz), F32),
            jax.ShapeDtypeStruct((N_HEADS, HEAD_DIM, HEAD_DIM, bsz), F32),
        ),
        grid=(N_HEADS,),
        in_specs=[whole(proj_t), whole(conv0_t), whole(shift0_t), whole(m0_t), head3, state4, state4]
                 + [whole(c) for c in consts],
        out_specs=(head2, head2, state4, head3, pl.BlockSpec((N_HEADS, bsz), lambda h: (0, 0)), state4),
        scratch_shapes=[pltpu.VMEM((HEAD_DIM, bsz), F32) for _ in range(4)],
        compiler_params=pltpu.CompilerParams(
            dimension_semantics=("arbitrary",), vmem_limit_bytes=VMEM_LIMIT),
        name="sample",
    )(proj_t, conv0_t, shift0_t, m0_t, n0_t, c0_t, s0_t, *consts)


def _layer_params(l, norm_mix_w, w_in, mlstm_conv_w, mlstm_conv_b, mlstm_i_b, mlstm_f_b, mlstm_norm_w,
                  rw_mu, rw_w0, rw_w_up, rw_a0, rw_a_up, rw_g_up, rw_k_k, rw_k_a, rw_r_k, rw_ln_w, rw_ln_b,
                  w_out, norm_mlp_w, mlp_up, mlp_down):
    ml_in = 4 * GROUP_W
    wi = w_in[l]
    zpad = jnp.zeros((D_MODEL, LANES - N_HEADS), BF16)
    w_in_parts = (
        wi[:, :ml_in].astype(BF16),
        jnp.concatenate([wi[:, ml_in:ml_in + N_HEADS].astype(BF16), zpad,
                         wi[:, ml_in + N_HEADS:ml_in + 2 * N_HEADS].astype(BF16), zpad], axis=1),
        wi[:, ml_in + 2 * N_HEADS:].astype(BF16),
    )
    gate_row = lambda b: jnp.concatenate([b, jnp.zeros((LANES - N_HEADS,), F32)])[None, :]
    zl = jnp.zeros((D_LORA, GROUP_W), F32)
    return {
        "norm_mix_w": norm_mix_w[l][None, :],
        "w_in_parts": w_in_parts,
        "conv_w": mlstm_conv_w[l], "conv_b": mlstm_conv_b[l][None, :],
        "ib_row": gate_row(mlstm_i_b[l]), "fb_row": gate_row(mlstm_f_b[l]),
        "gate_b": jnp.stack([mlstm_i_b[l], mlstm_f_b[l]], axis=1),
        "p1024": jnp.concatenate([mlstm_conv_w[l].T, mlstm_conv_b[l][:, None],
                                  jnp.zeros((2 * GROUP_W, SUBLANES - CONV_W - 1), F32)], axis=1),
        "p512": jnp.stack([rw_w0[l], rw_a0[l], mlstm_norm_w[l], rw_k_k[l], rw_k_a[l], rw_r_k[l].reshape(GROUP_W),
                           rw_ln_w[l], rw_ln_b[l]], axis=1),
        "mu_col": rw_mu[l][:, None],
        "mlnorm_w": mlstm_norm_w[l][None, :],
        "mu": rw_mu[l][None, :], "w0": rw_w0[l][None, :], "a0": rw_a0[l][None, :],
        "wup_p": jnp.concatenate([rw_w_up[l], zl], axis=0).astype(BF16),
        "aup_p": jnp.concatenate([zl, rw_a_up[l]], axis=0).astype(BF16),
        "gup": rw_g_up[l].astype(BF16),
        "wup_t": jnp.concatenate([rw_w_up[l], zl], axis=0).T.astype(BF16),
        "aup_t": jnp.concatenate([zl, rw_a_up[l]], axis=0).T.astype(BF16),
        "gup_t": rw_g_up[l].T.astype(BF16),
        "k_k": rw_k_k[l][None, :], "k_a": rw_k_a[l][None, :], "r_k": rw_r_k[l].reshape(1, GROUP_W),
        "ln_w": rw_ln_w[l][None, :], "ln_b": rw_ln_b[l][None, :],
        "w_out": w_out[l].astype(BF16), "norm_mlp_w": norm_mlp_w[l][None, :],
        "mlp_up": mlp_up[l].astype(BF16), "mlp_down": mlp_down[l].astype(BF16),
    }


def _prompt_layer(x, p, norm_f_w):
    bsz, seq, _ = x.shape
    x2d = x.reshape(bsz * seq, D_MODEL)
    mix, c_st, n_st, m_st, s_st, conv_tail, shift_tail = _mixer(x, p)
    y = _outmlp(x2d, mix.reshape(bsz * seq, D_MODEL), p["w_out"], p["norm_mlp_w"], p["mlp_up"],
                p["mlp_down"], norm_f_w)
    c_new = c_st
    n_new = n_st[:, :PAIRS, :].reshape(bsz, N_HEADS, HEAD_DIM)
    m_new = m_st[:, :, 0]
    conv_new = conv_tail[:, SUBLANES - (CONV_W - 1):, :]
    s_new = s_st
    shift_new = shift_tail[:, SUBLANES - 1:, :]
    return y.reshape(bsz, seq, D_MODEL), (c_new, n_new, m_new, conv_new, s_new, shift_new)


def _sample_layer(x, c0, n0, m0, conv0, s0, shift0, p, norm_f_w):
    bsz = x.shape[0]
    x2d = x.reshape(bsz, D_MODEL)
    proj, proj_t = _inproj_t(x2d, p["norm_mix_w"], p["w_in_parts"])
    hml_t, yrw_t, c_t, n_t, m_t, s_t = _sample(
        proj_t, jnp.transpose(conv0, (1, 2, 0)), shift0[:, 0, :].T, m0.T, jnp.transpose(n0, (1, 2, 0)),
        jnp.transpose(c0, (1, 2, 3, 0)), jnp.transpose(s0, (1, 2, 3, 0)), p)
    mix = jnp.concatenate([hml_t, yrw_t], axis=0).T
    y = _outmlp(x2d, mix, p["w_out"], p["norm_mlp_w"], p["mlp_up"], p["mlp_down"], norm_f_w)
    conv_new = jnp.concatenate([conv0[:, 1:], proj[:, None, Q0:Q0 + 2 * GROUP_W]], axis=1)
    shift_new = proj[:, None, RW0:RW0 + RW_IN_W]
    return y.reshape(bsz, 1, D_MODEL), (
        jnp.transpose(c_t, (3, 0, 1, 2)), jnp.transpose(n_t, (2, 0, 1)), m_t.T, conv_new,
        jnp.transpose(s_t, (3, 0, 1, 2)), shift_new)


def kernel(x_prompt, x_sample, state_mlstm_C, state_mlstm_n, state_mlstm_m, state_mlstm_conv, state_rwkv_S, state_rwkv_shift, norm_mix_w, w_in, mlstm_conv_w, mlstm_conv_b, mlstm_i_b, mlstm_f_b, mlstm_norm_w, rw_mu, rw_w0, rw_w_up, rw_a0, rw_a_up, rw_g_up, rw_k_k, rw_k_a, rw_r_k, rw_ln_w, rw_ln_b, w_out, norm_mlp_w, mlp_up, mlp_down, norm_f_w):
    depth = w_in.shape[0]
    assert depth == 1, "the final RMSNorm is fused into the layer's output kernel"
    weights = (norm_mix_w, w_in, mlstm_conv_w, mlstm_conv_b, mlstm_i_b, mlstm_f_b, mlstm_norm_w,
               rw_mu, rw_w0, rw_w_up, rw_a0, rw_a_up, rw_g_up, rw_k_k, rw_k_a, rw_r_k, rw_ln_w, rw_ln_b,
               w_out, norm_mlp_w, mlp_up, mlp_down)
    nfw = norm_f_w[None, :]
    p = _layer_params(0, *weights)
    y_p, st_p = _prompt_layer(x_prompt, p, nfw)
    y_s, st_s = _sample_layer(x_sample, state_mlstm_C[0], state_mlstm_n[0], state_mlstm_m[0],
                              state_mlstm_conv[0], state_rwkv_S[0], state_rwkv_shift[0], p, nfw)
    lead = lambda t: tuple(a[None] for a in t)
    return (y_p, y_s) + lead(st_p) + lead(st_s)
```
